```python
import numpy as np
import jax
import jax.numpy as jnp
from jax import lax

D_MODEL = 1024
BATCH = 32
SEQ = 256
DEPTH = 2
DEC_BATCH = 2
DEC_SEQ = 4096
PAST_LEN = 512

GRID_W = 64
HEAD_DIM = 64
N_BRANCH = 4
BRANCH_W = D_MODEL // N_BRANCH
A_HQ = BRANCH_W // HEAD_DIM
A_HKV = A_HQ // 2
WINDOW = 128
Q_BLOCK = 128
LRU_W = BRANCH_W
LRU_BLOCKS = 4
LRU_BW = LRU_W // LRU_BLOCKS
LRU_C = 8.0
CONV_W = 4
CONV_LEFT = CONV_W // 2
GDN_H = 4
GDN_DK = BRANCH_W // GDN_H
GDN_DV = BRANCH_W // GDN_H
CHUNK = 64
D_HQ = BRANCH_W // HEAD_DIM
D_HKV = D_HQ // 2
D_FF = 4 * D_MODEL
ROPE_BASE = 10000.0
EPS = 1e-6
NEG = -1e30
IN_WIDTHS = (A_HQ * HEAD_DIM, A_HKV * HEAD_DIM, A_HKV * HEAD_DIM,
             LRU_W, LRU_W,
             GDN_H * GDN_DK, GDN_H * GDN_DK, GDN_H * GDN_DV, GDN_H * GDN_DV, 2 * GDN_H, 2 * GDN_H,
             D_HQ * HEAD_DIM, D_HKV * HEAD_DIM, D_HKV * HEAD_DIM)
IN_COLS = sum(IN_WIDTHS)

kernel_name = "hybrid_diffusion_prefix_step"


def split_points():
    return tuple(int(s) for s in np.cumsum(IN_WIDTHS)[:-1])


def rms_norm(x, g):
    xf = x.astype(jnp.float32)
    y = xf * lax.rsqrt(jnp.mean(xf * xf, axis=-1, keepdims=True) + EPS)
    return (y * g.astype(jnp.float32)).astype(x.dtype)


def l2_normalize(x):
    return x * lax.rsqrt(jnp.sum(x * x, axis=-1, keepdims=True) + EPS)


def modulate(x, g, shift, scale):
    return rms_norm(x, g) * (1 + scale[:, None, :]) + shift[:, None, :]


def centred_dwconv(x, w):
    T = x.shape[1]
    xp = jnp.pad(x, ((0, 0), (CONV_LEFT, CONV_W - 1 - CONV_LEFT), (0, 0)))
    return sum(xp[:, j:j + T] * w[j] for j in range(CONV_W))


def axial_rope(T):
    rows = T // GRID_W
    row = jnp.repeat(jnp.arange(rows, dtype=jnp.float32), GRID_W)
    col = jnp.tile(jnp.arange(GRID_W, dtype=jnp.float32), rows)
    n_freq = HEAD_DIM // 4
    inv = ROPE_BASE ** (-jnp.arange(n_freq, dtype=jnp.float32) / n_freq)
    ang = jnp.stack([row[:, None] * inv, col[:, None] * inv], axis=1)
    return jnp.cos(ang), jnp.sin(ang)


def apply_rope(x, cos, sin):
    T = x.shape[1]
    shp = (T,) + (1,) * (x.ndim - 3) + cos.shape[1:]
    c = cos.reshape(shp)
    s = sin.reshape(shp)
    xr = x.astype(jnp.float32).reshape(x.shape[:-1] + (2, 2, HEAD_DIM // 4))
    x1 = xr[..., 0, :]
    x2 = xr[..., 1, :]
    out = jnp.stack([x1 * c - x2 * s, x2 * c + x1 * s], axis=-2)
    return out.reshape(x.shape).astype(x.dtype)


def sink_softmax(s, sink):
    if sink is None:
        return jax.nn.softmax(s, axis=-1)
    sk = sink.astype(jnp.float32)[:, :, None, None]
    m = jnp.maximum(jnp.max(s, axis=-1, keepdims=True), sk)
    e = jnp.exp(s - m)
    return e / (jnp.sum(e, axis=-1, keepdims=True) + jnp.exp(sk - m))


def attn_heads(q, k, v, qg, kg, hq, hkv):
    B, T, _ = q.shape
    q = rms_norm(q.reshape(B, T, hkv, hq // hkv, HEAD_DIM), qg)
    k = rms_norm(k.reshape(B, T, hkv, HEAD_DIM), kg)
    return q, k, v.reshape(B, T, hkv, HEAD_DIM)


def blocked_attention(q, k, v, sink):
    B, T, HKV, G, _ = q.shape
    nb = T // Q_BLOCK
    qb = jnp.moveaxis(q.reshape(B, nb, Q_BLOCK, HKV, G, HEAD_DIM), 1, 0)
    scale = HEAD_DIM ** -0.5

    def one_block(qi):
        s = jnp.einsum("bqhgd,bkhd->bhgqk", qi, k).astype(jnp.float32) * scale
        p = sink_softmax(s, sink).astype(v.dtype)
        return jnp.einsum("bhgqk,bkhd->bqhgd", p, v)

    o = lax.map(one_block, qb)
    return jnp.moveaxis(o, 0, 1).reshape(B, T, HKV * G * HEAD_DIM)


def banded_attention(q, k, v, ck, cv, sink):
    B, T, HKV, G, _ = q.shape
    nb = T // WINDOW
    qb = q.reshape(B, nb, WINDOW, HKV, G, HEAD_DIM)

    def band(t):
        tp = jnp.pad(t, ((0, 0), (WINDOW, WINDOW), (0, 0), (0, 0))).reshape(B, nb + 2, WINDOW, HKV, HEAD_DIM)
        return jnp.concatenate([tp[:, :-2], tp[:, 1:-1], tp[:, 2:]], axis=2)

    kb, vb = band(k), band(v)
    qpos = jnp.arange(nb)[:, None] * WINDOW + jnp.arange(WINDOW)[None]
    kpos = jnp.arange(nb)[:, None] * WINDOW - WINDOW + jnp.arange(3 * WINDOW)[None]
    valid = ((kpos >= 0) & (kpos < T))[:, None, :]
    mask = valid & (jnp.abs(qpos[:, :, None] - kpos[:, None, :]) <= WINDOW)
    scale = HEAD_DIM ** -0.5
    s_loc = jnp.einsum("bnqhgd,bnkhd->bnhgqk", qb, kb).astype(jnp.float32) * scale
    s_loc = jnp.where(mask[None, :, None, None], s_loc, NEG)
    s_ctx = jnp.einsum("bnqhgd,bkhd->bnhgqk", qb, ck).astype(jnp.float32) * scale
    p = sink_softmax(jnp.concatenate([s_ctx, s_loc], axis=-1), sink).astype(v.dtype)
    C = ck.shape[1]
    o = (jnp.einsum("bnhgqk,bkhd->bnqhgd", p[..., :C], cv)
         + jnp.einsum("bnhgqk,bnkhd->bnqhgd", p[..., C:], vb))
    return o.reshape(B, T, HKV * G * HEAD_DIM)


def rglru_direction(x, wr, br, wi, bi, lam, h0):
    B, T, _ = x.shape
    xb = x.reshape(B, T, LRU_BLOCKS, LRU_BW)
    r = jax.nn.sigmoid(jnp.einsum("btnc,ncd->btnd", xb, wr).reshape(B, T, LRU_W) + br)
    i = jax.nn.sigmoid(jnp.einsum("btnc,ncd->btnd", xb, wi).reshape(B, T, LRU_W) + bi)
    log_a = -LRU_C * r * jax.nn.softplus(-lam.astype(jnp.float32))
    a = jnp.exp(log_a)
    b = jnp.sqrt(-jnp.expm1(2.0 * log_a)) * (i * x)
    a_cum, h = lax.associative_scan(lambda e1, e2: (e1[0] * e2[0], e2[0] * e1[1] + e2[1]), (a, b), axis=1)
    h = h + a_cum * h0.astype(jnp.float32)[:, None, :]
    return h, h[:, -1]


def rglru_mixer(xr, gate, lp, st0):
    x = (centred_dwconv(xr, lp["lru_conv_w"]) + lp["lru_conv_b"]).astype(jnp.float32)
    hf, sf = rglru_direction(x, lp["lru_wr"][0], lp["lru_br"][0], lp["lru_wi"][0], lp["lru_bi"][0],
                             lp["lru_lam"][0], st0[:, 0])
    hb, sb = rglru_direction(jnp.flip(x, 1), lp["lru_wr"][1], lp["lru_br"][1], lp["lru_wi"][1],
                             lp["lru_bi"][1], lp["lru_lam"][1], st0[:, 1])
    y = (hf + jnp.flip(hb, 1)) * jax.nn.gelu(gate.astype(jnp.float32))
    return y.astype(xr.dtype), jnp.stack([sf, sb], axis=1)


def gdn_chunked(q, k, v, g, beta, s0):
    B, T, H, DK = q.shape
    DV = v.shape[-1]
    N = T // CHUNK

    def chunks(t):
        return jnp.moveaxis(t.reshape((B, N, CHUNK, H) + t.shape[3:]), 3, 1)

    q, k, v, g, beta = chunks(q), chunks(k), chunks(v), chunks(g), chunks(beta)
    gc = jnp.cumsum(g, axis=-1)
    idx = jnp.arange(CHUNK)
    incl = idx[:, None] >= idx[None, :]
    strict = idx[:, None] > idx[None, :]
    decay = jnp.exp(jnp.where(incl, gc[..., :, None] - gc[..., None, :], -jnp.inf))
    kk = jnp.einsum("bhnid,bhnjd->bhnij", k, k)
    lower = jnp.eye(CHUNK, dtype=jnp.float32) + jnp.where(strict, beta[..., None] * kk * decay, 0.0)
    rhs = jnp.concatenate([v * beta[..., None], k * (beta * jnp.exp(gc))[..., None]], axis=-1)
    sol = lax.linalg.triangular_solve(lower, rhs, left_side=True, lower=True)
    u, w = sol[..., :DV], sol[..., DV:]
    qk = jnp.einsum("bhnid,bhnjd->bhnij", q, k) * decay

    def step(S, xs):
        q_i, k_i, u_i, w_i, qk_i, gc_i = xs
        v_new = u_i - jnp.einsum("bhck,bhkv->bhcv", w_i, S)
        o = (jnp.einsum("bhck,bhkv->bhcv", q_i * jnp.exp(gc_i)[..., None], S)
             + jnp.einsum("bhij,bhjv->bhiv", qk_i, v_new))
        g_last = gc_i[..., -1:]
        S = (S * jnp.exp(g_last)[..., None]
             + jnp.einsum("bhck,bhcv->bhkv", k_i * jnp.exp(g_last - gc_i)[..., None], v_new))
        return S, o

    xs = tuple(jnp.moveaxis(t, 2, 0) for t in (q, k, u, w, qk, gc))
    S, o = lax.scan(step, s0.astype(jnp.float32), xs)
    o = jnp.moveaxis(jnp.moveaxis(o, 0, 2), 1, 3).reshape(B, T, H, DV)
    return o, S


def gdn_mixer(q, k, v, z, a, bb, lp, s0):
    B, T, _ = q.shape
    qkv = jax.nn.silu(centred_dwconv(jnp.concatenate([q, k, v], axis=-1), lp["gdn_conv_w"]).astype(jnp.float32))
    q, k, v = jnp.split(qkv, (GDN_H * GDN_DK, 2 * GDN_H * GDN_DK), axis=-1)
    q = l2_normalize(q.reshape(B, T, GDN_H, GDN_DK)) * (GDN_DK ** -0.5)
    k = l2_normalize(k.reshape(B, T, GDN_H, GDN_DK))
    v = v.reshape(B, T, GDN_H, GDN_DV)
    a = a.astype(jnp.float32).reshape(B, T, 2, GDN_H)
    g = -jnp.exp(lp["gdn_a_log"].astype(jnp.float32)) * jax.nn.softplus(a + lp["gdn_dt_bias"])
    beta = jax.nn.sigmoid(bb.astype(jnp.float32).reshape(B, T, 2, GDN_H))
    of, sf = gdn_chunked(q, k, v, g[:, :, 0], beta[:, :, 0], s0[:, 0])
    fl = lambda t: jnp.flip(t, 1)
    ob, sb = gdn_chunked(fl(q), fl(k), fl(v), fl(g[:, :, 1]), fl(beta[:, :, 1]), s0[:, 1])
    o = rms_norm(of + fl(ob), lp["gdn_norm_g"]) * jax.nn.silu(z.astype(jnp.float32).reshape(B, T, GDN_H, GDN_DV))
    return o.reshape(B, T, BRANCH_W).astype(z.dtype), jnp.stack([sf, sb], axis=1)


def token_mixing(h, lp, ctx, rope):
    B, T, _ = h.shape
    (aq, ak, av, lx, lg, gq, gk, gv, gz, ga, gb, dq, dk, dv) = jnp.split(h @ lp["w_in"], split_points(), axis=-1)
    aq, ak, av = attn_heads(aq, ak, av, lp["a_qn_g"], lp["a_kn_g"], A_HQ, A_HKV)
    dq, dk, dv = attn_heads(dq, dk, dv, lp["d_qn_g"], lp["d_kn_g"], D_HQ, D_HKV)
    sink = lp["a_sink"].reshape(A_HKV, A_HQ // A_HKV)
    if ctx is None:
        o_a = blocked_attention(aq, ak, av, sink)
        o_d = blocked_attention(dq, dk, dv, None)
        lru0 = jnp.zeros((B, 2, LRU_W), jnp.float32)
        gdn0 = jnp.zeros((B, 2, GDN_H, GDN_DK, GDN_DV), jnp.float32)
    else:
        cak, cav, cdk, cdv, lru0, gdn0 = ctx
        cos, sin = rope
        o_a = banded_attention(apply_rope(aq, cos, sin), apply_rope(ak, cos, sin), av, cak, cav, sink)
        o_d = blocked_attention(apply_rope(dq, cos, sin),
                                jnp.concatenate([cdk, apply_rope(dk, cos, sin)], axis=1),
                                jnp.concatenate([cdv, dv], axis=1), None)
    o_b, lru_s = rglru_mixer(lx, lg, lp, lru0)
    o_c, gdn_s = gdn_mixer(gq, gk, gv, gz, ga, gb, lp, gdn0)
    branches = jnp.stack([o_a, o_b, o_c, o_d], axis=2)
    proj = jnp.einsum("btmc,mcd->btmd", branches, lp["w_branch"])
    gates = jax.nn.sigmoid((h @ lp["w_merge"] + lp["b_merge"]).reshape(B, T, N_BRANCH, D_MODEL))
    y = jnp.sum(gates * proj, axis=2) @ lp["w_out"]
    return y, (ak, av, dk, dv, lru_s.astype(h.dtype), gdn_s.astype(h.dtype))


def trunk_layer(x, cond, lp, ctx, rope):
    mod = jax.nn.silu(cond) @ lp["mod_w"] + lp["mod_b"]
    sh1, sc1, g1, sh2, sc2, g2 = jnp.split(mod, 6, axis=-1)
    y, new_ctx = token_mixing(modulate(x, lp["norm1_g"], sh1, sc1), lp, ctx, rope)
    x = x + g1[:, None, :] * y
    h = modulate(x, lp["norm2_g"], sh2, sc2)
    x = x + g2[:, None, :] * (jnp.square(jax.nn.relu(h @ lp["mlp_w1"])) @ lp["mlp_w2"])
    return x, new_ctx


def setup_inputs(seed: int = 0) -> dict:
    key = jax.random.key(seed)
    ks = iter(jax.random.split(key, 64))
    nrm = lambda shape, s=1.0: jax.random.normal(next(ks), shape, jnp.float32) * s
    gain = lambda shape: 1.0 + nrm(shape, 0.05)
    u = jax.random.uniform(next(ks), (DEPTH, 2, LRU_W), jnp.float32, 0.9, 0.999)
    dt = jnp.exp(jax.random.uniform(next(ks), (DEPTH, 2, GDN_H), jnp.float32, np.log(1e-3), np.log(1e-1)))
    return {
        "x_prompt": nrm((BATCH, SEQ, D_MODEL)),
        "x_sample": nrm((DEC_BATCH, DEC_SEQ, D_MODEL)),
        "c": nrm((DEC_BATCH, D_MODEL)),
        "cache_a_k": nrm((DEC_BATCH, DEPTH, PAST_LEN, A_HKV, HEAD_DIM)),
        "cache_a_v": nrm((DEC_BATCH, DEPTH, PAST_LEN, A_HKV, HEAD_DIM)),
        "cache_d_k": nrm((DEC_BATCH, DEPTH, PAST_LEN, D_HKV, HEAD_DIM)),
        "cache_d_v": nrm((DEC_BATCH, DEPTH, PAST_LEN, D_HKV, HEAD_DIM)),
        "state_lru": nrm((DEC_BATCH, DEPTH, 2, LRU_W), 0.5),
        "state_gdn": nrm((DEC_BATCH, DEPTH, 2, GDN_H, GDN_DK, GDN_DV), 0.3),
        "c_ctx": nrm((D_MODEL,)),
        "mod_w": nrm((DEPTH, D_MODEL, 6 * D_MODEL), 0.5 * D_MODEL ** -0.5),
        "mod_b": nrm((DEPTH, 6 * D_MODEL), 0.01),
        "norm1_g": gain((DEPTH, D_MODEL)),
        "norm2_g": gain((DEPTH, D_MODEL)),
        "w_in": nrm((DEPTH, D_MODEL, IN_COLS), D_MODEL ** -0.5),
        "a_qn_g": gain((DEPTH, HEAD_DIM)),
        "a_kn_g": gain((DEPTH, HEAD_DIM)),
        "a_sink": nrm((DEPTH, A_HQ)),
        "lru_conv_w": nrm((DEPTH, CONV_W, LRU_W), CONV_W ** -0.5),
        "lru_conv_b": nrm((DEPTH, LRU_W), 0.01),
        "lru_wr": nrm((DEPTH, 2, LRU_BLOCKS, LRU_BW, LRU_BW), LRU_BW ** -0.5),
        "lru_br": nrm((DEPTH, 2, LRU_W), 0.01),
        "lru_wi": nrm((DEPTH, 2, LRU_BLOCKS, LRU_BW, LRU_BW), LRU_BW ** -0.5),
        "lru_bi": nrm((DEPTH, 2, LRU_W), 0.01),
        "lru_lam": jnp.log(u) - jnp.log1p(-u),
        "gdn_conv_w": nrm((DEPTH, CONV_W, 2 * GDN_H * GDN_DK + GDN_H * GDN_DV), CONV_W ** -0.5),
        "gdn_a_log": jnp.log(jax.random.uniform(next(ks), (DEPTH, 2, GDN_H), jnp.float32, 1.0, 16.0)),
        "gdn_dt_bias": dt + jnp.log(-jnp.expm1(-dt)),
        "gdn_norm_g": gain((DEPTH, GDN_DV)),
        "d_qn_g": gain((DEPTH, HEAD_DIM)),
        "d_kn_g": gain((DEPTH, HEAD_DIM)),
        "w_branch": nrm((DEPTH, N_BRANCH, BRANCH_W, D_MODEL), BRANCH_W ** -0.5),
        "w_merge": nrm((DEPTH, D_MODEL, N_BRANCH * D_MODEL), D_MODEL ** -0.5),
        "b_merge": nrm((DEPTH, N_BRANCH * D_MODEL), 0.01),
        "w_out": nrm((DEPTH, D_MODEL, D_MODEL), D_MODEL ** -0.5),
        "mlp_w1": nrm((DEPTH, D_MODEL, D_FF), D_MODEL ** -0.5),
        "mlp_w2": nrm((DEPTH, D_FF, D_MODEL), D_FF ** -0.5),
    }


def reference(x_prompt, x_sample, c, cache_a_k, cache_a_v, cache_d_k, cache_d_v, state_lru, state_gdn,
              c_ctx, mod_w, mod_b, norm1_g, norm2_g, w_in, a_qn_g, a_kn_g, a_sink, lru_conv_w, lru_conv_b,
              lru_wr, lru_br, lru_wi, lru_bi, lru_lam, gdn_conv_w, gdn_a_log, gdn_dt_bias, gdn_norm_g,
              d_qn_g, d_kn_g, w_branch, w_merge, b_merge, w_out, mlp_w1, mlp_w2):
    def layer_params(l):
        return dict(mod_w=mod_w[l], mod_b=mod_b[l], norm1_g=norm1_g[l], norm2_g=norm2_g[l], w_in=w_in[l],
                    a_qn_g=a_qn_g[l], a_kn_g=a_kn_g[l], a_sink=a_sink[l], lru_conv_w=lru_conv_w[l],
                    lru_conv_b=lru_conv_b[l], lru_wr=lru_wr[l], lru_br=lru_br[l], lru_wi=lru_wi[l],
                    lru_bi=lru_bi[l], lru_lam=lru_lam[l], gdn_conv_w=gdn_conv_w[l], gdn_a_log=gdn_a_log[l],
                    gdn_dt_bias=gdn_dt_bias[l], gdn_norm_g=gdn_norm_g[l], d_qn_g=d_qn_g[l], d_kn_g=d_kn_g[l],
                    w_branch=w_branch[l], w_merge=w_merge[l], b_merge=b_merge[l], w_out=w_out[l],
                    mlp_w1=mlp_w1[l], mlp_w2=mlp_w2[l])

    xp = x_prompt
    ctx_tensors = []
    for l in range(DEPTH):
        xp, ctx_l = trunk_layer(xp, c_ctx[None, :], layer_params(l), None, None)
        ctx_tensors.append(ctx_l)
    new_a_k, new_a_v, new_d_k, new_d_v, new_lru, new_gdn = [jnp.stack(t, axis=1) for t in zip(*ctx_tensors)]

    rope = axial_rope(x_sample.shape[1])
    xs = x_sample
    for l in range(DEPTH):
        ctx_l = (cache_a_k[:, l], cache_a_v[:, l], cache_d_k[:, l], cache_d_v[:, l], state_lru[:, l], state_gdn[:, l])
        xs, _ = trunk_layer(xs, c, layer_params(l), ctx_l, rope)

    return (xp, xs, new_a_k, new_a_v, new_d_k, new_d_v, new_lru, new_gdn)
```

```python
import functools

import numpy as np
import jax
import jax.numpy as jnp
from jax import lax
from jax.experimental import pallas as pl
from jax.experimental.pallas import tpu as pltpu

F32 = jnp.float32
BF16 = jnp.bfloat16
HIGHEST = lax.Precision.HIGHEST

D_MODEL = 1024
HEAD_DIM = 64
BRANCH_W = 256
N_BRANCH = 4
GRID_W = 64
WINDOW = 128
LRU_C = 8.0
CONV_W = 4
CONV_LEFT = 2
GDN_H = 4
GDN_DK = 64
CHUNK = 64
D_FF = 4 * D_MODEL
ROPE_BASE = 10000.0
EPS = 1e-6
NEG = -1e30
GDN_PAD = 112
IN_COLS_PACKED = 2688
HALO = 8

V7X_VMEM_LIMIT = 56 * 1024 * 1024


def _cparams(sem, vmem=V7X_VMEM_LIMIT):
    return pltpu.CompilerParams(dimension_semantics=sem, vmem_limit_bytes=vmem)


def _bdot(a, b):
    return jnp.dot(a.astype(BF16), b.astype(BF16), preferred_element_type=F32)


def _split(x):
    hi = x.astype(BF16)
    lo = (x - hi.astype(F32)).astype(BF16)
    return hi, lo


def _dot3(a, b):
    ah, al = _split(a)
    bh, bl = _split(b)
    return (jnp.dot(ah, bh, preferred_element_type=F32)
            + jnp.dot(al, bh, preferred_element_type=F32)
            + jnp.dot(ah, bl, preferred_element_type=F32))


def _group_sum(x, ones_bd):
    hi, lo = _split(x)
    return (jnp.dot(hi, ones_bd, preferred_element_type=F32)
            + jnp.dot(lo, ones_bd, preferred_element_type=F32))


def _sigmoid(x):
    return 1.0 / (1.0 + jnp.exp(-x))


def _silu(x):
    return x * _sigmoid(x)


def _softplus(x):
    return jnp.maximum(x, 0.0) + jnp.log1p(jnp.exp(-jnp.abs(x)))


def _modulate(x, g, shift, scale):
    ms = jnp.mean(x * x, axis=-1, keepdims=True)
    return (x * lax.rsqrt(ms + EPS) * g) * (1.0 + scale) + shift


def _lane_iota(shape):
    return lax.broadcasted_iota(jnp.int32, shape, len(shape) - 1)


def _row_iota(shape):
    return lax.broadcasted_iota(jnp.int32, shape, len(shape) - 2)


def _mod_kernel(cond_ref, w_ref, b_ref, o_ref):
    c = cond_ref[...]
    o_ref[0] = jnp.dot(_silu(c), w_ref[0], precision=HIGHEST, preferred_element_type=F32) + b_ref[0]


def _mod_call(cond8, mod_w, mod_b):
    depth, d, n = mod_w.shape
    tn = 1536
    return pl.pallas_call(
        _mod_kernel,
        grid=(depth, n // tn),
        in_specs=[
            pl.BlockSpec((8, d), lambda l, j: (0, 0)),
            pl.BlockSpec((1, d, tn), lambda l, j: (l, 0, j)),
            pl.BlockSpec((1, 1, tn), lambda l, j: (l, 0, j)),
        ],
        out_specs=pl.BlockSpec((1, 8, tn), lambda l, j: (l, 0, j)),
        out_shape=jax.ShapeDtypeStruct((depth, 8, n), F32),
        compiler_params=_cparams(("parallel", "parallel")),
        name="mod_vectors",
    )(cond8, mod_w, mod_b.reshape(depth, 1, n))


def _rope(x, cos, sin):
    outs = []
    for j in range(x.shape[1] // 128):
        xb = x[:, 128 * j:128 * (j + 1)]
        lane = _lane_iota(xb.shape)
        sw = jnp.where((lane & 16) == 0, pltpu.roll(xb, 112, 1), pltpu.roll(xb, 16, 1))
        outs.append(xb * cos + sw * sin)
    return outs[0] if len(outs) == 1 else jnp.concatenate(outs, axis=1)


def _head_rms(x, gain, ones_bd):
    ms = _group_sum(x * x, ones_bd) * (1.0 / HEAD_DIM)
    return x * lax.rsqrt(ms + EPS) * gain


def _centred_conv(g, w, tm):
    rows = g.shape[0]
    acc = None
    for j in range(CONV_W):
        sh = (CONV_LEFT - j) % rows
        gj = g if sh == 0 else pltpu.roll(g, sh, 0)
        term = gj[HALO:HALO + tm] * w[j:j + 1, :]
        acc = term if acc is None else acc + term
    return acc


def _inproj_kernel(modrow_ref, ropeblk_ref, isstart_ref, isend_ref,
                   xp_ref, x_ref, xn_ref, mod_ref, g1_ref, w_ref, cos_ref, sin_ref,
                   aqg_ref, akg_ref, dqg_ref, dkg_ref, lcw_ref, lcb_ref, gcw_ref,
                   alog_ref, dtb_ref, ones_ref,
                   qa_ref, ka_ref, va_ref, lx_ref, lg_ref, gq_ref, gk_ref, gv_ref, gz_ref, ggb_ref,
                   qd_ref, kd_ref, vd_ref, *, tm):
    i = pl.program_id(0)
    xfull = jnp.concatenate([xp_ref[...], x_ref[...], xn_ref[...]], axis=0)
    h = _modulate(xfull, g1_ref[...], mod_ref[0, 0:1, :], mod_ref[0, 1:2, :])
    p = jnp.dot(h.astype(BF16), w_ref[...], preferred_element_type=F32)

    ones_bd = ones_ref[...]
    cos = cos_ref[...]
    sin = sin_ref[...]
    lo, hi = HALO, HALO + tm

    qa_ref[...] = _rope(_head_rms(p[lo:hi, 0:256], aqg_ref[...], ones_bd), cos, sin)
    ka_ref[...] = _rope(_head_rms(p[lo:hi, 256:384], akg_ref[...], ones_bd[:128, :128]), cos, sin)
    va_ref[...] = p[lo:hi, 384:512]
    qd_ref[...] = _rope(_head_rms(p[lo:hi, 2176:2432], dqg_ref[...], ones_bd), cos, sin)
    kd_ref[...] = _rope(_head_rms(p[lo:hi, 2432:2560], dkg_ref[...], ones_bd[:128, :128]), cos, sin)
    vd_ref[...] = p[lo:hi, 2560:2688]

    row = _row_iota((tm + 2 * HALO, 1))
    keep = jnp.logical_and(jnp.logical_or(row >= HALO, isstart_ref[i] == 0),
                           jnp.logical_or(row < HALO + tm, isend_ref[i] == 0))

    lxg = jnp.where(keep, p[:, 512:768], 0.0)
    lx_ref[...] = _centred_conv(lxg, lcw_ref[...], tm) + lcb_ref[...]
    lg_ref[...] = p[lo:hi, 768:1024]

    qkv = jnp.where(keep, p[:, 1024:1792], 0.0)
    qkv = _silu(_centred_conv(qkv, gcw_ref[...], tm))
    gq = qkv[:, 0:256]
    gk = qkv[:, 256:512]
    gq_ref[...] = gq * lax.rsqrt(_group_sum(gq * gq, ones_bd) + EPS) * (GDN_DK ** -0.5)
    gk_ref[...] = gk * lax.rsqrt(_group_sum(gk * gk, ones_bd) + EPS)
    gv_ref[...] = qkv[:, 512:768]
    gz_ref[...] = p[lo:hi, 1792:2048]
    ab = p[lo:hi, 2048:2176]
    g = -jnp.exp(alog_ref[...]) * _softplus(ab + dtb_ref[...])
    lane = _lane_iota(ab.shape)
    ggb_ref[...] = jnp.where(lane < 2 * GDN_H, g, _sigmoid(ab))


def _inproj_call(x, meta, mod_l, g1, w_packed, cos_t, sin_t, vecs, ones_bd, *, tm):
    n, d = x.shape
    nt = n // tm
    hb = tm // HALO
    last_hb = n // HALO - 1
    row_tile = lambda w: pl.BlockSpec((tm, w), lambda i, *_: (i, 0))
    const = lambda shape: pl.BlockSpec(shape, lambda i, *_: tuple(0 for _ in shape))
    in_specs = [
        pl.BlockSpec((HALO, d), lambda i, *_: (jnp.maximum(i * hb - 1, 0), 0)),
        row_tile(d),
        pl.BlockSpec((HALO, d), lambda i, *_: (jnp.minimum((i + 1) * hb, last_hb), 0)),
        pl.BlockSpec((1, 6, d), lambda i, modrow, *_: (modrow[i], 0, 0)),
        const((1, d)),
        const((d, IN_COLS_PACKED)),
        pl.BlockSpec((tm, 128), lambda i, modrow, ropeblk, *_: (ropeblk[i], 0)),
        pl.BlockSpec((tm, 128), lambda i, modrow, ropeblk, *_: (ropeblk[i], 0)),
    ] + [const(v.shape) for v in vecs] + [const((256, 256))]
    widths = (256, 128, 128, 256, 256, 256, 256, 256, 256, 128, 256, 128, 128)
    grid_spec = pltpu.PrefetchScalarGridSpec(
        num_scalar_prefetch=4,
        grid=(nt,),
        in_specs=in_specs,
        out_specs=[row_tile(w) for w in widths],
    )
    return pl.pallas_call(
        functools.partial(_inproj_kernel, tm=tm),
        grid_spec=grid_spec,
        out_shape=[jax.ShapeDtypeStruct((n, w), F32) for w in widths],
        compiler_params=_cparams(("parallel",)),
        name="in_projection",
    )(*meta, x, x, x, mod_l, g1, w_packed, cos_t, sin_t, *vecs, ones_bd)


def _head_q(q, j, g):
    qj = q[:, 128 * j:128 * (j + 1)]
    lane = _lane_iota(qj.shape)
    sel = (lane < HEAD_DIM) if g == 0 else (lane >= HEAD_DIM)
    return jnp.where(sel, qj, 0.0).astype(BF16)


def _place_heads(res, j):
    r0 = res[0] if j == 0 else pltpu.roll(res[0], HEAD_DIM, 1)
    r1 = res[1] if j == 1 else pltpu.roll(res[1], HEAD_DIM, 1)
    lane = _lane_iota(r0.shape)
    return jnp.where(lane < HEAD_DIM, r0, r1)


def _attend(q, ks, vs, masks, sinks):
    ks_sw = [pltpu.roll(k, HEAD_DIM, 1) for k in ks]
    blocks = []
    for j in range(2):
        res = []
        for g in range(2):
            qm = _head_q(q, j, g)
            ss = []
            for k, ksw, mk in zip(ks, ks_sw, masks):
                kk = k if g == j else ksw
                s = lax.dot_general(qm, kk, (((1,), (1,)), ((), ())), preferred_element_type=F32)
                if mk is not None:
                    s = jnp.where(mk, s, NEG)
                ss.append(s)
            m = ss[0].max(axis=-1, keepdims=True)
            for s in ss[1:]:
                m = jnp.maximum(m, s.max(axis=-1, keepdims=True))
            if sinks is not None:
                m = jnp.maximum(m, sinks[2 * j + g])
            den = None
            acc = None
            for s, v in zip(ss, vs):
                e = jnp.exp(s - m)
                dsum = e.sum(axis=-1, keepdims=True)
                den = dsum if den is None else den + dsum
                pv = jnp.dot(e.astype(BF16), v, preferred_element_type=F32)
                acc = pv if acc is None else acc + pv
            if sinks is not None:
                den = den + jnp.exp(sinks[2 * j + g] - m)
            res.append(acc / den)
        blocks.append(_place_heads(res, j))
    return jnp.concatenate(blocks, axis=1)


def _bf(x):
    return x.astype(BF16)


def _attn_ctx_kernel(sink_ref, qa_ref, ka_ref, va_ref, qd_ref, kd_ref, vd_ref, oa_ref, od_ref):
    sinks = [sink_ref[t] for t in range(4)]
    oa_ref[...] = _attend(qa_ref[...], [_bf(ka_ref[...])], [_bf(va_ref[...])], [None], sinks)
    od_ref[...] = _attend(qd_ref[...], [_bf(kd_ref[...])], [_bf(vd_ref[...])], [None], None)


def _attn_ctx_call(sink, qa, ka, va, qd, kd, vd, *, n_seq, t):
    blk = lambda w: pl.BlockSpec((t, w), lambda b, *_: (b, 0))
    grid_spec = pltpu.PrefetchScalarGridSpec(
        num_scalar_prefetch=1, grid=(n_seq,),
        in_specs=[blk(256), blk(128), blk(128), blk(256), blk(128), blk(128)],
        out_specs=[blk(256), blk(256)])
    return pl.pallas_call(
        _attn_ctx_kernel, grid_spec=grid_spec,
        out_shape=[jax.ShapeDtypeStruct((n_seq * t, 256), F32)] * 2,
        compiler_params=_cparams(("parallel",)),
        name="attn_context",
    )(sink, qa, ka, va, qd, kd, vd)


def _attn_band_kernel(sink_ref, q_ref, k_ref, v_ref, ck_ref, cv_ref, o_ref, *, tq, t):
    i = pl.program_id(1)
    start = pl.multiple_of(i * tq, tq)
    prev = pl.multiple_of(jnp.maximum(start - WINDOW, 0), WINDOW)
    nxt = pl.multiple_of(jnp.minimum(start + tq, t - WINDOW), WINDOW)
    k_loc = jnp.concatenate([k_ref[pl.ds(prev, WINDOW), :], k_ref[pl.ds(start, tq), :],
                             k_ref[pl.ds(nxt, WINDOW), :]], axis=0)
    v_loc = jnp.concatenate([v_ref[pl.ds(prev, WINDOW), :], v_ref[pl.ds(start, tq), :],
                             v_ref[pl.ds(nxt, WINDOW), :]], axis=0)
    shape = (tq, tq + 2 * WINDOW)
    qpos = start + _row_iota(shape)
    kpos = start - WINDOW + _lane_iota(shape)
    mask = (kpos >= 0) & (kpos < t) & (jnp.abs(qpos - kpos) <= WINDOW)
    sinks = [sink_ref[u] for u in range(4)]
    o_ref[...] = _attend(q_ref[...], [_bf(ck_ref[0]), _bf(k_loc)], [_bf(cv_ref[0]), _bf(v_loc)],
                         [None, mask], sinks)


def _attn_band_call(sink, q, k, v, ck, cv, *, n_seq, t, row0, tq):
    nq = t // tq
    qb0 = row0 // tq
    sb0 = row0 // t
    past = ck.shape[1]
    grid_spec = pltpu.PrefetchScalarGridSpec(
        num_scalar_prefetch=1, grid=(n_seq, nq),
        in_specs=[
            pl.BlockSpec((tq, 256), lambda b, i, *_: (qb0 + b * nq + i, 0)),
            pl.BlockSpec((t, 128), lambda b, i, *_: (sb0 + b, 0)),
            pl.BlockSpec((t, 128), lambda b, i, *_: (sb0 + b, 0)),
            pl.BlockSpec((1, past, 128), lambda b, i, *_: (b, 0, 0)),
            pl.BlockSpec((1, past, 128), lambda b, i, *_: (b, 0, 0)),
        ],
        out_specs=pl.BlockSpec((tq, 256), lambda b, i, *_: (b * nq + i, 0)))
    return pl.pallas_call(
        functools.partial(_attn_band_kernel, tq=tq, t=t), grid_spec=grid_spec,
        out_shape=jax.ShapeDtypeStruct((n_seq * t, 256), F32),
        compiler_params=_cparams(("parallel", "parallel")),
        name="attn_banded",
    )(sink, q, k, v, ck, cv)


def _attn_full_kernel(q_ref, k_ref, v_ref, ck_ref, cv_ref, o_ref):
    o_ref[...] = _attend(q_ref[...], [_bf(ck_ref[0]), _bf(k_ref[...])], [_bf(cv_ref[0]), _bf(v_ref[...])],
                         [None, None], None)


def _attn_full_call(q, k, v, ck, cv, *, n_seq, t, row0, tq):
    nq = t // tq
    qb0 = row0 // tq
    sb0 = row0 // t
    past = ck.shape[1]
    return pl.pallas_call(
        _attn_full_kernel, grid=(n_seq, nq),
        in_specs=[
            pl.BlockSpec((tq, 256), lambda b, i: (qb0 + b * nq + i, 0)),
            pl.BlockSpec((t, 128), lambda b, i: (sb0 + b, 0)),
            pl.BlockSpec((t, 128), lambda b, i: (sb0 + b, 0)),
            pl.BlockSpec((1, past, 128), lambda b, i: (b, 0, 0)),
            pl.BlockSpec((1, past, 128), lambda b, i: (b, 0, 0)),
        ],
        out_specs=pl.BlockSpec((tq, 256), lambda b, i: (b * nq + i, 0)),
        out_shape=jax.ShapeDtypeStruct((n_seq * t, 256), F32),
        compiler_params=_cparams(("parallel", "parallel")),
        name="attn_full",
    )(q, k, v, ck, cv)


def _gelu_tanh(x):
    return 0.5 * x * (1.0 + jnp.tanh(0.7978845608028654 * (x + 0.044715 * (x * x * x))))


def _lru_kernel(lx_ref, lg_ref, w_ref, b_ref, lam_ref, h0_ref, y_ref, st_ref, hf_ref, a_ref, bb_ref, hb_ref,
                *, t, chunk):
    nc = t // chunk
    ng = chunk // 8
    sp = _softplus(-lam_ref[...])
    row8 = _row_iota((8, BRANCH_W))

    def gates(xc, d):
        pre = _bdot(xc, w_ref[:, 512 * d:512 * (d + 1)]) + b_ref[:, 512 * d:512 * (d + 1)]
        r = _sigmoid(pre[:, 0:256])
        ig = _sigmoid(pre[:, 256:512])
        log_a = (-LRU_C) * r * sp[d:d + 1, :]
        a = jnp.exp(log_a)
        a_ref[...] = a
        bb_ref[...] = jnp.sqrt(1.0 - a * a) * (ig * xc)

    def local_scan(a8, b8, reverse):
        for s in (1, 2, 4):
            sh = (8 - s) if reverse else s
            m = (row8 < 8 - s) if reverse else (row8 >= s)
            a_sh = pltpu.roll(a8, sh, 0)
            b_sh = pltpu.roll(b8, sh, 0)
            b8 = jnp.where(m, a8 * b_sh + b8, b8)
            a8 = jnp.where(m, a8 * a_sh, a8)
        return a8, b8

    def run(d, reverse, h_init, out_ref, out_base):
        def chunk_body(cc, h):
            c = (nc - 1 - cc) if reverse else cc
            r0 = pl.multiple_of(c * chunk, chunk)
            gates(lx_ref[pl.ds(r0, chunk), :], d)

            def grp(gg, hc):
                gi = (ng - 1 - gg) if reverse else gg
                g0 = pl.multiple_of(gi * 8, 8)
                a8, b8 = local_scan(a_ref[pl.ds(g0, 8), :], bb_ref[pl.ds(g0, 8), :], reverse)
                h8 = b8 + a8 * hc
                out_ref[pl.ds(out_base(r0) + g0, 8), :] = h8
                last = h8[0:1, :] if reverse else h8[7:8, :]
                return jnp.broadcast_to(last, (8, BRANCH_W))

            h = lax.fori_loop(0, ng, grp, h, unroll=4)
            if reverse:
                rows = pl.ds(r0, chunk)
                y_ref[rows, :] = (hf_ref[rows, :] + hb_ref[...]) * _gelu_tanh(lg_ref[rows, :])
            return h

        return lax.fori_loop(0, nc, chunk_body, jnp.broadcast_to(h_init, (8, BRANCH_W)))

    hf = run(0, False, h0_ref[0, 0:1, :], hf_ref, lambda r0: r0)
    hb = run(1, True, h0_ref[0, 1:2, :], hb_ref, lambda r0: 0)
    st_ref[0, 0:1, :] = hf[0:1, :]
    st_ref[0, 1:2, :] = hb[0:1, :]


def _lru_call(lx, lg, w, b, lam, h0, *, n_seq, t, row0):
    sb0 = row0 // t
    chunk = min(t, 256)
    seq = lambda: pl.BlockSpec((t, BRANCH_W), lambda s: (sb0 + s, 0))
    const = lambda shape: pl.BlockSpec(shape, lambda s: tuple(0 for _ in shape))
    return pl.pallas_call(
        functools.partial(_lru_kernel, t=t, chunk=chunk),
        grid=(n_seq,),
        in_specs=[seq(), seq(), const(w.shape), const(b.shape), const(lam.shape),
                  pl.BlockSpec((1, 2, BRANCH_W), lambda s: (s, 0, 0))],
        out_specs=[pl.BlockSpec((t, BRANCH_W), lambda s: (s, 0)),
                   pl.BlockSpec((1, 2, BRANCH_W), lambda s: (s, 0, 0))],
        out_shape=[jax.ShapeDtypeStruct((n_seq * t, BRANCH_W), F32),
                   jax.ShapeDtypeStruct((n_seq, 2, BRANCH_W), F32)],
        scratch_shapes=[pltpu.VMEM((t, BRANCH_W), F32), pltpu.VMEM((chunk, BRANCH_W), F32),
                        pltpu.VMEM((chunk, BRANCH_W), F32), pltpu.VMEM((chunk, BRANCH_W), F32)],
        compiler_params=_cparams(("parallel",)),
        name="rglru",
    )(lx, lg, w, b, lam, h0)


def _gdn_kernel(q_ref, k_ref, v_ref, ggb_ref, s0_ref, o_ref, sT_ref, s_ref, *, d, reverse, n_chunk, n_tile):
    ti = pl.program_id(1)
    w4 = GDN_H * GDN_DK
    blockmask = (_row_iota((w4, w4)) // CHUNK) == (_lane_iota((w4, w4)) // CHUNK)

    def expand_rows(y):
        yt = jnp.concatenate([y] * GDN_H, axis=0)
        if y.shape[1] == w4:
            return jnp.where(blockmask, yt, 0.0)
        return jnp.concatenate([jnp.where(blockmask, yt[:, w4 * u:w4 * (u + 1)], 0.0)
                                for u in range(y.shape[1] // w4)], axis=1)

    @pl.when(ti == 0)
    def _():
        s_ref[...] = expand_rows(s0_ref[0, 0])

    ri = _row_iota((CHUNK, w4))
    cj = _lane_iota((CHUNK, w4)) % CHUNK
    incl = (cj >= ri) if reverse else (cj <= ri)
    strict = (cj > ri) if reverse else (cj < ri)
    eye_sbs = (cj == ri)
    er = _row_iota((128, w4))
    ehead = _lane_iota((128, w4)) // CHUNK
    e_g = (er == d * GDN_H + ehead).astype(F32)
    e_b = (er == 2 * GDN_H + d * GDN_H + ehead).astype(F32)
    tri_r = _row_iota((CHUNK, CHUNK))
    tri_c = _lane_iota((CHUNK, CHUNK))
    tri = ((tri_c >= tri_r) if reverse else (tri_c <= tri_r)).astype(F32)
    ones64 = jnp.ones((CHUNK, CHUNK), F32)
    rep = (_row_iota((CHUNK, w4)) == (_lane_iota((CHUNK, w4)) % CHUNK)).astype(BF16)

    for cc in range(n_chunk):
        c = (n_chunk - 1 - cc) if reverse else cc
        rows = slice(c * CHUNK, (c + 1) * CHUNK)
        q = q_ref[rows, :]
        k = k_ref[rows, :]
        v = v_ref[rows, :]
        gb = ggb_ref[rows, :]
        g_exp = jnp.dot(gb, e_g, precision=HIGHEST, preferred_element_type=F32)
        beta = jnp.dot(gb, e_b, precision=HIGHEST, preferred_element_type=F32)
        gc = jnp.dot(tri, g_exp, precision=HIGHEST, preferred_element_type=F32)
        gct = jnp.dot(ones64, jnp.where(eye_sbs, gc, 0.0), precision=HIGHEST, preferred_element_type=F32)
        decay = jnp.where(incl, jnp.exp(jnp.where(incl, gc - gct, 0.0)), 0.0)
        kb = k.astype(BF16)
        w_k = jnp.where(blockmask,
                        lax.dot_general(kb, rep, (((0,), (0,)), ((), ())), preferred_element_type=F32),
                        0.0).astype(BF16)
        kk = jnp.dot(kb, w_k, preferred_element_type=F32)
        qk = jnp.dot(q.astype(BF16), w_k, preferred_element_type=F32) * decay
        a = jnp.where(strict, beta * kk * decay, 0.0)

        tm = jnp.where(eye_sbs, 1.0, 0.0) - a
        pw = a
        for _ in range(5):
            pw = _dot3(pw, expand_rows(pw))
            tm = tm + _dot3(pw, expand_rows(tm))
        egc = jnp.exp(gc)
        rhs = jnp.concatenate([v * beta, k * (beta * egc)], axis=1)
        sol = _dot3(tm, expand_rows(rhs))
        u = sol[:, 0:w4]
        w = sol[:, w4:2 * w4]

        s = s_ref[...]
        sb = s.astype(BF16)
        v_new = u - jnp.dot(w.astype(BF16), sb, preferred_element_type=F32)
        o = (jnp.dot((q * egc).astype(BF16), sb, preferred_element_type=F32)
             + jnp.dot(qk.astype(BF16), expand_rows(v_new).astype(BF16), preferred_element_type=F32))
        o_ref[0, rows, :] = o
        g_last = gc[0:1, :] if reverse else gc[CHUNK - 1:CHUNK, :]
        kdec = k * jnp.exp(g_last - gc)
        upd = lax.dot_general(kdec.astype(BF16), v_new.astype(BF16), (((0,), (0,)), ((), ())),
                              preferred_element_type=F32)
        s_ref[...] = s * jnp.exp(g_last) + jnp.where(blockmask, upd, 0.0)

    @pl.when(ti == n_tile - 1)
    def _():
        s = s_ref[...]
        sT_ref[0, 0] = s[0:64] + s[64:128] + s[128:192] + s[192:256]


def _gdn_call(q, k, v, ggb, s0, *, d, n_seq, t, row0, tt):
    reverse = d == 1
    n_tile = t // tt
    b0 = row0 // tt
    tidx = (lambda i: n_tile - 1 - i) if reverse else (lambda i: i)
    blk = lambda w: pl.BlockSpec((tt, w), lambda s, i: (b0 + s * n_tile + tidx(i), 0))
    return pl.pallas_call(
        functools.partial(_gdn_kernel, d=d, reverse=reverse, n_chunk=tt // CHUNK, n_tile=n_tile),
        grid=(n_seq, n_tile),
        in_specs=[blk(256), blk(256), blk(256), blk(128),
                  pl.BlockSpec((1, 1, CHUNK, 256), lambda s, i: (s, d, 0, 0))],
        out_specs=[pl.BlockSpec((1, tt, 256), lambda s, i: (0, s * n_tile + tidx(i), 0)),
                   pl.BlockSpec((1, 1, CHUNK, 256), lambda s, i: (s, 0, 0, 0))],
        out_shape=[jax.ShapeDtypeStruct((1, n_seq * t, 256), F32),
                   jax.ShapeDtypeStruct((n_seq, 1, CHUNK, 256), F32)],
        scratch_shapes=[pltpu.VMEM((256, 256), F32)],
        compiler_params=_cparams(("parallel", "arbitrary")),
        name="gdn_bwd" if reverse else "gdn_fwd",
    )(q, k, v, ggb, s0)


def _merge_kernel(modrow_ref, x_ref, mod_ref, g1_ref, oa_ref, ob_ref, ocf_ref, ocb_ref, gz_ref, od_ref,
                  gng_ref, ones_ref, wm_ref, bm_ref, wb_ref, wo_ref, y_ref):
    x = x_ref[...]
    h = _modulate(x, g1_ref[...], mod_ref[0, 0:1, :], mod_ref[0, 1:2, :]).astype(BF16)
    oc = ocf_ref[...] + ocb_ref[...]
    ms = _group_sum(oc * oc, ones_ref[...]) * (1.0 / GDN_DK)
    oc = (oc * lax.rsqrt(ms + EPS) * gng_ref[...]) * _silu(gz_ref[...])
    branches = (oa_ref[...], ob_ref[...], oc, od_ref[...])
    acc = None
    for m in range(N_BRANCH):
        cols = slice(D_MODEL * m, D_MODEL * (m + 1))
        gate = _sigmoid(jnp.dot(h, wm_ref[:, cols], preferred_element_type=F32) + bm_ref[:, cols])
        term = gate * jnp.dot(branches[m].astype(BF16), wb_ref[m], preferred_element_type=F32)
        acc = term if acc is None else acc + term
    y = jnp.dot(acc.astype(BF16), wo_ref[...], preferred_element_type=F32)
    y_ref[...] = x + mod_ref[0, 2:3, :] * y


def _merge_call(modrow, x, mod_l, g1, oa, ob, ocf, ocb, gz, od, gng, ones_bd, wm, bm, wb, wo, *, tm):
    n, d = x.shape
    row_tile = lambda w: pl.BlockSpec((tm, w), lambda i, *_: (i, 0))
    const = lambda shape: pl.BlockSpec(shape, lambda i, *_: tuple(0 for _ in shape))
    grid_spec = pltpu.PrefetchScalarGridSpec(
        num_scalar_prefetch=1, grid=(n // tm,),
        in_specs=[row_tile(d), pl.BlockSpec((1, 6, d), lambda i, modrow: (modrow[i], 0, 0)), const((1, d)),
                  row_tile(256), row_tile(256), row_tile(256), row_tile(256), row_tile(256), row_tile(256),
                  const((1, 256)), const((256, 256)),
                  const(wm.shape), const(bm.shape), const(wb.shape), const(wo.shape)],
        out_specs=row_tile(d))
    return pl.pallas_call(
        _merge_kernel, grid_spec=grid_spec,
        out_shape=jax.ShapeDtypeStruct((n, d), F32),
        compiler_params=_cparams(("parallel",)),
        name="branch_merge",
    )(modrow, x, mod_l, g1, oa, ob, ocf, ocb, gz, od, gng, ones_bd, wm, bm, wb, wo)


def _mlp_kernel(modrow_ref, x_ref, mod_ref, g2_ref, w1_ref, w2_ref, y_ref):
    x = x_ref[...]
    h = _modulate(x, g2_ref[...], mod_ref[0, 3:4, :], mod_ref[0, 4:5, :]).astype(BF16)
    acc = None
    for j in range(D_FF // D_MODEL):
        cols = slice(D_MODEL * j, D_MODEL * (j + 1))
        a = jnp.maximum(jnp.dot(h, w1_ref[:, cols], preferred_element_type=F32), 0.0)
        term = jnp.dot((a * a).astype(BF16), w2_ref[cols, :], preferred_element_type=F32)
        acc = term if acc is None else acc + term
    y_ref[...] = x + mod_ref[0, 5:6, :] * acc


def _mlp_call(modrow, x, mod_l, g2, w1, w2, *, tm):
    n, d = x.shape
    row_tile = pl.BlockSpec((tm, d), lambda i, *_: (i, 0))
    const = lambda shape: pl.BlockSpec(shape, lambda i, *_: tuple(0 for _ in shape))
    grid_spec = pltpu.PrefetchScalarGridSpec(
        num_scalar_prefetch=1, grid=(n // tm,),
        in_specs=[row_tile, pl.BlockSpec((1, 6, d), lambda i, modrow: (modrow[i], 0, 0)), const((1, d)),
                  const(w1.shape), const(w2.shape)],
        out_specs=row_tile)
    return pl.pallas_call(
        _mlp_kernel, grid_spec=grid_spec,
        out_shape=jax.ShapeDtypeStruct((n, d), F32),
        compiler_params=_cparams(("parallel",)),
        name="mlp",
    )(modrow, x, mod_l, g2, w1, w2)


def _rope_tables(t, tm):
    n_freq = HEAD_DIM // 4
    inv = np.float32(ROPE_BASE) ** (-np.arange(n_freq, dtype=np.float32) / np.float32(n_freq))
    pos = np.arange(t)
    row = (pos // GRID_W).astype(np.float32)[:, None]
    col = (pos % GRID_W).astype(np.float32)[:, None]
    ar = (row * inv).astype(np.float32)
    ac = (col * inv).astype(np.float32)
    cos64 = np.concatenate([np.cos(ar), np.cos(ar), np.cos(ac), np.cos(ac)], axis=1)
    sin64 = np.concatenate([-np.sin(ar), np.sin(ar), -np.sin(ac), np.sin(ac)], axis=1)
    cos = np.concatenate([np.ones((tm, 128), np.float32), np.tile(cos64, (1, 2)).astype(np.float32)], axis=0)
    sin = np.concatenate([np.zeros((tm, 128), np.float32), np.tile(sin64, (1, 2)).astype(np.float32)], axis=0)
    return jnp.asarray(cos), jnp.asarray(sin)


def _tile_meta(n_ctx, n_seq_dec, t_ctx, t_dec, tm):
    nct = n_ctx // tm
    per = t_dec // tm
    ndt = n_seq_dec * per
    idx = np.arange(nct + ndt)
    dec = idx >= nct
    di = np.maximum(idx - nct, 0)
    modrow = np.where(dec, 1 + di // per, 0)
    ropeblk = np.where(dec, 1 + di % per, 0)
    tiles_ctx = max(t_ctx // tm, 1)
    isstart = np.where(dec, di % per == 0, idx % tiles_ctx == 0)
    isend = np.where(dec, di % per == per - 1, idx % tiles_ctx == tiles_ctx - 1)
    as_i32 = lambda a: jnp.asarray(a.astype(np.int32))
    return as_i32(modrow), as_i32(ropeblk), as_i32(isstart), as_i32(isend)


def _block_diag(w):
    n, c, _ = w.shape
    out = jnp.zeros((n * c, n * c), w.dtype)
    for u in range(n):
        out = out.at[u * c:(u + 1) * c, u * c:(u + 1) * c].set(w[u])
    return out


def kernel(x_prompt, x_sample, c, cache_a_k, cache_a_v, cache_d_k, cache_d_v, state_lru, state_gdn, c_ctx, mod_w, mod_b, norm1_g, norm2_g, w_in, a_qn_g, a_kn_g, a_sink, lru_conv_w, lru_conv_b, lru_wr, lru_br, lru_wi, lru_bi, lru_lam, gdn_conv_w, gdn_a_log, gdn_dt_bias, gdn_norm_g, d_qn_g, d_kn_g, w_branch, w_merge, b_merge, w_out, mlp_w1, mlp_w2):
    batch, seq, d = x_prompt.shape
    dec_batch, dec_seq, _ = x_sample.shape
    depth = mod_w.shape[0]
    past = cache_a_k.shape[2]
    n_ctx = batch * seq
    n_dec = dec_batch * dec_seq
    tm = 256
    assert d == D_MODEL and seq % tm == 0 and dec_seq % tm == 0 and tm % seq == 0
    assert dec_batch + 1 <= 8 and n_ctx % dec_seq == 0

    cond8 = jnp.zeros((8, d), F32).at[0].set(c_ctx).at[1:1 + dec_batch].set(c)
    mod_all = _mod_call(cond8, mod_w, mod_b).reshape(depth, 8, 6, d)

    meta = _tile_meta(n_ctx, dec_batch, seq, dec_seq, tm)
    cos_t, sin_t = _rope_tables(dec_seq, tm)
    lane = np.arange(256)
    ones_bd = jnp.asarray((lane[:, None] // HEAD_DIM == lane[None, :] // HEAD_DIM).astype(np.float32)).astype(BF16)

    x = jnp.concatenate([x_prompt.reshape(n_ctx, d), x_sample.reshape(n_dec, d)], axis=0)
    zeros_lru = jnp.zeros((batch, 2, BRANCH_W), F32)
    zeros_gdn = jnp.zeros((batch, 2, CHUNK, 256), F32)

    new_ak, new_av, new_dk, new_dv, new_lru, new_gdn = [], [], [], [], [], []
    for l in range(depth):
        w_packed = jnp.concatenate([w_in[l][:, :2064], jnp.zeros((d, GDN_PAD), F32), w_in[l][:, 2064:]],
                                   axis=1).astype(BF16)
        pad128 = lambda v: jnp.zeros((1, 128), F32).at[0, :v.size].set(v.reshape(-1))
        vecs = (
            (jnp.tile(a_qn_g[l], 4) * (HEAD_DIM ** -0.5)).reshape(1, 256),
            jnp.tile(a_kn_g[l], 2).reshape(1, 128),
            (jnp.tile(d_qn_g[l], 4) * (HEAD_DIM ** -0.5)).reshape(1, 256),
            jnp.tile(d_kn_g[l], 2).reshape(1, 128),
            lru_conv_w[l], lru_conv_b[l].reshape(1, 256), gdn_conv_w[l],
            pad128(gdn_a_log[l]), pad128(gdn_dt_bias[l]),
        )
        g1 = norm1_g[l].reshape(1, d)
        g2 = norm2_g[l].reshape(1, d)
        mod_l = mod_all[l]

        (qa, ka, va, lx, lg, gq, gk, gv, gz, ggb, qd, kd, vd) = _inproj_call(
            x, meta, mod_l, g1, w_packed, cos_t, sin_t, vecs, ones_bd, tm=tm)

        sink = a_sink[l]
        oa_c, od_c = _attn_ctx_call(sink, qa, ka, va, qd, kd, vd, n_seq=batch, t=seq)
        cak = cache_a_k[:, l].reshape(dec_batch, past, 128)
        cav = cache_a_v[:, l].reshape(dec_batch, past, 128)
        cdk = cache_d_k[:, l].reshape(dec_batch, past, 128)
        cdv = cache_d_v[:, l].reshape(dec_batch, past, 128)
        oa_d = _attn_band_call(sink, qa, ka, va, cak, cav, n_seq=dec_batch, t=dec_seq, row0=n_ctx, tq=512)
        od_d = _attn_full_call(qd, kd, vd, cdk, cdv, n_seq=dec_batch, t=dec_seq, row0=n_ctx, tq=256)
        oa = jnp.concatenate([oa_c, oa_d], axis=0)
        od = jnp.concatenate([od_c, od_d], axis=0)

        w_lru = jnp.concatenate([_block_diag(lru_wr[l, 0]), _block_diag(lru_wi[l, 0]),
                                 _block_diag(lru_wr[l, 1]), _block_diag(lru_wi[l, 1])], axis=1).astype(BF16)
        b_lru = jnp.concatenate([lru_br[l, 0], lru_bi[l, 0], lru_br[l, 1], lru_bi[l, 1]]).reshape(1, 1024)
        ob_c, st_c = _lru_call(lx, lg, w_lru, b_lru, lru_lam[l], zeros_lru, n_seq=batch, t=seq, row0=0)
        ob_d, _ = _lru_call(lx, lg, w_lru, b_lru, lru_lam[l], state_lru[:, l], n_seq=dec_batch, t=dec_seq,
                            row0=n_ctx)
        ob = jnp.concatenate([ob_c, ob_d], axis=0)

        s0_d = state_gdn[:, l].transpose(0, 1, 3, 2, 4).reshape(dec_batch, 2, CHUNK, 256)
        ocs, sts = [], []
        for dd in range(2):
            o_c, s_c = _gdn_call(gq, gk, gv, ggb, zeros_gdn, d=dd, n_seq=batch, t=seq, row0=0, tt=seq)
            o_d, _ = _gdn_call(gq, gk, gv, ggb, s0_d, d=dd, n_seq=dec_batch, t=dec_seq, row0=n_ctx, tt=256)
            ocs.append(jnp.concatenate([o_c[0], o_d[0]], axis=0))
            sts.append(s_c)

        x = _merge_call(meta[0], x, mod_l, g1, oa, ob, ocs[0], ocs[1], gz, od,
                        jnp.tile(gdn_norm_g[l], 4).reshape(1, 256), ones_bd,
                        w_merge[l].astype(BF16), b_merge[l].reshape(1, -1), w_branch[l].astype(BF16),
                        w_out[l].astype(BF16), tm=tm)
        x = _mlp_call(meta[0], x, mod_l, g2, mlp_w1[l].astype(BF16), mlp_w2[l].astype(BF16), tm=tm)

        new_ak.append(ka[:n_ctx].reshape(batch, seq, 2, HEAD_DIM))
        new_av.append(va[:n_ctx].reshape(batch, seq, 2, HEAD_DIM))
        new_dk.append(kd[:n_ctx].reshape(batch, seq, 2, HEAD_DIM))
        new_dv.append(vd[:n_ctx].reshape(batch, seq, 2, HEAD_DIM))
        new_lru.append(st_c)
        s_fb = jnp.concatenate(sts, axis=1)
        new_gdn.append(s_fb.reshape(batch, 2, GDN_DK, GDN_H, GDN_DK).transpose(0, 1, 3, 2, 4))

    y_prompt = x[:n_ctx].reshape(batch, seq, d)
    y_sample = x[n_ctx:].reshape(dec_batch, dec_seq, d)
    stack = lambda ts: jnp.stack(ts, axis=1)
    return (y_prompt, y_sample, stack(new_ak), stack(new_av), stack(new_dk), stack(new_dv),
            stack(new_lru), stack(new_gdn))
```

```python
import functools

import numpy as np
import jax
import jax.numpy as jnp
from jax import lax
from jax.experimental import pallas as pl
from jax.experimental.pallas import tpu as pltpu

F32 = jnp.float32
BF16 = jnp.bfloat16
HIGHEST = lax.Precision.HIGHEST

D_MODEL = 1024
HEAD_DIM = 64
BRANCH_W = 256
N_BRANCH = 4
GRID_W = 64
WINDOW = 128
LRU_C = 8.0
CONV_W = 4
CONV_LEFT = 2
GDN_H = 4
GDN_DK = 64
CHUNK = 64
D_FF = 4 * D_MODEL
ROPE_BASE = 10000.0
EPS = 1e-6
NEG = -1e30
GDN_PAD = 112
IN_COLS_PACKED = 2688
HALO = 8

V7X_VMEM_LIMIT = 56 * 1024 * 1024


def _cparams(sem, vmem=V7X_VMEM_LIMIT):
    return pltpu.CompilerParams(dimension_semantics=sem, vmem_limit_bytes=vmem)


def _bdot(a, b):
    return jnp.dot(a.astype(BF16), b.astype(BF16), preferred_element_type=F32)


def _split(x):
    hi = x.astype(BF16)
    lo = (x - hi.astype(F32)).astype(BF16)
    return hi, lo


def _dot3(a, b):
    ah, al = _split(a)
    bh, bl = _split(b)
    return (jnp.dot(ah, bh, preferred_element_type=F32)
            + jnp.dot(al, bh, preferred_element_type=F32)
            + jnp.dot(ah, bl, preferred_element_type=F32))


def _group_sum(x, ones_bd):
    hi, lo = _split(x)
    return (jnp.dot(hi, ones_bd, preferred_element_type=F32)
            + jnp.dot(lo, ones_bd, preferred_element_type=F32))


def _sigmoid(x):
    return 1.0 / (1.0 + jnp.exp(-x))


def _silu(x):
    return x * _sigmoid(x)


def _softplus(x):
    return jnp.maximum(x, 0.0) + jnp.log1p(jnp.exp(-jnp.abs(x)))


def _modulate(x, g, shift, scale):
    ms = jnp.mean(x * x, axis=-1, keepdims=True)
    return (x * lax.rsqrt(ms + EPS) * g) * (1.0 + scale) + shift


def _lane_iota(shape):
    return lax.broadcasted_iota(jnp.int32, shape, len(shape) - 1)


def _row_iota(shape):
    return lax.broadcasted_iota(jnp.int32, shape, len(shape) - 2)


def _mod_kernel(cond_ref, w_ref, b_ref, o_ref):
    c = cond_ref[...]
    o_ref[0] = jnp.dot(_silu(c), w_ref[0], precision=HIGHEST, preferred_element_type=F32) + b_ref[0]


def _mod_call(cond8, mod_w, mod_b):
    depth, d, n = mod_w.shape
    tn = 1536
    return pl.pallas_call(
        _mod_kernel,
        grid=(depth, n // tn),
        in_specs=[
            pl.BlockSpec((8, d), lambda l, j: (0, 0)),
            pl.BlockSpec((1, d, tn), lambda l, j: (l, 0, j)),
            pl.BlockSpec((1, 1, tn), lambda l, j: (l, 0, j)),
        ],
        out_specs=pl.BlockSpec((1, 8, tn), lambda l, j: (l, 0, j)),
        out_shape=jax.ShapeDtypeStruct((depth, 8, n), F32),
        compiler_params=_cparams(("parallel", "parallel")),
        name="mod_vectors",
    )(cond8, mod_w, mod_b.reshape(depth, 1, n))


def _rope(x, cos, sin):
    outs = []
    for j in range(x.shape[1] // 128):
        xb = x[:, 128 * j:128 * (j + 1)]
        lane = _lane_iota(xb.shape)
        sw = jnp.where((lane & 16) == 0, pltpu.roll(xb, 112, 1), pltpu.roll(xb, 16, 1))
        outs.append(xb * cos + sw * sin)
    return outs[0] if len(outs) == 1 else jnp.concatenate(outs, axis=1)


def _head_rms(x, gain, ones_bd):
    ms = _group_sum(x * x, ones_bd) * (1.0 / HEAD_DIM)
    return x * lax.rsqrt(ms + EPS) * gain


def _centred_conv(g, w, tm):
    rows = g.shape[0]
    acc = None
    for j in range(CONV_W):
        sh = (CONV_LEFT - j) % rows
        gj = g if sh == 0 else pltpu.roll(g, sh, 0)
        term = gj[HALO:HALO + tm] * w[j:j + 1, :]
        acc = term if acc is None else acc + term
    return acc


def _inproj_kernel(modrow_ref, ropeblk_ref, isstart_ref, isend_ref,
                   xp_ref, x_ref, xn_ref, mod_ref, g1_ref, w_ref, cos_ref, sin_ref,
                   aqg_ref, akg_ref, dqg_ref, dkg_ref, lcw_ref, lcb_ref, gcw_ref,
                   alog_ref, dtb_ref, ones_ref,
                   qa_ref, ka_ref, va_ref, lx_ref, lg_ref, gq_ref, gk_ref, gv_ref, gz_ref, ggb_ref,
                   qd_ref, kd_ref, vd_ref, *, tm):
    i = pl.program_id(0)
    xfull = jnp.concatenate([xp_ref[...], x_ref[...], xn_ref[...]], axis=0)
    h = _modulate(xfull, g1_ref[...], mod_ref[0, 0:1, :], mod_ref[0, 1:2, :])
    p = jnp.dot(h.astype(BF16), w_ref[...], preferred_element_type=F32)

    ones_bd = ones_ref[...]
    cos = cos_ref[...]
    sin = sin_ref[...]
    lo, hi = HALO, HALO + tm

    qa_ref[...] = _rope(_head_rms(p[lo:hi, 0:256], aqg_ref[...], ones_bd), cos, sin)
    ka_ref[...] = _rope(_head_rms(p[lo:hi, 256:384], akg_ref[...], ones_bd[:128, :128]), cos, sin)
    va_ref[...] = p[lo:hi, 384:512]
    qd_ref[...] = _rope(_head_rms(p[lo:hi, 2176:2432], dqg_ref[...], ones_bd), cos, sin)
    kd_ref[...] = _rope(_head_rms(p[lo:hi, 2432:2560], dkg_ref[...], ones_bd[:128, :128]), cos, sin)
    vd_ref[...] = p[lo:hi, 2560:2688]

    row = _row_iota((tm + 2 * HALO, 1))
    keep = jnp.logical_and(jnp.logical_or(row >= HALO, isstart_ref[i] == 0),
                           jnp.logical_or(row < HALO + tm, isend_ref[i] == 0))

    lxg = jnp.where(keep, p[:, 512:768], 0.0)
    lx_ref[...] = _centred_conv(lxg, lcw_ref[...], tm) + lcb_ref[...]
    lg_ref[...] = p[lo:hi, 768:1024]

    qkv = jnp.where(keep, p[:, 1024:1792], 0.0)
    qkv = _silu(_centred_conv(qkv, gcw_ref[...], tm))
    gq = qkv[:, 0:256]
    gk = qkv[:, 256:512]
    gq_ref[...] = gq * lax.rsqrt(_group_sum(gq * gq, ones_bd) + EPS) * (GDN_DK ** -0.5)
    gk_ref[...] = gk * lax.rsqrt(_group_sum(gk * gk, ones_bd) + EPS)
    gv_ref[...] = qkv[:, 512:768]
    gz_ref[...] = p[lo:hi, 1792:2048]
    ab = p[lo:hi, 2048:2176]
    g = -jnp.exp(alog_ref[...]) * _softplus(ab + dtb_ref[...])
    lane = _lane_iota(ab.shape)
    ggb_ref[...] = jnp.where(lane < 2 * GDN_H, g, _sigmoid(ab))


def _inproj_call(x, meta, mod_l, g1, w_packed, cos_t, sin_t, vecs, ones_bd, *, tm):
    n, d = x.shape
    nt = n // tm
    hb = tm // HALO
    last_hb = n // HALO - 1
    row_tile = lambda w: pl.BlockSpec((tm, w), lambda i, *_: (i, 0))
    const = lambda shape: pl.BlockSpec(shape, lambda i, *_: tuple(0 for _ in shape))
    in_specs = [
        pl.BlockSpec((HALO, d), lambda i, *_: (jnp.maximum(i * hb - 1, 0), 0)),
        row_tile(d),
        pl.BlockSpec((HALO, d), lambda i, *_: (jnp.minimum((i + 1) * hb, last_hb), 0)),
        pl.BlockSpec((1, 6, d), lambda i, modrow, *_: (modrow[i], 0, 0)),
        const((1, d)),
        const((d, IN_COLS_PACKED)),
        pl.BlockSpec((tm, 128), lambda i, modrow, ropeblk, *_: (ropeblk[i], 0)),
        pl.BlockSpec((tm, 128), lambda i, modrow, ropeblk, *_: (ropeblk[i], 0)),
    ] + [const(v.shape) for v in vecs] + [const((256, 256))]
    widths = (256, 128, 128, 256, 256, 256, 256, 256, 256, 128, 256, 128, 128)
    grid_spec = pltpu.PrefetchScalarGridSpec(
        num_scalar_prefetch=4,
        grid=(nt,),
        in_specs=in_specs,
        out_specs=[row_tile(w) for w in widths],
    )
    return pl.pallas_call(
        functools.partial(_inproj_kernel, tm=tm),
        grid_spec=grid_spec,
        out_shape=[jax.ShapeDtypeStruct((n, w), F32) for w in widths],
        compiler_params=_cparams(("parallel",)),
        name="in_projection",
    )(*meta, x, x, x, mod_l, g1, w_packed, cos_t, sin_t, *vecs, ones_bd)


def _head_q(q, j, g):
    qj = q[:, 128 * j:128 * (j + 1)]
    lane = _lane_iota(qj.shape)
    sel = (lane < HEAD_DIM) if g == 0 else (lane >= HEAD_DIM)
    return jnp.where(sel, qj, 0.0).astype(BF16)


def _place_heads(res, j):
    r0 = res[0] if j == 0 else pltpu.roll(res[0], HEAD_DIM, 1)
    r1 = res[1] if j == 1 else pltpu.roll(res[1], HEAD_DIM, 1)
    lane = _lane_iota(r0.shape)
    return jnp.where(lane < HEAD_DIM, r0, r1)


def _attend(q, ks, vs, masks, sinks):
    ks_sw = [pltpu.roll(k, HEAD_DIM, 1) for k in ks]
    blocks = []
    for j in range(2):
        res = []
        for g in range(2):
            qm = _head_q(q, j, g)
            ss = []
            for k, ksw, mk in zip(ks, ks_sw, masks):
                kk = k if g == j else ksw
                s = lax.dot_general(qm, kk, (((1,), (1,)), ((), ())), preferred_element_type=F32)
                if mk is not None:
                    s = jnp.where(mk, s, NEG)
                ss.append(s)
            m = ss[0].max(axis=-1, keepdims=True)
            for s in ss[1:]:
                m = jnp.maximum(m, s.max(axis=-1, keepdims=True))
            if sinks is not None:
                m = jnp.maximum(m, sinks[2 * j + g])
            den = None
            acc = None
            for s, v in zip(ss, vs):
                e = jnp.exp(s - m)
                dsum = e.sum(axis=-1, keepdims=True)
                den = dsum if den is None else den + dsum
                pv = jnp.dot(e.astype(BF16), v, preferred_element_type=F32)
                acc = pv if acc is None else acc + pv
            if sinks is not None:
                den = den + jnp.exp(sinks[2 * j + g] - m)
            res.append(acc / den)
        blocks.append(_place_heads(res, j))
    return jnp.concatenate(blocks, axis=1)


def _bf(x):
    return x.astype(BF16)


def _attn_ctx_kernel(sink_ref, qa_ref, ka_ref, va_ref, qd_ref, kd_ref, vd_ref, oa_ref, od_ref):
    sinks = [sink_ref[t] for t in range(4)]
    oa_ref[...] = _attend(qa_ref[...], [_bf(ka_ref[...])], [_bf(va_ref[...])], [None], sinks)
    od_ref[...] = _attend(qd_ref[...], [_bf(kd_ref[...])], [_bf(vd_ref[...])], [None], None)


def _attn_ctx_call(sink, qa, ka, va, qd, kd, vd, *, n_seq, t):
    blk = lambda w: pl.BlockSpec((t, w), lambda b, *_: (b, 0))
    grid_spec = pltpu.PrefetchScalarGridSpec(
        num_scalar_prefetch=1, grid=(n_seq,),
        in_specs=[blk(256), blk(128), blk(128), blk(256), blk(128), blk(128)],
        out_specs=[blk(256), blk(256)])
    return pl.pallas_call(
        _attn_ctx_kernel, grid_spec=grid_spec,
        out_shape=[jax.ShapeDtypeStruct((n_seq * t, 256), F32)] * 2,
        compiler_params=_cparams(("parallel",)),
        name="attn_context",
    )(sink, qa, ka, va, qd, kd, vd)


def _attn_band_kernel(sink_ref, q_ref, k_ref, v_ref, ck_ref, cv_ref, o_ref, *, tq, t):
    i = pl.program_id(1)
    start = pl.multiple_of(i * tq, tq)
    prev = pl.multiple_of(jnp.maximum(start - WINDOW, 0), WINDOW)
    nxt = pl.multiple_of(jnp.minimum(start + tq, t - WINDOW), WINDOW)
    k_loc = jnp.concatenate([k_ref[pl.ds(prev, WINDOW), :], k_ref[pl.ds(start, tq), :],
                             k_ref[pl.ds(nxt, WINDOW), :]], axis=0)
    v_loc = jnp.concatenate([v_ref[pl.ds(prev, WINDOW), :], v_ref[pl.ds(start, tq), :],
                             v_ref[pl.ds(nxt, WINDOW), :]], axis=0)
    shape = (tq, tq + 2 * WINDOW)
    qpos = start + _row_iota(shape)
    kpos = start - WINDOW + _lane_iota(shape)
    mask = (kpos >= 0) & (kpos < t) & (jnp.abs(qpos - kpos) <= WINDOW)
    sinks = [sink_ref[u] for u in range(4)]
    o_ref[...] = _attend(q_ref[...], [_bf(ck_ref[0]), _bf(k_loc)], [_bf(cv_ref[0]), _bf(v_loc)],
                         [None, mask], sinks)


def _attn_band_call(sink, q, k, v, ck, cv, *, n_seq, t, row0, tq):
    nq = t // tq
    qb0 = row0 // tq
    sb0 = row0 // t
    past = ck.shape[1]
    grid_spec = pltpu.PrefetchScalarGridSpec(
        num_scalar_prefetch=1, grid=(n_seq, nq),
        in_specs=[
            pl.BlockSpec((tq, 256), lambda b, i, *_: (qb0 + b * nq + i, 0)),
            pl.BlockSpec((t, 128), lambda b, i, *_: (sb0 + b, 0)),
            pl.BlockSpec((t, 128), lambda b, i, *_: (sb0 + b, 0)),
            pl.BlockSpec((1, past, 128), lambda b, i, *_: (b, 0, 0)),
            pl.BlockSpec((1, past, 128), lambda b, i, *_: (b, 0, 0)),
        ],
        out_specs=pl.BlockSpec((tq, 256), lambda b, i, *_: (b * nq + i, 0)))
    return pl.pallas_call(
        functools.partial(_attn_band_kernel, tq=tq, t=t), grid_spec=grid_spec,
        out_shape=jax.ShapeDtypeStruct((n_seq * t, 256), F32),
        compiler_params=_cparams(("parallel", "parallel")),
        name="attn_banded",
    )(sink, q, k, v, ck, cv)


def _attn_full_kernel(q_ref, k_ref, v_ref, ck_ref, cv_ref, o_ref):
    o_ref[...] = _attend(q_ref[...], [_bf(ck_ref[0]), _bf(k_ref[...])], [_bf(cv_ref[0]), _bf(v_ref[...])],
                         [None, None], None)


def _attn_full_call(q, k, v, ck, cv, *, n_seq, t, row0, tq):
    nq = t // tq
    qb0 = row0 // tq
    sb0 = row0 // t
    past = ck.shape[1]
    return pl.pallas_call(
        _attn_full_kernel, grid=(n_seq, nq),
        in_specs=[
            pl.BlockSpec((tq, 256), lambda b, i: (qb0 + b * nq + i, 0)),
            pl.BlockSpec((t, 128), lambda b, i: (sb0 + b, 0)),
            pl.BlockSpec((t, 128), lambda b, i: (sb0 + b, 0)),
            pl.BlockSpec((1, past, 128), lambda b, i: (b, 0, 0)),
            pl.BlockSpec((1, past, 128), lambda b, i: (b, 0, 0)),
        ],
        out_specs=pl.BlockSpec((tq, 256), lambda b, i: (b * nq + i, 0)),
        out_shape=jax.ShapeDtypeStruct((n_seq * t, 256), F32),
        compiler_params=_cparams(("parallel", "parallel")),
        name="attn_full",
    )(q, k, v, ck, cv)


def _gelu_tanh(x):
    return 0.5 * x * (1.0 + jnp.tanh(0.7978845608028654 * (x + 0.044715 * (x * x * x))))


def _lru_kernel(lx_ref, lg_ref, w_ref, b_ref, lam_ref, h0_ref, y_ref, st_ref, hf_ref, a_ref, bb_ref, hb_ref,
                *, t, chunk):
    nc = t // chunk
    ng = chunk // 8
    sp = _softplus(-lam_ref[...])
    row8 = _row_iota((8, BRANCH_W))

    def gates(xc, d):
        pre = _bdot(xc, w_ref[:, 512 * d:512 * (d + 1)]) + b_ref[:, 512 * d:512 * (d + 1)]
        r = _sigmoid(pre[:, 0:256])
        ig = _sigmoid(pre[:, 256:512])
        log_a = (-LRU_C) * r * sp[d:d + 1, :]
        a = jnp.exp(log_a)
        a_ref[...] = a
        bb_ref[...] = jnp.sqrt(1.0 - a * a) * (ig * xc)

    def local_scan(a8, b8, reverse):
        for s in (1, 2, 4):
            sh = (8 - s) if reverse else s
            m = (row8 < 8 - s) if reverse else (row8 >= s)
            a_sh = pltpu.roll(a8, sh, 0)
            b_sh = pltpu.roll(b8, sh, 0)
            b8 = jnp.where(m, a8 * b_sh + b8, b8)
            a8 = jnp.where(m, a8 * a_sh, a8)
        return a8, b8

    def run(d, reverse, h_init, out_ref, out_base):
        def chunk_body(cc, h):
            c = (nc - 1 - cc) if reverse else cc
            r0 = pl.multiple_of(c * chunk, chunk)
            gates(lx_ref[pl.ds(r0, chunk), :], d)

            def grp(gg, hc):
                gi = (ng - 1 - gg) if reverse else gg
                g0 = pl.multiple_of(gi * 8, 8)
                a8, b8 = local_scan(a_ref[pl.ds(g0, 8), :], bb_ref[pl.ds(g0, 8), :], reverse)
                h8 = b8 + a8 * hc
                out_ref[pl.ds(out_base(r0) + g0, 8), :] = h8
                last = h8[0:1, :] if reverse else h8[7:8, :]
                return jnp.broadcast_to(last, (8, BRANCH_W))

            h = lax.fori_loop(0, ng, grp, h, unroll=4)
            if reverse:
                rows = pl.ds(r0, chunk)
                y_ref[rows, :] = (hf_ref[rows, :] + hb_ref[...]) * _gelu_tanh(lg_ref[rows, :])
            return h

        return lax.fori_loop(0, nc, chunk_body, jnp.broadcast_to(h_init, (8, BRANCH_W)))

    hf = run(0, False, h0_ref[0, 0:1, :], hf_ref, lambda r0: r0)
    hb = run(1, True, h0_ref[0, 1:2, :], hb_ref, lambda r0: 0)
    st_ref[0, 0:1, :] = hf[0:1, :]
    st_ref[0, 1:2, :] = hb[0:1, :]


def _lru_call(lx, lg, w, b, lam, h0, *, n_seq, t, row0):
    sb0 = row0 // t
    chunk = min(t, 256)
    seq = lambda: pl.BlockSpec((t, BRANCH_W), lambda s: (sb0 + s, 0))
    const = lambda shape: pl.BlockSpec(shape, lambda s: tuple(0 for _ in shape))
    return pl.pallas_call(
        functools.partial(_lru_kernel, t=t, chunk=chunk),
        grid=(n_seq,),
        in_specs=[seq(), seq(), const(w.shape), const(b.shape), const(lam.shape),
                  pl.BlockSpec((1, 2, BRANCH_W), lambda s: (s, 0, 0))],
        out_specs=[pl.BlockSpec((t, BRANCH_W), lambda s: (s, 0)),
                   pl.BlockSpec((1, 2, BRANCH_W), lambda s: (s, 0, 0))],
        out_shape=[jax.ShapeDtypeStruct((n_seq * t, BRANCH_W), F32),
                   jax.ShapeDtypeStruct((n_seq, 2, BRANCH_W), F32)],
        scratch_shapes=[pltpu.VMEM((t, BRANCH_W), F32), pltpu.VMEM((chunk, BRANCH_W), F32),
                        pltpu.VMEM((chunk, BRANCH_W), F32), pltpu.VMEM((chunk, BRANCH_W), F32)],
        compiler_params=_cparams(("parallel",)),
        name="rglru",
    )(lx, lg, w, b, lam, h0)


def _split3(x):
    hi = x.astype(BF16)
    r = x - hi.astype(F32)
    mid = r.astype(BF16)
    lo = (r - mid.astype(F32)).astype(BF16)
    return hi, mid, lo


def _rows_dot_exact(x, w01):
    r = x.shape[0]
    out = jnp.dot(jnp.concatenate(_split3(x), axis=0), w01, preferred_element_type=F32)
    return out[0:r] + out[r:2 * r] + out[2 * r:3 * r]


def _dot01_exact(m01, y):
    return sum(jnp.dot(m01, part, preferred_element_type=F32) for part in _split3(y))


def _gdn_kernel(q_ref, k_ref, v_ref, ggb_ref, s0_ref, o_ref, sT_ref, s_ref, *, d, reverse, n_chunk, n_tile):
    ti = pl.program_id(1)
    w4 = GDN_H * GDN_DK
    tt = n_chunk * CHUNK
    blockmask = (_row_iota((w4, w4)) // CHUNK) == (_lane_iota((w4, w4)) // CHUNK)

    def expand_rows(y):
        yt = jnp.concatenate([y] * GDN_H, axis=0)
        zero = jnp.zeros((), y.dtype)
        parts = [jnp.where(blockmask, yt[:, w4 * u:w4 * (u + 1)], zero) for u in range(y.shape[1] // w4)]
        return parts[0] if len(parts) == 1 else jnp.concatenate(parts, axis=1)

    def heads_dot3(lhs, y):
        r = lhs.shape[0]
        lh, ll = _split(lhs)
        yh, yl = _split(y)
        out = jnp.dot(jnp.concatenate([lh, ll], axis=0), expand_rows(yh), preferred_element_type=F32)
        return out[0:r] + out[r:2 * r] + jnp.dot(lh, expand_rows(yl), preferred_element_type=F32)

    @pl.when(ti == 0)
    def _():
        s_ref[...] = expand_rows(s0_ref[0, 0])

    ri = _row_iota((CHUNK, w4))
    cj = _lane_iota((CHUNK, w4)) % CHUNK
    incl = (cj >= ri) if reverse else (cj <= ri)
    strict = (cj > ri) if reverse else (cj < ri)
    eye_sbs = (cj == ri)
    rep = eye_sbs.astype(BF16)

    er = _row_iota((128, 2 * w4))
    el = _lane_iota((128, 2 * w4))
    e_gb = (er == jnp.where(el < w4, d * GDN_H, 2 * GDN_H + d * GDN_H) + (el % w4) // CHUNK).astype(BF16)
    gbe = _rows_dot_exact(ggb_ref[...], e_gb)
    g_all = gbe[:, 0:w4]
    beta_all = gbe[:, w4:2 * w4]
    tr = _row_iota((tt, tt))
    tc = _lane_iota((tt, tt))
    same_chunk = (tr // CHUNK) == (tc // CHUNK)
    tri_bd = (same_chunk & ((tc >= tr) if reverse else (tc <= tr))).astype(BF16)
    ones_bd = same_chunk.astype(BF16)
    gc_all = _dot01_exact(tri_bd, g_all)
    diag_sel = (_row_iota((tt, w4)) % CHUNK) == (_lane_iota((tt, w4)) % CHUNK)
    gct_all = _dot01_exact(ones_bd, jnp.where(diag_sel, gc_all, 0.0))

    cs = range(n_chunk)
    rows = [slice(c * CHUNK, (c + 1) * CHUNK) for c in cs]
    q = [q_ref[r, :] for r in rows]
    k = [k_ref[r, :] for r in rows]
    v = [v_ref[r, :] for r in rows]
    beta = [beta_all[r, :] for r in rows]
    gc = [gc_all[r, :] for r in rows]
    decay = [jnp.where(incl, jnp.exp(jnp.where(incl, gc[c] - gct_all[rows[c], :], 0.0)), 0.0) for c in cs]
    kb = [k[c].astype(BF16) for c in cs]
    w_k = [jnp.where(blockmask,
                     lax.dot_general(kb[c], rep, (((0,), (0,)), ((), ())), preferred_element_type=F32),
                     0.0).astype(BF16) for c in cs]
    kq = [jnp.dot(jnp.concatenate([kb[c], q[c].astype(BF16)], axis=0), w_k[c], preferred_element_type=F32)
          for c in cs]
    qk = [(kq[c][CHUNK:2 * CHUNK] * decay[c]).astype(BF16) for c in cs]
    a = [jnp.where(strict, beta[c] * kq[c][0:CHUNK] * decay[c], 0.0) for c in cs]
    t_inv = [jnp.where(eye_sbs, 1.0, 0.0) - a[c] for c in cs]
    pw = [heads_dot3(a[c], a[c]) for c in cs]
    for _ in range(4):
        both = [heads_dot3(jnp.concatenate([t_inv[c], pw[c]], axis=0), pw[c]) for c in cs]
        t_inv = [t_inv[c] + both[c][0:CHUNK] for c in cs]
        pw = [both[c][CHUNK:2 * CHUNK] for c in cs]
    t_inv = [t_inv[c] + heads_dot3(t_inv[c], pw[c]) for c in cs]
    egc = [jnp.exp(gc[c]) for c in cs]
    sol = [heads_dot3(t_inv[c], jnp.concatenate([v[c] * beta[c], k[c] * (beta[c] * egc[c])], axis=1)) for c in cs]
    g_last = [gc[c][0:1, :] if reverse else gc[c][CHUNK - 1:CHUNK, :] for c in cs]
    wq = [jnp.concatenate([sol[c][:, w4:2 * w4], q[c] * egc[c]], axis=0).astype(BF16) for c in cs]
    kdec = [(k[c] * jnp.exp(g_last[c] - gc[c])).astype(BF16) for c in cs]

    for cc in cs:
        c = (n_chunk - 1 - cc) if reverse else cc
        s = s_ref[...]
        ws_qs = jnp.dot(wq[c], s.astype(BF16), preferred_element_type=F32)
        v_new = sol[c][:, 0:w4] - ws_qs[0:CHUNK]
        vb = v_new.astype(BF16)
        o_ref[0, rows[c], :] = (
            ws_qs[CHUNK:2 * CHUNK] + jnp.dot(qk[c], expand_rows(vb), preferred_element_type=F32))
        upd = lax.dot_general(kdec[c], vb, (((0,), (0,)), ((), ())), preferred_element_type=F32)
        s_ref[...] = s * jnp.exp(g_last[c]) + jnp.where(blockmask, upd, 0.0)

    @pl.when(ti == n_tile - 1)
    def _():
        s = s_ref[...]
        sT_ref[0, 0] = s[0:64] + s[64:128] + s[128:192] + s[192:256]


def _gdn_call(q, k, v, ggb, s0, *, d, n_seq, t, row0, tt):
    reverse = d == 1
    n_tile = t // tt
    b0 = row0 // tt
    tidx = (lambda i: n_tile - 1 - i) if reverse else (lambda i: i)
    blk = lambda w: pl.BlockSpec((tt, w), lambda s, i: (b0 + s * n_tile + tidx(i), 0))
    return pl.pallas_call(
        functools.partial(_gdn_kernel, d=d, reverse=reverse, n_chunk=tt // CHUNK, n_tile=n_tile),
        grid=(n_seq, n_tile),
        in_specs=[blk(256), blk(256), blk(256), blk(128),
                  pl.BlockSpec((1, 1, CHUNK, 256), lambda s, i: (s, d, 0, 0))],
        out_specs=[pl.BlockSpec((1, tt, 256), lambda s, i: (0, s * n_tile + tidx(i), 0)),
                   pl.BlockSpec((1, 1, CHUNK, 256), lambda s, i: (s, 0, 0, 0))],
        out_shape=[jax.ShapeDtypeStruct((1, n_seq * t, 256), F32),
                   jax.ShapeDtypeStruct((n_seq, 1, CHUNK, 256), F32)],
        scratch_shapes=[pltpu.VMEM((256, 256), F32)],
        compiler_params=_cparams(("parallel", "arbitrary")),
        name="gdn_bwd" if reverse else "gdn_fwd",
    )(q, k, v, ggb, s0)


def _merge_kernel(modrow_ref, x_ref, mod_ref, g1_ref, oa_ref, ob_ref, ocf_ref, ocb_ref, gz_ref, od_ref,
                  gng_ref, ones_ref, wm_ref, bm_ref, wb_ref, wo_ref, y_ref):
    x = x_ref[...]
    h = _modulate(x, g1_ref[...], mod_ref[0, 0:1, :], mod_ref[0, 1:2, :]).astype(BF16)
    oc = ocf_ref[...] + ocb_ref[...]
    ms = _group_sum(oc * oc, ones_ref[...]) * (1.0 / GDN_DK)
    oc = (oc * lax.rsqrt(ms + EPS) * gng_ref[...]) * _silu(gz_ref[...])
    branches = (oa_ref[...], ob_ref[...], oc, od_ref[...])
    acc = None
    for m in range(N_BRANCH):
        cols = slice(D_MODEL * m, D_MODEL * (m + 1))
        gate = _sigmoid(jnp.dot(h, wm_ref[:, cols], preferred_element_type=F32) + bm_ref[:, cols])
        term = gate * jnp.dot(branches[m].astype(BF16), wb_ref[m], preferred_element_type=F32)
        acc = term if acc is None else acc + term
    y = jnp.dot(acc.astype(BF16), wo_ref[...], preferred_element_type=F32)
    y_ref[...] = x + mod_ref[0, 2:3, :] * y


def _merge_call(modrow, x, mod_l, g1, oa, ob, ocf, ocb, gz, od, gng, ones_bd, wm, bm, wb, wo, *, tm):
    n, d = x.shape
    row_tile = lambda w: pl.BlockSpec((tm, w), lambda i, *_: (i, 0))
    const = lambda shape: pl.BlockSpec(shape, lambda i, *_: tuple(0 for _ in shape))
    grid_spec = pltpu.PrefetchScalarGridSpec(
        num_scalar_prefetch=1, grid=(n // tm,),
        in_specs=[row_tile(d), pl.BlockSpec((1, 6, d), lambda i, modrow: (modrow[i], 0, 0)), const((1, d)),
                  row_tile(256), row_tile(256), row_tile(256), row_tile(256), row_tile(256), row_tile(256),
                  const((1, 256)), const((256, 256)),
                  const(wm.shape), const(bm.shape), const(wb.shape), const(wo.shape)],
        out_specs=row_tile(d))
    return pl.pallas_call(
        _merge_kernel, grid_spec=grid_spec,
        out_shape=jax.ShapeDtypeStruct((n, d), F32),
        compiler_params=_cparams(("parallel",)),
        name="branch_merge",
    )(modrow, x, mod_l, g1, oa, ob, ocf, ocb, gz, od, gng, ones_bd, wm, bm, wb, wo)


def _mlp_kernel(modrow_ref, x_ref, mod_ref, g2_ref, w1_ref, w2_ref, y_ref):
    x = x_ref[...]
    h = _modulate(x, g2_ref[...], mod_ref[0, 3:4, :], mod_ref[0, 4:5, :]).astype(BF16)
    acc = None
    for j in range(D_FF // D_MODEL):
        cols = slice(D_MODEL * j, D_MODEL * (j + 1))
        a = jnp.maximum(jnp.dot(h, w1_ref[:, cols], preferred_element_type=F32), 0.0)
        term = jnp.dot((a * a).astype(BF16), w2_ref[cols, :], preferred_element_type=F32)
        acc = term if acc is None else acc + term
    y_ref[...] = x + mod_ref[0, 5:6, :] * acc


def _mlp_call(modrow, x, mod_l, g2, w1, w2, *, tm):
    n, d = x.shape
    row_tile = pl.BlockSpec((tm, d), lambda i, *_: (i, 0))
    const = lambda shape: pl.BlockSpec(shape, lambda i, *_: tuple(0 for _ in shape))
    grid_spec = pltpu.PrefetchScalarGridSpec(
        num_scalar_prefetch=1, grid=(n // tm,),
        in_specs=[row_tile, pl.BlockSpec((1, 6, d), lambda i, modrow: (modrow[i], 0, 0)), const((1, d)),
                  const(w1.shape), const(w2.shape)],
        out_specs=row_tile)
    return pl.pallas_call(
        _mlp_kernel, grid_spec=grid_spec,
        out_shape=jax.ShapeDtypeStruct((n, d), F32),
        compiler_params=_cparams(("parallel",)),
        name="mlp",
    )(modrow, x, mod_l, g2, w1, w2)


def _rope_tables(t, tm):
    n_freq = HEAD_DIM // 4
    inv = np.float32(ROPE_BASE) ** (-np.arange(n_freq, dtype=np.float32) / np.float32(n_freq))
    pos = np.arange(t)
    row = (pos // GRID_W).astype(np.float32)[:, None]
    col = (pos % GRID_W).astype(np.float32)[:, None]
    ar = (row * inv).astype(np.float32)
    ac = (col * inv).astype(np.float32)
    cos64 = np.concatenate([np.cos(ar), np.cos(ar), np.cos(ac), np.cos(ac)], axis=1)
    sin64 = np.concatenate([-np.sin(ar), np.sin(ar), -np.sin(ac), np.sin(ac)], axis=1)
    cos = np.concatenate([np.ones((tm, 128), np.float32), np.tile(cos64, (1, 2)).astype(np.float32)], axis=0)
    sin = np.concatenate([np.zeros((tm, 128), np.float32), np.tile(sin64, (1, 2)).astype(np.float32)], axis=0)
    return jnp.asarray(cos), jnp.asarray(sin)


def _tile_meta(n_ctx, n_seq_dec, t_ctx, t_dec, tm):
    nct = n_ctx // tm
    per = t_dec // tm
    ndt = n_seq_dec * per
    idx = np.arange(nct + ndt)
    dec = idx >= nct
    di = np.maximum(idx - nct, 0)
    modrow = np.where(dec, 1 + di // per, 0)
    ropeblk = np.where(dec, 1 + di % per, 0)
    tiles_ctx = max(t_ctx // tm, 1)
    isstart = np.where(dec, di % per == 0, idx % tiles_ctx == 0)
    isend = np.where(dec, di % per == per - 1, idx % tiles_ctx == tiles_ctx - 1)
    as_i32 = lambda a: jnp.asarray(a.astype(np.int32))
    return as_i32(modrow), as_i32(ropeblk), as_i32(isstart), as_i32(isend)


def _block_diag(w):
    n, c, _ = w.shape
    out = jnp.zeros((n * c, n * c), w.dtype)
    for u in range(n):
        out = out.at[u * c:(u + 1) * c, u * c:(u + 1) * c].set(w[u])
    return out


def kernel(x_prompt, x_sample, c, cache_a_k, cache_a_v, cache_d_k, cache_d_v, state_lru, state_gdn, c_ctx, mod_w, mod_b, norm1_g, norm2_g, w_in, a_qn_g, a_kn_g, a_sink, lru_conv_w, lru_conv_b, lru_wr, lru_br, lru_wi, lru_bi, lru_lam, gdn_conv_w, gdn_a_log, gdn_dt_bias, gdn_norm_g, d_qn_g, d_kn_g, w_branch, w_merge, b_merge, w_out, mlp_w1, mlp_w2):
    batch, seq, d = x_prompt.shape
    dec_batch, dec_seq, _ = x_sample.shape
    depth = mod_w.shape[0]
    past = cache_a_k.shape[2]
    n_ctx = batch * seq
    n_dec = dec_batch * dec_seq
    tm = 256
    assert d == D_MODEL and seq % tm == 0 and dec_seq % tm == 0 and tm % seq == 0
    assert dec_batch + 1 <= 8 and n_ctx % dec_seq == 0

    cond8 = jnp.zeros((8, d), F32).at[0].set(c_ctx).at[1:1 + dec_batch].set(c)
    mod_all = _mod_call(cond8, mod_w, mod_b).reshape(depth, 8, 6, d)

    meta = _tile_meta(n_ctx, dec_batch, seq, dec_seq, tm)
    cos_t, sin_t = _rope_tables(dec_seq, tm)
    lane = np.arange(256)
    ones_bd = jnp.asarray((lane[:, None] // HEAD_DIM == lane[None, :] // HEAD_DIM).astype(np.float32)).astype(BF16)

    x = jnp.concatenate([x_prompt.reshape(n_ctx, d), x_sample.reshape(n_dec, d)], axis=0)
    zeros_lru = jnp.zeros((batch, 2, BRANCH_W), F32)
    zeros_gdn = jnp.zeros((batch, 2, CHUNK, 256), F32)

    new_ak, new_av, new_dk, new_dv, new_lru, new_gdn = [], [], [], [], [], []
    for l in range(depth):
        w_packed = jnp.concatenate([w_in[l][:, :2064], jnp.zeros((d, GDN_PAD), F32), w_in[l][:, 2064:]],
                                   axis=1).astype(BF16)
        pad128 = lambda v: jnp.zeros((1, 128), F32).at[0, :v.size].set(v.reshape(-1))
        vecs = (
            (jnp.tile(a_qn_g[l], 4) * (HEAD_DIM ** -0.5)).reshape(1, 256),
            jnp.tile(a_kn_g[l], 2).reshape(1, 128),
            (jnp.tile(d_qn_g[l], 4) * (HEAD_DIM ** -0.5)).reshape(1, 256),
            jnp.tile(d_kn_g[l], 2).reshape(1, 128),
            lru_conv_w[l], lru_conv_b[l].reshape(1, 256), gdn_conv_w[l],
            pad128(gdn_a_log[l]), pad128(gdn_dt_bias[l]),
        )
        g1 = norm1_g[l].reshape(1, d)
        g2 = norm2_g[l].reshape(1, d)
        mod_l = mod_all[l]

        (qa, ka, va, lx, lg, gq, gk, gv, gz, ggb, qd, kd, vd) = _inproj_call(
            x, meta, mod_l, g1, w_packed, cos_t, sin_t, vecs, ones_bd, tm=tm)

        sink = a_sink[l]
        oa_c, od_c = _attn_ctx_call(sink, qa, ka, va, qd, kd, vd, n_seq=batch, t=seq)
        cak = cache_a_k[:, l].reshape(dec_batch, past, 128)
        cav = cache_a_v[:, l].reshape(dec_batch, past, 128)
        cdk = cache_d_k[:, l].reshape(dec_batch, past, 128)
        cdv = cache_d_v[:, l].reshape(dec_batch, past, 128)
        oa_d = _attn_band_call(sink, qa, ka, va, cak, cav, n_seq=dec_batch, t=dec_seq, row0=n_ctx, tq=512)
        od_d = _attn_full_call(qd, kd, vd, cdk, cdv, n_seq=dec_batch, t=dec_seq, row0=n_ctx, tq=256)
        oa = jnp.concatenate([oa_c, oa_d], axis=0)
        od = jnp.concatenate([od_c, od_d], axis=0)

        w_lru = jnp.concatenate([_block_diag(lru_wr[l, 0]), _block_diag(lru_wi[l, 0]),
                                 _block_diag(lru_wr[l, 1]), _block_diag(lru_wi[l, 1])], axis=1).astype(BF16)
        b_lru = jnp.concatenate([lru_br[l, 0], lru_bi[l, 0], lru_br[l, 1], lru_bi[l, 1]]).reshape(1, 1024)
        ob_c, st_c = _lru_call(lx, lg, w_lru, b_lru, lru_lam[l], zeros_lru, n_seq=batch, t=seq, row0=0)
        ob_d, _ = _lru_call(lx, lg, w_lru, b_lru, lru_lam[l], state_lru[:, l], n_seq=dec_batch, t=dec_seq,
                            row0=n_ctx)
        ob = jnp.concatenate([ob_c, ob_d], axis=0)

        s0_d = state_gdn[:, l].transpose(0, 1, 3, 2, 4).reshape(dec_batch, 2, CHUNK, 256)
        ocs, sts = [], []
        for dd in range(2):
            o_c, s_c = _gdn_call(gq, gk, gv, ggb, zeros_gdn, d=dd, n_seq=batch, t=seq, row0=0, tt=seq)
            o_d, _ = _gdn_call(gq, gk, gv, ggb, s0_d, d=dd, n_seq=dec_batch, t=dec_seq, row0=n_ctx, tt=256)
            ocs.append(jnp.concatenate([o_c[0], o_d[0]], axis=0))
            sts.append(s_c)

        x = _merge_call(meta[0], x, mod_l, g1, oa, ob, ocs[0], ocs[1], gz, od,
                        jnp.tile(gdn_norm_g[l], 4).reshape(1, 256), ones_bd,
                        w_merge[l].astype(BF16), b_merge[l].reshape(1, -1), w_branch[l].astype(BF16),
                        w_out[l].astype(BF16), tm=tm)
        x = _mlp_call(meta[0], x, mod_l, g2, mlp_w1[l].astype(BF16), mlp_w2[l].astype(BF16), tm=tm)

        new_ak.append(ka[:n_ctx].reshape(batch, seq, 2, HEAD_DIM))
        new_av.append(va[:n_ctx].reshape(batch, seq, 2, HEAD_DIM))
        new_dk.append(kd[:n_ctx].reshape(batch, seq, 2, HEAD_DIM))
        new_dv.append(vd[:n_ctx].reshape(batch, seq, 2, HEAD_DIM))
        new_lru.append(st_c)
        s_fb = jnp.concatenate(sts, axis=1)
        new_gdn.append(s_fb.reshape(batch, 2, GDN_DK, GDN_H, GDN_DK).transpose(0, 1, 3, 2, 4))

    y_prompt = x[:n_ctx].reshape(batch, seq, d)
    y_sample = x[n_ctx:].reshape(dec_batch, dec_seq, d)
    stack = lambda ts: jnp.stack(ts, axis=1)
    return (y_prompt, y_sample, stack(new_ak), stack(new_av), stack(new_dk), stack(new_dv),
            stack(new_lru), stack(new_gdn))
```

```python
import functools

import numpy as np
import jax
import jax.numpy as jnp
from jax import lax
from jax.experimental import pallas as pl
from jax.experimental.pallas import tpu as pltpu

F32 = jnp.float32
BF16 = jnp.bfloat16
HIGHEST = lax.Precision.HIGHEST

D_MODEL = 1024
HEAD_DIM = 64
BRANCH_W = 256
N_BRANCH = 4
GRID_W = 64
WINDOW = 128
LRU_C = 8.0
CONV_W = 4
CONV_LEFT = 2
GDN_H = 4
GDN_DK = 64
CHUNK = 64
D_FF = 4 * D_MODEL
ROPE_BASE = 10000.0
EPS = 1e-6
NEG = -1e30
GDN_PAD = 112
IN_COLS_PACKED = 2688
HALO = 8

V7X_VMEM_LIMIT = 56 * 1024 * 1024


def _cparams(sem, vmem=V7X_VMEM_LIMIT):
    return pltpu.CompilerParams(dimension_semantics=sem, vmem_limit_bytes=vmem)


def _layer_spec(arr, l):
    nd = arr.ndim - 1
    return pl.BlockSpec((None,) + arr.shape[1:], lambda *_: (l,) + (0,) * nd)


def _bdot(a, b):
    return jnp.dot(a.astype(BF16), b.astype(BF16), preferred_element_type=F32)


def _split(x):
    hi = x.astype(BF16)
    lo = (x - hi.astype(F32)).astype(BF16)
    return hi, lo


def _split3(x):
    hi = x.astype(BF16)
    r = x - hi.astype(F32)
    mid = r.astype(BF16)
    lo = (r - mid.astype(F32)).astype(BF16)
    return hi, mid, lo


def _group_sum(x, ones_bd):
    hi, lo = _split(x)
    return (jnp.dot(hi, ones_bd, preferred_element_type=F32)
            + jnp.dot(lo, ones_bd, preferred_element_type=F32))


def _sigmoid(x):
    return 1.0 / (1.0 + jnp.exp(-x))


def _silu(x):
    return x * _sigmoid(x)


def _softplus(x):
    return jnp.maximum(x, 0.0) + jnp.log1p(jnp.exp(-jnp.abs(x)))


def _modulate(x, g, shift, scale):
    ms = jnp.mean(x * x, axis=-1, keepdims=True)
    return (x * lax.rsqrt(ms + EPS) * g) * (1.0 + scale) + shift


def _lane_iota(shape):
    return lax.broadcasted_iota(jnp.int32, shape, len(shape) - 1)


def _row_iota(shape):
    return lax.broadcasted_iota(jnp.int32, shape, len(shape) - 2)


def _mod_kernel(cond_ref, w_ref, b_ref, o_ref):
    c = cond_ref[...]
    o_ref[0] = jnp.dot(_silu(c), w_ref[0], precision=HIGHEST, preferred_element_type=F32) + b_ref[0]


def _mod_call(cond8, mod_w, mod_b):
    depth, d, n = mod_w.shape
    tn = 1536
    return pl.pallas_call(
        _mod_kernel,
        grid=(depth, n // tn),
        in_specs=[
            pl.BlockSpec((8, d), lambda l, j: (0, 0)),
            pl.BlockSpec((1, d, tn), lambda l, j: (l, 0, j)),
            pl.BlockSpec((1, 1, tn), lambda l, j: (l, 0, j)),
        ],
        out_specs=pl.BlockSpec((1, 8, tn), lambda l, j: (l, 0, j)),
        out_shape=jax.ShapeDtypeStruct((depth, 8, n), F32),
        compiler_params=_cparams(("parallel", "parallel")),
        name="mod_vectors",
    )(cond8, mod_w, mod_b.reshape(depth, 1, n))


def _rope(x, cos, sin):
    outs = []
    for j in range(x.shape[1] // 128):
        xb = x[:, 128 * j:128 * (j + 1)]
        lane = _lane_iota(xb.shape)
        sw = jnp.where((lane & 16) == 0, pltpu.roll(xb, 112, 1), pltpu.roll(xb, 16, 1))
        outs.append(xb * cos + sw * sin)
    return outs[0] if len(outs) == 1 else jnp.concatenate(outs, axis=1)


def _head_rms(x, gain, ones_bd):
    ms = _group_sum(x * x, ones_bd) * (1.0 / HEAD_DIM)
    return x * lax.rsqrt(ms + EPS) * gain


def _centred_conv(g, w, tm):
    rows = g.shape[0]
    acc = None
    for j in range(CONV_W):
        sh = (CONV_LEFT - j) % rows
        gj = g if sh == 0 else pltpu.roll(g, sh, 0)
        term = gj[HALO:HALO + tm] * w[j:j + 1, :]
        acc = term if acc is None else acc + term
    return acc


def _inproj_kernel(modrow_ref, ropeblk_ref, isstart_ref, isend_ref,
                   xp_ref, x_ref, xn_ref, mod_ref, g1_ref, w_ref, cos_ref, sin_ref,
                   aqg_ref, akg_ref, dqg_ref, dkg_ref, lcw_ref, lcb_ref, gcw_ref,
                   alog_ref, dtb_ref, ones_ref,
                   qa_ref, ka_ref, va_ref, lx_ref, lg_ref, gq_ref, gk_ref, gv_ref, gz_ref, ggb_ref,
                   qd_ref, kd_ref, vd_ref, *, tm):
    i = pl.program_id(0)
    xfull = jnp.concatenate([xp_ref[...], x_ref[...], xn_ref[...]], axis=0)
    h = _modulate(xfull, g1_ref[...], mod_ref[0, 0:1, :], mod_ref[0, 1:2, :])
    p = jnp.dot(h.astype(BF16), w_ref[...], preferred_element_type=F32)

    ones_bd = ones_ref[...]
    cos = cos_ref[...]
    sin = sin_ref[...]
    lo, hi = HALO, HALO + tm

    qa_ref[...] = _rope(_head_rms(p[lo:hi, 0:256], aqg_ref[...], ones_bd), cos, sin)
    ka_ref[...] = _rope(_head_rms(p[lo:hi, 256:384], akg_ref[...], ones_bd[:128, :128]), cos, sin)
    va_ref[...] = p[lo:hi, 384:512]
    qd_ref[...] = _rope(_head_rms(p[lo:hi, 2176:2432], dqg_ref[...], ones_bd), cos, sin)
    kd_ref[...] = _rope(_head_rms(p[lo:hi, 2432:2560], dkg_ref[...], ones_bd[:128, :128]), cos, sin)
    vd_ref[...] = p[lo:hi, 2560:2688]

    row = _row_iota((tm + 2 * HALO, 1))
    keep = jnp.logical_and(jnp.logical_or(row >= HALO, isstart_ref[i] == 0),
                           jnp.logical_or(row < HALO + tm, isend_ref[i] == 0))

    lxg = jnp.where(keep, p[:, 512:768], 0.0)
    lx_ref[...] = _centred_conv(lxg, lcw_ref[...], tm) + lcb_ref[...]
    lg_ref[...] = p[lo:hi, 768:1024]

    qkv = jnp.where(keep, p[:, 1024:1792], 0.0)
    qkv = _silu(_centred_conv(qkv, gcw_ref[...], tm))
    gq = qkv[:, 0:256]
    gk = qkv[:, 256:512]
    gq_ref[...] = gq * lax.rsqrt(_group_sum(gq * gq, ones_bd) + EPS) * (GDN_DK ** -0.5)
    gk_ref[...] = gk * lax.rsqrt(_group_sum(gk * gk, ones_bd) + EPS)
    gv_ref[...] = qkv[:, 512:768]
    gz_ref[...] = p[lo:hi, 1792:2048]
    ab = p[lo:hi, 2048:2176]
    g = -jnp.exp(alog_ref[...]) * _softplus(ab + dtb_ref[...])
    lane = _lane_iota(ab.shape)
    ggb_ref[...] = jnp.where(lane < 2 * GDN_H, g, _sigmoid(ab))


def _inproj_call(x, meta, mod_all, g1, w_packed, cos_t, sin_t, vecs, ones_bd, *, l, tm):
    n, d = x.shape
    nt = n // tm
    hb = tm // HALO
    last_hb = n // HALO - 1
    row_tile = lambda w: pl.BlockSpec((tm, w), lambda i, *_: (i, 0))
    in_specs = [
        pl.BlockSpec((HALO, d), lambda i, *_: (jnp.maximum(i * hb - 1, 0), 0)),
        row_tile(d),
        pl.BlockSpec((HALO, d), lambda i, *_: (jnp.minimum((i + 1) * hb, last_hb), 0)),
        pl.BlockSpec((None, 1, 6, d), lambda i, modrow, *_: (l, modrow[i], 0, 0)),
        _layer_spec(g1, l),
        _layer_spec(w_packed, l),
        pl.BlockSpec((tm, 128), lambda i, modrow, ropeblk, *_: (ropeblk[i], 0)),
        pl.BlockSpec((tm, 128), lambda i, modrow, ropeblk, *_: (ropeblk[i], 0)),
    ] + [_layer_spec(v, l) for v in vecs] + [pl.BlockSpec((256, 256), lambda i, *_: (0, 0))]
    widths = (256, 128, 128, 256, 256, 256, 256, 256, 256, 128, 256, 128, 128)
    grid_spec = pltpu.PrefetchScalarGridSpec(
        num_scalar_prefetch=4,
        grid=(nt,),
        in_specs=in_specs,
        out_specs=[row_tile(w) for w in widths],
    )
    return pl.pallas_call(
        functools.partial(_inproj_kernel, tm=tm),
        grid_spec=grid_spec,
        out_shape=[jax.ShapeDtypeStruct((n, w), F32) for w in widths],
        compiler_params=_cparams(("parallel",)),
        name="in_projection",
    )(*meta, x, x, x, mod_all, g1, w_packed, cos_t, sin_t, *vecs, ones_bd)


def _head_q(q, j, g):
    qj = q[:, 128 * j:128 * (j + 1)]
    lane = _lane_iota(qj.shape)
    sel = (lane < HEAD_DIM) if g == 0 else (lane >= HEAD_DIM)
    return jnp.where(sel, qj, 0.0).astype(BF16)


def _place_heads(res, j):
    r0 = res[0] if j == 0 else pltpu.roll(res[0], HEAD_DIM, 1)
    r1 = res[1] if j == 1 else pltpu.roll(res[1], HEAD_DIM, 1)
    lane = _lane_iota(r0.shape)
    return jnp.where(lane < HEAD_DIM, r0, r1)


def _attend(q, ks, vs, masks, sinks):
    ks_sw = [pltpu.roll(k, HEAD_DIM, 1) for k in ks]
    blocks = []
    for j in range(2):
        res = []
        for g in range(2):
            qm = _head_q(q, j, g)
            ss = []
            for k, ksw, mk in zip(ks, ks_sw, masks):
                kk = k if g == j else ksw
                s = lax.dot_general(qm, kk, (((1,), (1,)), ((), ())), preferred_element_type=F32)
                if mk is not None:
                    s = jnp.where(mk, s, NEG)
                ss.append(s)
            m = ss[0].max(axis=-1, keepdims=True)
            for s in ss[1:]:
                m = jnp.maximum(m, s.max(axis=-1, keepdims=True))
            if sinks is not None:
                m = jnp.maximum(m, sinks[2 * j + g])
            den = None
            acc = None
            for s, v in zip(ss, vs):
                e = jnp.exp(s - m)
                dsum = e.sum(axis=-1, keepdims=True)
                den = dsum if den is None else den + dsum
                pv = jnp.dot(e.astype(BF16), v, preferred_element_type=F32)
                acc = pv if acc is None else acc + pv
            if sinks is not None:
                den = den + jnp.exp(sinks[2 * j + g] - m)
            res.append(acc / den)
        blocks.append(_place_heads(res, j))
    return jnp.concatenate(blocks, axis=1)


def _bf(x):
    return x.astype(BF16)


def _attn_ctx_kernel(sink_ref, qa_ref, ka_ref, va_ref, qd_ref, kd_ref, vd_ref, oa_ref, od_ref, *, l):
    sinks = [sink_ref[4 * l + u] for u in range(4)]
    oa_ref[...] = _attend(qa_ref[...], [_bf(ka_ref[...])], [_bf(va_ref[...])], [None], sinks)
    od_ref[...] = _attend(qd_ref[...], [_bf(kd_ref[...])], [_bf(vd_ref[...])], [None], None)


def _attn_ctx_call(sink, qa, ka, va, qd, kd, vd, *, l, n_seq, t):
    blk = lambda w: pl.BlockSpec((t, w), lambda b, *_: (b, 0))
    grid_spec = pltpu.PrefetchScalarGridSpec(
        num_scalar_prefetch=1, grid=(n_seq,),
        in_specs=[blk(256), blk(128), blk(128), blk(256), blk(128), blk(128)],
        out_specs=[blk(256), blk(256)])
    return pl.pallas_call(
        functools.partial(_attn_ctx_kernel, l=l), grid_spec=grid_spec,
        out_shape=[jax.ShapeDtypeStruct((n_seq * t, 256), F32)] * 2,
        compiler_params=_cparams(("parallel",)),
        name="attn_context",
    )(sink, qa, ka, va, qd, kd, vd)


def _attn_band_kernel(sink_ref, q_ref, k_ref, v_ref, ck_ref, cv_ref, o_ref, *, l, tq, t):
    i = pl.program_id(1)
    start = pl.multiple_of(i * tq, tq)
    prev = pl.multiple_of(jnp.maximum(start - WINDOW, 0), WINDOW)
    nxt = pl.multiple_of(jnp.minimum(start + tq, t - WINDOW), WINDOW)
    k_loc = jnp.concatenate([k_ref[pl.ds(prev, WINDOW), :], k_ref[pl.ds(start, tq), :],
                             k_ref[pl.ds(nxt, WINDOW), :]], axis=0)
    v_loc = jnp.concatenate([v_ref[pl.ds(prev, WINDOW), :], v_ref[pl.ds(start, tq), :],
                             v_ref[pl.ds(nxt, WINDOW), :]], axis=0)
    shape = (tq, tq + 2 * WINDOW)
    qpos = start + _row_iota(shape)
    kpos = start - WINDOW + _lane_iota(shape)
    mask = (kpos >= 0) & (kpos < t) & (jnp.abs(qpos - kpos) <= WINDOW)
    sinks = [sink_ref[4 * l + u] for u in range(4)]
    o_ref[...] = _attend(q_ref[...], [_bf(ck_ref[0]), _bf(k_loc)], [_bf(cv_ref[0]), _bf(v_loc)],
                         [None, mask], sinks)


def _cache_spec(c, l):
    return pl.BlockSpec((1, None) + c.shape[2:], lambda b, i, *_: (b, l, 0, 0))


def _attn_band_call(sink, q, k, v, ck, cv, *, l, n_seq, t, row0, tq):
    nq = t // tq
    qb0 = row0 // tq
    sb0 = row0 // t
    grid_spec = pltpu.PrefetchScalarGridSpec(
        num_scalar_prefetch=1, grid=(n_seq, nq),
        in_specs=[
            pl.BlockSpec((tq, 256), lambda b, i, *_: (qb0 + b * nq + i, 0)),
            pl.BlockSpec((t, 128), lambda b, i, *_: (sb0 + b, 0)),
            pl.BlockSpec((t, 128), lambda b, i, *_: (sb0 + b, 0)),
            _cache_spec(ck, l), _cache_spec(cv, l),
        ],
        out_specs=pl.BlockSpec((tq, 256), lambda b, i, *_: (b * nq + i, 0)))
    return pl.pallas_call(
        functools.partial(_attn_band_kernel, l=l, tq=tq, t=t), grid_spec=grid_spec,
        out_shape=jax.ShapeDtypeStruct((n_seq * t, 256), F32),
        compiler_params=_cparams(("parallel", "parallel")),
        name="attn_banded",
    )(sink, q, k, v, ck, cv)


def _attn_full_kernel(q_ref, k_ref, v_ref, ck_ref, cv_ref, o_ref):
    o_ref[...] = _attend(q_ref[...], [_bf(ck_ref[0]), _bf(k_ref[...])], [_bf(cv_ref[0]), _bf(v_ref[...])],
                         [None, None], None)


def _attn_full_call(q, k, v, ck, cv, *, l, n_seq, t, row0, tq):
    nq = t // tq
    qb0 = row0 // tq
    sb0 = row0 // t
    return pl.pallas_call(
        _attn_full_kernel, grid=(n_seq, nq),
        in_specs=[
            pl.BlockSpec((tq, 256), lambda b, i: (qb0 + b * nq + i, 0)),
            pl.BlockSpec((t, 128), lambda b, i: (sb0 + b, 0)),
            pl.BlockSpec((t, 128), lambda b, i: (sb0 + b, 0)),
            _cache_spec(ck, l), _cache_spec(cv, l),
        ],
        out_specs=pl.BlockSpec((tq, 256), lambda b, i: (b * nq + i, 0)),
        out_shape=jax.ShapeDtypeStruct((n_seq * t, 256), F32),
        compiler_params=_cparams(("parallel", "parallel")),
        name="attn_full",
    )(q, k, v, ck, cv)


def _gelu_tanh(x):
    return 0.5 * x * (1.0 + jnp.tanh(0.7978845608028654 * (x + 0.044715 * (x * x * x))))


def _lru_kernel(lx_ref, lg_ref, w_ref, b_ref, lam_ref, h0_ref, y_ref, st_ref,
                hf_ref, hb_ref, af_ref, bf_ref, ab_ref, bb_ref, *, t, chunk):
    nc = t // chunk
    ng = chunk // 8
    sp = _softplus(-lam_ref[...])
    row8 = _row_iota((8, BRANCH_W))

    def gates(xc, d, a_ref, b2_ref):
        pre = _bdot(xc, w_ref[:, 512 * d:512 * (d + 1)]) + b_ref[:, 512 * d:512 * (d + 1)]
        r = _sigmoid(pre[:, 0:256])
        ig = _sigmoid(pre[:, 256:512])
        a = jnp.exp((-LRU_C) * r * sp[d:d + 1, :])
        a_ref[...] = a
        b2_ref[...] = jnp.sqrt(1.0 - a * a) * (ig * xc)

    def local_scan(a8, b8, reverse):
        for s in (1, 2, 4):
            sh = (8 - s) if reverse else s
            m = (row8 < 8 - s) if reverse else (row8 >= s)
            a_sh = pltpu.roll(a8, sh, 0)
            b_sh = pltpu.roll(b8, sh, 0)
            b8 = jnp.where(m, a8 * b_sh + b8, b8)
            a8 = jnp.where(m, a8 * a_sh, a8)
        return a8, b8

    def chunk_body(cc, carry):
        rf = pl.multiple_of(cc * chunk, chunk)
        rb = pl.multiple_of((nc - 1 - cc) * chunk, chunk)
        gates(lx_ref[pl.ds(rf, chunk), :], 0, af_ref, bf_ref)
        gates(lx_ref[pl.ds(rb, chunk), :], 1, ab_ref, bb_ref)

        def grp(gg, c2):
            hf, hb = c2
            gf = pl.multiple_of(gg * 8, 8)
            gb = pl.multiple_of((ng - 1 - gg) * 8, 8)
            a8, b8 = local_scan(af_ref[pl.ds(gf, 8), :], bf_ref[pl.ds(gf, 8), :], False)
            h8 = b8 + a8 * hf
            hf_ref[pl.ds(rf + gf, 8), :] = h8
            a8, b8 = local_scan(ab_ref[pl.ds(gb, 8), :], bb_ref[pl.ds(gb, 8), :], True)
            g8 = b8 + a8 * hb
            hb_ref[pl.ds(rb + gb, 8), :] = g8
            return (jnp.broadcast_to(h8[7:8, :], (8, BRANCH_W)), jnp.broadcast_to(g8[0:1, :], (8, BRANCH_W)))

        return lax.fori_loop(0, ng, grp, carry, unroll=4)

    init = (jnp.broadcast_to(h0_ref[0, 0:1, :], (8, BRANCH_W)), jnp.broadcast_to(h0_ref[0, 1:2, :], (8, BRANCH_W)))
    hf, hb = lax.fori_loop(0, nc, chunk_body, init)
    st_ref[0, 0:1, :] = hf[0:1, :]
    st_ref[0, 1:2, :] = hb[0:1, :]

    def combine(c, _):
        rows = pl.ds(pl.multiple_of(c * chunk, chunk), chunk)
        y_ref[rows, :] = (hf_ref[rows, :] + hb_ref[rows, :]) * _gelu_tanh(lg_ref[rows, :])
        return 0

    lax.fori_loop(0, nc, combine, 0)


def _lru_call(lx, lg, w, b, lam, h0, *, l, l_state, n_seq, t, row0):
    sb0 = row0 // t
    chunk = min(t, 256)
    seq = lambda: pl.BlockSpec((t, BRANCH_W), lambda s: (sb0 + s, 0))
    return pl.pallas_call(
        functools.partial(_lru_kernel, t=t, chunk=chunk),
        grid=(n_seq,),
        in_specs=[seq(), seq(), _layer_spec(w, l), _layer_spec(b, l), _layer_spec(lam, l),
                  pl.BlockSpec((1, None, 2, BRANCH_W), lambda s: (s, l_state, 0, 0))],
        out_specs=[pl.BlockSpec((t, BRANCH_W), lambda s: (s, 0)),
                   pl.BlockSpec((1, 2, BRANCH_W), lambda s: (s, 0, 0))],
        out_shape=[jax.ShapeDtypeStruct((n_seq * t, BRANCH_W), F32),
                   jax.ShapeDtypeStruct((n_seq, 2, BRANCH_W), F32)],
        scratch_shapes=[pltpu.VMEM((t, BRANCH_W), F32), pltpu.VMEM((t, BRANCH_W), F32)]
                       + [pltpu.VMEM((chunk, BRANCH_W), F32)] * 4,
        compiler_params=_cparams(("parallel",)),
        name="rglru",
    )(lx, lg, w, b, lam, h0)


def _rows_dot_exact(x, w01):
    r = x.shape[0]
    out = jnp.dot(jnp.concatenate(_split3(x), axis=0), w01, preferred_element_type=F32)
    return out[0:r] + out[r:2 * r] + out[2 * r:3 * r]


def _dot01_exact(m01, y):
    return sum(jnp.dot(m01, part, preferred_element_type=F32) for part in _split3(y))


def _gdn_kernel(q_ref, k_ref, v_ref, ggb_ref, s0_ref, o_ref, sT_ref, s_ref, *, d, reverse, n_par, n_chunk, n_tile):
    ti = pl.program_id(1)
    w4 = GDN_H * GDN_DK
    n_all = n_par * n_chunk
    tt = n_all * CHUNK
    blockmask = (_row_iota((w4, w4)) // CHUNK) == (_lane_iota((w4, w4)) // CHUNK)

    def expand_rows(y):
        yt = jnp.concatenate([y] * GDN_H, axis=0)
        zero = jnp.zeros((), y.dtype)
        parts = [jnp.where(blockmask, yt[:, w4 * u:w4 * (u + 1)], zero) for u in range(y.shape[1] // w4)]
        return parts[0] if len(parts) == 1 else jnp.concatenate(parts, axis=1)

    def heads_dot3(lhs, y):
        r = lhs.shape[0]
        lh, ll = _split(lhs)
        yh, yl = _split(y)
        out = jnp.dot(jnp.concatenate([lh, ll], axis=0), expand_rows(yh), preferred_element_type=F32)
        return out[0:r] + out[r:2 * r] + jnp.dot(lh, expand_rows(yl), preferred_element_type=F32)

    @pl.when(ti == 0)
    def _():
        for p in range(n_par):
            s_ref[p] = expand_rows(s0_ref[p, 0])

    ri = _row_iota((CHUNK, w4))
    cj = _lane_iota((CHUNK, w4)) % CHUNK
    incl = (cj >= ri) if reverse else (cj <= ri)
    strict = (cj > ri) if reverse else (cj < ri)
    eye_sbs = (cj == ri)
    rep = eye_sbs.astype(BF16)

    er = _row_iota((128, 2 * w4))
    el = _lane_iota((128, 2 * w4))
    e_gb = (er == jnp.where(el < w4, d * GDN_H, 2 * GDN_H + d * GDN_H) + (el % w4) // CHUNK).astype(BF16)
    gbe = _rows_dot_exact(ggb_ref[...], e_gb)
    g_all = gbe[:, 0:w4]
    beta_all = gbe[:, w4:2 * w4]
    tr = _row_iota((tt, tt))
    tc = _lane_iota((tt, tt))
    same_chunk = (tr // CHUNK) == (tc // CHUNK)
    tri_bd = (same_chunk & ((tc >= tr) if reverse else (tc <= tr))).astype(BF16)
    ones_bd = same_chunk.astype(BF16)
    gc_all = _dot01_exact(tri_bd, g_all)
    diag_sel = (_row_iota((tt, w4)) % CHUNK) == (_lane_iota((tt, w4)) % CHUNK)
    gct_all = _dot01_exact(ones_bd, jnp.where(diag_sel, gc_all, 0.0))

    cs = range(n_all)
    rows = [slice(c * CHUNK, (c + 1) * CHUNK) for c in cs]
    q = [q_ref[r, :] for r in rows]
    k = [k_ref[r, :] for r in rows]
    v = [v_ref[r, :] for r in rows]
    beta = [beta_all[r, :] for r in rows]
    gc = [gc_all[r, :] for r in rows]
    decay = [jnp.where(incl, jnp.exp(jnp.where(incl, gc[c] - gct_all[rows[c], :], 0.0)), 0.0) for c in cs]
    kb = [k[c].astype(BF16) for c in cs]
    w_k = [jnp.where(blockmask,
                     lax.dot_general(kb[c], rep, (((0,), (0,)), ((), ())), preferred_element_type=F32),
                     0.0).astype(BF16) for c in cs]
    kq = [jnp.dot(jnp.concatenate([kb[c], q[c].astype(BF16)], axis=0), w_k[c], preferred_element_type=F32)
          for c in cs]
    qk = [(kq[c][CHUNK:2 * CHUNK] * decay[c]).astype(BF16) for c in cs]
    a = [jnp.where(strict, beta[c] * kq[c][0:CHUNK] * decay[c], 0.0) for c in cs]
    t_inv = [jnp.where(eye_sbs, 1.0, 0.0) - a[c] for c in cs]
    pw = [heads_dot3(a[c], a[c]) for c in cs]
    for _ in range(4):
        both = [heads_dot3(jnp.concatenate([t_inv[c], pw[c]], axis=0), pw[c]) for c in cs]
        t_inv = [t_inv[c] + both[c][0:CHUNK] for c in cs]
        pw = [both[c][CHUNK:2 * CHUNK] for c in cs]
    t_inv = [t_inv[c] + heads_dot3(t_inv[c], pw[c]) for c in cs]
    egc = [jnp.exp(gc[c]) for c in cs]
    sol = [heads_dot3(t_inv[c], jnp.concatenate([v[c] * beta[c], k[c] * (beta[c] * egc[c])], axis=1)) for c in cs]
    g_last = [gc[c][0:1, :] if reverse else gc[c][CHUNK - 1:CHUNK, :] for c in cs]
    wq = [jnp.concatenate([sol[c][:, w4:2 * w4], q[c] * egc[c]], axis=0).astype(BF16) for c in cs]
    kdec = [(k[c] * jnp.exp(g_last[c] - gc[c])).astype(BF16) for c in cs]

    for cc in range(n_chunk):
        for p in range(n_par):
            c = p * n_chunk + ((n_chunk - 1 - cc) if reverse else cc)
            s = s_ref[p]
            ws_qs = jnp.dot(wq[c], s.astype(BF16), preferred_element_type=F32)
            v_new = sol[c][:, 0:w4] - ws_qs[0:CHUNK]
            vb = v_new.astype(BF16)
            o_ref[rows[c], :] = (
                ws_qs[CHUNK:2 * CHUNK] + jnp.dot(qk[c], expand_rows(vb), preferred_element_type=F32))
            upd = lax.dot_general(kdec[c], vb, (((0,), (0,)), ((), ())), preferred_element_type=F32)
            s_ref[p] = s * jnp.exp(g_last[c]) + jnp.where(blockmask, upd, 0.0)

    @pl.when(ti == n_tile - 1)
    def _():
        for p in range(n_par):
            s = s_ref[p]
            sT_ref[p, 0] = s[0:64] + s[64:128] + s[128:192] + s[192:256]


def _gdn_call(q, k, v, ggb, s0, *, d, l_state, n_seq, t, row0, tt, n_par):
    reverse = d == 1
    n_tile = t // tt
    assert n_par == 1 or n_tile == 1
    b0 = row0 // (tt * n_par)
    tidx = (lambda i: n_tile - 1 - i) if reverse else (lambda i: i)
    blk = lambda w: pl.BlockSpec((tt * n_par, w), lambda s, i: (b0 + s * n_tile + tidx(i), 0))
    return pl.pallas_call(
        functools.partial(_gdn_kernel, d=d, reverse=reverse, n_par=n_par, n_chunk=tt // CHUNK, n_tile=n_tile),
        grid=(n_seq // n_par, n_tile),
        in_specs=[blk(256), blk(256), blk(256), blk(128),
                  pl.BlockSpec((n_par, None, 1, CHUNK, 256), lambda s, i: (s, l_state, d, 0, 0))],
        out_specs=[pl.BlockSpec((tt * n_par, 256), lambda s, i: (s * n_tile + tidx(i), 0)),
                   pl.BlockSpec((n_par, 1, CHUNK, 256), lambda s, i: (s, 0, 0, 0))],
        out_shape=[jax.ShapeDtypeStruct((n_seq * t, 256), F32),
                   jax.ShapeDtypeStruct((n_seq, 1, CHUNK, 256), F32)],
        scratch_shapes=[pltpu.VMEM((n_par, 256, 256), F32)],
        compiler_params=_cparams(("parallel", "arbitrary")),
        name="gdn_bwd" if reverse else "gdn_fwd",
    )(q, k, v, ggb, s0)


def _merge_kernel(modrow_ref, x_ref, mod_ref, g1_ref,
                  oac_ref, oad_ref, obc_ref, obd_ref, ofc_ref, ofd_ref, orc_ref, ord_ref, odc_ref, odd_ref,
                  gz_ref, gng_ref, ones_ref, wm_ref, bm_ref, wb_ref, wo_ref, y_ref, *, n_ctx_tiles):
    is_ctx = pl.program_id(0) < n_ctx_tiles
    pick = lambda c_ref, d_ref: jnp.where(is_ctx, c_ref[...], d_ref[...])
    x = x_ref[...]
    h = _modulate(x, g1_ref[...], mod_ref[0, 0:1, :], mod_ref[0, 1:2, :]).astype(BF16)
    oc = pick(ofc_ref, ofd_ref) + pick(orc_ref, ord_ref)
    ms = _group_sum(oc * oc, ones_ref[...]) * (1.0 / GDN_DK)
    oc = (oc * lax.rsqrt(ms + EPS) * gng_ref[...]) * _silu(gz_ref[...])
    branches = (pick(oac_ref, oad_ref), pick(obc_ref, obd_ref), oc, pick(odc_ref, odd_ref))
    acc = None
    for m in range(N_BRANCH):
        cols = slice(D_MODEL * m, D_MODEL * (m + 1))
        gate = _sigmoid(jnp.dot(h, wm_ref[:, cols], preferred_element_type=F32) + bm_ref[:, cols])
        term = gate * jnp.dot(branches[m].astype(BF16), wb_ref[m], preferred_element_type=F32)
        acc = term if acc is None else acc + term
    y = jnp.dot(acc.astype(BF16), wo_ref[...], preferred_element_type=F32)
    y_ref[...] = x + mod_ref[0, 2:3, :] * y


def _merge_call(modrow, x, mod_all, g1, pairs, gz, gng, ones_bd, wm, bm, wb, wo, *, l, tm, n_ctx):
    n, d = x.shape
    nct = n_ctx // tm
    row_tile = lambda w: pl.BlockSpec((tm, w), lambda i, *_: (i, 0))
    pair_specs = []
    for _ in pairs:
        pair_specs.append(pl.BlockSpec((tm, 256), lambda i, *_: (jnp.minimum(i, nct - 1), 0)))
        pair_specs.append(pl.BlockSpec((tm, 256), lambda i, *_: (jnp.maximum(i - nct, 0), 0)))
    grid_spec = pltpu.PrefetchScalarGridSpec(
        num_scalar_prefetch=1, grid=(n // tm,),
        in_specs=[row_tile(d), pl.BlockSpec((None, 1, 6, d), lambda i, modrow: (l, modrow[i], 0, 0)),
                  _layer_spec(g1, l)] + pair_specs
                 + [row_tile(256), _layer_spec(gng, l), pl.BlockSpec((256, 256), lambda i, *_: (0, 0)),
                    _layer_spec(wm, l), _layer_spec(bm, l), _layer_spec(wb, l), _layer_spec(wo, l)],
        out_specs=row_tile(d))
    flat = [a for pr in pairs for a in pr]
    return pl.pallas_call(
        functools.partial(_merge_kernel, n_ctx_tiles=nct), grid_spec=grid_spec,
        out_shape=jax.ShapeDtypeStruct((n, d), F32),
        compiler_params=_cparams(("parallel",)),
        name="branch_merge",
    )(modrow, x, mod_all, g1, *flat, gz, gng, ones_bd, wm, bm, wb, wo)


def _mlp_kernel(modrow_ref, x_ref, mod_ref, g2_ref, w1_ref, w2_ref, y_ref):
    x = x_ref[...]
    h = _modulate(x, g2_ref[...], mod_ref[0, 3:4, :], mod_ref[0, 4:5, :]).astype(BF16)
    acc = None
    for j in range(D_FF // D_MODEL):
        cols = slice(D_MODEL * j, D_MODEL * (j + 1))
        a = jnp.maximum(jnp.dot(h, w1_ref[:, cols], preferred_element_type=F32), 0.0)
        term = jnp.dot((a * a).astype(BF16), w2_ref[cols, :], preferred_element_type=F32)
        acc = term if acc is None else acc + term
    y_ref[...] = x + mod_ref[0, 5:6, :] * acc


def _mlp_call(modrow, x, mod_all, g2, w1, w2, *, l, tm):
    n, d = x.shape
    row_tile = pl.BlockSpec((tm, d), lambda i, *_: (i, 0))
    grid_spec = pltpu.PrefetchScalarGridSpec(
        num_scalar_prefetch=1, grid=(n // tm,),
        in_specs=[row_tile, pl.BlockSpec((None, 1, 6, d), lambda i, modrow: (l, modrow[i], 0, 0)),
                  _layer_spec(g2, l), _layer_spec(w1, l), _layer_spec(w2, l)],
        out_specs=row_tile)
    return pl.pallas_call(
        _mlp_kernel, grid_spec=grid_spec,
        out_shape=jax.ShapeDtypeStruct((n, d), F32),
        compiler_params=_cparams(("parallel",)),
        name="mlp",
    )(modrow, x, mod_all, g2, w1, w2)


def _rope_tables(t, tm):
    n_freq = HEAD_DIM // 4
    inv = np.float32(ROPE_BASE) ** (-np.arange(n_freq, dtype=np.float32) / np.float32(n_freq))
    pos = np.arange(t)
    row = (pos // GRID_W).astype(np.float32)[:, None]
    col = (pos % GRID_W).astype(np.float32)[:, None]
    ar = (row * inv).astype(np.float32)
    ac = (col * inv).astype(np.float32)
    cos64 = np.concatenate([np.cos(ar), np.cos(ar), np.cos(ac), np.cos(ac)], axis=1)
    sin64 = np.concatenate([-np.sin(ar), np.sin(ar), -np.sin(ac), np.sin(ac)], axis=1)
    cos = np.concatenate([np.ones((tm, 128), np.float32), np.tile(cos64, (1, 2)).astype(np.float32)], axis=0)
    sin = np.concatenate([np.zeros((tm, 128), np.float32), np.tile(sin64, (1, 2)).astype(np.float32)], axis=0)
    return jnp.asarray(cos), jnp.asarray(sin)


def _tile_meta(n_ctx, n_seq_dec, t_ctx, t_dec, tm):
    nct = n_ctx // tm
    per = t_dec // tm
    ndt = n_seq_dec * per
    idx = np.arange(nct + ndt)
    dec = idx >= nct
    di = np.maximum(idx - nct, 0)
    modrow = np.where(dec, 1 + di // per, 0)
    ropeblk = np.where(dec, 1 + di % per, 0)
    tiles_ctx = max(t_ctx // tm, 1)
    isstart = np.where(dec, di % per == 0, idx % tiles_ctx == 0)
    isend = np.where(dec, di % per == per - 1, idx % tiles_ctx == tiles_ctx - 1)
    as_i32 = lambda a: jnp.asarray(a.astype(np.int32))
    return as_i32(modrow), as_i32(ropeblk), as_i32(isstart), as_i32(isend)


def _block_diag_gates(w):
    depth, two, n, c, _ = w.shape
    eye = jnp.eye(n, dtype=w.dtype)
    return jnp.einsum("ldnij,nm->ldnimj", w, eye).reshape(depth, two, n * c, n * c)


def kernel(x_prompt, x_sample, c, cache_a_k, cache_a_v, cache_d_k, cache_d_v, state_lru, state_gdn, c_ctx, mod_w, mod_b, norm1_g, norm2_g, w_in, a_qn_g, a_kn_g, a_sink, lru_conv_w, lru_conv_b, lru_wr, lru_br, lru_wi, lru_bi, lru_lam, gdn_conv_w, gdn_a_log, gdn_dt_bias, gdn_norm_g, d_qn_g, d_kn_g, w_branch, w_merge, b_merge, w_out, mlp_w1, mlp_w2):
    batch, seq, d = x_prompt.shape
    dec_batch, dec_seq, _ = x_sample.shape
    depth = mod_w.shape[0]
    past = cache_a_k.shape[2]
    n_ctx = batch * seq
    n_dec = dec_batch * dec_seq
    tm = 256
    gdn_par = 2 if batch % 2 == 0 else 1
    gdn_tt = 512 if dec_seq % 512 == 0 else 256
    assert d == D_MODEL and seq % tm == 0 and dec_seq % tm == 0 and tm % seq == 0
    assert dec_batch + 1 <= 8 and n_ctx % dec_seq == 0

    cond8 = jnp.zeros((8, d), F32).at[0].set(c_ctx).at[1:1 + dec_batch].set(c)
    mod_all = _mod_call(cond8, mod_w, mod_b).reshape(depth, 8, 6, d)

    meta = _tile_meta(n_ctx, dec_batch, seq, dec_seq, tm)
    cos_t, sin_t = _rope_tables(dec_seq, tm)
    lane = np.arange(256)
    ones_bd = jnp.asarray((lane[:, None] // HEAD_DIM == lane[None, :] // HEAD_DIM).astype(np.float32)).astype(BF16)

    w_packed = jnp.concatenate([w_in[:, :, :2064], jnp.zeros((depth, d, GDN_PAD), F32), w_in[:, :, 2064:]],
                               axis=2).astype(BF16)
    pad128 = lambda v: jnp.pad(v.reshape(depth, 1, -1), ((0, 0), (0, 0), (0, 128 - v[0].size)))
    vecs = (
        (jnp.tile(a_qn_g, (1, 4)) * (HEAD_DIM ** -0.5))[:, None, :],
        jnp.tile(a_kn_g, (1, 2))[:, None, :],
        (jnp.tile(d_qn_g, (1, 4)) * (HEAD_DIM ** -0.5))[:, None, :],
        jnp.tile(d_kn_g, (1, 2))[:, None, :],
        lru_conv_w, lru_conv_b[:, None, :], gdn_conv_w,
        pad128(gdn_a_log), pad128(gdn_dt_bias),
    )
    g1 = norm1_g[:, None, :]
    g2 = norm2_g[:, None, :]
    wr_bd = _block_diag_gates(lru_wr)
    wi_bd = _block_diag_gates(lru_wi)
    w_lru = jnp.concatenate([wr_bd[:, 0], wi_bd[:, 0], wr_bd[:, 1], wi_bd[:, 1]], axis=-1).astype(BF16)
    b_lru = jnp.concatenate([lru_br[:, 0], lru_bi[:, 0], lru_br[:, 1], lru_bi[:, 1]], axis=-1)[:, None, :]
    gng = jnp.tile(gdn_norm_g, (1, 4))[:, None, :]
    wm = w_merge.astype(BF16)
    bm = b_merge[:, None, :]
    wb = w_branch.astype(BF16)
    wo = w_out.astype(BF16)
    w1 = mlp_w1.astype(BF16)
    w2 = mlp_w2.astype(BF16)
    sink = a_sink.reshape(-1)
    caches = [t.reshape(dec_batch, depth, past, 2 * HEAD_DIM) for t in (cache_a_k, cache_a_v, cache_d_k, cache_d_v)]
    s0_dec = state_gdn.transpose(0, 1, 2, 4, 3, 5).reshape(dec_batch, depth, 2, CHUNK, 256)
    zeros_lru = jnp.zeros((batch, 1, 2, BRANCH_W), F32)
    zeros_gdn = jnp.zeros((batch, 1, 2, CHUNK, 256), F32)

    x = jnp.concatenate([x_prompt.reshape(n_ctx, d), x_sample.reshape(n_dec, d)], axis=0)

    kv_ctx, lru_states, gdn_states = [], [], []
    for l in range(depth):
        (qa, ka, va, lx, lg, gq, gk, gv, gz, ggb, qd, kd, vd) = _inproj_call(
            x, meta, mod_all, g1, w_packed, cos_t, sin_t, vecs, ones_bd, l=l, tm=tm)

        oa_c, od_c = _attn_ctx_call(sink, qa, ka, va, qd, kd, vd, l=l, n_seq=batch, t=seq)
        oa_d = _attn_band_call(sink, qa, ka, va, caches[0], caches[1], l=l, n_seq=dec_batch, t=dec_seq,
                               row0=n_ctx, tq=512)
        od_d = _attn_full_call(qd, kd, vd, caches[2], caches[3], l=l, n_seq=dec_batch, t=dec_seq,
                               row0=n_ctx, tq=256)

        ob_c, st_c = _lru_call(lx, lg, w_lru, b_lru, lru_lam, zeros_lru, l=l, l_state=0, n_seq=batch, t=seq,
                               row0=0)
        ob_d, _ = _lru_call(lx, lg, w_lru, b_lru, lru_lam, state_lru, l=l, l_state=l, n_seq=dec_batch,
                            t=dec_seq, row0=n_ctx)

        oc_pairs, sts = [], []
        for dd in range(2):
            o_c, s_c = _gdn_call(gq, gk, gv, ggb, zeros_gdn, d=dd, l_state=0, n_seq=batch, t=seq, row0=0,
                                 tt=seq, n_par=gdn_par)
            o_d, _ = _gdn_call(gq, gk, gv, ggb, s0_dec, d=dd, l_state=l, n_seq=dec_batch, t=dec_seq,
                               row0=n_ctx, tt=gdn_tt, n_par=1)
            oc_pairs.append((o_c, o_d))
            sts.append(s_c)

        x = _merge_call(meta[0], x, mod_all, g1,
                        [(oa_c, oa_d), (ob_c, ob_d), oc_pairs[0], oc_pairs[1], (od_c, od_d)],
                        gz, gng, ones_bd, wm, bm, wb, wo, l=l, tm=tm, n_ctx=n_ctx)
        x = _mlp_call(meta[0], x, mod_all, g2, w1, w2, l=l, tm=tm)

        kv_ctx.append([t[:n_ctx] for t in (ka, va, kd, vd)])
        lru_states.append(st_c)
        gdn_states.append(jnp.concatenate(sts, axis=1))

    y_prompt = x[:n_ctx].reshape(batch, seq, d)
    y_sample = x[n_ctx:].reshape(dec_batch, dec_seq, d)
    new_kv = [jnp.stack([kv_ctx[l][u].reshape(batch, seq, 2, HEAD_DIM) for l in range(depth)], axis=1)
              for u in range(4)]
    new_lru = jnp.stack(lru_states, axis=1)
    new_gdn = (jnp.stack(gdn_states, axis=1).reshape(batch, depth, 2, GDN_DK, GDN_H, GDN_DK)
               .transpose(0, 1, 2, 4, 3, 5))
    return (y_prompt, y_sample, new_kv[0], new_kv[1], new_kv[2], new_kv[3], new_lru, new_gdn)
```

```python
import functools

import numpy as np
import jax
import jax.numpy as jnp
from jax import lax
from jax.experimental import pallas as pl
from jax.experimental.pallas import tpu as pltpu

F32 = jnp.float32
BF16 = jnp.bfloat16
HIGHEST = lax.Precision.HIGHEST

D_MODEL = 1024
HEAD_DIM = 64
BRANCH_W = 256
N_BRANCH = 4
GRID_W = 64
WINDOW = 128
LRU_C = 8.0
CONV_W = 4
CONV_LEFT = 2
GDN_H = 4
GDN_DK = 64
CHUNK = 64
D_FF = 4 * D_MODEL
ROPE_BASE = 10000.0
EPS = 1e-6
NEG = -1e30
LOG2E = 1.4426950408889634
Q_SCALE = HEAD_DIM ** -0.5 * LOG2E
GDN_PAD = 112
IN_COLS_PACKED = 2688
HALO = 8
N_PRECISE = 4

V7X_VMEM_LIMIT = 56 * 1024 * 1024


def _cparams(sem, vmem=V7X_VMEM_LIMIT):
    return pltpu.CompilerParams(dimension_semantics=sem, vmem_limit_bytes=vmem)


def _layer_spec(arr, l):
    nd = arr.ndim - 1
    return pl.BlockSpec((None,) + arr.shape[1:], lambda *_: (l,) + (0,) * nd)


def _bdot(a, b):
    return jnp.dot(a.astype(BF16), b.astype(BF16), preferred_element_type=F32)


def _split(x):
    hi = x.astype(BF16)
    lo = (x - hi.astype(F32)).astype(BF16)
    return hi, lo


def _split3(x):
    hi = x.astype(BF16)
    r = x - hi.astype(F32)
    mid = r.astype(BF16)
    lo = (r - mid.astype(F32)).astype(BF16)
    return hi, mid, lo


def _group_sum(x, ones_bd):
    hi, lo = _split(x)
    return (jnp.dot(hi, ones_bd, preferred_element_type=F32)
            + jnp.dot(lo, ones_bd, preferred_element_type=F32))


def _sigmoid(x):
    return 0.5 * jnp.tanh(0.5 * x) + 0.5


def _silu(x):
    return x * _sigmoid(x)


def _softplus(x):
    return jnp.maximum(x, 0.0) + jnp.log1p(jnp.exp(-jnp.abs(x)))


def _modulate(x, g, shift, scale):
    ms = jnp.mean(x * x, axis=-1, keepdims=True)
    return (x * lax.rsqrt(ms + EPS) * g) * (1.0 + scale) + shift


def _lane_iota(shape):
    return lax.broadcasted_iota(jnp.int32, shape, len(shape) - 1)


def _row_iota(shape):
    return lax.broadcasted_iota(jnp.int32, shape, len(shape) - 2)


def _mod_kernel(cond_ref, w_ref, b_ref, o_ref):
    c = cond_ref[...]
    o_ref[0] = jnp.dot(_silu(c), w_ref[0], precision=HIGHEST, preferred_element_type=F32) + b_ref[0]


def _mod_call(cond8, mod_w, mod_b):
    depth, d, n = mod_w.shape
    tn = 1536
    return pl.pallas_call(
        _mod_kernel,
        grid=(depth, n // tn),
        in_specs=[
            pl.BlockSpec((8, d), lambda l, j: (0, 0)),
            pl.BlockSpec((1, d, tn), lambda l, j: (l, 0, j)),
            pl.BlockSpec((1, 1, tn), lambda l, j: (l, 0, j)),
        ],
        out_specs=pl.BlockSpec((1, 8, tn), lambda l, j: (l, 0, j)),
        out_shape=jax.ShapeDtypeStruct((depth, 8, n), F32),
        compiler_params=_cparams(("parallel", "parallel")),
        name="mod_vectors",
    )(cond8, mod_w, mod_b.reshape(depth, 1, n))


def _rope(x, cos, sin):
    outs = []
    for j in range(x.shape[1] // 128):
        xb = x[:, 128 * j:128 * (j + 1)]
        lane = _lane_iota(xb.shape)
        sw = jnp.where((lane & 16) == 0, pltpu.roll(xb, 112, 1), pltpu.roll(xb, 16, 1))
        outs.append(xb * cos + sw * sin)
    return outs[0] if len(outs) == 1 else jnp.concatenate(outs, axis=1)


def _head_rms(x, gain, ones_bd):
    ms = _group_sum(x * x, ones_bd) * (1.0 / HEAD_DIM)
    return x * lax.rsqrt(ms + EPS) * gain


def _centred_conv(g, w, tm):
    rows = g.shape[0]
    acc = None
    for j in range(CONV_W):
        sh = (CONV_LEFT - j) % rows
        gj = g if sh == 0 else pltpu.roll(g, sh, 0)
        term = gj[HALO:HALO + tm] * w[j:j + 1, :]
        acc = term if acc is None else acc + term
    return acc


def _inproj_kernel(modrow_ref, ropeblk_ref, isstart_ref, isend_ref,
                   xp_ref, x_ref, xn_ref, mod_ref, g1_ref, w_ref, cos_ref, sin_ref,
                   aqg_ref, akg_ref, dqg_ref, dkg_ref, lcw_ref, lcb_ref, gcw_ref,
                   alog_ref, dtb_ref, ones_ref,
                   qa_ref, ka_ref, va_ref, lx_ref, lg_ref, gq_ref, gk_ref, gv_ref, gz_ref, ggb_ref,
                   qd_ref, kd_ref, vd_ref, *, tm):
    i = pl.program_id(0)
    xfull = jnp.concatenate([xp_ref[...], x_ref[...], xn_ref[...]], axis=0)
    h = _modulate(xfull, g1_ref[...], mod_ref[0, 0:1, :], mod_ref[0, 1:2, :])
    p = jnp.dot(h.astype(BF16), w_ref[...], preferred_element_type=F32)
    hb, hm = slice(0, tm + 2 * HALO), slice(HALO, HALO + tm)
    proj = lambda rows, c0, c1: p[rows, c0:c1]

    ones_bd = ones_ref[...]
    cos = cos_ref[...]
    sin = sin_ref[...]

    pa = proj(hm, 0, 512)
    qa_ref[...] = _rope(_head_rms(pa[:, 0:256], aqg_ref[...], ones_bd), cos, sin)
    ka_ref[...] = _rope(_head_rms(pa[:, 256:384], akg_ref[...], ones_bd[:128, :128]), cos, sin)
    va_ref[...] = pa[:, 384:512]

    row = _row_iota((tm + 2 * HALO, 1))
    keep = jnp.logical_and(jnp.logical_or(row >= HALO, isstart_ref[i] == 0),
                           jnp.logical_or(row < HALO + tm, isend_ref[i] == 0))

    lxg = jnp.where(keep, proj(hb, 512, 768), 0.0)
    lx_ref[...] = _centred_conv(lxg, lcw_ref[...], tm) + lcb_ref[...]
    lg_ref[...] = proj(hm, 768, 1024)

    qkv = jnp.where(keep, proj(hb, 1024, 1792), 0.0)
    qkv = _silu(_centred_conv(qkv, gcw_ref[...], tm))
    gq = qkv[:, 0:256]
    gk = qkv[:, 256:512]
    gq_ref[...] = gq * lax.rsqrt(_group_sum(gq * gq, ones_bd) + EPS) * (GDN_DK ** -0.5)
    gk_ref[...] = gk * lax.rsqrt(_group_sum(gk * gk, ones_bd) + EPS)
    gv_ref[...] = qkv[:, 512:768]
    pz = proj(hm, 1792, 2176)
    gz_ref[...] = pz[:, 0:256]
    ab = pz[:, 256:384]
    g = -jnp.exp(alog_ref[...]) * _softplus(ab + dtb_ref[...])
    lane = _lane_iota(ab.shape)
    ggb_ref[...] = jnp.where(lane < 2 * GDN_H, g, _sigmoid(ab))

    pd = proj(hm, 2176, 2688)
    qd_ref[...] = _rope(_head_rms(pd[:, 0:256], dqg_ref[...], ones_bd), cos, sin)
    kd_ref[...] = _rope(_head_rms(pd[:, 256:384], dkg_ref[...], ones_bd[:128, :128]), cos, sin)
    vd_ref[...] = pd[:, 384:512]


def _inproj_call(x, meta, mod_all, g1, w_packed, cos_t, sin_t, vecs, ones_bd, *, l, tm):
    n, d = x.shape
    nt = n // tm
    hb = tm // HALO
    last_hb = n // HALO - 1
    row_tile = lambda w: pl.BlockSpec((tm, w), lambda i, *_: (i, 0))
    in_specs = [
        pl.BlockSpec((HALO, d), lambda i, *_: (jnp.maximum(i * hb - 1, 0), 0)),
        row_tile(d),
        pl.BlockSpec((HALO, d), lambda i, *_: (jnp.minimum((i + 1) * hb, last_hb), 0)),
        pl.BlockSpec((None, 1, 6, d), lambda i, modrow, *_: (l, modrow[i], 0, 0)),
        _layer_spec(g1, l),
        _layer_spec(w_packed, l),
        pl.BlockSpec((tm, 128), lambda i, modrow, ropeblk, *_: (ropeblk[i], 0)),
        pl.BlockSpec((tm, 128), lambda i, modrow, ropeblk, *_: (ropeblk[i], 0)),
    ] + [_layer_spec(v, l) for v in vecs] + [pl.BlockSpec((256, 256), lambda i, *_: (0, 0))]
    widths = (256, 128, 128, 256, 256, 256, 256, 256, 256, 128, 256, 128, 128)
    grid_spec = pltpu.PrefetchScalarGridSpec(
        num_scalar_prefetch=4,
        grid=(nt,),
        in_specs=in_specs,
        out_specs=[row_tile(w) for w in widths],
    )
    return pl.pallas_call(
        functools.partial(_inproj_kernel, tm=tm),
        grid_spec=grid_spec,
        out_shape=[jax.ShapeDtypeStruct((n, w), F32) for w in widths],
        compiler_params=_cparams(("parallel",)),
        name="in_projection",
    )(*meta, x, x, x, mod_all, g1, w_packed, cos_t, sin_t, *vecs, ones_bd)


def _head_q(q, j, g):
    qj = q[:, 128 * j:128 * (j + 1)]
    lane = _lane_iota(qj.shape)
    sel = (lane < HEAD_DIM) if g == 0 else (lane >= HEAD_DIM)
    return jnp.where(sel, qj, 0.0).astype(BF16)


def _place_heads(res, j):
    r0 = res[0] if j == 0 else pltpu.roll(res[0], HEAD_DIM, 1)
    r1 = res[1] if j == 1 else pltpu.roll(res[1], HEAD_DIM, 1)
    lane = _lane_iota(r0.shape)
    return jnp.where(lane < HEAD_DIM, r0, r1)


def _attend(q, ks, vs, masks, sinks, ks_sw=None):
    if ks_sw is None:
        ks_sw = [pltpu.roll(k, HEAD_DIM, 1) for k in ks]
    blocks = []
    for j in range(2):
        res = []
        for g in range(2):
            qm = _head_q(q, j, g)
            ss = []
            for k, ksw, mk in zip(ks, ks_sw, masks):
                kk = k if g == j else ksw
                s = lax.dot_general(qm, kk, (((1,), (1,)), ((), ())), preferred_element_type=F32)
                if mk is not None:
                    s = jnp.where(mk, s, NEG)
                ss.append(s)
            m = ss[0].max(axis=-1, keepdims=True)
            for s in ss[1:]:
                m = jnp.maximum(m, s.max(axis=-1, keepdims=True))
            if sinks is not None:
                m = jnp.maximum(m, sinks[2 * j + g])
            den = None
            acc = None
            for s, v in zip(ss, vs):
                e = jnp.exp2(s - m)
                dsum = e.sum(axis=-1, keepdims=True)
                den = dsum if den is None else den + dsum
                pv = jnp.dot(e.astype(BF16), v, preferred_element_type=F32)
                acc = pv if acc is None else acc + pv
            if sinks is not None:
                den = den + jnp.exp2(sinks[2 * j + g] - m)
            res.append(acc / den)
        blocks.append(_place_heads(res, j))
    return jnp.concatenate(blocks, axis=1)


def _bf(x):
    return x.astype(BF16)


def _attn_ctx_kernel(sink_ref, qa_ref, ka_ref, va_ref, qd_ref, kd_ref, vd_ref, oa_ref, od_ref, *, l):
    sinks = [sink_ref[4 * l + u] * LOG2E for u in range(4)]
    oa_ref[...] = _attend(qa_ref[...], [_bf(ka_ref[...])], [_bf(va_ref[...])], [None], sinks)
    od_ref[...] = _attend(qd_ref[...], [_bf(kd_ref[...])], [_bf(vd_ref[...])], [None], None)


def _attn_ctx_call(sink, qa, ka, va, qd, kd, vd, *, l, n_seq, t):
    blk = lambda w: pl.BlockSpec((t, w), lambda b, *_: (b, 0))
    grid_spec = pltpu.PrefetchScalarGridSpec(
        num_scalar_prefetch=1, grid=(n_seq,),
        in_specs=[blk(256), blk(128), blk(128), blk(256), blk(128), blk(128)],
        out_specs=[blk(256), blk(256)])
    return pl.pallas_call(
        functools.partial(_attn_ctx_kernel, l=l), grid_spec=grid_spec,
        out_shape=[jax.ShapeDtypeStruct((n_seq * t, 256), F32)] * 2,
        compiler_params=_cparams(("parallel",)),
        name="attn_context",
    )(sink, qa, ka, va, qd, kd, vd)


def _attn_band_kernel(sink_ref, q_ref, k_ref, v_ref, ck_ref, cv_ref, o_ref, *, l, tq, t):
    i = pl.program_id(1)
    start = pl.multiple_of(i * tq, tq)
    prev = pl.multiple_of(jnp.maximum(start - WINDOW, 0), WINDOW)
    nxt = pl.multiple_of(jnp.minimum(start + tq, t - WINDOW), WINDOW)
    k_loc = jnp.concatenate([k_ref[pl.ds(prev, WINDOW), :], k_ref[pl.ds(start, tq), :],
                             k_ref[pl.ds(nxt, WINDOW), :]], axis=0)
    v_loc = jnp.concatenate([v_ref[pl.ds(prev, WINDOW), :], v_ref[pl.ds(start, tq), :],
                             v_ref[pl.ds(nxt, WINDOW), :]], axis=0)
    shape = (tq, tq + 2 * WINDOW)
    qpos = start + _row_iota(shape)
    kpos = start - WINDOW + _lane_iota(shape)
    mask = (kpos >= 0) & (kpos < t) & (jnp.abs(qpos - kpos) <= WINDOW)
    sinks = [sink_ref[4 * l + u] * LOG2E for u in range(4)]
    o_ref[...] = _attend(q_ref[...], [_bf(ck_ref[0]), _bf(k_loc)], [_bf(cv_ref[0]), _bf(v_loc)],
                         [None, mask], sinks)


def _cache_spec(c, l):
    return pl.BlockSpec((1, None) + c.shape[2:], lambda b, i, *_: (b, l, 0, 0))


def _attn_band_call(sink, q, k, v, ck, cv, *, l, n_seq, t, row0, tq):
    nq = t // tq
    qb0 = row0 // tq
    sb0 = row0 // t
    grid_spec = pltpu.PrefetchScalarGridSpec(
        num_scalar_prefetch=1, grid=(n_seq, nq),
        in_specs=[
            pl.BlockSpec((tq, 256), lambda b, i, *_: (qb0 + b * nq + i, 0)),
            pl.BlockSpec((t, 128), lambda b, i, *_: (sb0 + b, 0)),
            pl.BlockSpec((t, 128), lambda b, i, *_: (sb0 + b, 0)),
            _cache_spec(ck, l), _cache_spec(cv, l),
        ],
        out_specs=pl.BlockSpec((tq, 256), lambda b, i, *_: (b * nq + i, 0)))
    return pl.pallas_call(
        functools.partial(_attn_band_kernel, l=l, tq=tq, t=t), grid_spec=grid_spec,
        out_shape=jax.ShapeDtypeStruct((n_seq * t, 256), F32),
        compiler_params=_cparams(("parallel", "parallel")),
        name="attn_banded",
    )(sink, q, k, v, ck, cv)


def _attn_full_kernel(q_ref, k_ref, v_ref, ck_ref, cv_ref, o_ref, ks_ref, vs_ref, *, past, t):
    @pl.when(pl.program_id(1) == 0)
    def _():
        kc = _bf(ck_ref[0])
        kl = _bf(k_ref[...])
        ks_ref[0, 0:past, :] = kc
        ks_ref[0, past:past + t, :] = kl
        ks_ref[1, 0:past, :] = pltpu.roll(kc, HEAD_DIM, 1)
        ks_ref[1, past:past + t, :] = pltpu.roll(kl, HEAD_DIM, 1)
        vs_ref[0:past, :] = _bf(cv_ref[0])
        vs_ref[past:past + t, :] = _bf(v_ref[...])

    o_ref[...] = _attend(q_ref[...], [ks_ref[0]], [vs_ref[...]], [None], None, ks_sw=[ks_ref[1]])


def _attn_full_call(q, k, v, ck, cv, *, l, n_seq, t, row0, tq):
    nq = t // tq
    qb0 = row0 // tq
    sb0 = row0 // t
    past = ck.shape[2]
    assert past % 16 == 0
    return pl.pallas_call(
        functools.partial(_attn_full_kernel, past=past, t=t), grid=(n_seq, nq),
        scratch_shapes=[pltpu.VMEM((2, past + t, 128), BF16), pltpu.VMEM((past + t, 128), BF16)],
        in_specs=[
            pl.BlockSpec((tq, 256), lambda b, i: (qb0 + b * nq + i, 0)),
            pl.BlockSpec((t, 128), lambda b, i: (sb0 + b, 0)),
            pl.BlockSpec((t, 128), lambda b, i: (sb0 + b, 0)),
            _cache_spec(ck, l), _cache_spec(cv, l),
        ],
        out_specs=pl.BlockSpec((tq, 256), lambda b, i: (b * nq + i, 0)),
        out_shape=jax.ShapeDtypeStruct((n_seq * t, 256), F32),
        compiler_params=_cparams(("parallel", "arbitrary")),
        name="attn_full",
    )(q, k, v, ck, cv)


def _gelu_tanh(x):
    return 0.5 * x * (1.0 + jnp.tanh(0.7978845608028654 * (x + 0.044715 * (x * x * x))))


def _lru_kernel(lx_ref, lg_ref, w_ref, b_ref, lam_ref, h0_ref, y_ref, st_ref,
                hf_ref, hb_ref, af_ref, bf_ref, ab_ref, bb_ref, *, t, chunk):
    nc = t // chunk
    ng = chunk // 8
    sp = _softplus(-lam_ref[...])
    row8 = _row_iota((8, BRANCH_W))

    def gates(xc, d, a_ref, b2_ref):
        pre = _bdot(xc, w_ref[:, 512 * d:512 * (d + 1)]) + b_ref[:, 512 * d:512 * (d + 1)]
        r = _sigmoid(pre[:, 0:256])
        ig = _sigmoid(pre[:, 256:512])
        a = jnp.exp((-LRU_C) * r * sp[d:d + 1, :])
        a_ref[...] = a
        b2_ref[...] = jnp.sqrt(1.0 - a * a) * (ig * xc)

    def local_scan(a8, b8, reverse):
        for s in (1, 2, 4):
            sh = (8 - s) if reverse else s
            m = (row8 < 8 - s) if reverse else (row8 >= s)
            a_sh = pltpu.roll(a8, sh, 0)
            b_sh = pltpu.roll(b8, sh, 0)
            b8 = jnp.where(m, a8 * b_sh + b8, b8)
            a8 = jnp.where(m, a8 * a_sh, a8)
        return a8, b8

    def chunk_body(cc, carry):
        rf = pl.multiple_of(cc * chunk, chunk)
        rb = pl.multiple_of((nc - 1 - cc) * chunk, chunk)
        gates(lx_ref[pl.ds(rf, chunk), :], 0, af_ref, bf_ref)
        gates(lx_ref[pl.ds(rb, chunk), :], 1, ab_ref, bb_ref)

        def grp(gg, c2):
            hf, hb = c2
            gf = pl.multiple_of(gg * 8, 8)
            gb = pl.multiple_of((ng - 1 - gg) * 8, 8)
            a8, b8 = local_scan(af_ref[pl.ds(gf, 8), :], bf_ref[pl.ds(gf, 8), :], False)
            h8 = b8 + a8 * hf
            hf_ref[pl.ds(rf + gf, 8), :] = h8
            a8, b8 = local_scan(ab_ref[pl.ds(gb, 8), :], bb_ref[pl.ds(gb, 8), :], True)
            g8 = b8 + a8 * hb
            hb_ref[pl.ds(rb + gb, 8), :] = g8
            return (jnp.broadcast_to(h8[7:8, :], (8, BRANCH_W)), jnp.broadcast_to(g8[0:1, :], (8, BRANCH_W)))

        return lax.fori_loop(0, ng, grp, carry, unroll=4)

    init = (jnp.broadcast_to(h0_ref[0, 0:1, :], (8, BRANCH_W)), jnp.broadcast_to(h0_ref[0, 1:2, :], (8, BRANCH_W)))
    hf, hb = lax.fori_loop(0, nc, chunk_body, init)
    st_ref[0, 0:1, :] = hf[0:1, :]
    st_ref[0, 1:2, :] = hb[0:1, :]

    def combine(c, _):
        rows = pl.ds(pl.multiple_of(c * chunk, chunk), chunk)
        y_ref[rows, :] = (hf_ref[rows, :] + hb_ref[rows, :]) * _gelu_tanh(lg_ref[rows, :])
        return 0

    lax.fori_loop(0, nc, combine, 0)


def _lru_call(lx, lg, w, b, lam, h0, *, l, l_state, n_seq, t, row0):
    sb0 = row0 // t
    chunk = min(t, 256)
    seq = lambda: pl.BlockSpec((t, BRANCH_W), lambda s: (sb0 + s, 0))
    return pl.pallas_call(
        functools.partial(_lru_kernel, t=t, chunk=chunk),
        grid=(n_seq,),
        in_specs=[seq(), seq(), _layer_spec(w, l), _layer_spec(b, l), _layer_spec(lam, l),
                  pl.BlockSpec((1, None, 2, BRANCH_W), lambda s: (s, l_state, 0, 0))],
        out_specs=[pl.BlockSpec((t, BRANCH_W), lambda s: (s, 0)),
                   pl.BlockSpec((1, 2, BRANCH_W), lambda s: (s, 0, 0))],
        out_shape=[jax.ShapeDtypeStruct((n_seq * t, BRANCH_W), F32),
                   jax.ShapeDtypeStruct((n_seq, 2, BRANCH_W), F32)],
        scratch_shapes=[pltpu.VMEM((t, BRANCH_W), F32), pltpu.VMEM((t, BRANCH_W), F32)]
                       + [pltpu.VMEM((chunk, BRANCH_W), F32)] * 4,
        compiler_params=_cparams(("parallel",)),
        name="rglru",
    )(lx, lg, w, b, lam, h0)


def _rows_dot_exact(x, w01):
    r = x.shape[0]
    out = jnp.dot(jnp.concatenate(_split3(x), axis=0), w01, preferred_element_type=F32)
    return out[0:r] + out[r:2 * r] + out[2 * r:3 * r]


def _dot01_exact(m01, y):
    return sum(jnp.dot(m01, part, preferred_element_type=F32) for part in _split3(y))


def _gdn_kernel(q_ref, k_ref, v_ref, ggb_ref, s0_ref, o_ref, sT_ref, s_ref, *, d, reverse, n_par, n_chunk, n_tile):
    ti = pl.program_id(1)
    w4 = GDN_H * GDN_DK
    n_all = n_par * n_chunk
    tt = n_all * CHUNK
    blockmask = (_row_iota((w4, w4)) // CHUNK) == (_lane_iota((w4, w4)) // CHUNK)

    def expand_rows(y):
        yt = jnp.concatenate([y] * GDN_H, axis=0)
        zero = jnp.zeros((), y.dtype)
        parts = [jnp.where(blockmask, yt[:, w4 * u:w4 * (u + 1)], zero) for u in range(y.shape[1] // w4)]
        return parts[0] if len(parts) == 1 else jnp.concatenate(parts, axis=1)

    def heads_dot3(lhs, y):
        r = lhs.shape[0]
        lh, ll = _split(lhs)
        yh, yl = _split(y)
        out = jnp.dot(jnp.concatenate([lh, ll], axis=0), expand_rows(yh), preferred_element_type=F32)
        return out[0:r] + out[r:2 * r] + jnp.dot(lh, expand_rows(yl), preferred_element_type=F32)

    def heads_dot1(lhs, y):
        return jnp.dot(lhs.astype(BF16), expand_rows(y.astype(BF16)), preferred_element_type=F32)

    @pl.when(ti == 0)
    def _():
        for p in range(n_par):
            s_ref[p] = expand_rows(s0_ref[p, 0])

    ri = _row_iota((CHUNK, w4))
    cj = _lane_iota((CHUNK, w4)) % CHUNK
    incl = (cj >= ri) if reverse else (cj <= ri)
    strict = (cj > ri) if reverse else (cj < ri)
    eye_sbs = (cj == ri)
    rep = eye_sbs.astype(BF16)

    er = _row_iota((128, 2 * w4))
    el = _lane_iota((128, 2 * w4))
    e_gb = (er == jnp.where(el < w4, d * GDN_H, 2 * GDN_H + d * GDN_H) + (el % w4) // CHUNK).astype(BF16)
    gbe = _rows_dot_exact(ggb_ref[...], e_gb)
    g_all = gbe[:, 0:w4]
    beta_all = gbe[:, w4:2 * w4]
    tr = _row_iota((tt, tt))
    tc = _lane_iota((tt, tt))
    same_chunk = (tr // CHUNK) == (tc // CHUNK)
    tri_bd = (same_chunk & ((tc >= tr) if reverse else (tc <= tr))).astype(BF16)
    ones_bd = same_chunk.astype(BF16)
    gc_all = _dot01_exact(tri_bd, g_all)
    diag_sel = (_row_iota((tt, w4)) % CHUNK) == (_lane_iota((tt, w4)) % CHUNK)
    gct_all = _dot01_exact(ones_bd, jnp.where(diag_sel, gc_all, 0.0))

    cs = range(n_all)
    rows = [slice(c * CHUNK, (c + 1) * CHUNK) for c in cs]
    q = [q_ref[r, :] for r in rows]
    k = [k_ref[r, :] for r in rows]
    v = [v_ref[r, :] for r in rows]
    beta = [beta_all[r, :] for r in rows]
    gc = [gc_all[r, :] for r in rows]
    decay = [jnp.where(incl, jnp.exp(jnp.where(incl, gc[c] - gct_all[rows[c], :], 0.0)), 0.0) for c in cs]
    kb = [k[c].astype(BF16) for c in cs]
    w_k = [jnp.where(blockmask,
                     lax.dot_general(kb[c], rep, (((0,), (0,)), ((), ())), preferred_element_type=F32),
                     0.0).astype(BF16) for c in cs]
    kq = [jnp.dot(jnp.concatenate([kb[c], q[c].astype(BF16)], axis=0), w_k[c], preferred_element_type=F32)
          for c in cs]
    qk = [(kq[c][CHUNK:2 * CHUNK] * decay[c]).astype(BF16) for c in cs]
    a = [jnp.where(strict, beta[c] * kq[c][0:CHUNK] * decay[c], 0.0) for c in cs]
    t_inv = [jnp.where(eye_sbs, 1.0, 0.0) - a[c] for c in cs]
    pw = [heads_dot3(a[c], a[c]) for c in cs]
    for stage in range(4):
        mm = heads_dot3 if stage < N_PRECISE else heads_dot1
        both = [mm(jnp.concatenate([t_inv[c], pw[c]], axis=0), pw[c]) for c in cs]
        t_inv = [t_inv[c] + both[c][0:CHUNK] for c in cs]
        pw = [both[c][CHUNK:2 * CHUNK] for c in cs]
    t_inv = [t_inv[c] + heads_dot1(t_inv[c], pw[c]) for c in cs]
    egc = [jnp.exp(gc[c]) for c in cs]
    sol = [heads_dot3(t_inv[c], jnp.concatenate([v[c] * beta[c], k[c] * (beta[c] * egc[c])], axis=1)) for c in cs]
    g_last = [gc[c][0:1, :] if reverse else gc[c][CHUNK - 1:CHUNK, :] for c in cs]
    wq = [jnp.concatenate([sol[c][:, w4:2 * w4], q[c] * egc[c]], axis=0).astype(BF16) for c in cs]
    kdec = [(k[c] * jnp.exp(g_last[c] - gc[c])).astype(BF16) for c in cs]

    for cc in range(n_chunk):
        for p in range(n_par):
            c = p * n_chunk + ((n_chunk - 1 - cc) if reverse else cc)
            s = s_ref[p]
            ws_qs = jnp.dot(wq[c], s.astype(BF16), preferred_element_type=F32)
            v_new = sol[c][:, 0:w4] - ws_qs[0:CHUNK]
            vb = v_new.astype(BF16)
            o_ref[rows[c], :] = (
                ws_qs[CHUNK:2 * CHUNK] + jnp.dot(qk[c], expand_rows(vb), preferred_element_type=F32))
            upd = lax.dot_general(kdec[c], vb, (((0,), (0,)), ((), ())), preferred_element_type=F32)
            s_ref[p] = s * jnp.exp(g_last[c]) + jnp.where(blockmask, upd, 0.0)

    @pl.when(ti == n_tile - 1)
    def _():
        for p in range(n_par):
            s = s_ref[p]
            sT_ref[p, 0] = s[0:64] + s[64:128] + s[128:192] + s[192:256]


def _gdn_call(q, k, v, ggb, s0, *, d, l_state, n_seq, t, row0, tt, n_par):
    reverse = d == 1
    n_tile = t // tt
    assert n_par == 1 or n_tile == 1
    b0 = row0 // (tt * n_par)
    tidx = (lambda i: n_tile - 1 - i) if reverse else (lambda i: i)
    blk = lambda w: pl.BlockSpec((tt * n_par, w), lambda s, i: (b0 + s * n_tile + tidx(i), 0))
    return pl.pallas_call(
        functools.partial(_gdn_kernel, d=d, reverse=reverse, n_par=n_par, n_chunk=tt // CHUNK, n_tile=n_tile),
        grid=(n_seq // n_par, n_tile),
        in_specs=[blk(256), blk(256), blk(256), blk(128),
                  pl.BlockSpec((n_par, None, 1, CHUNK, 256), lambda s, i: (s, l_state, d, 0, 0))],
        out_specs=[pl.BlockSpec((tt * n_par, 256), lambda s, i: (s * n_tile + tidx(i), 0)),
                   pl.BlockSpec((n_par, 1, CHUNK, 256), lambda s, i: (s, 0, 0, 0))],
        out_shape=[jax.ShapeDtypeStruct((n_seq * t, 256), F32),
                   jax.ShapeDtypeStruct((n_seq, 1, CHUNK, 256), F32)],
        scratch_shapes=[pltpu.VMEM((n_par, 256, 256), F32)],
        compiler_params=_cparams(("parallel", "arbitrary")),
        name="gdn_bwd" if reverse else "gdn_fwd",
    )(q, k, v, ggb, s0)


def _merge_kernel(modrow_ref, x_ref, mod_ref, g1_ref,
                  oac_ref, oad_ref, obc_ref, obd_ref, ofc_ref, ofd_ref, orc_ref, ord_ref, odc_ref, odd_ref,
                  gz_ref, gng_ref, ones_ref, wm_ref, bm_ref, wb_ref, wo_ref, y_ref, *, n_ctx_tiles):
    is_ctx = pl.program_id(0) < n_ctx_tiles
    pick = lambda c_ref, d_ref: jnp.where(is_ctx, c_ref[...], d_ref[...])
    x = x_ref[...]
    h = _modulate(x, g1_ref[...], mod_ref[0, 0:1, :], mod_ref[0, 1:2, :]).astype(BF16)
    oc = pick(ofc_ref, ofd_ref) + pick(orc_ref, ord_ref)
    ms = _group_sum(oc * oc, ones_ref[...]) * (1.0 / GDN_DK)
    oc = (oc * lax.rsqrt(ms + EPS) * gng_ref[...]) * _silu(gz_ref[...])
    branches = (pick(oac_ref, oad_ref), pick(obc_ref, obd_ref), oc, pick(odc_ref, odd_ref))
    acc = None
    for m in range(N_BRANCH):
        cols = slice(D_MODEL * m, D_MODEL * (m + 1))
        gate = _sigmoid(jnp.dot(h, wm_ref[:, cols], preferred_element_type=F32) + bm_ref[:, cols])
        term = gate * jnp.dot(branches[m].astype(BF16), wb_ref[m], preferred_element_type=F32)
        acc = term if acc is None else acc + term
    y = jnp.dot(acc.astype(BF16), wo_ref[...], preferred_element_type=F32)
    y_ref[...] = x + mod_ref[0, 2:3, :] * y


def _merge_call(modrow, x, mod_all, g1, pairs, gz, gng, ones_bd, wm, bm, wb, wo, *, l, tm, n_ctx):
    n, d = x.shape
    nct = n_ctx // tm
    row_tile = lambda w: pl.BlockSpec((tm, w), lambda i, *_: (i, 0))
    pair_specs = []
    for _ in pairs:
        pair_specs.append(pl.BlockSpec((tm, 256), lambda i, *_: (jnp.minimum(i, nct - 1), 0)))
        pair_specs.append(pl.BlockSpec((tm, 256), lambda i, *_: (jnp.maximum(i - nct, 0), 0)))
    grid_spec = pltpu.PrefetchScalarGridSpec(
        num_scalar_prefetch=1, grid=(n // tm,),
        in_specs=[row_tile(d), pl.BlockSpec((None, 1, 6, d), lambda i, modrow: (l, modrow[i], 0, 0)),
                  _layer_spec(g1, l)] + pair_specs
                 + [row_tile(256), _layer_spec(gng, l), pl.BlockSpec((256, 256), lambda i, *_: (0, 0)),
                    _layer_spec(wm, l), _layer_spec(bm, l), _layer_spec(wb, l), _layer_spec(wo, l)],
        out_specs=row_tile(d))
    flat = [a for pr in pairs for a in pr]
    return pl.pallas_call(
        functools.partial(_merge_kernel, n_ctx_tiles=nct), grid_spec=grid_spec,
        out_shape=jax.ShapeDtypeStruct((n, d), F32),
        compiler_params=_cparams(("parallel",)),
        name="branch_merge",
    )(modrow, x, mod_all, g1, *flat, gz, gng, ones_bd, wm, bm, wb, wo)


def _mlp_kernel(modrow_ref, x_ref, mod_ref, g2_ref, w1_ref, w2_ref, y_ref):
    x = x_ref[...]
    h = _modulate(x, g2_ref[...], mod_ref[0, 3:4, :], mod_ref[0, 4:5, :]).astype(BF16)
    acc = None
    for j in range(D_FF // D_MODEL):
        cols = slice(D_MODEL * j, D_MODEL * (j + 1))
        a = jnp.maximum(jnp.dot(h, w1_ref[:, cols], preferred_element_type=F32), 0.0)
        term = jnp.dot((a * a).astype(BF16), w2_ref[cols, :], preferred_element_type=F32)
        acc = term if acc is None else acc + term
    y_ref[...] = x + mod_ref[0, 5:6, :] * acc


def _mlp_call(modrow, x, mod_all, g2, w1, w2, *, l, tm):
    n, d = x.shape
    row_tile = pl.BlockSpec((tm, d), lambda i, *_: (i, 0))
    grid_spec = pltpu.PrefetchScalarGridSpec(
        num_scalar_prefetch=1, grid=(n // tm,),
        in_specs=[row_tile, pl.BlockSpec((None, 1, 6, d), lambda i, modrow: (l, modrow[i], 0, 0)),
                  _layer_spec(g2, l), _layer_spec(w1, l), _layer_spec(w2, l)],
        out_specs=row_tile)
    return pl.pallas_call(
        _mlp_kernel, grid_spec=grid_spec,
        out_shape=jax.ShapeDtypeStruct((n, d), F32),
        compiler_params=_cparams(("parallel",)),
        name="mlp",
    )(modrow, x, mod_all, g2, w1, w2)


def _rope_tables(t, tm):
    n_freq = HEAD_DIM // 4
    inv = np.float32(ROPE_BASE) ** (-np.arange(n_freq, dtype=np.float32) / np.float32(n_freq))
    pos = np.arange(t)
    row = (pos // GRID_W).astype(np.float32)[:, None]
    col = (pos % GRID_W).astype(np.float32)[:, None]
    ar = (row * inv).astype(np.float32)
    ac = (col * inv).astype(np.float32)
    cos64 = np.concatenate([np.cos(ar), np.cos(ar), np.cos(ac), np.cos(ac)], axis=1)
    sin64 = np.concatenate([-np.sin(ar), np.sin(ar), -np.sin(ac), np.sin(ac)], axis=1)
    cos = np.concatenate([np.ones((tm, 128), np.float32), np.tile(cos64, (1, 2)).astype(np.float32)], axis=0)
    sin = np.concatenate([np.zeros((tm, 128), np.float32), np.tile(sin64, (1, 2)).astype(np.float32)], axis=0)
    return jnp.asarray(cos), jnp.asarray(sin)


def _tile_meta(n_ctx, n_seq_dec, t_ctx, t_dec, tm):
    nct = n_ctx // tm
    per = t_dec // tm
    ndt = n_seq_dec * per
    idx = np.arange(nct + ndt)
    dec = idx >= nct
    di = np.maximum(idx - nct, 0)
    modrow = np.where(dec, 1 + di // per, 0)
    ropeblk = np.where(dec, 1 + di % per, 0)
    tiles_ctx = max(t_ctx // tm, 1)
    isstart = np.where(dec, di % per == 0, idx % tiles_ctx == 0)
    isend = np.where(dec, di % per == per - 1, idx % tiles_ctx == tiles_ctx - 1)
    as_i32 = lambda a: jnp.asarray(a.astype(np.int32))
    return as_i32(modrow), as_i32(ropeblk), as_i32(isstart), as_i32(isend)


def _block_diag_gates(w):
    depth, two, n, c, _ = w.shape
    eye = jnp.eye(n, dtype=w.dtype)
    return jnp.einsum("ldnij,nm->ldnimj", w, eye).reshape(depth, two, n * c, n * c)


def kernel(x_prompt, x_sample, c, cache_a_k, cache_a_v, cache_d_k, cache_d_v, state_lru, state_gdn, c_ctx, mod_w, mod_b, norm1_g, norm2_g, w_in, a_qn_g, a_kn_g, a_sink, lru_conv_w, lru_conv_b, lru_wr, lru_br, lru_wi, lru_bi, lru_lam, gdn_conv_w, gdn_a_log, gdn_dt_bias, gdn_norm_g, d_qn_g, d_kn_g, w_branch, w_merge, b_merge, w_out, mlp_w1, mlp_w2):
    batch, seq, d = x_prompt.shape
    dec_batch, dec_seq, _ = x_sample.shape
    depth = mod_w.shape[0]
    past = cache_a_k.shape[2]
    n_ctx = batch * seq
    n_dec = dec_batch * dec_seq
    tm = 256
    gdn_par = 2 if batch % 2 == 0 else 1
    gdn_tt = 512 if dec_seq % 512 == 0 else 256
    assert d == D_MODEL and seq % tm == 0 and dec_seq % tm == 0 and tm % seq == 0
    assert dec_batch + 1 <= 8 and n_ctx % dec_seq == 0

    cond8 = jnp.zeros((8, d), F32).at[0].set(c_ctx).at[1:1 + dec_batch].set(c)
    mod_all = _mod_call(cond8, mod_w, mod_b).reshape(depth, 8, 6, d)

    meta = _tile_meta(n_ctx, dec_batch, seq, dec_seq, tm)
    cos_t, sin_t = _rope_tables(dec_seq, tm)
    lane = np.arange(256)
    ones_bd = jnp.asarray((lane[:, None] // HEAD_DIM == lane[None, :] // HEAD_DIM).astype(np.float32)).astype(BF16)

    w_packed = jnp.concatenate([w_in[:, :, :2064], jnp.zeros((depth, d, GDN_PAD), F32), w_in[:, :, 2064:]],
                               axis=2).astype(BF16)
    pad128 = lambda v: jnp.pad(v.reshape(depth, 1, -1), ((0, 0), (0, 0), (0, 128 - v[0].size)))
    vecs = (
        (jnp.tile(a_qn_g, (1, 4)) * Q_SCALE)[:, None, :],
        jnp.tile(a_kn_g, (1, 2))[:, None, :],
        (jnp.tile(d_qn_g, (1, 4)) * Q_SCALE)[:, None, :],
        jnp.tile(d_kn_g, (1, 2))[:, None, :],
        lru_conv_w, lru_conv_b[:, None, :], gdn_conv_w,
        pad128(gdn_a_log), pad128(gdn_dt_bias),
    )
    g1 = norm1_g[:, None, :]
    g2 = norm2_g[:, None, :]
    wr_bd = _block_diag_gates(lru_wr)
    wi_bd = _block_diag_gates(lru_wi)
    w_lru = jnp.concatenate([wr_bd[:, 0], wi_bd[:, 0], wr_bd[:, 1], wi_bd[:, 1]], axis=-1).astype(BF16)
    b_lru = jnp.concatenate([lru_br[:, 0], lru_bi[:, 0], lru_br[:, 1], lru_bi[:, 1]], axis=-1)[:, None, :]
    gng = jnp.tile(gdn_norm_g, (1, 4))[:, None, :]
    wm = w_merge.astype(BF16)
    bm = b_merge[:, None, :]
    wb = w_branch.astype(BF16)
    wo = w_out.astype(BF16)
    w1 = mlp_w1.astype(BF16)
    w2 = mlp_w2.astype(BF16)
    sink = a_sink.reshape(-1)
    caches = [t.reshape(dec_batch, depth, past, 2 * HEAD_DIM) for t in (cache_a_k, cache_a_v, cache_d_k, cache_d_v)]
    s0_dec = state_gdn.transpose(0, 1, 2, 4, 3, 5).reshape(dec_batch, depth, 2, CHUNK, 256)
    zeros_lru = jnp.zeros((batch, 1, 2, BRANCH_W), F32)
    zeros_gdn = jnp.zeros((batch, 1, 2, CHUNK, 256), F32)

    x = jnp.concatenate([x_prompt.reshape(n_ctx, d), x_sample.reshape(n_dec, d)], axis=0)

    kv_ctx, lru_states, gdn_states = [], [], []
    for l in range(depth):
        (qa, ka, va, lx, lg, gq, gk, gv, gz, ggb, qd, kd, vd) = _inproj_call(
            x, meta, mod_all, g1, w_packed, cos_t, sin_t, vecs, ones_bd, l=l, tm=tm)

        oa_c, od_c = _attn_ctx_call(sink, qa, ka, va, qd, kd, vd, l=l, n_seq=batch, t=seq)
        oa_d = _attn_band_call(sink, qa, ka, va, caches[0], caches[1], l=l, n_seq=dec_batch, t=dec_seq,
                               row0=n_ctx, tq=512)
        od_d = _attn_full_call(qd, kd, vd, caches[2], caches[3], l=l, n_seq=dec_batch, t=dec_seq,
                               row0=n_ctx, tq=256)

        ob_c, st_c = _lru_call(lx, lg, w_lru, b_lru, lru_lam, zeros_lru, l=l, l_state=0, n_seq=batch, t=seq,
                               row0=0)
        ob_d, _ = _lru_call(lx, lg, w_lru, b_lru, lru_lam, state_lru, l=l, l_state=l, n_seq=dec_batch,
                            t=dec_seq, row0=n_ctx)

        oc_pairs, sts = [], []
        for dd in range(2):
            o_c, s_c = _gdn_call(gq, gk, gv, ggb, zeros_gdn, d=dd, l_state=0, n_seq=batch, t=seq, row0=0,
                                 tt=seq, n_par=gdn_par)
            o_d, _ = _gdn_call(gq, gk, gv, ggb, s0_dec, d=dd, l_state=l, n_seq=dec_batch, t=dec_seq,
                               row0=n_ctx, tt=gdn_tt, n_par=1)
            oc_pairs.append((o_c, o_d))
            sts.append(s_c)

        x = _merge_call(meta[0], x, mod_all, g1,
                        [(oa_c, oa_d), (ob_c, ob_d), oc_pairs[0], oc_pairs[1], (od_c, od_d)],
                        gz, gng, ones_bd, wm, bm, wb, wo, l=l, tm=tm, n_ctx=n_ctx)
        x = _mlp_call(meta[0], x, mod_all, g2, w1, w2, l=l, tm=tm)

        kv_ctx.append([t[:n_ctx] for t in (ka, va, kd, vd)])
        lru_states.append(st_c)
        gdn_states.append(jnp.concatenate(sts, axis=1))

    y_prompt = x[:n_ctx].reshape(batch, seq, d)
    y_sample = x[n_ctx:].reshape(dec_batch, dec_seq, d)
    new_kv = [jnp.stack([kv_ctx[l][u].reshape(batch, seq, 2, HEAD_DIM) for l in range(depth)], axis=1)
              for u in range(4)]
    new_lru = jnp.stack(lru_states, axis=1)
    new_gdn = (jnp.stack(gdn_states, axis=1).reshape(batch, depth, 2, GDN_DK, GDN_H, GDN_DK)
               .transpose(0, 1, 2, 4, 3, 5))
    return (y_prompt, y_sample, new_kv[0], new_kv[1], new_kv[2], new_kv[3], new_lru, new_gdn)
```

```python
import functools

import numpy as np
import jax
import jax.numpy as jnp
from jax import lax
from jax.experimental import pallas as pl
from jax.experimental.pallas import tpu as pltpu

F32 = jnp.float32
BF16 = jnp.bfloat16
HIGHEST = lax.Precision.HIGHEST

D_MODEL = 1024
HEAD_DIM = 64
BRANCH_W = 256
N_BRANCH = 4
GRID_W = 64
WINDOW = 128
LRU_C = 8.0
CONV_W = 4
CONV_LEFT = 2
GDN_H = 4
GDN_DK = 64
CHUNK = 64
D_FF = 4 * D_MODEL
ROPE_BASE = 10000.0
EPS = 1e-6
NEG = -1e30
LOG2E = 1.4426950408889634
Q_SCALE = HEAD_DIM ** -0.5 * LOG2E
GDN_PAD = 112
IN_COLS_PACKED = 2688
HALO = 8
N_PRECISE = 4

V7X_VMEM_LIMIT = 56 * 1024 * 1024


def _cparams(sem, vmem=V7X_VMEM_LIMIT):
    return pltpu.CompilerParams(dimension_semantics=sem, vmem_limit_bytes=vmem)


def _layer_spec(arr, l):
    nd = arr.ndim - 1
    return pl.BlockSpec((None,) + arr.shape[1:], lambda *_: (l,) + (0,) * nd)


def _bdot(a, b):
    return jnp.dot(a.astype(BF16), b.astype(BF16), preferred_element_type=F32)


def _split(x):
    hi = x.astype(BF16)
    lo = (x - hi.astype(F32)).astype(BF16)
    return hi, lo


def _split3(x):
    hi = x.astype(BF16)
    r = x - hi.astype(F32)
    mid = r.astype(BF16)
    lo = (r - mid.astype(F32)).astype(BF16)
    return hi, mid, lo


def _group_sum(x, ones_bd):
    return jnp.dot(x.astype(BF16), ones_bd, preferred_element_type=F32)


def _sigmoid(x):
    return 0.5 * jnp.tanh(0.5 * x) + 0.5


def _silu(x):
    return x * _sigmoid(x)


def _softplus(x):
    return jnp.maximum(x, 0.0) + jnp.log1p(jnp.exp(-jnp.abs(x)))


def _modulate(x, g, shift, scale):
    ms = jnp.mean(x * x, axis=-1, keepdims=True)
    return (x * lax.rsqrt(ms + EPS) * g) * (1.0 + scale) + shift


def _lane_iota(shape):
    return lax.broadcasted_iota(jnp.int32, shape, len(shape) - 1)


def _row_iota(shape):
    return lax.broadcasted_iota(jnp.int32, shape, len(shape) - 2)


def _mod_kernel(cond_ref, w_ref, b_ref, o_ref):
    c = cond_ref[...]
    o_ref[0] = jnp.dot(_silu(c), w_ref[0], precision=HIGHEST, preferred_element_type=F32) + b_ref[0]


def _mod_call(cond8, mod_w, mod_b):
    depth, d, n = mod_w.shape
    tn = 1536
    return pl.pallas_call(
        _mod_kernel,
        grid=(depth, n // tn),
        in_specs=[
            pl.BlockSpec((8, d), lambda l, j: (0, 0)),
            pl.BlockSpec((1, d, tn), lambda l, j: (l, 0, j)),
            pl.BlockSpec((1, 1, tn), lambda l, j: (l, 0, j)),
        ],
        out_specs=pl.BlockSpec((1, 8, tn), lambda l, j: (l, 0, j)),
        out_shape=jax.ShapeDtypeStruct((depth, 8, n), F32),
        compiler_params=_cparams(("parallel", "parallel")),
        name="mod_vectors",
    )(cond8, mod_w, mod_b.reshape(depth, 1, n))


def _rope(x, cos, sin):
    outs = []
    for j in range(x.shape[1] // 128):
        xb = x[:, 128 * j:128 * (j + 1)]
        lane = _lane_iota(xb.shape)
        sw = jnp.where((lane & 16) == 0, pltpu.roll(xb, 112, 1), pltpu.roll(xb, 16, 1))
        outs.append(xb * cos + sw * sin)
    return outs[0] if len(outs) == 1 else jnp.concatenate(outs, axis=1)


def _head_rms(x, gain, ones_bd):
    ms = _group_sum(x * x, ones_bd) * (1.0 / HEAD_DIM)
    return x * lax.rsqrt(ms + EPS) * gain


def _centred_conv(g, w, tm):
    rows = g.shape[0]
    acc = None
    for j in range(CONV_W):
        sh = (CONV_LEFT - j) % rows
        gj = g if sh == 0 else pltpu.roll(g, sh, 0)
        term = gj[HALO:HALO + tm] * w[j:j + 1, :]
        acc = term if acc is None else acc + term
    return acc


def _inproj_kernel(modrow_ref, ropeblk_ref, isstart_ref, isend_ref,
                   xpc_ref, xc_ref, xnc_ref, xpd_ref, xd_ref, xnd_ref, mod_ref, g1_ref, w_ref, cos_ref, sin_ref,
                   aqg_ref, akg_ref, dqg_ref, dkg_ref, lcw_ref, lcb_ref, gcw_ref,
                   alog_ref, dtb_ref, ones_ref,
                   qa_ref, ka_ref, va_ref, lx_ref, lg_ref, gq_ref, gk_ref, gv_ref, gz_ref, ggb_ref,
                   qd_ref, kd_ref, vd_ref, *, tm, n_ctx_tiles):
    i = pl.program_id(0)
    is_ctx = i < n_ctx_tiles
    pick = lambda c_ref, d_ref: jnp.where(is_ctx, c_ref[...], d_ref[...])
    xfull = jnp.concatenate([pick(xpc_ref, xpd_ref), pick(xc_ref, xd_ref), pick(xnc_ref, xnd_ref)],
                            axis=0)
    h = _modulate(xfull, g1_ref[...], mod_ref[0, 0:1, :], mod_ref[0, 1:2, :])
    p = jnp.dot(h.astype(BF16), w_ref[...], preferred_element_type=F32)
    hb, hm = slice(0, tm + 2 * HALO), slice(HALO, HALO + tm)
    proj = lambda rows, c0, c1: p[rows, c0:c1]

    ones_bd = ones_ref[...]
    cos = cos_ref[...]
    sin = sin_ref[...]

    pa = proj(hm, 0, 512)
    qa_ref[...] = _rope(_head_rms(pa[:, 0:256], aqg_ref[...], ones_bd), cos, sin)
    ka_ref[...] = _rope(_head_rms(pa[:, 256:384], akg_ref[...], ones_bd[:128, :128]), cos, sin)
    va_ref[...] = pa[:, 384:512]

    row = _row_iota((tm + 2 * HALO, 1))
    keep = jnp.logical_and(jnp.logical_or(row >= HALO, isstart_ref[i] == 0),
                           jnp.logical_or(row < HALO + tm, isend_ref[i] == 0))

    lxg = jnp.where(keep, proj(hb, 512, 768), 0.0)
    lx_ref[...] = _centred_conv(lxg, lcw_ref[...], tm) + lcb_ref[...]
    lg_ref[...] = proj(hm, 768, 1024)

    qkv = jnp.where(keep, proj(hb, 1024, 1792), 0.0)
    qkv = _silu(_centred_conv(qkv, gcw_ref[...], tm))
    gq = qkv[:, 0:256]
    gk = qkv[:, 256:512]
    gq_ref[...] = gq * lax.rsqrt(_group_sum(gq * gq, ones_bd) + EPS) * (GDN_DK ** -0.5)
    gk_ref[...] = gk * lax.rsqrt(_group_sum(gk * gk, ones_bd) + EPS)
    gv_ref[...] = qkv[:, 512:768]
    pz = proj(hm, 1792, 2176)
    gz_ref[...] = pz[:, 0:256]
    ab = pz[:, 256:384]
    g = -jnp.exp(alog_ref[...]) * _softplus(ab + dtb_ref[...])
    lane = _lane_iota(ab.shape)
    ggb_ref[...] = jnp.where(lane < 2 * GDN_H, g, _sigmoid(ab))

    pd = proj(hm, 2176, 2688)
    qd_ref[...] = _rope(_head_rms(pd[:, 0:256], dqg_ref[...], ones_bd), cos, sin)
    kd_ref[...] = _rope(_head_rms(pd[:, 256:384], dkg_ref[...], ones_bd[:128, :128]), cos, sin)
    vd_ref[...] = pd[:, 384:512]


def _x_pair_specs(x_pair, tm):
    x_c, x_d = x_pair
    d = x_c.shape[1]
    nct = x_c.shape[0] // tm
    hb = tm // HALO
    specs = []
    for arr, first in ((x_c, 0), (x_d, nct)):
        nblk = arr.shape[0] // tm
        last_hb = arr.shape[0] // HALO - 1
        tile = lambda i, first=first, nblk=nblk: jnp.clip(i - first, 0, nblk - 1)
        specs += [
            pl.BlockSpec((HALO, d), lambda i, *_, t=tile, m=last_hb: (jnp.clip(t(i) * hb - 1, 0, m), 0)),
            pl.BlockSpec((tm, d), lambda i, *_, t=tile: (t(i), 0)),
            pl.BlockSpec((HALO, d), lambda i, *_, t=tile, m=last_hb: (jnp.clip((t(i) + 1) * hb, 0, m), 0)),
        ]
    return specs


def _inproj_call(x_pair, meta, mod_all, g1, w_packed, cos_t, sin_t, vecs, ones_bd, *, l, tm):
    x_c, x_d = x_pair
    n, d = x_c.shape[0] + x_d.shape[0], x_c.shape[1]
    nt = n // tm
    row_tile = lambda w: pl.BlockSpec((tm, w), lambda i, *_: (i, 0))
    in_specs = _x_pair_specs(x_pair, tm) + [
        pl.BlockSpec((None, 1, 6, d), lambda i, modrow, *_: (l, modrow[i], 0, 0)),
        _layer_spec(g1, l),
        _layer_spec(w_packed, l),
        pl.BlockSpec((tm, 128), lambda i, modrow, ropeblk, *_: (ropeblk[i], 0)),
        pl.BlockSpec((tm, 128), lambda i, modrow, ropeblk, *_: (ropeblk[i], 0)),
    ] + [_layer_spec(v, l) for v in vecs] + [pl.BlockSpec((256, 256), lambda i, *_: (0, 0))]
    widths = (256, 128, 128, 256, 256, 256, 256, 256, 256, 128, 256, 128, 128)
    grid_spec = pltpu.PrefetchScalarGridSpec(
        num_scalar_prefetch=4,
        grid=(nt,),
        in_specs=in_specs,
        out_specs=[row_tile(w) for w in widths],
    )
    return pl.pallas_call(
        functools.partial(_inproj_kernel, tm=tm, n_ctx_tiles=x_c.shape[0] // tm),
        grid_spec=grid_spec,
        out_shape=[jax.ShapeDtypeStruct((n, w), F32) for w in widths],
        compiler_params=_cparams(("parallel",)),
        name="in_projection",
    )(*meta, x_c, x_c, x_c, x_d, x_d, x_d, mod_all, g1, w_packed, cos_t, sin_t, *vecs, ones_bd)


def _head_q(q, j, g):
    qj = q[:, 128 * j:128 * (j + 1)]
    lane = _lane_iota(qj.shape)
    sel = (lane < HEAD_DIM) if g == 0 else (lane >= HEAD_DIM)
    return jnp.where(sel, qj, 0.0).astype(BF16)


def _place_heads(res, j):
    r0 = res[0] if j == 0 else pltpu.roll(res[0], HEAD_DIM, 1)
    r1 = res[1] if j == 1 else pltpu.roll(res[1], HEAD_DIM, 1)
    lane = _lane_iota(r0.shape)
    return jnp.where(lane < HEAD_DIM, r0, r1)


def _attend(q, ks, vs, masks, sinks, ks_sw=None):
    if ks_sw is None:
        ks_sw = [pltpu.roll(k, HEAD_DIM, 1) for k in ks]
    blocks = []
    for j in range(2):
        res = []
        for g in range(2):
            qm = _head_q(q, j, g)
            ss = []
            for k, ksw, mk in zip(ks, ks_sw, masks):
                kk = k if g == j else ksw
                s = lax.dot_general(qm, kk, (((1,), (1,)), ((), ())), preferred_element_type=F32)
                if mk is not None:
                    s = jnp.where(mk, s, NEG)
                ss.append(s)
            m = ss[0].max(axis=-1, keepdims=True)
            for s in ss[1:]:
                m = jnp.maximum(m, s.max(axis=-1, keepdims=True))
            if sinks is not None:
                m = jnp.maximum(m, sinks[2 * j + g])
            den = None
            acc = None
            for s, v in zip(ss, vs):
                e = jnp.exp2(s - m)
                dsum = e.sum(axis=-1, keepdims=True)
                den = dsum if den is None else den + dsum
                pv = jnp.dot(e.astype(BF16), v, preferred_element_type=F32)
                acc = pv if acc is None else acc + pv
            if sinks is not None:
                den = den + jnp.exp2(sinks[2 * j + g] - m)
            res.append(acc / den)
        blocks.append(_place_heads(res, j))
    return jnp.concatenate(blocks, axis=1)


def _bf(x):
    return x.astype(BF16)


def _attn_ctx_kernel(sink_ref, qa_ref, ka_ref, va_ref, qd_ref, kd_ref, vd_ref, oa_ref, od_ref, *, l):
    sinks = [sink_ref[4 * l + u] * LOG2E for u in range(4)]
    oa_ref[...] = _attend(qa_ref[...], [_bf(ka_ref[...])], [_bf(va_ref[...])], [None], sinks)
    od_ref[...] = _attend(qd_ref[...], [_bf(kd_ref[...])], [_bf(vd_ref[...])], [None], None)


def _attn_ctx_call(sink, qa, ka, va, qd, kd, vd, *, l, n_seq, t):
    blk = lambda w: pl.BlockSpec((t, w), lambda b, *_: (b, 0))
    grid_spec = pltpu.PrefetchScalarGridSpec(
        num_scalar_prefetch=1, grid=(n_seq,),
        in_specs=[blk(256), blk(128), blk(128), blk(256), blk(128), blk(128)],
        out_specs=[blk(256), blk(256)])
    return pl.pallas_call(
        functools.partial(_attn_ctx_kernel, l=l), grid_spec=grid_spec,
        out_shape=[jax.ShapeDtypeStruct((n_seq * t, 256), F32)] * 2,
        compiler_params=_cparams(("parallel",)),
        name="attn_context",
    )(sink, qa, ka, va, qd, kd, vd)


def _attn_band_kernel(sink_ref, q_ref, k_ref, v_ref, ck_ref, cv_ref, o_ref, *, l, tq, t):
    i = pl.program_id(1)
    start = pl.multiple_of(i * tq, tq)
    prev = pl.multiple_of(jnp.maximum(start - WINDOW, 0), WINDOW)
    nxt = pl.multiple_of(jnp.minimum(start + tq, t - WINDOW), WINDOW)
    k_loc = jnp.concatenate([k_ref[pl.ds(prev, WINDOW), :], k_ref[pl.ds(start, tq), :],
                             k_ref[pl.ds(nxt, WINDOW), :]], axis=0)
    v_loc = jnp.concatenate([v_ref[pl.ds(prev, WINDOW), :], v_ref[pl.ds(start, tq), :],
                             v_ref[pl.ds(nxt, WINDOW), :]], axis=0)
    shape = (tq, tq + 2 * WINDOW)
    qpos = start + _row_iota(shape)
    kpos = start - WINDOW + _lane_iota(shape)
    mask = (kpos >= 0) & (kpos < t) & (jnp.abs(qpos - kpos) <= WINDOW)
    sinks = [sink_ref[4 * l + u] * LOG2E for u in range(4)]
    o_ref[...] = _attend(q_ref[...], [_bf(ck_ref[0]), _bf(k_loc)], [_bf(cv_ref[0]), _bf(v_loc)],
                         [None, mask], sinks)


def _cache_spec(c, l):
    return pl.BlockSpec((1, None) + c.shape[2:], lambda b, i, *_: (b, l, 0, 0))


def _attn_band_call(sink, q, k, v, ck, cv, *, l, n_seq, t, row0, tq):
    nq = t // tq
    qb0 = row0 // tq
    sb0 = row0 // t
    grid_spec = pltpu.PrefetchScalarGridSpec(
        num_scalar_prefetch=1, grid=(n_seq, nq),
        in_specs=[
            pl.BlockSpec((tq, 256), lambda b, i, *_: (qb0 + b * nq + i, 0)),
            pl.BlockSpec((t, 128), lambda b, i, *_: (sb0 + b, 0)),
            pl.BlockSpec((t, 128), lambda b, i, *_: (sb0 + b, 0)),
            _cache_spec(ck, l), _cache_spec(cv, l),
        ],
        out_specs=pl.BlockSpec((tq, 256), lambda b, i, *_: (b * nq + i, 0)))
    return pl.pallas_call(
        functools.partial(_attn_band_kernel, l=l, tq=tq, t=t), grid_spec=grid_spec,
        out_shape=jax.ShapeDtypeStruct((n_seq * t, 256), F32),
        compiler_params=_cparams(("parallel", "parallel")),
        name="attn_banded",
    )(sink, q, k, v, ck, cv)


def _attn_full_kernel(q_ref, k_ref, v_ref, ck_ref, cv_ref, o_ref, ks_ref, vs_ref, *, past, t):
    @pl.when(pl.program_id(1) == 0)
    def _():
        kc = _bf(ck_ref[0])
        kl = _bf(k_ref[...])
        ks_ref[0, 0:past, :] = kc
        ks_ref[0, past:past + t, :] = kl
        ks_ref[1, 0:past, :] = pltpu.roll(kc, HEAD_DIM, 1)
        ks_ref[1, past:past + t, :] = pltpu.roll(kl, HEAD_DIM, 1)
        vs_ref[0:past, :] = _bf(cv_ref[0])
        vs_ref[past:past + t, :] = _bf(v_ref[...])

    o_ref[...] = _attend(q_ref[...], [ks_ref[0]], [vs_ref[...]], [None], None, ks_sw=[ks_ref[1]])


def _attn_full_call(q, k, v, ck, cv, *, l, n_seq, t, row0, tq):
    nq = t // tq
    qb0 = row0 // tq
    sb0 = row0 // t
    past = ck.shape[2]
    assert past % 16 == 0
    return pl.pallas_call(
        functools.partial(_attn_full_kernel, past=past, t=t), grid=(n_seq, nq),
        scratch_shapes=[pltpu.VMEM((2, past + t, 128), BF16), pltpu.VMEM((past + t, 128), BF16)],
        in_specs=[
            pl.BlockSpec((tq, 256), lambda b, i: (qb0 + b * nq + i, 0)),
            pl.BlockSpec((t, 128), lambda b, i: (sb0 + b, 0)),
            pl.BlockSpec((t, 128), lambda b, i: (sb0 + b, 0)),
            _cache_spec(ck, l), _cache_spec(cv, l),
        ],
        out_specs=pl.BlockSpec((tq, 256), lambda b, i: (b * nq + i, 0)),
        out_shape=jax.ShapeDtypeStruct((n_seq * t, 256), F32),
        compiler_params=_cparams(("parallel", "arbitrary")),
        name="attn_full",
    )(q, k, v, ck, cv)


def _gelu_tanh(x):
    return 0.5 * x * (1.0 + jnp.tanh(0.7978845608028654 * (x + 0.044715 * (x * x * x))))


def _lru_kernel(lx_ref, lg_ref, w_ref, b_ref, lam_ref, h0_ref, y_ref, st_ref,
                hf_ref, hb_ref, af_ref, bf_ref, ab_ref, bb_ref, *, t, chunk):
    nc = t // chunk
    ng = chunk // 8
    sp = _softplus(-lam_ref[...])
    row8 = _row_iota((8, BRANCH_W))

    def gates(xc, d, a_ref, b2_ref):
        pre = _bdot(xc, w_ref[:, 512 * d:512 * (d + 1)]) + b_ref[:, 512 * d:512 * (d + 1)]
        r = _sigmoid(pre[:, 0:256])
        ig = _sigmoid(pre[:, 256:512])
        a = jnp.exp((-LRU_C) * r * sp[d:d + 1, :])
        a_ref[...] = a
        b2_ref[...] = jnp.sqrt(1.0 - a * a) * (ig * xc)

    def local_scan(a8, b8, reverse):
        for s in (1, 2, 4):
            sh = (8 - s) if reverse else s
            m = (row8 < 8 - s) if reverse else (row8 >= s)
            a_sh = pltpu.roll(a8, sh, 0)
            b_sh = pltpu.roll(b8, sh, 0)
            b8 = jnp.where(m, a8 * b_sh + b8, b8)
            a8 = jnp.where(m, a8 * a_sh, a8)
        return a8, b8

    def chunk_body(cc, carry):
        rf = pl.multiple_of(cc * chunk, chunk)
        rb = pl.multiple_of((nc - 1 - cc) * chunk, chunk)
        gates(lx_ref[pl.ds(rf, chunk), :], 0, af_ref, bf_ref)
        gates(lx_ref[pl.ds(rb, chunk), :], 1, ab_ref, bb_ref)

        def grp(gg, c2):
            hf, hb = c2
            gf = pl.multiple_of(gg * 8, 8)
            gb = pl.multiple_of((ng - 1 - gg) * 8, 8)
            a8, b8 = local_scan(af_ref[pl.ds(gf, 8), :], bf_ref[pl.ds(gf, 8), :], False)
            h8 = b8 + a8 * hf
            hf_ref[pl.ds(rf + gf, 8), :] = h8
            a8, b8 = local_scan(ab_ref[pl.ds(gb, 8), :], bb_ref[pl.ds(gb, 8), :], True)
            g8 = b8 + a8 * hb
            hb_ref[pl.ds(rb + gb, 8), :] = g8
            return (jnp.broadcast_to(h8[7:8, :], (8, BRANCH_W)), jnp.broadcast_to(g8[0:1, :], (8, BRANCH_W)))

        return lax.fori_loop(0, ng, grp, carry, unroll=4)

    init = (jnp.broadcast_to(h0_ref[0, 0:1, :], (8, BRANCH_W)), jnp.broadcast_to(h0_ref[0, 1:2, :], (8, BRANCH_W)))
    hf, hb = lax.fori_loop(0, nc, chunk_body, init)
    st_ref[0, 0:1, :] = hf[0:1, :]
    st_ref[0, 1:2, :] = hb[0:1, :]

    def combine(c, _):
        rows = pl.ds(pl.multiple_of(c * chunk, chunk), chunk)
        y_ref[rows, :] = (hf_ref[rows, :] + hb_ref[rows, :]) * _gelu_tanh(lg_ref[rows, :])
        return 0

    lax.fori_loop(0, nc, combine, 0)


def _lru_call(lx, lg, w, b, lam, h0, *, l, l_state, n_seq, t, row0):
    sb0 = row0 // t
    chunk = min(t, 256)
    seq = lambda: pl.BlockSpec((t, BRANCH_W), lambda s: (sb0 + s, 0))
    return pl.pallas_call(
        functools.partial(_lru_kernel, t=t, chunk=chunk),
        grid=(n_seq,),
        in_specs=[seq(), seq(), _layer_spec(w, l), _layer_spec(b, l), _layer_spec(lam, l),
                  pl.BlockSpec((1, None, 2, BRANCH_W), lambda s: (s, l_state, 0, 0))],
        out_specs=[pl.BlockSpec((t, BRANCH_W), lambda s: (s, 0)),
                   pl.BlockSpec((1, 2, BRANCH_W), lambda s: (s, 0, 0))],
        out_shape=[jax.ShapeDtypeStruct((n_seq * t, BRANCH_W), F32),
                   jax.ShapeDtypeStruct((n_seq, 2, BRANCH_W), F32)],
        scratch_shapes=[pltpu.VMEM((t, BRANCH_W), F32), pltpu.VMEM((t, BRANCH_W), F32)]
                       + [pltpu.VMEM((chunk, BRANCH_W), F32)] * 4,
        compiler_params=_cparams(("parallel",)),
        name="rglru",
    )(lx, lg, w, b, lam, h0)


def _rows_dot_exact(x, w01):
    r = x.shape[0]
    out = jnp.dot(jnp.concatenate(_split3(x), axis=0), w01, preferred_element_type=F32)
    return out[0:r] + out[r:2 * r] + out[2 * r:3 * r]


def _dot01_exact(m01, y):
    return sum(jnp.dot(m01, part, preferred_element_type=F32) for part in _split3(y))


def _gdn_kernel(q_ref, k_ref, v_ref, ggb_ref, s0_ref, o_ref, sT_ref, s_ref, *, d, reverse, n_par, n_chunk, n_tile):
    ti = pl.program_id(1)
    w4 = GDN_H * GDN_DK
    n_all = n_par * n_chunk
    tt = n_all * CHUNK
    blockmask = (_row_iota((w4, w4)) // CHUNK) == (_lane_iota((w4, w4)) // CHUNK)

    def expand_rows(y):
        yt = jnp.concatenate([y] * GDN_H, axis=0)
        zero = jnp.zeros((), y.dtype)
        parts = [jnp.where(blockmask, yt[:, w4 * u:w4 * (u + 1)], zero) for u in range(y.shape[1] // w4)]
        return parts[0] if len(parts) == 1 else jnp.concatenate(parts, axis=1)

    def heads_dot3(lhs, y):
        r = lhs.shape[0]
        lh, ll = _split(lhs)
        yh, yl = _split(y)
        out = jnp.dot(jnp.concatenate([lh, ll], axis=0), expand_rows(yh), preferred_element_type=F32)
        return out[0:r] + out[r:2 * r] + jnp.dot(lh, expand_rows(yl), preferred_element_type=F32)

    def heads_dot2(lhs, y):
        r = lhs.shape[0]
        lh, ll = _split(lhs)
        out = jnp.dot(jnp.concatenate([lh, ll], axis=0), expand_rows(y.astype(BF16)), preferred_element_type=F32)
        return out[0:r] + out[r:2 * r]

    def heads_dot1(lhs, y):
        return jnp.dot(lhs.astype(BF16), expand_rows(y.astype(BF16)), preferred_element_type=F32)

    @pl.when(ti == 0)
    def _():
        for p in range(n_par):
            s_ref[p] = expand_rows(s0_ref[p, 0])

    ri = _row_iota((CHUNK, w4))
    cj = _lane_iota((CHUNK, w4)) % CHUNK
    incl = (cj >= ri) if reverse else (cj <= ri)
    strict = (cj > ri) if reverse else (cj < ri)
    eye_sbs = (cj == ri)
    rep = eye_sbs.astype(BF16)

    er = _row_iota((128, 2 * w4))
    el = _lane_iota((128, 2 * w4))
    e_gb = (er == jnp.where(el < w4, d * GDN_H, 2 * GDN_H + d * GDN_H) + (el % w4) // CHUNK).astype(BF16)
    gbe = _rows_dot_exact(ggb_ref[...], e_gb)
    g_all = gbe[:, 0:w4]
    beta_all = gbe[:, w4:2 * w4]
    tr = _row_iota((tt, tt))
    tc = _lane_iota((tt, tt))
    same_chunk = (tr // CHUNK) == (tc // CHUNK)
    tri_bd = (same_chunk & ((tc >= tr) if reverse else (tc <= tr))).astype(BF16)
    ones_bd = same_chunk.astype(BF16)
    gc_all = _dot01_exact(tri_bd, g_all)
    diag_sel = (_row_iota((tt, w4)) % CHUNK) == (_lane_iota((tt, w4)) % CHUNK)
    gct_all = _dot01_exact(ones_bd, jnp.where(diag_sel, gc_all, 0.0))

    cs = range(n_all)
    rows = [slice(c * CHUNK, (c + 1) * CHUNK) for c in cs]
    q = [q_ref[r, :] for r in rows]
    k = [k_ref[r, :] for r in rows]
    v = [v_ref[r, :] for r in rows]
    beta = [beta_all[r, :] for r in rows]
    gc = [gc_all[r, :] for r in rows]
    decay = [jnp.where(incl, jnp.exp(jnp.where(incl, gc[c] - gct_all[rows[c], :], 0.0)), 0.0) for c in cs]
    kb = [k[c].astype(BF16) for c in cs]
    w_k = [jnp.where(blockmask,
                     lax.dot_general(kb[c], rep, (((0,), (0,)), ((), ())), preferred_element_type=F32),
                     0.0).astype(BF16) for c in cs]
    kq = [jnp.dot(jnp.concatenate([kb[c], q[c].astype(BF16)], axis=0), w_k[c], preferred_element_type=F32)
          for c in cs]
    qk = [(kq[c][CHUNK:2 * CHUNK] * decay[c]).astype(BF16) for c in cs]
    a = [jnp.where(strict, beta[c] * kq[c][0:CHUNK] * decay[c], 0.0) for c in cs]
    t_inv = [jnp.where(eye_sbs, 1.0, 0.0) - a[c] for c in cs]
    pw = [heads_dot3(a[c], a[c]) for c in cs]
    for stage in range(4):
        mm = heads_dot3 if stage < N_PRECISE else heads_dot1
        both = [mm(jnp.concatenate([t_inv[c], pw[c]], axis=0), pw[c]) for c in cs]
        t_inv = [t_inv[c] + both[c][0:CHUNK] for c in cs]
        pw = [both[c][CHUNK:2 * CHUNK] for c in cs]
    t_inv = [t_inv[c] + heads_dot1(t_inv[c], pw[c]) for c in cs]
    egc = [jnp.exp(gc[c]) for c in cs]
    sol = [heads_dot2(t_inv[c], jnp.concatenate([v[c] * beta[c], k[c] * (beta[c] * egc[c])], axis=1)) for c in cs]
    g_last = [gc[c][0:1, :] if reverse else gc[c][CHUNK - 1:CHUNK, :] for c in cs]
    wq = [jnp.concatenate([sol[c][:, w4:2 * w4], q[c] * egc[c]], axis=0).astype(BF16) for c in cs]
    kdec = [(k[c] * jnp.exp(g_last[c] - gc[c])).astype(BF16) for c in cs]

    for cc in range(n_chunk):
        for p in range(n_par):
            c = p * n_chunk + ((n_chunk - 1 - cc) if reverse else cc)
            s = s_ref[p]
            ws_qs = jnp.dot(wq[c], s.astype(BF16), preferred_element_type=F32)
            v_new = sol[c][:, 0:w4] - ws_qs[0:CHUNK]
            vb = v_new.astype(BF16)
            o_ref[rows[c], :] = (
                ws_qs[CHUNK:2 * CHUNK] + jnp.dot(qk[c], expand_rows(vb), preferred_element_type=F32))
            upd = lax.dot_general(kdec[c], vb, (((0,), (0,)), ((), ())), preferred_element_type=F32)
            s_ref[p] = s * jnp.exp(g_last[c]) + jnp.where(blockmask, upd, 0.0)

    @pl.when(ti == n_tile - 1)
    def _():
        for p in range(n_par):
            s = s_ref[p]
            sT_ref[p, 0] = s[0:64] + s[64:128] + s[128:192] + s[192:256]


def _gdn_call(q, k, v, ggb, s0, *, d, l_state, n_seq, t, row0, tt, n_par):
    reverse = d == 1
    n_tile = t // tt
    assert n_par == 1 or n_tile == 1
    b0 = row0 // (tt * n_par)
    tidx = (lambda i: n_tile - 1 - i) if reverse else (lambda i: i)
    blk = lambda w: pl.BlockSpec((tt * n_par, w), lambda s, i: (b0 + s * n_tile + tidx(i), 0))
    return pl.pallas_call(
        functools.partial(_gdn_kernel, d=d, reverse=reverse, n_par=n_par, n_chunk=tt // CHUNK, n_tile=n_tile),
        grid=(n_seq // n_par, n_tile),
        in_specs=[blk(256), blk(256), blk(256), blk(128),
                  pl.BlockSpec((n_par, None, 1, CHUNK, 256), lambda s, i: (s, l_state, d, 0, 0))],
        out_specs=[pl.BlockSpec((tt * n_par, 256), lambda s, i: (s * n_tile + tidx(i), 0)),
                   pl.BlockSpec((n_par, 1, CHUNK, 256), lambda s, i: (s, 0, 0, 0))],
        out_shape=[jax.ShapeDtypeStruct((n_seq * t, 256), F32),
                   jax.ShapeDtypeStruct((n_seq, 1, CHUNK, 256), F32)],
        scratch_shapes=[pltpu.VMEM((n_par, 256, 256), F32)],
        compiler_params=_cparams(("parallel", "arbitrary")),
        name="gdn_bwd" if reverse else "gdn_fwd",
    )(q, k, v, ggb, s0)


def _merge_kernel(modrow_ref, xc_ref, xd_ref, mod_ref, g1_ref,
                  oac_ref, oad_ref, obc_ref, obd_ref, ofc_ref, ofd_ref, orc_ref, ord_ref, odc_ref, odd_ref,
                  gz_ref, gng_ref, ones_ref, wm_ref, bm_ref, wb_ref, wo_ref, y_ref, *, n_ctx_tiles):
    is_ctx = pl.program_id(0) < n_ctx_tiles
    pick = lambda c_ref, d_ref: jnp.where(is_ctx, c_ref[...], d_ref[...])
    x = pick(xc_ref, xd_ref)
    h = _modulate(x, g1_ref[...], mod_ref[0, 0:1, :], mod_ref[0, 1:2, :]).astype(BF16)
    oc = pick(ofc_ref, ofd_ref) + pick(orc_ref, ord_ref)
    ms = _group_sum(oc * oc, ones_ref[...]) * (1.0 / GDN_DK)
    oc = (oc * lax.rsqrt(ms + EPS) * gng_ref[...]) * _silu(gz_ref[...])
    branches = (pick(oac_ref, oad_ref), pick(obc_ref, obd_ref), oc, pick(odc_ref, odd_ref))
    acc = None
    for m in range(N_BRANCH):
        cols = slice(D_MODEL * m, D_MODEL * (m + 1))
        gate = _sigmoid(jnp.dot(h, wm_ref[:, cols], preferred_element_type=F32) + bm_ref[:, cols])
        term = gate * jnp.dot(branches[m].astype(BF16), wb_ref[m], preferred_element_type=F32)
        acc = term if acc is None else acc + term
    y = jnp.dot(acc.astype(BF16), wo_ref[...], preferred_element_type=F32)
    y_ref[...] = x + mod_ref[0, 2:3, :] * y


def _merge_call(modrow, x_pair, mod_all, g1, pairs, gz, gng, ones_bd, wm, bm, wb, wo, *, l, tm):
    x_c, x_d = x_pair
    n, d = x_c.shape[0] + x_d.shape[0], x_c.shape[1]
    nct = x_c.shape[0] // tm
    row_tile = lambda w: pl.BlockSpec((tm, w), lambda i, *_: (i, 0))
    pair = lambda w: [pl.BlockSpec((tm, w), lambda i, *_: (jnp.minimum(i, nct - 1), 0)),
                      pl.BlockSpec((tm, w), lambda i, *_: (jnp.maximum(i - nct, 0), 0))]
    pair_specs = [sp for _ in pairs for sp in pair(256)]
    grid_spec = pltpu.PrefetchScalarGridSpec(
        num_scalar_prefetch=1, grid=(n // tm,),
        in_specs=pair(d) + [pl.BlockSpec((None, 1, 6, d), lambda i, modrow: (l, modrow[i], 0, 0)),
                            _layer_spec(g1, l)] + pair_specs
                 + [row_tile(256), _layer_spec(gng, l), pl.BlockSpec((256, 256), lambda i, *_: (0, 0)),
                    _layer_spec(wm, l), _layer_spec(bm, l), _layer_spec(wb, l), _layer_spec(wo, l)],
        out_specs=row_tile(d))
    flat = [a for pr in pairs for a in pr]
    return pl.pallas_call(
        functools.partial(_merge_kernel, n_ctx_tiles=nct), grid_spec=grid_spec,
        out_shape=jax.ShapeDtypeStruct((n, d), F32),
        compiler_params=_cparams(("parallel",)),
        name="branch_merge",
    )(modrow, x_c, x_d, mod_all, g1, *flat, gz, gng, ones_bd, wm, bm, wb, wo)


def _mlp_kernel(modrow_ref, x_ref, mod_ref, g2_ref, w1_ref, w2_ref, yc_ref, yd_ref, *, n_ctx_tiles):
    x = x_ref[...]
    h = _modulate(x, g2_ref[...], mod_ref[0, 3:4, :], mod_ref[0, 4:5, :]).astype(BF16)
    acc = None
    for j in range(D_FF // D_MODEL):
        cols = slice(D_MODEL * j, D_MODEL * (j + 1))
        a = jnp.maximum(jnp.dot(h, w1_ref[:, cols], preferred_element_type=F32), 0.0)
        term = jnp.dot((a * a).astype(BF16), w2_ref[cols, :], preferred_element_type=F32)
        acc = term if acc is None else acc + term
    y = x + mod_ref[0, 5:6, :] * acc
    is_ctx = pl.program_id(0) < n_ctx_tiles

    @pl.when(is_ctx)
    def _():
        yc_ref[...] = y

    @pl.when(jnp.logical_not(is_ctx))
    def _():
        yd_ref[...] = y


def _mlp_call(modrow, x, mod_all, g2, w1, w2, *, l, tm, n_ctx):
    n, d = x.shape
    nct = n_ctx // tm
    row_tile = pl.BlockSpec((tm, d), lambda i, *_: (i, 0))
    grid_spec = pltpu.PrefetchScalarGridSpec(
        num_scalar_prefetch=1, grid=(n // tm,),
        in_specs=[row_tile, pl.BlockSpec((None, 1, 6, d), lambda i, modrow: (l, modrow[i], 0, 0)),
                  _layer_spec(g2, l), _layer_spec(w1, l), _layer_spec(w2, l)],
        out_specs=[pl.BlockSpec((tm, d), lambda i, *_: (jnp.minimum(i, nct - 1), 0)),
                   pl.BlockSpec((tm, d), lambda i, *_: (jnp.maximum(i - nct, 0), 0))])
    return pl.pallas_call(
        functools.partial(_mlp_kernel, n_ctx_tiles=nct), grid_spec=grid_spec,
        out_shape=[jax.ShapeDtypeStruct((n_ctx, d), F32), jax.ShapeDtypeStruct((n - n_ctx, d), F32)],
        compiler_params=_cparams(("arbitrary",)),
        name="mlp",
    )(modrow, x, mod_all, g2, w1, w2)


def _rope_tables(t, tm):
    n_freq = HEAD_DIM // 4
    inv = np.float32(ROPE_BASE) ** (-np.arange(n_freq, dtype=np.float32) / np.float32(n_freq))
    pos = np.arange(t)
    row = (pos // GRID_W).astype(np.float32)[:, None]
    col = (pos % GRID_W).astype(np.float32)[:, None]
    ar = (row * inv).astype(np.float32)
    ac = (col * inv).astype(np.float32)
    cos64 = np.concatenate([np.cos(ar), np.cos(ar), np.cos(ac), np.cos(ac)], axis=1)
    sin64 = np.concatenate([-np.sin(ar), np.sin(ar), -np.sin(ac), np.sin(ac)], axis=1)
    cos = np.concatenate([np.ones((tm, 128), np.float32), np.tile(cos64, (1, 2)).astype(np.float32)], axis=0)
    sin = np.concatenate([np.zeros((tm, 128), np.float32), np.tile(sin64, (1, 2)).astype(np.float32)], axis=0)
    return jnp.asarray(cos), jnp.asarray(sin)


def _tile_meta(n_ctx, n_seq_dec, t_ctx, t_dec, tm):
    nct = n_ctx // tm
    per = t_dec // tm
    ndt = n_seq_dec * per
    idx = np.arange(nct + ndt)
    dec = idx >= nct
    di = np.maximum(idx - nct, 0)
    modrow = np.where(dec, 1 + di // per, 0)
    ropeblk = np.where(dec, 1 + di % per, 0)
    tiles_ctx = max(t_ctx // tm, 1)
    isstart = np.where(dec, di % per == 0, idx % tiles_ctx == 0)
    isend = np.where(dec, di % per == per - 1, idx % tiles_ctx == tiles_ctx - 1)
    as_i32 = lambda a: jnp.asarray(a.astype(np.int32))
    return as_i32(modrow), as_i32(ropeblk), as_i32(isstart), as_i32(isend)


def _block_diag_gates(w):
    depth, two, n, c, _ = w.shape
    eye = jnp.eye(n, dtype=w.dtype)
    return jnp.einsum("ldnij,nm->ldnimj", w, eye).reshape(depth, two, n * c, n * c)


def kernel(x_prompt, x_sample, c, cache_a_k, cache_a_v, cache_d_k, cache_d_v, state_lru, state_gdn, c_ctx, mod_w, mod_b, norm1_g, norm2_g, w_in, a_qn_g, a_kn_g, a_sink, lru_conv_w, lru_conv_b, lru_wr, lru_br, lru_wi, lru_bi, lru_lam, gdn_conv_w, gdn_a_log, gdn_dt_bias, gdn_norm_g, d_qn_g, d_kn_g, w_branch, w_merge, b_merge, w_out, mlp_w1, mlp_w2):
    batch, seq, d = x_prompt.shape
    dec_batch, dec_seq, _ = x_sample.shape
    depth = mod_w.shape[0]
    past = cache_a_k.shape[2]
    n_ctx = batch * seq
    n_dec = dec_batch * dec_seq
    tm = 256
    gdn_par = 2 if batch % 2 == 0 else 1
    gdn_tt = 512 if dec_seq % 512 == 0 else 256
    assert d == D_MODEL and seq % tm == 0 and dec_seq % tm == 0 and tm % seq == 0
    assert dec_batch + 1 <= 8 and n_ctx % dec_seq == 0

    cond8 = jnp.zeros((8, d), F32).at[0].set(c_ctx).at[1:1 + dec_batch].set(c)
    mod_all = _mod_call(cond8, mod_w, mod_b).reshape(depth, 8, 6, d)

    meta = _tile_meta(n_ctx, dec_batch, seq, dec_seq, tm)
    cos_t, sin_t = _rope_tables(dec_seq, tm)
    lane = np.arange(256)
    ones_bd = jnp.asarray((lane[:, None] // HEAD_DIM == lane[None, :] // HEAD_DIM).astype(np.float32)).astype(BF16)

    w_packed = jnp.concatenate([w_in[:, :, :2064], jnp.zeros((depth, d, GDN_PAD), F32), w_in[:, :, 2064:]],
                               axis=2).astype(BF16)
    pad128 = lambda v: jnp.pad(v.reshape(depth, 1, -1), ((0, 0), (0, 0), (0, 128 - v[0].size)))
    vecs = (
        (jnp.tile(a_qn_g, (1, 4)) * Q_SCALE)[:, None, :],
        jnp.tile(a_kn_g, (1, 2))[:, None, :],
        (jnp.tile(d_qn_g, (1, 4)) * Q_SCALE)[:, None, :],
        jnp.tile(d_kn_g, (1, 2))[:, None, :],
        lru_conv_w, lru_conv_b[:, None, :], gdn_conv_w,
        pad128(gdn_a_log), pad128(gdn_dt_bias),
    )
    g1 = norm1_g[:, None, :]
    g2 = norm2_g[:, None, :]
    wr_bd = _block_diag_gates(lru_wr)
    wi_bd = _block_diag_gates(lru_wi)
    w_lru = jnp.concatenate([wr_bd[:, 0], wi_bd[:, 0], wr_bd[:, 1], wi_bd[:, 1]], axis=-1).astype(BF16)
    b_lru = jnp.concatenate([lru_br[:, 0], lru_bi[:, 0], lru_br[:, 1], lru_bi[:, 1]], axis=-1)[:, None, :]
    gng = jnp.tile(gdn_norm_g, (1, 4))[:, None, :]
    wm = w_merge.astype(BF16)
    bm = b_merge[:, None, :]
    wb = w_branch.astype(BF16)
    wo = w_out.astype(BF16)
    w1 = mlp_w1.astype(BF16)
    w2 = mlp_w2.astype(BF16)
    sink = a_sink.reshape(-1)
    caches = [t.reshape(dec_batch, depth, past, 2 * HEAD_DIM) for t in (cache_a_k, cache_a_v, cache_d_k, cache_d_v)]
    s0_dec = state_gdn.transpose(0, 1, 2, 4, 3, 5).reshape(dec_batch, depth, 2, CHUNK, 256)
    zeros_lru = jnp.zeros((batch, 1, 2, BRANCH_W), F32)
    zeros_gdn = jnp.zeros((batch, 1, 2, CHUNK, 256), F32)

    x = (x_prompt.reshape(n_ctx, d), x_sample.reshape(n_dec, d))

    kv_ctx, lru_states, gdn_states = [], [], []
    for l in range(depth):
        (qa, ka, va, lx, lg, gq, gk, gv, gz, ggb, qd, kd, vd) = _inproj_call(
            x, meta, mod_all, g1, w_packed, cos_t, sin_t, vecs, ones_bd, l=l, tm=tm)

        oa_c, od_c = _attn_ctx_call(sink, qa, ka, va, qd, kd, vd, l=l, n_seq=batch, t=seq)
        oa_d = _attn_band_call(sink, qa, ka, va, caches[0], caches[1], l=l, n_seq=dec_batch, t=dec_seq,
                               row0=n_ctx, tq=512)
        od_d = _attn_full_call(qd, kd, vd, caches[2], caches[3], l=l, n_seq=dec_batch, t=dec_seq,
                               row0=n_ctx, tq=256)

        ob_c, st_c = _lru_call(lx, lg, w_lru, b_lru, lru_lam, zeros_lru, l=l, l_state=0, n_seq=batch, t=seq,
                               row0=0)
        ob_d, _ = _lru_call(lx, lg, w_lru, b_lru, lru_lam, state_lru, l=l, l_state=l, n_seq=dec_batch,
                            t=dec_seq, row0=n_ctx)

        oc_pairs, sts = [], []
        for dd in range(2):
            o_c, s_c = _gdn_call(gq, gk, gv, ggb, zeros_gdn, d=dd, l_state=0, n_seq=batch, t=seq, row0=0,
                                 tt=seq, n_par=gdn_par)
            o_d, _ = _gdn_call(gq, gk, gv, ggb, s0_dec, d=dd, l_state=l, n_seq=dec_batch, t=dec_seq,
                               row0=n_ctx, tt=gdn_tt, n_par=1)
            oc_pairs.append((o_c, o_d))
            sts.append(s_c)

        x_mid = _merge_call(meta[0], x, mod_all, g1,
                            [(oa_c, oa_d), (ob_c, ob_d), oc_pairs[0], oc_pairs[1], (od_c, od_d)],
                            gz, gng, ones_bd, wm, bm, wb, wo, l=l, tm=tm)
        x = _mlp_call(meta[0], x_mid, mod_all, g2, w1, w2, l=l, tm=tm, n_ctx=n_ctx)

        kv_ctx.append([t[:n_ctx] for t in (ka, va, kd, vd)])
        lru_states.append(st_c)
        gdn_states.append(jnp.concatenate(sts, axis=1))

    y_prompt = x[0].reshape(batch, seq, d)
    y_sample = x[1].reshape(dec_batch, dec_seq, d)
    new_kv = [jnp.stack([kv_ctx[l][u].reshape(batch, seq, 2, HEAD_DIM) for l in range(depth)], axis=1)
              for u in range(4)]
    new_lru = jnp.stack(lru_states, axis=1)
    new_gdn = (jnp.stack(gdn_states, axis=1).reshape(batch, depth, 2, GDN_DK, GDN_H, GDN_DK)
               .transpose(0, 1, 2, 4, 3, 5))
    return (y_prompt, y_sample, new_kv[0], new_kv[1], new_kv[2], new_kv[3], new_lru, new_gdn)
```

```python
import functools

import numpy as np
import jax
import jax.numpy as jnp
from jax import lax
from jax.experimental import pallas as pl
from jax.experimental.pallas import tpu as pltpu

F32 = jnp.float32
BF16 = jnp.bfloat16
HIGHEST = lax.Precision.HIGHEST

D_MODEL = 1024
HEAD_DIM = 64
BRANCH_W = 256
N_BRANCH = 4
GRID_W = 64
WINDOW = 128
LRU_C = 8.0
CONV_W = 4
CONV_LEFT = 2
GDN_H = 4
GDN_DK = 64
CHUNK = 64
D_FF = 4 * D_MODEL
ROPE_BASE = 10000.0
EPS = 1e-6
NEG = -1e30
LOG2E = 1.4426950408889634
Q_SCALE = HEAD_DIM ** -0.5 * LOG2E
GDN_PAD = 112
IN_COLS_PACKED = 2688
HALO = 8
N_PRECISE = 4

V7X_VMEM_LIMIT = 56 * 1024 * 1024


def _cparams(sem, vmem=V7X_VMEM_LIMIT):
    return pltpu.CompilerParams(dimension_semantics=sem, vmem_limit_bytes=vmem)


def _layer_spec(arr, l, single_buffer=False):
    nd = arr.ndim - 1
    mode = dict(pipeline_mode=pl.Buffered(1)) if single_buffer else {}
    return pl.BlockSpec((None,) + arr.shape[1:], lambda *_: (l,) + (0,) * nd, **mode)


def _bdot(a, b):
    return jnp.dot(a.astype(BF16), b.astype(BF16), preferred_element_type=F32)


def _split(x):
    hi = x.astype(BF16)
    lo = (x - hi.astype(F32)).astype(BF16)
    return hi, lo


def _split3(x):
    hi = x.astype(BF16)
    r = x - hi.astype(F32)
    mid = r.astype(BF16)
    lo = (r - mid.astype(F32)).astype(BF16)
    return hi, mid, lo


def _group_sum(x, ones_bd):
    return jnp.dot(x.astype(BF16), ones_bd, preferred_element_type=F32)


def _sigmoid(x):
    return 0.5 * jnp.tanh(0.5 * x) + 0.5


def _silu(x):
    return x * _sigmoid(x)


def _softplus(x):
    return jnp.maximum(x, 0.0) + jnp.log1p(jnp.exp(-jnp.abs(x)))


def _modulate(x, g, shift, scale):
    ms = jnp.mean(x * x, axis=-1, keepdims=True)
    return (x * lax.rsqrt(ms + EPS) * g) * (1.0 + scale) + shift


def _lane_iota(shape):
    return lax.broadcasted_iota(jnp.int32, shape, len(shape) - 1)


def _row_iota(shape):
    return lax.broadcasted_iota(jnp.int32, shape, len(shape) - 2)


def _mod_kernel(cond_ref, w_ref, b_ref, o_ref):
    c = cond_ref[...]
    o_ref[0] = jnp.dot(_silu(c), w_ref[0], precision=HIGHEST, preferred_element_type=F32) + b_ref[0]


def _mod_call(cond8, mod_w, mod_b):
    depth, d, n = mod_w.shape
    tn = 1536
    return pl.pallas_call(
        _mod_kernel,
        grid=(depth, n // tn),
        in_specs=[
            pl.BlockSpec((8, d), lambda l, j: (0, 0)),
            pl.BlockSpec((1, d, tn), lambda l, j: (l, 0, j)),
            pl.BlockSpec((1, 1, tn), lambda l, j: (l, 0, j)),
        ],
        out_specs=pl.BlockSpec((1, 8, tn), lambda l, j: (l, 0, j)),
        out_shape=jax.ShapeDtypeStruct((depth, 8, n), F32),
        compiler_params=_cparams(("parallel", "parallel")),
        name="mod_vectors",
    )(cond8, mod_w, mod_b.reshape(depth, 1, n))


def _rope(x, cos, sin):
    outs = []
    for j in range(x.shape[1] // 128):
        xb = x[:, 128 * j:128 * (j + 1)]
        lane = _lane_iota(xb.shape)
        sw = jnp.where((lane & 16) == 0, pltpu.roll(xb, 112, 1), pltpu.roll(xb, 16, 1))
        outs.append(xb * cos + sw * sin)
    return outs[0] if len(outs) == 1 else jnp.concatenate(outs, axis=1)


def _head_rms(x, gain, ones_bd):
    ms = _group_sum(x * x, ones_bd) * (1.0 / HEAD_DIM)
    return x * lax.rsqrt(ms + EPS) * gain


def _centred_conv(g, w, tm):
    rows = g.shape[0]
    acc = None
    for j in range(CONV_W):
        sh = (CONV_LEFT - j) % rows
        gj = g if sh == 0 else pltpu.roll(g, sh, 0)
        term = gj[HALO:HALO + tm] * w[j:j + 1, :]
        acc = term if acc is None else acc + term
    return acc


def _inproj_kernel(modrow_ref, ropeblk_ref, isstart_ref, isend_ref,
                   xpc_ref, xc_ref, xnc_ref, xpd_ref, xd_ref, xnd_ref, mod_ref, g1_ref, w_ref, cos_ref, sin_ref,
                   aqg_ref, akg_ref, dqg_ref, dkg_ref, lcw_ref, lcb_ref, gcw_ref,
                   alog_ref, dtb_ref, ones_ref,
                   qa_ref, ka_ref, va_ref, lx_ref, lg_ref, gq_ref, gk_ref, gv_ref, gz_ref, ggb_ref,
                   qd_ref, kd_ref, vd_ref, *, tm, n_ctx_tiles):
    i = pl.program_id(0)
    is_ctx = i < n_ctx_tiles
    pick = lambda c_ref, d_ref: jnp.where(is_ctx, c_ref[...], d_ref[...])
    xfull = jnp.concatenate([pick(xpc_ref, xpd_ref), pick(xc_ref, xd_ref), pick(xnc_ref, xnd_ref)],
                            axis=0)
    h = _modulate(xfull, g1_ref[...], mod_ref[0, 0:1, :], mod_ref[0, 1:2, :])
    p = jnp.dot(h.astype(BF16), w_ref[...], preferred_element_type=F32)
    hb, hm = slice(0, tm + 2 * HALO), slice(HALO, HALO + tm)
    proj = lambda rows, c0, c1: p[rows, c0:c1]

    ones_bd = ones_ref[...]
    cos = cos_ref[...]
    sin = sin_ref[...]

    pa = proj(hm, 0, 512)
    qa_ref[...] = _rope(_head_rms(pa[:, 0:256], aqg_ref[...], ones_bd), cos, sin)
    ka_ref[...] = _rope(_head_rms(pa[:, 256:384], akg_ref[...], ones_bd[:128, :128]), cos, sin)
    va_ref[...] = pa[:, 384:512]

    row = _row_iota((tm + 2 * HALO, 1))
    keep = jnp.logical_and(jnp.logical_or(row >= HALO, isstart_ref[i] == 0),
                           jnp.logical_or(row < HALO + tm, isend_ref[i] == 0))

    lxg = jnp.where(keep, proj(hb, 512, 768), 0.0)
    lx_ref[...] = _centred_conv(lxg, lcw_ref[...], tm) + lcb_ref[...]
    lg_ref[...] = proj(hm, 768, 1024)

    qkv = jnp.where(keep, proj(hb, 1024, 1792), 0.0)
    qkv = _silu(_centred_conv(qkv, gcw_ref[...], tm))
    gq = qkv[:, 0:256]
    gk = qkv[:, 256:512]
    gq_ref[...] = gq * lax.rsqrt(_group_sum(gq * gq, ones_bd) + EPS) * (GDN_DK ** -0.5)
    gk_ref[...] = gk * lax.rsqrt(_group_sum(gk * gk, ones_bd) + EPS)
    gv_ref[...] = qkv[:, 512:768]
    pz = proj(hm, 1792, 2176)
    gz_ref[...] = pz[:, 0:256]
    ab = pz[:, 256:384]
    g = -jnp.exp(alog_ref[...]) * _softplus(ab + dtb_ref[...])
    lane = _lane_iota(ab.shape)
    ggb_ref[...] = jnp.where(lane < 2 * GDN_H, g, _sigmoid(ab))

    pd = proj(hm, 2176, 2688)
    qd_ref[...] = _rope(_head_rms(pd[:, 0:256], dqg_ref[...], ones_bd), cos, sin)
    kd_ref[...] = _rope(_head_rms(pd[:, 256:384], dkg_ref[...], ones_bd[:128, :128]), cos, sin)
    vd_ref[...] = pd[:, 384:512]


def _x_pair_specs(x_pair, tm):
    x_c, x_d = x_pair
    d = x_c.shape[1]
    nct = x_c.shape[0] // tm
    hb = tm // HALO
    specs = []
    for arr, first in ((x_c, 0), (x_d, nct)):
        nblk = arr.shape[0] // tm
        last_hb = arr.shape[0] // HALO - 1
        tile = lambda i, first=first, nblk=nblk: jnp.clip(i - first, 0, nblk - 1)
        specs += [
            pl.BlockSpec((HALO, d), lambda i, *_, t=tile, m=last_hb: (jnp.clip(t(i) * hb - 1, 0, m), 0)),
            pl.BlockSpec((tm, d), lambda i, *_, t=tile: (t(i), 0)),
            pl.BlockSpec((HALO, d), lambda i, *_, t=tile, m=last_hb: (jnp.clip((t(i) + 1) * hb, 0, m), 0)),
        ]
    return specs


def _inproj_call(x_pair, meta, mod_all, g1, w_packed, cos_t, sin_t, vecs, ones_bd, *, l, tm):
    x_c, x_d = x_pair
    n, d = x_c.shape[0] + x_d.shape[0], x_c.shape[1]
    nt = n // tm
    row_tile = lambda w: pl.BlockSpec((tm, w), lambda i, *_: (i, 0))
    in_specs = _x_pair_specs(x_pair, tm) + [
        pl.BlockSpec((None, 1, 6, d), lambda i, modrow, *_: (l, modrow[i], 0, 0)),
        _layer_spec(g1, l),
        _layer_spec(w_packed, l),
        pl.BlockSpec((tm, 128), lambda i, modrow, ropeblk, *_: (ropeblk[i], 0)),
        pl.BlockSpec((tm, 128), lambda i, modrow, ropeblk, *_: (ropeblk[i], 0)),
    ] + [_layer_spec(v, l) for v in vecs] + [pl.BlockSpec((256, 256), lambda i, *_: (0, 0))]
    widths = (256, 128, 128, 256, 256, 256, 256, 256, 256, 128, 256, 128, 128)
    grid_spec = pltpu.PrefetchScalarGridSpec(
        num_scalar_prefetch=4,
        grid=(nt,),
        in_specs=in_specs,
        out_specs=[row_tile(w) for w in widths],
    )
    return pl.pallas_call(
        functools.partial(_inproj_kernel, tm=tm, n_ctx_tiles=x_c.shape[0] // tm),
        grid_spec=grid_spec,
        out_shape=[jax.ShapeDtypeStruct((n, w), F32) for w in widths],
        compiler_params=_cparams(("parallel",)),
        name="in_projection",
    )(*meta, x_c, x_c, x_c, x_d, x_d, x_d, mod_all, g1, w_packed, cos_t, sin_t, *vecs, ones_bd)


def _head_q(q, j, g):
    qj = q[:, 128 * j:128 * (j + 1)]
    lane = _lane_iota(qj.shape)
    sel = (lane < HEAD_DIM) if g == 0 else (lane >= HEAD_DIM)
    return jnp.where(sel, qj, 0.0).astype(BF16)


def _place_heads(res, j):
    r0 = res[0] if j == 0 else pltpu.roll(res[0], HEAD_DIM, 1)
    r1 = res[1] if j == 1 else pltpu.roll(res[1], HEAD_DIM, 1)
    lane = _lane_iota(r0.shape)
    return jnp.where(lane < HEAD_DIM, r0, r1)


def _attend(q, ks, vs, masks, sinks, ks_sw=None):
    if ks_sw is None:
        ks_sw = [pltpu.roll(k, HEAD_DIM, 1) for k in ks]
    blocks = []
    for j in range(2):
        res = []
        for g in range(2):
            qm = _head_q(q, j, g)
            ss = []
            for k, ksw, mk in zip(ks, ks_sw, masks):
                kk = k if g == j else ksw
                s = lax.dot_general(qm, kk, (((1,), (1,)), ((), ())), preferred_element_type=F32)
                if mk is not None:
                    s = jnp.where(mk, s, NEG)
                ss.append(s)
            m = ss[0].max(axis=-1, keepdims=True)
            for s in ss[1:]:
                m = jnp.maximum(m, s.max(axis=-1, keepdims=True))
            if sinks is not None:
                m = jnp.maximum(m, sinks[2 * j + g])
            den = None
            acc = None
            for s, v in zip(ss, vs):
                e = jnp.exp2(s - m)
                dsum = e.sum(axis=-1, keepdims=True)
                den = dsum if den is None else den + dsum
                pv = jnp.dot(e.astype(BF16), v, preferred_element_type=F32)
                acc = pv if acc is None else acc + pv
            if sinks is not None:
                den = den + jnp.exp2(sinks[2 * j + g] - m)
            res.append(acc / den)
        blocks.append(_place_heads(res, j))
    return jnp.concatenate(blocks, axis=1)


def _bf(x):
    return x.astype(BF16)


def _attn_ctx_kernel(sink_ref, qa_ref, ka_ref, va_ref, qd_ref, kd_ref, vd_ref, oa_ref, od_ref, *, l):
    sinks = [sink_ref[4 * l + u] * LOG2E for u in range(4)]
    oa_ref[...] = _attend(qa_ref[...], [_bf(ka_ref[...])], [_bf(va_ref[...])], [None], sinks)
    od_ref[...] = _attend(qd_ref[...], [_bf(kd_ref[...])], [_bf(vd_ref[...])], [None], None)


def _attn_ctx_call(sink, qa, ka, va, qd, kd, vd, *, l, n_seq, t):
    blk = lambda w: pl.BlockSpec((t, w), lambda b, *_: (b, 0))
    grid_spec = pltpu.PrefetchScalarGridSpec(
        num_scalar_prefetch=1, grid=(n_seq,),
        in_specs=[blk(256), blk(128), blk(128), blk(256), blk(128), blk(128)],
        out_specs=[blk(256), blk(256)])
    return pl.pallas_call(
        functools.partial(_attn_ctx_kernel, l=l), grid_spec=grid_spec,
        out_shape=[jax.ShapeDtypeStruct((n_seq * t, 256), F32)] * 2,
        compiler_params=_cparams(("parallel",)),
        name="attn_context",
    )(sink, qa, ka, va, qd, kd, vd)


def _attn_band_kernel(sink_ref, q_ref, k_ref, v_ref, ck_ref, cv_ref, o_ref, *, l, tq, t):
    i = pl.program_id(1)
    start = pl.multiple_of(i * tq, tq)
    prev = pl.multiple_of(jnp.maximum(start - WINDOW, 0), WINDOW)
    nxt = pl.multiple_of(jnp.minimum(start + tq, t - WINDOW), WINDOW)
    k_loc = jnp.concatenate([k_ref[pl.ds(prev, WINDOW), :], k_ref[pl.ds(start, tq), :],
                             k_ref[pl.ds(nxt, WINDOW), :]], axis=0)
    v_loc = jnp.concatenate([v_ref[pl.ds(prev, WINDOW), :], v_ref[pl.ds(start, tq), :],
                             v_ref[pl.ds(nxt, WINDOW), :]], axis=0)
    shape = (tq, tq + 2 * WINDOW)
    qpos = start + _row_iota(shape)
    kpos = start - WINDOW + _lane_iota(shape)
    mask = (kpos >= 0) & (kpos < t) & (jnp.abs(qpos - kpos) <= WINDOW)
    sinks = [sink_ref[4 * l + u] * LOG2E for u in range(4)]
    o_ref[...] = _attend(q_ref[...], [_bf(ck_ref[0]), _bf(k_loc)], [_bf(cv_ref[0]), _bf(v_loc)],
                         [None, mask], sinks)


def _cache_spec(c, l):
    return pl.BlockSpec((1, None) + c.shape[2:], lambda b, i, *_: (b, l, 0, 0))


def _attn_band_call(sink, q, k, v, ck, cv, *, l, n_seq, t, row0, tq):
    nq = t // tq
    qb0 = row0 // tq
    sb0 = row0 // t
    grid_spec = pltpu.PrefetchScalarGridSpec(
        num_scalar_prefetch=1, grid=(n_seq, nq),
        in_specs=[
            pl.BlockSpec((tq, 256), lambda b, i, *_: (qb0 + b * nq + i, 0)),
            pl.BlockSpec((t, 128), lambda b, i, *_: (sb0 + b, 0)),
            pl.BlockSpec((t, 128), lambda b, i, *_: (sb0 + b, 0)),
            _cache_spec(ck, l), _cache_spec(cv, l),
        ],
        out_specs=pl.BlockSpec((tq, 256), lambda b, i, *_: (b * nq + i, 0)))
    return pl.pallas_call(
        functools.partial(_attn_band_kernel, l=l, tq=tq, t=t), grid_spec=grid_spec,
        out_shape=jax.ShapeDtypeStruct((n_seq * t, 256), F32),
        compiler_params=_cparams(("parallel", "parallel")),
        name="attn_banded",
    )(sink, q, k, v, ck, cv)


def _attn_full_kernel(q_ref, k_ref, v_ref, ck_ref, cv_ref, o_ref, ks_ref, vs_ref, *, past, t):
    @pl.when(pl.program_id(1) == 0)
    def _():
        kc = _bf(ck_ref[0])
        kl = _bf(k_ref[...])
        ks_ref[0, 0:past, :] = kc
        ks_ref[0, past:past + t, :] = kl
        ks_ref[1, 0:past, :] = pltpu.roll(kc, HEAD_DIM, 1)
        ks_ref[1, past:past + t, :] = pltpu.roll(kl, HEAD_DIM, 1)
        vs_ref[0:past, :] = _bf(cv_ref[0])
        vs_ref[past:past + t, :] = _bf(v_ref[...])

    o_ref[...] = _attend(q_ref[...], [ks_ref[0]], [vs_ref[...]], [None], None, ks_sw=[ks_ref[1]])


def _attn_full_call(q, k, v, ck, cv, *, l, n_seq, t, row0, tq):
    nq = t // tq
    qb0 = row0 // tq
    sb0 = row0 // t
    past = ck.shape[2]
    assert past % 16 == 0
    return pl.pallas_call(
        functools.partial(_attn_full_kernel, past=past, t=t), grid=(n_seq, nq),
        scratch_shapes=[pltpu.VMEM((2, past + t, 128), BF16), pltpu.VMEM((past + t, 128), BF16)],
        in_specs=[
            pl.BlockSpec((tq, 256), lambda b, i: (qb0 + b * nq + i, 0)),
            pl.BlockSpec((t, 128), lambda b, i: (sb0 + b, 0)),
            pl.BlockSpec((t, 128), lambda b, i: (sb0 + b, 0)),
            _cache_spec(ck, l), _cache_spec(cv, l),
        ],
        out_specs=pl.BlockSpec((tq, 256), lambda b, i: (b * nq + i, 0)),
        out_shape=jax.ShapeDtypeStruct((n_seq * t, 256), F32),
        compiler_params=_cparams(("parallel", "arbitrary")),
        name="attn_full",
    )(q, k, v, ck, cv)


def _gelu_tanh(x):
    return 0.5 * x * (1.0 + jnp.tanh(0.7978845608028654 * (x + 0.044715 * (x * x * x))))


def _lru_kernel(lx_ref, lg_ref, w_ref, b_ref, lam_ref, h0_ref, y_ref, st_ref,
                hf_ref, hb_ref, af_ref, bf_ref, ab_ref, bb_ref, *, t, chunk):
    nc = t // chunk
    ng = chunk // 8
    sp = _softplus(-lam_ref[...])
    row8 = _row_iota((8, BRANCH_W))

    def gates(xc, d, a_ref, b2_ref):
        pre = _bdot(xc, w_ref[:, 512 * d:512 * (d + 1)]) + b_ref[:, 512 * d:512 * (d + 1)]
        r = _sigmoid(pre[:, 0:256])
        ig = _sigmoid(pre[:, 256:512])
        a = jnp.exp((-LRU_C) * r * sp[d:d + 1, :])
        a_ref[...] = a
        b2_ref[...] = jnp.sqrt(1.0 - a * a) * (ig * xc)

    def local_scan(a8, b8, reverse):
        for s in (1, 2, 4):
            sh = (8 - s) if reverse else s
            m = (row8 < 8 - s) if reverse else (row8 >= s)
            a_sh = pltpu.roll(a8, sh, 0)
            b_sh = pltpu.roll(b8, sh, 0)
            b8 = jnp.where(m, a8 * b_sh + b8, b8)
            a8 = jnp.where(m, a8 * a_sh, a8)
        return a8, b8

    def chunk_body(cc, carry):
        rf = pl.multiple_of(cc * chunk, chunk)
        rb = pl.multiple_of((nc - 1 - cc) * chunk, chunk)
        gates(lx_ref[pl.ds(rf, chunk), :], 0, af_ref, bf_ref)
        gates(lx_ref[pl.ds(rb, chunk), :], 1, ab_ref, bb_ref)

        def grp(gg, c2):
            hf, hb = c2
            gf = pl.multiple_of(gg * 8, 8)
            gb = pl.multiple_of((ng - 1 - gg) * 8, 8)
            a8, b8 = local_scan(af_ref[pl.ds(gf, 8), :], bf_ref[pl.ds(gf, 8), :], False)
            h8 = b8 + a8 * hf
            hf_ref[pl.ds(rf + gf, 8), :] = h8
            a8, b8 = local_scan(ab_ref[pl.ds(gb, 8), :], bb_ref[pl.ds(gb, 8), :], True)
            g8 = b8 + a8 * hb
            hb_ref[pl.ds(rb + gb, 8), :] = g8
            return (jnp.broadcast_to(h8[7:8, :], (8, BRANCH_W)), jnp.broadcast_to(g8[0:1, :], (8, BRANCH_W)))

        return lax.fori_loop(0, ng, grp, carry, unroll=4)

    init = (jnp.broadcast_to(h0_ref[0, 0:1, :], (8, BRANCH_W)), jnp.broadcast_to(h0_ref[0, 1:2, :], (8, BRANCH_W)))
    hf, hb = lax.fori_loop(0, nc, chunk_body, init)
    st_ref[0, 0:1, :] = hf[0:1, :]
    st_ref[0, 1:2, :] = hb[0:1, :]

    def combine(c, _):
        rows = pl.ds(pl.multiple_of(c * chunk, chunk), chunk)
        y_ref[rows, :] = (hf_ref[rows, :] + hb_ref[rows, :]) * _gelu_tanh(lg_ref[rows, :])
        return 0

    lax.fori_loop(0, nc, combine, 0)


def _lru_call(lx, lg, w, b, lam, h0, *, l, l_state, n_seq, t, row0):
    sb0 = row0 // t
    chunk = min(t, 256)
    seq = lambda: pl.BlockSpec((t, BRANCH_W), lambda s: (sb0 + s, 0))
    return pl.pallas_call(
        functools.partial(_lru_kernel, t=t, chunk=chunk),
        grid=(n_seq,),
        in_specs=[seq(), seq(), _layer_spec(w, l), _layer_spec(b, l), _layer_spec(lam, l),
                  pl.BlockSpec((1, None, 2, BRANCH_W), lambda s: (s, l_state, 0, 0))],
        out_specs=[pl.BlockSpec((t, BRANCH_W), lambda s: (s, 0)),
                   pl.BlockSpec((1, 2, BRANCH_W), lambda s: (s, 0, 0))],
        out_shape=[jax.ShapeDtypeStruct((n_seq * t, BRANCH_W), F32),
                   jax.ShapeDtypeStruct((n_seq, 2, BRANCH_W), F32)],
        scratch_shapes=[pltpu.VMEM((t, BRANCH_W), F32), pltpu.VMEM((t, BRANCH_W), F32)]
                       + [pltpu.VMEM((chunk, BRANCH_W), F32)] * 4,
        compiler_params=_cparams(("parallel",)),
        name="rglru",
    )(lx, lg, w, b, lam, h0)


def _rows_dot_exact(x, w01):
    r = x.shape[0]
    out = jnp.dot(jnp.concatenate(_split3(x), axis=0), w01, preferred_element_type=F32)
    return out[0:r] + out[r:2 * r] + out[2 * r:3 * r]


def _dot01_exact(m01, y):
    return sum(jnp.dot(m01, part, preferred_element_type=F32) for part in _split3(y))


def _gdn_kernel(q_ref, k_ref, v_ref, ggb_ref, s0_ref, o_ref, sT_ref, s_ref, *, d, reverse, n_par, n_chunk, n_tile):
    ti = pl.program_id(1)
    w4 = GDN_H * GDN_DK
    n_all = n_par * n_chunk
    tt = n_all * CHUNK
    blockmask = (_row_iota((w4, w4)) // CHUNK) == (_lane_iota((w4, w4)) // CHUNK)

    def expand_rows(y):
        yt = jnp.concatenate([y] * GDN_H, axis=0)
        zero = jnp.zeros((), y.dtype)
        parts = [jnp.where(blockmask, yt[:, w4 * u:w4 * (u + 1)], zero) for u in range(y.shape[1] // w4)]
        return parts[0] if len(parts) == 1 else jnp.concatenate(parts, axis=1)

    def heads_dot3(lhs, y):
        r = lhs.shape[0]
        lh, ll = _split(lhs)
        yh, yl = _split(y)
        out = jnp.dot(jnp.concatenate([lh, ll], axis=0), expand_rows(yh), preferred_element_type=F32)
        return out[0:r] + out[r:2 * r] + jnp.dot(lh, expand_rows(yl), preferred_element_type=F32)

    def heads_dot2(lhs, y):
        r = lhs.shape[0]
        lh, ll = _split(lhs)
        out = jnp.dot(jnp.concatenate([lh, ll], axis=0), expand_rows(y.astype(BF16)), preferred_element_type=F32)
        return out[0:r] + out[r:2 * r]

    def heads_dot1(lhs, y):
        return jnp.dot(lhs.astype(BF16), expand_rows(y.astype(BF16)), preferred_element_type=F32)

    @pl.when(ti == 0)
    def _():
        for p in range(n_par):
            s_ref[p] = expand_rows(s0_ref[p, 0])

    ri = _row_iota((CHUNK, w4))
    cj = _lane_iota((CHUNK, w4)) % CHUNK
    incl = (cj >= ri) if reverse else (cj <= ri)
    strict = (cj > ri) if reverse else (cj < ri)
    eye_sbs = (cj == ri)
    rep = eye_sbs.astype(BF16)

    er = _row_iota((128, 2 * w4))
    el = _lane_iota((128, 2 * w4))
    e_gb = (er == jnp.where(el < w4, d * GDN_H, 2 * GDN_H + d * GDN_H) + (el % w4) // CHUNK).astype(BF16)
    gbe = _rows_dot_exact(ggb_ref[...], e_gb)
    g_all = gbe[:, 0:w4]
    beta_all = gbe[:, w4:2 * w4]
    tr = _row_iota((tt, tt))
    tc = _lane_iota((tt, tt))
    same_chunk = (tr // CHUNK) == (tc // CHUNK)
    tri_bd = (same_chunk & ((tc >= tr) if reverse else (tc <= tr))).astype(BF16)
    ones_bd = same_chunk.astype(BF16)
    gc_all = _dot01_exact(tri_bd, g_all)
    diag_sel = (_row_iota((tt, w4)) % CHUNK) == (_lane_iota((tt, w4)) % CHUNK)
    gct_all = _dot01_exact(ones_bd, jnp.where(diag_sel, gc_all, 0.0))

    cs = range(n_all)
    rows = [slice(c * CHUNK, (c + 1) * CHUNK) for c in cs]
    q = [q_ref[r, :] for r in rows]
    k = [k_ref[r, :] for r in rows]
    v = [v_ref[r, :] for r in rows]
    beta = [beta_all[r, :] for r in rows]
    gc = [gc_all[r, :] for r in rows]
    decay = [jnp.where(incl, jnp.exp(jnp.where(incl, gc[c] - gct_all[rows[c], :], 0.0)), 0.0) for c in cs]
    kb = [k[c].astype(BF16) for c in cs]
    w_k = [jnp.where(blockmask,
                     lax.dot_general(kb[c], rep, (((0,), (0,)), ((), ())), preferred_element_type=F32),
                     0.0).astype(BF16) for c in cs]
    kq = [jnp.dot(jnp.concatenate([kb[c], q[c].astype(BF16)], axis=0), w_k[c], preferred_element_type=F32)
          for c in cs]
    qk = [(kq[c][CHUNK:2 * CHUNK] * decay[c]).astype(BF16) for c in cs]
    a = [jnp.where(strict, beta[c] * kq[c][0:CHUNK] * decay[c], 0.0) for c in cs]
    t_inv = [jnp.where(eye_sbs, 1.0, 0.0) - a[c] for c in cs]
    pw = [heads_dot3(a[c], a[c]) for c in cs]
    for stage in range(4):
        mm = heads_dot3 if stage < N_PRECISE else heads_dot1
        both = [mm(jnp.concatenate([t_inv[c], pw[c]], axis=0), pw[c]) for c in cs]
        t_inv = [t_inv[c] + both[c][0:CHUNK] for c in cs]
        pw = [both[c][CHUNK:2 * CHUNK] for c in cs]
    t_inv = [t_inv[c] + heads_dot1(t_inv[c], pw[c]) for c in cs]
    egc = [jnp.exp(gc[c]) for c in cs]
    sol = [heads_dot2(t_inv[c], jnp.concatenate([v[c] * beta[c], k[c] * (beta[c] * egc[c])], axis=1)) for c in cs]
    g_last = [gc[c][0:1, :] if reverse else gc[c][CHUNK - 1:CHUNK, :] for c in cs]
    wq = [jnp.concatenate([sol[c][:, w4:2 * w4], q[c] * egc[c]], axis=0).astype(BF16) for c in cs]
    kdec = [(k[c] * jnp.exp(g_last[c] - gc[c])).astype(BF16) for c in cs]

    for cc in range(n_chunk):
        for p in range(n_par):
            c = p * n_chunk + ((n_chunk - 1 - cc) if reverse else cc)
            s = s_ref[p]
            ws_qs = jnp.dot(wq[c], s.astype(BF16), preferred_element_type=F32)
            v_new = sol[c][:, 0:w4] - ws_qs[0:CHUNK]
            vb = v_new.astype(BF16)
            o_ref[rows[c], :] = (
                ws_qs[CHUNK:2 * CHUNK] + jnp.dot(qk[c], expand_rows(vb), preferred_element_type=F32))
            upd = lax.dot_general(kdec[c], vb, (((0,), (0,)), ((), ())), preferred_element_type=F32)
            s_ref[p] = s * jnp.exp(g_last[c]) + jnp.where(blockmask, upd, 0.0)

    @pl.when(ti == n_tile - 1)
    def _():
        for p in range(n_par):
            s = s_ref[p]
            sT_ref[p, 0] = s[0:64] + s[64:128] + s[128:192] + s[192:256]


def _gdn_call(q, k, v, ggb, s0, *, d, l_state, n_seq, t, row0, tt, n_par):
    reverse = d == 1
    n_tile = t // tt
    assert n_par == 1 or n_tile == 1
    b0 = row0 // (tt * n_par)
    tidx = (lambda i: n_tile - 1 - i) if reverse else (lambda i: i)
    blk = lambda w: pl.BlockSpec((tt * n_par, w), lambda s, i: (b0 + s * n_tile + tidx(i), 0))
    return pl.pallas_call(
        functools.partial(_gdn_kernel, d=d, reverse=reverse, n_par=n_par, n_chunk=tt // CHUNK, n_tile=n_tile),
        grid=(n_seq // n_par, n_tile),
        in_specs=[blk(256), blk(256), blk(256), blk(128),
                  pl.BlockSpec((n_par, None, 1, CHUNK, 256), lambda s, i: (s, l_state, d, 0, 0))],
        out_specs=[pl.BlockSpec((tt * n_par, 256), lambda s, i: (s * n_tile + tidx(i), 0)),
                   pl.BlockSpec((n_par, 1, CHUNK, 256), lambda s, i: (s, 0, 0, 0))],
        out_shape=[jax.ShapeDtypeStruct((n_seq * t, 256), F32),
                   jax.ShapeDtypeStruct((n_seq, 1, CHUNK, 256), F32)],
        scratch_shapes=[pltpu.VMEM((n_par, 256, 256), F32)],
        compiler_params=_cparams(("parallel", "arbitrary")),
        name="gdn_bwd" if reverse else "gdn_fwd",
    )(q, k, v, ggb, s0)


def _mix_mlp_kernel(modrow_ref, xc_ref, xd_ref, mod_ref, g1_ref,
                    oac_ref, oad_ref, obc_ref, obd_ref, ofc_ref, ofd_ref, orc_ref, ord_ref, odc_ref, odd_ref,
                    gz_ref, gng_ref, ones_ref, wm_ref, bm_ref, wb_ref, wo_ref, g2_ref, w1_ref, w2_ref,
                    yc_ref, yd_ref, *, n_ctx_tiles):
    is_ctx = pl.program_id(0) < n_ctx_tiles
    pick = lambda c_ref, d_ref: jnp.where(is_ctx, c_ref[...], d_ref[...])
    x = pick(xc_ref, xd_ref)
    h = _modulate(x, g1_ref[...], mod_ref[0, 0:1, :], mod_ref[0, 1:2, :]).astype(BF16)
    oc = pick(ofc_ref, ofd_ref) + pick(orc_ref, ord_ref)
    ms = _group_sum(oc * oc, ones_ref[...]) * (1.0 / GDN_DK)
    oc = (oc * lax.rsqrt(ms + EPS) * gng_ref[...]) * _silu(gz_ref[...])
    branches = (pick(oac_ref, oad_ref), pick(obc_ref, obd_ref), oc, pick(odc_ref, odd_ref))
    acc = None
    for m in range(N_BRANCH):
        cols = slice(D_MODEL * m, D_MODEL * (m + 1))
        gate = _sigmoid(jnp.dot(h, wm_ref[:, cols], preferred_element_type=F32) + bm_ref[:, cols])
        term = gate * jnp.dot(branches[m].astype(BF16), wb_ref[m], preferred_element_type=F32)
        acc = term if acc is None else acc + term
    x = x + mod_ref[0, 2:3, :] * jnp.dot(acc.astype(BF16), wo_ref[...], preferred_element_type=F32)

    h = _modulate(x, g2_ref[...], mod_ref[0, 3:4, :], mod_ref[0, 4:5, :]).astype(BF16)
    acc = None
    for j in range(D_FF // D_MODEL):
        cols = slice(D_MODEL * j, D_MODEL * (j + 1))
        a = jnp.maximum(jnp.dot(h, w1_ref[:, cols], preferred_element_type=F32), 0.0)
        term = jnp.dot((a * a).astype(BF16), w2_ref[cols, :], preferred_element_type=F32)
        acc = term if acc is None else acc + term
    y = x + mod_ref[0, 5:6, :] * acc

    @pl.when(is_ctx)
    def _():
        yc_ref[...] = y

    @pl.when(jnp.logical_not(is_ctx))
    def _():
        yd_ref[...] = y


def _mix_mlp_call(modrow, x_pair, mod_all, g1, pairs, gz, gng, ones_bd, wm, bm, wb, wo, g2, w1, w2, *, l, tm):
    x_c, x_d = x_pair
    n, d = x_c.shape[0] + x_d.shape[0], x_c.shape[1]
    nct = x_c.shape[0] // tm
    row_tile = lambda w: pl.BlockSpec((tm, w), lambda i, *_: (i, 0))
    pair = lambda w: [pl.BlockSpec((tm, w), lambda i, *_: (jnp.minimum(i, nct - 1), 0)),
                      pl.BlockSpec((tm, w), lambda i, *_: (jnp.maximum(i - nct, 0), 0))]
    pair_specs = [sp for _ in pairs for sp in pair(256)]
    resident = lambda arr: _layer_spec(arr, l, single_buffer=True)
    grid_spec = pltpu.PrefetchScalarGridSpec(
        num_scalar_prefetch=1, grid=(n // tm,),
        in_specs=pair(d) + [pl.BlockSpec((None, 1, 6, d), lambda i, modrow: (l, modrow[i], 0, 0)),
                            _layer_spec(g1, l)] + pair_specs
                 + [row_tile(256), _layer_spec(gng, l), pl.BlockSpec((256, 256), lambda i, *_: (0, 0)),
                    resident(wm), _layer_spec(bm, l), resident(wb), resident(wo),
                    _layer_spec(g2, l), resident(w1), resident(w2)],
        out_specs=pair(d))
    flat = [a for pr in pairs for a in pr]
    return pl.pallas_call(
        functools.partial(_mix_mlp_kernel, n_ctx_tiles=nct), grid_spec=grid_spec,
        out_shape=[jax.ShapeDtypeStruct(x_c.shape, F32), jax.ShapeDtypeStruct(x_d.shape, F32)],
        compiler_params=_cparams(("arbitrary",)),
        name="merge_mlp",
    )(modrow, x_c, x_d, mod_all, g1, *flat, gz, gng, ones_bd, wm, bm, wb, wo, g2, w1, w2)


def _rope_tables(t, tm):
    n_freq = HEAD_DIM // 4
    inv = np.float32(ROPE_BASE) ** (-np.arange(n_freq, dtype=np.float32) / np.float32(n_freq))
    pos = np.arange(t)
    row = (pos // GRID_W).astype(np.float32)[:, None]
    col = (pos % GRID_W).astype(np.float32)[:, None]
    ar = (row * inv).astype(np.float32)
    ac = (col * inv).astype(np.float32)
    cos64 = np.concatenate([np.cos(ar), np.cos(ar), np.cos(ac), np.cos(ac)], axis=1)
    sin64 = np.concatenate([-np.sin(ar), np.sin(ar), -np.sin(ac), np.sin(ac)], axis=1)
    cos = np.concatenate([np.ones((tm, 128), np.float32), np.tile(cos64, (1, 2)).astype(np.float32)], axis=0)
    sin = np.concatenate([np.zeros((tm, 128), np.float32), np.tile(sin64, (1, 2)).astype(np.float32)], axis=0)
    return jnp.asarray(cos), jnp.asarray(sin)


def _tile_meta(n_ctx, n_seq_dec, t_ctx, t_dec, tm):
    nct = n_ctx // tm
    per = t_dec // tm
    ndt = n_seq_dec * per
    idx = np.arange(nct + ndt)
    dec = idx >= nct
    di = np.maximum(idx - nct, 0)
    modrow = np.where(dec, 1 + di // per, 0)
    ropeblk = np.where(dec, 1 + di % per, 0)
    tiles_ctx = max(t_ctx // tm, 1)
    isstart = np.where(dec, di % per == 0, idx % tiles_ctx == 0)
    isend = np.where(dec, di % per == per - 1, idx % tiles_ctx == tiles_ctx - 1)
    as_i32 = lambda a: jnp.asarray(a.astype(np.int32))
    return as_i32(modrow), as_i32(ropeblk), as_i32(isstart), as_i32(isend)


def _block_diag_gates(w):
    depth, two, n, c, _ = w.shape
    eye = jnp.eye(n, dtype=w.dtype)
    return jnp.einsum("ldnij,nm->ldnimj", w, eye).reshape(depth, two, n * c, n * c)


def kernel(x_prompt, x_sample, c, cache_a_k, cache_a_v, cache_d_k, cache_d_v, state_lru, state_gdn, c_ctx, mod_w, mod_b, norm1_g, norm2_g, w_in, a_qn_g, a_kn_g, a_sink, lru_conv_w, lru_conv_b, lru_wr, lru_br, lru_wi, lru_bi, lru_lam, gdn_conv_w, gdn_a_log, gdn_dt_bias, gdn_norm_g, d_qn_g, d_kn_g, w_branch, w_merge, b_merge, w_out, mlp_w1, mlp_w2):
    batch, seq, d = x_prompt.shape
    dec_batch, dec_seq, _ = x_sample.shape
    depth = mod_w.shape[0]
    past = cache_a_k.shape[2]
    n_ctx = batch * seq
    n_dec = dec_batch * dec_seq
    tm = 256
    gdn_par = 2 if batch % 2 == 0 else 1
    gdn_tt = 512 if dec_seq % 512 == 0 else 256
    assert d == D_MODEL and seq % tm == 0 and dec_seq % tm == 0 and tm % seq == 0
    assert dec_batch + 1 <= 8 and n_ctx % dec_seq == 0

    cond8 = jnp.zeros((8, d), F32).at[0].set(c_ctx).at[1:1 + dec_batch].set(c)
    mod_all = _mod_call(cond8, mod_w, mod_b).reshape(depth, 8, 6, d)

    meta = _tile_meta(n_ctx, dec_batch, seq, dec_seq, tm)
    cos_t, sin_t = _rope_tables(dec_seq, tm)
    lane = np.arange(256)
    ones_bd = jnp.asarray((lane[:, None] // HEAD_DIM == lane[None, :] // HEAD_DIM).astype(np.float32)).astype(BF16)

    w_packed = jnp.concatenate([w_in[:, :, :2064], jnp.zeros((depth, d, GDN_PAD), F32), w_in[:, :, 2064:]],
                               axis=2).astype(BF16)
    pad128 = lambda v: jnp.pad(v.reshape(depth, 1, -1), ((0, 0), (0, 0), (0, 128 - v[0].size)))
    vecs = (
        (jnp.tile(a_qn_g, (1, 4)) * Q_SCALE)[:, None, :],
        jnp.tile(a_kn_g, (1, 2))[:, None, :],
        (jnp.tile(d_qn_g, (1, 4)) * Q_SCALE)[:, None, :],
        jnp.tile(d_kn_g, (1, 2))[:, None, :],
        lru_conv_w, lru_conv_b[:, None, :], gdn_conv_w,
        pad128(gdn_a_log), pad128(gdn_dt_bias),
    )
    g1 = norm1_g[:, None, :]
    g2 = norm2_g[:, None, :]
    wr_bd = _block_diag_gates(lru_wr)
    wi_bd = _block_diag_gates(lru_wi)
    w_lru = jnp.concatenate([wr_bd[:, 0], wi_bd[:, 0], wr_bd[:, 1], wi_bd[:, 1]], axis=-1).astype(BF16)
    b_lru = jnp.concatenate([lru_br[:, 0], lru_bi[:, 0], lru_br[:, 1], lru_bi[:, 1]], axis=-1)[:, None, :]
    gng = jnp.tile(gdn_norm_g, (1, 4))[:, None, :]
    wm = w_merge.astype(BF16)
    bm = b_merge[:, None, :]
    wb = w_branch.astype(BF16)
    wo = w_out.astype(BF16)
    w1 = mlp_w1.astype(BF16)
    w2 = mlp_w2.astype(BF16)
    sink = a_sink.reshape(-1)
    caches = [t.reshape(dec_batch, depth, past, 2 * HEAD_DIM) for t in (cache_a_k, cache_a_v, cache_d_k, cache_d_v)]
    s0_dec = state_gdn.transpose(0, 1, 2, 4, 3, 5).reshape(dec_batch, depth, 2, CHUNK, 256)
    zeros_lru = jnp.zeros((batch, 1, 2, BRANCH_W), F32)
    zeros_gdn = jnp.zeros((batch, 1, 2, CHUNK, 256), F32)

    x = (x_prompt.reshape(n_ctx, d), x_sample.reshape(n_dec, d))

    kv_ctx, lru_states, gdn_states = [], [], []
    for l in range(depth):
        (qa, ka, va, lx, lg, gq, gk, gv, gz, ggb, qd, kd, vd) = _inproj_call(
            x, meta, mod_all, g1, w_packed, cos_t, sin_t, vecs, ones_bd, l=l, tm=tm)

        oa_c, od_c = _attn_ctx_call(sink, qa, ka, va, qd, kd, vd, l=l, n_seq=batch, t=seq)
        oa_d = _attn_band_call(sink, qa, ka, va, caches[0], caches[1], l=l, n_seq=dec_batch, t=dec_seq,
                               row0=n_ctx, tq=512)
        od_d = _attn_full_call(qd, kd, vd, caches[2], caches[3], l=l, n_seq=dec_batch, t=dec_seq,
                               row0=n_ctx, tq=256)

        ob_c, st_c = _lru_call(lx, lg, w_lru, b_lru, lru_lam, zeros_lru, l=l, l_state=0, n_seq=batch, t=seq,
                               row0=0)
        ob_d, _ = _lru_call(lx, lg, w_lru, b_lru, lru_lam, state_lru, l=l, l_state=l, n_seq=dec_batch,
                            t=dec_seq, row0=n_ctx)

        oc_pairs, sts = [], []
        for dd in range(2):
            o_c, s_c = _gdn_call(gq, gk, gv, ggb, zeros_gdn, d=dd, l_state=0, n_seq=batch, t=seq, row0=0,
                                 tt=seq, n_par=gdn_par)
            o_d, _ = _gdn_call(gq, gk, gv, ggb, s0_dec, d=dd, l_state=l, n_seq=dec_batch, t=dec_seq,
                               row0=n_ctx, tt=gdn_tt, n_par=1)
            oc_pairs.append((o_c, o_d))
            sts.append(s_c)

        x = _mix_mlp_call(meta[0], x, mod_all, g1,
                          [(oa_c, oa_d), (ob_c, ob_d), oc_pairs[0], oc_pairs[1], (od_c, od_d)],
                          gz, gng, ones_bd, wm, bm, wb, wo, g2, w1, w2, l=l, tm=tm)

        kv_ctx.append([t[:n_ctx] for t in (ka, va, kd, vd)])
        lru_states.append(st_c)
        gdn_states.append(jnp.concatenate(sts, axis=1))

    y_prompt = x[0].reshape(batch, seq, d)
    y_sample = x[1].reshape(dec_batch, dec_seq, d)
    new_kv = [jnp.stack([kv_ctx[l][u].reshape(batch, seq, 2, HEAD_DIM) for l in range(depth)], axis=1)
              for u in range(4)]
    new_lru = jnp.stack(lru_states, axis=1)
    new_gdn = (jnp.stack(gdn_states, axis=1).reshape(batch, depth, 2, GDN_DK, GDN_H, GDN_DK)
               .transpose(0, 1, 2, 4, 3, 5))
    return (y_prompt, y_sample, new_kv[0], new_kv[1], new_kv[2], new_kv[3], new_lru, new_gdn)
```

```python
import functools

import numpy as np
import jax
import jax.numpy as jnp
from jax import lax
from jax.experimental import pallas as pl
from jax.experimental.pallas import tpu as pltpu

F32 = jnp.float32
BF16 = jnp.bfloat16
HIGHEST = lax.Precision.HIGHEST

D_MODEL = 1024
HEAD_DIM = 64
BRANCH_W = 256
N_BRANCH = 4
GRID_W = 64
WINDOW = 128
LRU_C = 8.0
CONV_W = 4
CONV_LEFT = 2
GDN_H = 4
GDN_DK = 64
CHUNK = 64
D_FF = 4 * D_MODEL
ROPE_BASE = 10000.0
EPS = 1e-6
NEG = -1e30
LOG2E = 1.4426950408889634
Q_SCALE = HEAD_DIM ** -0.5 * LOG2E
GDN_PAD = 112
IN_COLS_PACKED = 2688
HALO = 8
N_PRECISE = 4

V7X_VMEM_LIMIT = 56 * 1024 * 1024


def _cparams(sem, vmem=V7X_VMEM_LIMIT):
    return pltpu.CompilerParams(dimension_semantics=sem, vmem_limit_bytes=vmem)


def _layer_spec(arr, l, single_buffer=False):
    nd = arr.ndim - 1
    mode = dict(pipeline_mode=pl.Buffered(1)) if single_buffer else {}
    return pl.BlockSpec((None,) + arr.shape[1:], lambda *_: (l,) + (0,) * nd, **mode)


def _bdot(a, b):
    return jnp.dot(a.astype(BF16), b.astype(BF16), preferred_element_type=F32)


def _split(x):
    hi = x.astype(BF16)
    lo = (x - hi.astype(F32)).astype(BF16)
    return hi, lo


def _split3(x):
    hi = x.astype(BF16)
    r = x - hi.astype(F32)
    mid = r.astype(BF16)
    lo = (r - mid.astype(F32)).astype(BF16)
    return hi, mid, lo


def _group_sum(x, ones_bd):
    return jnp.dot(x.astype(BF16), ones_bd, preferred_element_type=F32)


def _sigmoid(x):
    return 0.5 * jnp.tanh(0.5 * x) + 0.5


def _silu(x):
    return x * _sigmoid(x)


def _softplus(x):
    return jnp.maximum(x, 0.0) + jnp.log1p(jnp.exp(-jnp.abs(x)))


def _modulate(x, g, shift, scale):
    ms = jnp.mean(x * x, axis=-1, keepdims=True)
    return (x * lax.rsqrt(ms + EPS) * g) * (1.0 + scale) + shift


def _lane_iota(shape):
    return lax.broadcasted_iota(jnp.int32, shape, len(shape) - 1)


def _row_iota(shape):
    return lax.broadcasted_iota(jnp.int32, shape, len(shape) - 2)


def _mod_kernel(cond_ref, w_ref, b_ref, o_ref):
    c = cond_ref[...]
    o_ref[0] = jnp.dot(_silu(c), w_ref[0], precision=HIGHEST, preferred_element_type=F32) + b_ref[0]


def _mod_call(cond8, mod_w, mod_b):
    depth, d, n = mod_w.shape
    tn = 1536
    return pl.pallas_call(
        _mod_kernel,
        grid=(depth, n // tn),
        in_specs=[
            pl.BlockSpec((8, d), lambda l, j: (0, 0)),
            pl.BlockSpec((1, d, tn), lambda l, j: (l, 0, j)),
            pl.BlockSpec((1, 1, tn), lambda l, j: (l, 0, j)),
        ],
        out_specs=pl.BlockSpec((1, 8, tn), lambda l, j: (l, 0, j)),
        out_shape=jax.ShapeDtypeStruct((depth, 8, n), F32),
        compiler_params=_cparams(("parallel", "parallel")),
        name="mod_vectors",
    )(cond8, mod_w, mod_b.reshape(depth, 1, n))


def _rope(x, cos, sin):
    outs = []
    for j in range(x.shape[1] // 128):
        xb = x[:, 128 * j:128 * (j + 1)]
        lane = _lane_iota(xb.shape)
        sw = jnp.where((lane & 16) == 0, pltpu.roll(xb, 112, 1), pltpu.roll(xb, 16, 1))
        outs.append(xb * cos + sw * sin)
    return outs[0] if len(outs) == 1 else jnp.concatenate(outs, axis=1)


def _head_rms(x, gain, ones_bd):
    ms = _group_sum(x * x, ones_bd) * (1.0 / HEAD_DIM)
    return x * lax.rsqrt(ms + EPS) * gain


def _centred_conv(g, w, tm):
    rows = g.shape[0]
    acc = None
    for j in range(CONV_W):
        sh = (CONV_LEFT - j) % rows
        gj = g if sh == 0 else pltpu.roll(g, sh, 0)
        term = gj[HALO:HALO + tm] * w[j:j + 1, :]
        acc = term if acc is None else acc + term
    return acc


def _inproj_kernel(modrow_ref, ropeblk_ref, isstart_ref, isend_ref,
                   xpc_ref, xc_ref, xnc_ref, xpd_ref, xd_ref, xnd_ref, mod_ref, g1_ref, w_ref, cos_ref, sin_ref,
                   aqg_ref, akg_ref, dqg_ref, dkg_ref, lcw_ref, lcb_ref, gcw_ref,
                   alog_ref, dtb_ref, ones_ref,
                   qa_ref, ka_ref, va_ref, lx_ref, lg_ref, gq_ref, gk_ref, gv_ref, gz_ref, ggb_ref,
                   qd_ref, kd_ref, vd_ref, *, tm, n_ctx_tiles):
    i = pl.program_id(0)
    is_ctx = i < n_ctx_tiles
    pick = lambda c_ref, d_ref: jnp.where(is_ctx, c_ref[...], d_ref[...])
    xfull = jnp.concatenate([pick(xpc_ref, xpd_ref), pick(xc_ref, xd_ref), pick(xnc_ref, xnd_ref)],
                            axis=0)
    h = _modulate(xfull, g1_ref[...], mod_ref[0, 0:1, :], mod_ref[0, 1:2, :])
    p = jnp.dot(h.astype(BF16), w_ref[...], preferred_element_type=F32)
    hb, hm = slice(0, tm + 2 * HALO), slice(HALO, HALO + tm)
    proj = lambda rows, c0, c1: p[rows, c0:c1]

    ones_bd = ones_ref[...]
    cos = cos_ref[...]
    sin = sin_ref[...]

    pa = proj(hm, 0, 512)
    qa_ref[...] = _rope(_head_rms(pa[:, 0:256], aqg_ref[...], ones_bd), cos, sin)
    ka_ref[...] = _rope(_head_rms(pa[:, 256:384], akg_ref[...], ones_bd[:128, :128]), cos, sin)
    va_ref[...] = pa[:, 384:512]

    row = _row_iota((tm + 2 * HALO, 1))
    keep = jnp.logical_and(jnp.logical_or(row >= HALO, isstart_ref[i] == 0),
                           jnp.logical_or(row < HALO + tm, isend_ref[i] == 0))

    lxg = jnp.where(keep, proj(hb, 512, 768), 0.0)
    lx_ref[...] = _centred_conv(lxg, lcw_ref[...], tm) + lcb_ref[...]
    lg_ref[...] = proj(hm, 768, 1024)

    qkv = jnp.where(keep, proj(hb, 1024, 1792), 0.0)
    qkv = _silu(_centred_conv(qkv, gcw_ref[...], tm))
    gq = qkv[:, 0:256]
    gk = qkv[:, 256:512]
    gq_ref[...] = gq * lax.rsqrt(_group_sum(gq * gq, ones_bd) + EPS) * (GDN_DK ** -0.5)
    gk_ref[...] = gk * lax.rsqrt(_group_sum(gk * gk, ones_bd) + EPS)
    gv_ref[...] = qkv[:, 512:768]
    pz = proj(hm, 1792, 2176)
    gz_ref[...] = pz[:, 0:256]
    ab = pz[:, 256:384]
    g = -jnp.exp(alog_ref[...]) * _softplus(ab + dtb_ref[...])
    lane = _lane_iota(ab.shape)
    ggb_ref[...] = jnp.where(lane < 2 * GDN_H, g, _sigmoid(ab))

    pd = proj(hm, 2176, 2688)
    qd_ref[...] = _rope(_head_rms(pd[:, 0:256], dqg_ref[...], ones_bd), cos, sin)
    kd_ref[...] = _rope(_head_rms(pd[:, 256:384], dkg_ref[...], ones_bd[:128, :128]), cos, sin)
    vd_ref[...] = pd[:, 384:512]


def _x_pair_specs(x_pair, tm):
    x_c, x_d = x_pair
    d = x_c.shape[1]
    nct = x_c.shape[0] // tm
    hb = tm // HALO
    specs = []
    for arr, first in ((x_c, 0), (x_d, nct)):
        nblk = arr.shape[0] // tm
        last_hb = arr.shape[0] // HALO - 1
        tile = lambda i, first=first, nblk=nblk: jnp.clip(i - first, 0, nblk - 1)
        specs += [
            pl.BlockSpec((HALO, d), lambda i, *_, t=tile, m=last_hb: (jnp.clip(t(i) * hb - 1, 0, m), 0)),
            pl.BlockSpec((tm, d), lambda i, *_, t=tile: (t(i), 0)),
            pl.BlockSpec((HALO, d), lambda i, *_, t=tile, m=last_hb: (jnp.clip((t(i) + 1) * hb, 0, m), 0)),
        ]
    return specs


def _inproj_call(x_pair, meta, mod_all, g1, w_packed, cos_t, sin_t, vecs, ones_bd, *, l, tm):
    x_c, x_d = x_pair
    n, d = x_c.shape[0] + x_d.shape[0], x_c.shape[1]
    nt = n // tm
    row_tile = lambda w: pl.BlockSpec((tm, w), lambda i, *_: (i, 0))
    in_specs = _x_pair_specs(x_pair, tm) + [
        pl.BlockSpec((None, 1, 6, d), lambda i, modrow, *_: (l, modrow[i], 0, 0)),
        _layer_spec(g1, l),
        _layer_spec(w_packed, l),
        pl.BlockSpec((tm, 128), lambda i, modrow, ropeblk, *_: (ropeblk[i], 0)),
        pl.BlockSpec((tm, 128), lambda i, modrow, ropeblk, *_: (ropeblk[i], 0)),
    ] + [_layer_spec(v, l) for v in vecs] + [pl.BlockSpec((256, 256), lambda i, *_: (0, 0))]
    widths = (256, 128, 128, 256, 256, 256, 256, 256, 256, 128, 256, 128, 128)
    grid_spec = pltpu.PrefetchScalarGridSpec(
        num_scalar_prefetch=4,
        grid=(nt,),
        in_specs=in_specs,
        out_specs=[row_tile(w) for w in widths],
    )
    return pl.pallas_call(
        functools.partial(_inproj_kernel, tm=tm, n_ctx_tiles=x_c.shape[0] // tm),
        grid_spec=grid_spec,
        out_shape=[jax.ShapeDtypeStruct((n, w), F32) for w in widths],
        compiler_params=_cparams(("parallel",)),
        name="in_projection",
    )(*meta, x_c, x_c, x_c, x_d, x_d, x_d, mod_all, g1, w_packed, cos_t, sin_t, *vecs, ones_bd)


def _head_q(q, j, g):
    qj = q[:, 128 * j:128 * (j + 1)]
    lane = _lane_iota(qj.shape)
    sel = (lane < HEAD_DIM) if g == 0 else (lane >= HEAD_DIM)
    return jnp.where(sel, qj, 0.0).astype(BF16)


def _place_heads(res, j):
    r0 = res[0] if j == 0 else pltpu.roll(res[0], HEAD_DIM, 1)
    r1 = res[1] if j == 1 else pltpu.roll(res[1], HEAD_DIM, 1)
    lane = _lane_iota(r0.shape)
    return jnp.where(lane < HEAD_DIM, r0, r1)


def _attend_many(problems, order="staged"):
    work = []
    for pi, (q, ks, vs, masks, sinks, ks_sw) in enumerate(problems):
        if ks_sw is None:
            ks_sw = [pltpu.roll(k, HEAD_DIM, 1) for k in ks]
        work += [(pi, j, g, q, ks, vs, masks, sinks, ks_sw) for j in range(2) for g in range(2)]
    scores, probs, res = {}, {}, {}

    def score(w):
        pi, j, g, q, ks, vs, masks, sinks, ks_sw = w
        qm = _head_q(q, j, g)
        ss = []
        for k, ksw, mk in zip(ks, ks_sw, masks):
            s = lax.dot_general(qm, k if g == j else ksw, (((1,), (1,)), ((), ())), preferred_element_type=F32)
            ss.append(s if mk is None else jnp.where(mk, s, NEG))
        scores[w[:3]] = ss

    def softmax_num(w):
        pi, j, g, q, ks, vs, masks, sinks, ks_sw = w
        ss = scores[w[:3]]
        m = ss[0].max(axis=-1, keepdims=True)
        for s in ss[1:]:
            m = jnp.maximum(m, s.max(axis=-1, keepdims=True))
        if sinks is not None:
            m = jnp.maximum(m, sinks[2 * j + g])
        es = [jnp.exp2(s - m) for s in ss]
        den = sum(e.sum(axis=-1, keepdims=True) for e in es)
        if sinks is not None:
            den = den + jnp.exp2(sinks[2 * j + g] - m)
        probs[w[:3]] = ([e.astype(BF16) for e in es], den)

    def values(w):
        es, den = probs[w[:3]]
        res[w[:3]] = sum(jnp.dot(e, v, preferred_element_type=F32) for e, v in zip(es, w[5])) / den

    if order == "staged":
        for step in (score, softmax_num, values):
            for w in work:
                step(w)
    else:
        score(work[0])
        for prev, nxt in zip(work, work[1:] + [None]):
            if nxt is not None:
                score(nxt)
            softmax_num(prev)
            values(prev)
    return [jnp.concatenate([_place_heads([res[(pi, j, 0)], res[(pi, j, 1)]], j) for j in range(2)], axis=1)
            for pi in range(len(problems))]


def _attend(q, ks, vs, masks, sinks, ks_sw=None, order="staged"):
    return _attend_many([(q, ks, vs, masks, sinks, ks_sw)], order)[0]


def _bf(x):
    return x.astype(BF16)


def _attn_ctx_kernel(sink_ref, qa_ref, ka_ref, va_ref, qd_ref, kd_ref, vd_ref, oa_ref, od_ref, *, l, t, n_par):
    sinks = [sink_ref[4 * l + u] * LOG2E for u in range(4)]
    problems = []
    for p in range(n_par):
        r = slice(p * t, (p + 1) * t)
        problems.append((qa_ref[r, :], [_bf(ka_ref[r, :])], [_bf(va_ref[r, :])], [None], sinks, None))
        problems.append((qd_ref[r, :], [_bf(kd_ref[r, :])], [_bf(vd_ref[r, :])], [None], None, None))
    outs = _attend_many(problems)
    for p in range(n_par):
        oa_ref[p * t:(p + 1) * t, :] = outs[2 * p]
        od_ref[p * t:(p + 1) * t, :] = outs[2 * p + 1]


def _attn_ctx_call(sink, qa, ka, va, qd, kd, vd, *, l, n_seq, t):
    n_par = 2 if n_seq % 2 == 0 else 1
    blk = lambda w: pl.BlockSpec((n_par * t, w), lambda b, *_: (b, 0))
    grid_spec = pltpu.PrefetchScalarGridSpec(
        num_scalar_prefetch=1, grid=(n_seq // n_par,),
        in_specs=[blk(256), blk(128), blk(128), blk(256), blk(128), blk(128)],
        out_specs=[blk(256), blk(256)])
    return pl.pallas_call(
        functools.partial(_attn_ctx_kernel, l=l, t=t, n_par=n_par), grid_spec=grid_spec,
        out_shape=[jax.ShapeDtypeStruct((n_seq * t, 256), F32)] * 2,
        compiler_params=_cparams(("parallel",)),
        name="attn_context",
    )(sink, qa, ka, va, qd, kd, vd)


def _attn_band_kernel(sink_ref, q_ref, k_ref, v_ref, ck_ref, cv_ref, o_ref, *, l, tq, t):
    i = pl.program_id(1)
    start = pl.multiple_of(i * tq, tq)
    prev = pl.multiple_of(jnp.maximum(start - WINDOW, 0), WINDOW)
    nxt = pl.multiple_of(jnp.minimum(start + tq, t - WINDOW), WINDOW)
    k_loc = jnp.concatenate([k_ref[pl.ds(prev, WINDOW), :], k_ref[pl.ds(start, tq), :],
                             k_ref[pl.ds(nxt, WINDOW), :]], axis=0)
    v_loc = jnp.concatenate([v_ref[pl.ds(prev, WINDOW), :], v_ref[pl.ds(start, tq), :],
                             v_ref[pl.ds(nxt, WINDOW), :]], axis=0)
    shape = (tq, tq + 2 * WINDOW)
    qpos = start + _row_iota(shape)
    kpos = start - WINDOW + _lane_iota(shape)
    mask = (kpos >= 0) & (kpos < t) & (jnp.abs(qpos - kpos) <= WINDOW)
    sinks = [sink_ref[4 * l + u] * LOG2E for u in range(4)]
    o_ref[...] = _attend(q_ref[...], [_bf(ck_ref[0]), _bf(k_loc)], [_bf(cv_ref[0]), _bf(v_loc)],
                         [None, mask], sinks)


def _cache_spec(c, l):
    return pl.BlockSpec((1, None) + c.shape[2:], lambda b, i, *_: (b, l, 0, 0))


def _attn_band_call(sink, q, k, v, ck, cv, *, l, n_seq, t, row0, tq):
    nq = t // tq
    qb0 = row0 // tq
    sb0 = row0 // t
    grid_spec = pltpu.PrefetchScalarGridSpec(
        num_scalar_prefetch=1, grid=(n_seq, nq),
        in_specs=[
            pl.BlockSpec((tq, 256), lambda b, i, *_: (qb0 + b * nq + i, 0)),
            pl.BlockSpec((t, 128), lambda b, i, *_: (sb0 + b, 0)),
            pl.BlockSpec((t, 128), lambda b, i, *_: (sb0 + b, 0)),
            _cache_spec(ck, l), _cache_spec(cv, l),
        ],
        out_specs=pl.BlockSpec((tq, 256), lambda b, i, *_: (b * nq + i, 0)))
    return pl.pallas_call(
        functools.partial(_attn_band_kernel, l=l, tq=tq, t=t), grid_spec=grid_spec,
        out_shape=jax.ShapeDtypeStruct((n_seq * t, 256), F32),
        compiler_params=_cparams(("parallel", "parallel")),
        name="attn_banded",
    )(sink, q, k, v, ck, cv)


def _attn_full_kernel(q_ref, k_ref, v_ref, ck_ref, cv_ref, o_ref, ks_ref, vs_ref, *, past, t):
    @pl.when(pl.program_id(1) == 0)
    def _():
        kc = _bf(ck_ref[0])
        kl = _bf(k_ref[...])
        ks_ref[0, 0:past, :] = kc
        ks_ref[0, past:past + t, :] = kl
        ks_ref[1, 0:past, :] = pltpu.roll(kc, HEAD_DIM, 1)
        ks_ref[1, past:past + t, :] = pltpu.roll(kl, HEAD_DIM, 1)
        vs_ref[0:past, :] = _bf(cv_ref[0])
        vs_ref[past:past + t, :] = _bf(v_ref[...])

    o_ref[...] = _attend(q_ref[...], [ks_ref[0]], [vs_ref[...]], [None], None, ks_sw=[ks_ref[1]], order="skewed")


def _attn_full_call(q, k, v, ck, cv, *, l, n_seq, t, row0, tq):
    nq = t // tq
    qb0 = row0 // tq
    sb0 = row0 // t
    past = ck.shape[2]
    assert past % 16 == 0
    return pl.pallas_call(
        functools.partial(_attn_full_kernel, past=past, t=t), grid=(n_seq, nq),
        scratch_shapes=[pltpu.VMEM((2, past + t, 128), BF16), pltpu.VMEM((past + t, 128), BF16)],
        in_specs=[
            pl.BlockSpec((tq, 256), lambda b, i: (qb0 + b * nq + i, 0)),
            pl.BlockSpec((t, 128), lambda b, i: (sb0 + b, 0)),
            pl.BlockSpec((t, 128), lambda b, i: (sb0 + b, 0)),
            _cache_spec(ck, l), _cache_spec(cv, l),
        ],
        out_specs=pl.BlockSpec((tq, 256), lambda b, i: (b * nq + i, 0)),
        out_shape=jax.ShapeDtypeStruct((n_seq * t, 256), F32),
        compiler_params=_cparams(("parallel", "arbitrary")),
        name="attn_full",
    )(q, k, v, ck, cv)


def _gelu_tanh(x):
    return 0.5 * x * (1.0 + jnp.tanh(0.7978845608028654 * (x + 0.044715 * (x * x * x))))


def _lru_kernel(lx_ref, lg_ref, w_ref, b_ref, lam_ref, h0_ref, y_ref, st_ref,
                hf_ref, hb_ref, af_ref, bf_ref, ab_ref, bb_ref, *, t, chunk):
    nc = t // chunk
    ng = chunk // 8
    sp = _softplus(-lam_ref[...])
    row8 = _row_iota((8, BRANCH_W))

    def gates(xc, d, a_ref, b2_ref):
        pre = _bdot(xc, w_ref[:, 512 * d:512 * (d + 1)]) + b_ref[:, 512 * d:512 * (d + 1)]
        r = _sigmoid(pre[:, 0:256])
        ig = _sigmoid(pre[:, 256:512])
        a = jnp.exp((-LRU_C) * r * sp[d:d + 1, :])
        a_ref[...] = a
        b2_ref[...] = jnp.sqrt(1.0 - a * a) * (ig * xc)

    def local_scan(a8, b8, reverse):
        for s in (1, 2, 4):
            sh = (8 - s) if reverse else s
            m = (row8 < 8 - s) if reverse else (row8 >= s)
            a_sh = pltpu.roll(a8, sh, 0)
            b_sh = pltpu.roll(b8, sh, 0)
            b8 = jnp.where(m, a8 * b_sh + b8, b8)
            a8 = jnp.where(m, a8 * a_sh, a8)
        return a8, b8

    def chunk_body(cc, carry):
        rf = pl.multiple_of(cc * chunk, chunk)
        rb = pl.multiple_of((nc - 1 - cc) * chunk, chunk)
        gates(lx_ref[pl.ds(rf, chunk), :], 0, af_ref, bf_ref)
        gates(lx_ref[pl.ds(rb, chunk), :], 1, ab_ref, bb_ref)

        def grp(gg, c2):
            hf, hb = c2
            gf = pl.multiple_of(gg * 8, 8)
            gb = pl.multiple_of((ng - 1 - gg) * 8, 8)
            a8, b8 = local_scan(af_ref[pl.ds(gf, 8), :], bf_ref[pl.ds(gf, 8), :], False)
            h8 = b8 + a8 * hf
            hf_ref[pl.ds(rf + gf, 8), :] = h8
            a8, b8 = local_scan(ab_ref[pl.ds(gb, 8), :], bb_ref[pl.ds(gb, 8), :], True)
            g8 = b8 + a8 * hb
            hb_ref[pl.ds(rb + gb, 8), :] = g8
            return (jnp.broadcast_to(h8[7:8, :], (8, BRANCH_W)), jnp.broadcast_to(g8[0:1, :], (8, BRANCH_W)))

        return lax.fori_loop(0, ng, grp, carry, unroll=4)

    init = (jnp.broadcast_to(h0_ref[0, 0:1, :], (8, BRANCH_W)), jnp.broadcast_to(h0_ref[0, 1:2, :], (8, BRANCH_W)))
    hf, hb = lax.fori_loop(0, nc, chunk_body, init)
    st_ref[0, 0:1, :] = hf[0:1, :]
    st_ref[0, 1:2, :] = hb[0:1, :]

    def combine(c, _):
        rows = pl.ds(pl.multiple_of(c * chunk, chunk), chunk)
        y_ref[rows, :] = (hf_ref[rows, :] + hb_ref[rows, :]) * _gelu_tanh(lg_ref[rows, :])
        return 0

    lax.fori_loop(0, nc, combine, 0)


def _lru_call(lx, lg, w, b, lam, h0, *, l, l_state, n_seq, t, row0):
    sb0 = row0 // t
    chunk = min(t, 256)
    seq = lambda: pl.BlockSpec((t, BRANCH_W), lambda s: (sb0 + s, 0))
    return pl.pallas_call(
        functools.partial(_lru_kernel, t=t, chunk=chunk),
        grid=(n_seq,),
        in_specs=[seq(), seq(), _layer_spec(w, l), _layer_spec(b, l), _layer_spec(lam, l),
                  pl.BlockSpec((1, None, 2, BRANCH_W), lambda s: (s, l_state, 0, 0))],
        out_specs=[pl.BlockSpec((t, BRANCH_W), lambda s: (s, 0)),
                   pl.BlockSpec((1, 2, BRANCH_W), lambda s: (s, 0, 0))],
        out_shape=[jax.ShapeDtypeStruct((n_seq * t, BRANCH_W), F32),
                   jax.ShapeDtypeStruct((n_seq, 2, BRANCH_W), F32)],
        scratch_shapes=[pltpu.VMEM((t, BRANCH_W), F32), pltpu.VMEM((t, BRANCH_W), F32)]
                       + [pltpu.VMEM((chunk, BRANCH_W), F32)] * 4,
        compiler_params=_cparams(("parallel",)),
        name="rglru",
    )(lx, lg, w, b, lam, h0)


def _rows_dot_exact(x, w01):
    r = x.shape[0]
    out = jnp.dot(jnp.concatenate(_split3(x), axis=0), w01, preferred_element_type=F32)
    return out[0:r] + out[r:2 * r] + out[2 * r:3 * r]


def _dot01_exact(m01, y):
    return sum(jnp.dot(m01, part, preferred_element_type=F32) for part in _split3(y))


def _gdn_kernel(q_ref, k_ref, v_ref, ggb_ref, s0_ref, o_ref, sT_ref, s_ref, *, d, reverse, n_par, n_chunk, n_tile):
    ti = pl.program_id(1)
    w4 = GDN_H * GDN_DK
    n_all = n_par * n_chunk
    tt = n_all * CHUNK
    blockmask = (_row_iota((w4, w4)) // CHUNK) == (_lane_iota((w4, w4)) // CHUNK)

    def expand_rows(y):
        yt = jnp.concatenate([y] * GDN_H, axis=0)
        zero = jnp.zeros((), y.dtype)
        parts = [jnp.where(blockmask, yt[:, w4 * u:w4 * (u + 1)], zero) for u in range(y.shape[1] // w4)]
        return parts[0] if len(parts) == 1 else jnp.concatenate(parts, axis=1)

    def heads_dot3(lhs, y):
        r = lhs.shape[0]
        lh, ll = _split(lhs)
        yh, yl = _split(y)
        out = jnp.dot(jnp.concatenate([lh, ll], axis=0), expand_rows(yh), preferred_element_type=F32)
        return out[0:r] + out[r:2 * r] + jnp.dot(lh, expand_rows(yl), preferred_element_type=F32)

    def heads_dot2(lhs, y):
        r = lhs.shape[0]
        lh, ll = _split(lhs)
        out = jnp.dot(jnp.concatenate([lh, ll], axis=0), expand_rows(y.astype(BF16)), preferred_element_type=F32)
        return out[0:r] + out[r:2 * r]

    def heads_dot1(lhs, y):
        return jnp.dot(lhs.astype(BF16), expand_rows(y.astype(BF16)), preferred_element_type=F32)

    @pl.when(ti == 0)
    def _():
        for p in range(n_par):
            s_ref[p] = expand_rows(s0_ref[p, 0])

    ri = _row_iota((CHUNK, w4))
    cj = _lane_iota((CHUNK, w4)) % CHUNK
    incl = (cj >= ri) if reverse else (cj <= ri)
    strict = (cj > ri) if reverse else (cj < ri)
    eye_sbs = (cj == ri)
    rep = eye_sbs.astype(BF16)

    er = _row_iota((128, 2 * w4))
    el = _lane_iota((128, 2 * w4))
    e_gb = (er == jnp.where(el < w4, d * GDN_H, 2 * GDN_H + d * GDN_H) + (el % w4) // CHUNK).astype(BF16)
    gbe = _rows_dot_exact(ggb_ref[...], e_gb)
    g_all = gbe[:, 0:w4]
    beta_all = gbe[:, w4:2 * w4]
    tr = _row_iota((tt, tt))
    tc = _lane_iota((tt, tt))
    same_chunk = (tr // CHUNK) == (tc // CHUNK)
    tri_bd = (same_chunk & ((tc >= tr) if reverse else (tc <= tr))).astype(BF16)
    ones_bd = same_chunk.astype(BF16)
    gc_all = _dot01_exact(tri_bd, g_all)
    diag_sel = (_row_iota((tt, w4)) % CHUNK) == (_lane_iota((tt, w4)) % CHUNK)
    gct_all = _dot01_exact(ones_bd, jnp.where(diag_sel, gc_all, 0.0))

    cs = range(n_all)
    rows = [slice(c * CHUNK, (c + 1) * CHUNK) for c in cs]
    q = [q_ref[r, :] for r in rows]
    k = [k_ref[r, :] for r in rows]
    v = [v_ref[r, :] for r in rows]
    beta = [beta_all[r, :] for r in rows]
    gc = [gc_all[r, :] for r in rows]
    decay = [jnp.where(incl, jnp.exp(jnp.where(incl, gc[c] - gct_all[rows[c], :], 0.0)), 0.0) for c in cs]
    kb = [k[c].astype(BF16) for c in cs]
    w_k = [jnp.where(blockmask,
                     lax.dot_general(kb[c], rep, (((0,), (0,)), ((), ())), preferred_element_type=F32),
                     0.0).astype(BF16) for c in cs]
    kq = [jnp.dot(jnp.concatenate([kb[c], q[c].astype(BF16)], axis=0), w_k[c], preferred_element_type=F32)
          for c in cs]
    qk = [(kq[c][CHUNK:2 * CHUNK] * decay[c]).astype(BF16) for c in cs]
    a = [jnp.where(strict, beta[c] * kq[c][0:CHUNK] * decay[c], 0.0) for c in cs]
    t_inv = [jnp.where(eye_sbs, 1.0, 0.0) - a[c] for c in cs]
    pw = [heads_dot3(a[c], a[c]) for c in cs]
    for stage in range(4):
        mm = heads_dot3 if stage < N_PRECISE else heads_dot1
        both = [mm(jnp.concatenate([t_inv[c], pw[c]], axis=0), pw[c]) for c in cs]
        t_inv = [t_inv[c] + both[c][0:CHUNK] for c in cs]
        pw = [both[c][CHUNK:2 * CHUNK] for c in cs]
    t_inv = [t_inv[c] + heads_dot1(t_inv[c], pw[c]) for c in cs]
    egc = [jnp.exp(gc[c]) for c in cs]
    sol = [heads_dot2(t_inv[c], jnp.concatenate([v[c] * beta[c], k[c] * (beta[c] * egc[c])], axis=1)) for c in cs]
    g_last = [gc[c][0:1, :] if reverse else gc[c][CHUNK - 1:CHUNK, :] for c in cs]
    wq = [jnp.concatenate([sol[c][:, w4:2 * w4], q[c] * egc[c]], axis=0).astype(BF16) for c in cs]
    kdec = [(k[c] * jnp.exp(g_last[c] - gc[c])).astype(BF16) for c in cs]

    for cc in range(n_chunk):
        for p in range(n_par):
            c = p * n_chunk + ((n_chunk - 1 - cc) if reverse else cc)
            s = s_ref[p]
            ws_qs = jnp.dot(wq[c], s.astype(BF16), preferred_element_type=F32)
            v_new = sol[c][:, 0:w4] - ws_qs[0:CHUNK]
            vb = v_new.astype(BF16)
            o_ref[rows[c], :] = (
                ws_qs[CHUNK:2 * CHUNK] + jnp.dot(qk[c], expand_rows(vb), preferred_element_type=F32))
            upd = lax.dot_general(kdec[c], vb, (((0,), (0,)), ((), ())), preferred_element_type=F32)
            s_ref[p] = s * jnp.exp(g_last[c]) + jnp.where(blockmask, upd, 0.0)

    @pl.when(ti == n_tile - 1)
    def _():
        for p in range(n_par):
            s = s_ref[p]
            sT_ref[p, 0] = s[0:64] + s[64:128] + s[128:192] + s[192:256]


def _gdn_call(q, k, v, ggb, s0, *, d, l_state, n_seq, t, row0, tt, n_par):
    reverse = d == 1
    n_tile = t // tt
    assert n_par == 1 or n_tile == 1
    b0 = row0 // (tt * n_par)
    tidx = (lambda i: n_tile - 1 - i) if reverse else (lambda i: i)
    blk = lambda w: pl.BlockSpec((tt * n_par, w), lambda s, i: (b0 + s * n_tile + tidx(i), 0))
    return pl.pallas_call(
        functools.partial(_gdn_kernel, d=d, reverse=reverse, n_par=n_par, n_chunk=tt // CHUNK, n_tile=n_tile),
        grid=(n_seq // n_par, n_tile),
        in_specs=[blk(256), blk(256), blk(256), blk(128),
                  pl.BlockSpec((n_par, None, 1, CHUNK, 256), lambda s, i: (s, l_state, d, 0, 0))],
        out_specs=[pl.BlockSpec((tt * n_par, 256), lambda s, i: (s * n_tile + tidx(i), 0)),
                   pl.BlockSpec((n_par, 1, CHUNK, 256), lambda s, i: (s, 0, 0, 0))],
        out_shape=[jax.ShapeDtypeStruct((n_seq * t, 256), F32),
                   jax.ShapeDtypeStruct((n_seq, 1, CHUNK, 256), F32)],
        scratch_shapes=[pltpu.VMEM((n_par, 256, 256), F32)],
        compiler_params=_cparams(("parallel", "arbitrary")),
        name="gdn_bwd" if reverse else "gdn_fwd",
    )(q, k, v, ggb, s0)


def _mix_mlp_kernel(modrow_ref, xc_ref, xd_ref, mod_ref, g1_ref,
                    oac_ref, oad_ref, obc_ref, obd_ref, ofc_ref, ofd_ref, orc_ref, ord_ref, odc_ref, odd_ref,
                    gz_ref, gng_ref, ones_ref, wm_ref, bm_ref, wb_ref, wo_ref, g2_ref, w1_ref, w2_ref,
                    yc_ref, yd_ref, *, n_ctx_tiles):
    is_ctx = pl.program_id(0) < n_ctx_tiles
    pick = lambda c_ref, d_ref: jnp.where(is_ctx, c_ref[...], d_ref[...])
    x = pick(xc_ref, xd_ref)
    h = _modulate(x, g1_ref[...], mod_ref[0, 0:1, :], mod_ref[0, 1:2, :]).astype(BF16)
    oc = pick(ofc_ref, ofd_ref) + pick(orc_ref, ord_ref)
    ms = _group_sum(oc * oc, ones_ref[...]) * (1.0 / GDN_DK)
    oc = (oc * lax.rsqrt(ms + EPS) * gng_ref[...]) * _silu(gz_ref[...])
    branches = (pick(oac_ref, oad_ref), pick(obc_ref, obd_ref), oc, pick(odc_ref, odd_ref))
    acc = None
    for m in range(N_BRANCH):
        cols = slice(D_MODEL * m, D_MODEL * (m + 1))
        gate = _sigmoid(jnp.dot(h, wm_ref[:, cols], preferred_element_type=F32) + bm_ref[:, cols])
        term = gate * jnp.dot(branches[m].astype(BF16), wb_ref[m], preferred_element_type=F32)
        acc = term if acc is None else acc + term
    x = x + mod_ref[0, 2:3, :] * jnp.dot(acc.astype(BF16), wo_ref[...], preferred_element_type=F32)

    h = _modulate(x, g2_ref[...], mod_ref[0, 3:4, :], mod_ref[0, 4:5, :]).astype(BF16)
    acc = None
    for j in range(D_FF // D_MODEL):
        cols = slice(D_MODEL * j, D_MODEL * (j + 1))
        a = jnp.maximum(jnp.dot(h, w1_ref[:, cols], preferred_element_type=F32), 0.0)
        term = jnp.dot((a * a).astype(BF16), w2_ref[cols, :], preferred_element_type=F32)
        acc = term if acc is None else acc + term
    y = x + mod_ref[0, 5:6, :] * acc

    @pl.when(is_ctx)
    def _():
        yc_ref[...] = y

    @pl.when(jnp.logical_not(is_ctx))
    def _():
        yd_ref[...] = y


def _mix_mlp_call(modrow, x_pair, mod_all, g1, pairs, gz, gng, ones_bd, wm, bm, wb, wo, g2, w1, w2, *, l, tm):
    x_c, x_d = x_pair
    n, d = x_c.shape[0] + x_d.shape[0], x_c.shape[1]
    nct = x_c.shape[0] // tm
    row_tile = lambda w: pl.BlockSpec((tm, w), lambda i, *_: (i, 0))
    pair = lambda w: [pl.BlockSpec((tm, w), lambda i, *_: (jnp.minimum(i, nct - 1), 0)),
                      pl.BlockSpec((tm, w), lambda i, *_: (jnp.maximum(i - nct, 0), 0))]
    pair_specs = [sp for _ in pairs for sp in pair(256)]
    resident = lambda arr: _layer_spec(arr, l, single_buffer=True)
    grid_spec = pltpu.PrefetchScalarGridSpec(
        num_scalar_prefetch=1, grid=(n // tm,),
        in_specs=pair(d) + [pl.BlockSpec((None, 1, 6, d), lambda i, modrow: (l, modrow[i], 0, 0)),
                            _layer_spec(g1, l)] + pair_specs
                 + [row_tile(256), _layer_spec(gng, l), pl.BlockSpec((256, 256), lambda i, *_: (0, 0)),
                    resident(wm), _layer_spec(bm, l), resident(wb), resident(wo),
                    _layer_spec(g2, l), resident(w1), resident(w2)],
        out_specs=pair(d))
    flat = [a for pr in pairs for a in pr]
    return pl.pallas_call(
        functools.partial(_mix_mlp_kernel, n_ctx_tiles=nct), grid_spec=grid_spec,
        out_shape=[jax.ShapeDtypeStruct(x_c.shape, F32), jax.ShapeDtypeStruct(x_d.shape, F32)],
        compiler_params=_cparams(("arbitrary",)),
        name="merge_mlp",
    )(modrow, x_c, x_d, mod_all, g1, *flat, gz, gng, ones_bd, wm, bm, wb, wo, g2, w1, w2)


def _rope_tables(t, tm):
    n_freq = HEAD_DIM // 4
    inv = np.float32(ROPE_BASE) ** (-np.arange(n_freq, dtype=np.float32) / np.float32(n_freq))
    pos = np.arange(t)
    row = (pos // GRID_W).astype(np.float32)[:, None]
    col = (pos % GRID_W).astype(np.float32)[:, None]
    ar = (row * inv).astype(np.float32)
    ac = (col * inv).astype(np.float32)
    cos64 = np.concatenate([np.cos(ar), np.cos(ar), np.cos(ac), np.cos(ac)], axis=1)
    sin64 = np.concatenate([-np.sin(ar), np.sin(ar), -np.sin(ac), np.sin(ac)], axis=1)
    cos = np.concatenate([np.ones((tm, 128), np.float32), np.tile(cos64, (1, 2)).astype(np.float32)], axis=0)
    sin = np.concatenate([np.zeros((tm, 128), np.float32), np.tile(sin64, (1, 2)).astype(np.float32)], axis=0)
    return jnp.asarray(cos), jnp.asarray(sin)


def _tile_meta(n_ctx, n_seq_dec, t_ctx, t_dec, tm):
    nct = n_ctx // tm
    per = t_dec // tm
    ndt = n_seq_dec * per
    idx = np.arange(nct + ndt)
    dec = idx >= nct
    di = np.maximum(idx - nct, 0)
    modrow = np.where(dec, 1 + di // per, 0)
    ropeblk = np.where(dec, 1 + di % per, 0)
    tiles_ctx = max(t_ctx // tm, 1)
    isstart = np.where(dec, di % per == 0, idx % tiles_ctx == 0)
    isend = np.where(dec, di % per == per - 1, idx % tiles_ctx == tiles_ctx - 1)
    as_i32 = lambda a: jnp.asarray(a.astype(np.int32))
    return as_i32(modrow), as_i32(ropeblk), as_i32(isstart), as_i32(isend)


def _block_diag_gates(w):
    depth, two, n, c, _ = w.shape
    eye = jnp.eye(n, dtype=w.dtype)
    return jnp.einsum("ldnij,nm->ldnimj", w, eye).reshape(depth, two, n * c, n * c)


def kernel(x_prompt, x_sample, c, cache_a_k, cache_a_v, cache_d_k, cache_d_v, state_lru, state_gdn, c_ctx, mod_w, mod_b, norm1_g, norm2_g, w_in, a_qn_g, a_kn_g, a_sink, lru_conv_w, lru_conv_b, lru_wr, lru_br, lru_wi, lru_bi, lru_lam, gdn_conv_w, gdn_a_log, gdn_dt_bias, gdn_norm_g, d_qn_g, d_kn_g, w_branch, w_merge, b_merge, w_out, mlp_w1, mlp_w2):
    batch, seq, d = x_prompt.shape
    dec_batch, dec_seq, _ = x_sample.shape
    depth = mod_w.shape[0]
    past = cache_a_k.shape[2]
    n_ctx = batch * seq
    n_dec = dec_batch * dec_seq
    tm = 256
    gdn_par = 2 if batch % 2 == 0 else 1
    gdn_tt = 512 if dec_seq % 512 == 0 else 256
    assert d == D_MODEL and seq % tm == 0 and dec_seq % tm == 0 and tm % seq == 0
    assert dec_batch + 1 <= 8 and n_ctx % dec_seq == 0

    cond8 = jnp.zeros((8, d), F32).at[0].set(c_ctx).at[1:1 + dec_batch].set(c)
    mod_all = _mod_call(cond8, mod_w, mod_b).reshape(depth, 8, 6, d)

    meta = _tile_meta(n_ctx, dec_batch, seq, dec_seq, tm)
    cos_t, sin_t = _rope_tables(dec_seq, tm)
    lane = np.arange(256)
    ones_bd = jnp.asarray((lane[:, None] // HEAD_DIM == lane[None, :] // HEAD_DIM).astype(np.float32)).astype(BF16)

    w_packed = jnp.concatenate([w_in[:, :, :2064], jnp.zeros((depth, d, GDN_PAD), F32), w_in[:, :, 2064:]],
                               axis=2).astype(BF16)
    pad128 = lambda v: jnp.pad(v.reshape(depth, 1, -1), ((0, 0), (0, 0), (0, 128 - v[0].size)))
    vecs = (
        (jnp.tile(a_qn_g, (1, 4)) * Q_SCALE)[:, None, :],
        jnp.tile(a_kn_g, (1, 2))[:, None, :],
        (jnp.tile(d_qn_g, (1, 4)) * Q_SCALE)[:, None, :],
        jnp.tile(d_kn_g, (1, 2))[:, None, :],
        lru_conv_w, lru_conv_b[:, None, :], gdn_conv_w,
        pad128(gdn_a_log), pad128(gdn_dt_bias),
    )
    g1 = norm1_g[:, None, :]
    g2 = norm2_g[:, None, :]
    wr_bd = _block_diag_gates(lru_wr)
    wi_bd = _block_diag_gates(lru_wi)
    w_lru = jnp.concatenate([wr_bd[:, 0], wi_bd[:, 0], wr_bd[:, 1], wi_bd[:, 1]], axis=-1).astype(BF16)
    b_lru = jnp.concatenate([lru_br[:, 0], lru_bi[:, 0], lru_br[:, 1], lru_bi[:, 1]], axis=-1)[:, None, :]
    gng = jnp.tile(gdn_norm_g, (1, 4))[:, None, :]
    wm = w_merge.astype(BF16)
    bm = b_merge[:, None, :]
    wb = w_branch.astype(BF16)
    wo = w_out.astype(BF16)
    w1 = mlp_w1.astype(BF16)
    w2 = mlp_w2.astype(BF16)
    sink = a_sink.reshape(-1)
    caches = [t.reshape(dec_batch, depth, past, 2 * HEAD_DIM) for t in (cache_a_k, cache_a_v, cache_d_k, cache_d_v)]
    s0_dec = state_gdn.transpose(0, 1, 2, 4, 3, 5).reshape(dec_batch, depth, 2, CHUNK, 256)
    zeros_lru = jnp.zeros((batch, 1, 2, BRANCH_W), F32)
    zeros_gdn = jnp.zeros((batch, 1, 2, CHUNK, 256), F32)

    x = (x_prompt.reshape(n_ctx, d), x_sample.reshape(n_dec, d))

    kv_ctx, lru_states, gdn_states = [], [], []
    for l in range(depth):
        (qa, ka, va, lx, lg, gq, gk, gv, gz, ggb, qd, kd, vd) = _inproj_call(
            x, meta, mod_all, g1, w_packed, cos_t, sin_t, vecs, ones_bd, l=l, tm=tm)

        oa_c, od_c = _attn_ctx_call(sink, qa, ka, va, qd, kd, vd, l=l, n_seq=batch, t=seq)
        oa_d = _attn_band_call(sink, qa, ka, va, caches[0], caches[1], l=l, n_seq=dec_batch, t=dec_seq,
                               row0=n_ctx, tq=512)
        od_d = _attn_full_call(qd, kd, vd, caches[2], caches[3], l=l, n_seq=dec_batch, t=dec_seq,
                               row0=n_ctx, tq=256)

        ob_c, st_c = _lru_call(lx, lg, w_lru, b_lru, lru_lam, zeros_lru, l=l, l_state=0, n_seq=batch, t=seq,
                               row0=0)
        ob_d, _ = _lru_call(lx, lg, w_lru, b_lru, lru_lam, state_lru, l=l, l_state=l, n_seq=dec_batch,
                            t=dec_seq, row0=n_ctx)

        oc_pairs, sts = [], []
        for dd in range(2):
            o_c, s_c = _gdn_call(gq, gk, gv, ggb, zeros_gdn, d=dd, l_state=0, n_seq=batch, t=seq, row0=0,
                                 tt=seq, n_par=gdn_par)
            o_d, _ = _gdn_call(gq, gk, gv, ggb, s0_dec, d=dd, l_state=l, n_seq=dec_batch, t=dec_seq,
                               row0=n_ctx, tt=gdn_tt, n_par=1)
            oc_pairs.append((o_c, o_d))
            sts.append(s_c)

        x = _mix_mlp_call(meta[0], x, mod_all, g1,
                          [(oa_c, oa_d), (ob_c, ob_d), oc_pairs[0], oc_pairs[1], (od_c, od_d)],
                          gz, gng, ones_bd, wm, bm, wb, wo, g2, w1, w2, l=l, tm=tm)

        kv_ctx.append([t[:n_ctx] for t in (ka, va, kd, vd)])
        lru_states.append(st_c)
        gdn_states.append(jnp.concatenate(sts, axis=1))

    y_prompt = x[0].reshape(batch, seq, d)
    y_sample = x[1].reshape(dec_batch, dec_seq, d)
    new_kv = [jnp.stack([kv_ctx[l][u].reshape(batch, seq, 2, HEAD_DIM) for l in range(depth)], axis=1)
              for u in range(4)]
    new_lru = jnp.stack(lru_states, axis=1)
    new_gdn = (jnp.stack(gdn_states, axis=1).reshape(batch, depth, 2, GDN_DK, GDN_H, GDN_DK)
               .transpose(0, 1, 2, 4, 3, 5))
    return (y_prompt, y_sample, new_kv[0], new_kv[1], new_kv[2], new_kv[3], new_lru, new_gdn)
```

```python
import functools

import numpy as np
import jax
import jax.numpy as jnp
from jax import lax
from jax.experimental import pallas as pl
from jax.experimental.pallas import tpu as pltpu

F32 = jnp.float32
BF16 = jnp.bfloat16
HIGHEST = lax.Precision.HIGHEST

D_MODEL = 1024
HEAD_DIM = 64
BRANCH_W = 256
N_BRANCH = 4
GRID_W = 64
WINDOW = 128
LRU_C = 8.0
CONV_W = 4
CONV_LEFT = 2
GDN_H = 4
GDN_DK = 64
CHUNK = 64
D_FF = 4 * D_MODEL
ROPE_BASE = 10000.0
EPS = 1e-6
NEG = -1e30
LOG2E = 1.4426950408889634
Q_SCALE = HEAD_DIM ** -0.5 * LOG2E
GDN_PAD = 112
IN_COLS_PACKED = 2688
HALO = 8
N_PRECISE = 4

V7X_VMEM_LIMIT = 56 * 1024 * 1024


def _cparams(sem, vmem=V7X_VMEM_LIMIT):
    return pltpu.CompilerParams(dimension_semantics=sem, vmem_limit_bytes=vmem)


def _layer_spec(arr, l, single_buffer=False):
    nd = arr.ndim - 1
    mode = dict(pipeline_mode=pl.Buffered(1)) if single_buffer else {}
    return pl.BlockSpec((None,) + arr.shape[1:], lambda *_: (l,) + (0,) * nd, **mode)


def _bdot(a, b):
    return jnp.dot(a.astype(BF16), b.astype(BF16), preferred_element_type=F32)


def _split(x):
    hi = x.astype(BF16)
    lo = (x - hi.astype(F32)).astype(BF16)
    return hi, lo


def _split3(x):
    hi = x.astype(BF16)
    r = x - hi.astype(F32)
    mid = r.astype(BF16)
    lo = (r - mid.astype(F32)).astype(BF16)
    return hi, mid, lo


def _group_sum(x, ones_bd):
    return jnp.dot(x.astype(BF16), ones_bd, preferred_element_type=F32)


def _sigmoid(x):
    return 0.5 * jnp.tanh(0.5 * x) + 0.5


def _silu(x):
    return x * _sigmoid(x)


def _softplus(x):
    return jnp.maximum(x, 0.0) + jnp.log1p(jnp.exp(-jnp.abs(x)))


def _modulate(x, g, shift, scale):
    ms = jnp.mean(x * x, axis=-1, keepdims=True)
    return (x * lax.rsqrt(ms + EPS)) * (g * (1.0 + scale)) + shift


def _lane_iota(shape):
    return lax.broadcasted_iota(jnp.int32, shape, len(shape) - 1)


def _row_iota(shape):
    return lax.broadcasted_iota(jnp.int32, shape, len(shape) - 2)


def _mod_kernel(cond_ref, w_ref, b_ref, o_ref):
    hi, lo = _split(_silu(cond_ref[...]))
    w = w_ref[0].astype(BF16)
    o_ref[0] = (jnp.dot(hi, w, preferred_element_type=F32) + jnp.dot(lo, w, preferred_element_type=F32)
                + b_ref[0])


def _mod_call(cond8, mod_w, mod_b):
    depth, d, n = mod_w.shape
    tn = 1536
    return pl.pallas_call(
        _mod_kernel,
        grid=(depth, n // tn),
        in_specs=[
            pl.BlockSpec((8, d), lambda l, j: (0, 0)),
            pl.BlockSpec((1, d, tn), lambda l, j: (l, 0, j)),
            pl.BlockSpec((1, 1, tn), lambda l, j: (l, 0, j)),
        ],
        out_specs=pl.BlockSpec((1, 8, tn), lambda l, j: (l, 0, j)),
        out_shape=jax.ShapeDtypeStruct((depth, 8, n), F32),
        compiler_params=_cparams(("parallel", "parallel")),
        name="mod_vectors",
    )(cond8, mod_w, mod_b.reshape(depth, 1, n))


def _rope(x, cos, sin):
    outs = []
    for j in range(x.shape[1] // 128):
        xb = x[:, 128 * j:128 * (j + 1)]
        lane = _lane_iota(xb.shape)
        sw = jnp.where((lane & 16) == 0, pltpu.roll(xb, 112, 1), pltpu.roll(xb, 16, 1))
        outs.append(xb * cos + sw * sin)
    return outs[0] if len(outs) == 1 else jnp.concatenate(outs, axis=1)


def _head_rms(x, gain, ones_bd):
    ms = _group_sum(x * x, ones_bd) * (1.0 / HEAD_DIM)
    return x * lax.rsqrt(ms + EPS) * gain


def _centred_conv(g, w, tm):
    rows = g.shape[0]
    acc = None
    for j in range(CONV_W):
        sh = (CONV_LEFT - j) % rows
        gj = g if sh == 0 else pltpu.roll(g, sh, 0)
        term = gj[HALO:HALO + tm] * w[j:j + 1, :]
        acc = term if acc is None else acc + term
    return acc


def _inproj_kernel(modrow_ref, ropeblk_ref, isstart_ref, isend_ref,
                   xpc_ref, xc_ref, xnc_ref, xpd_ref, xd_ref, xnd_ref, mod_ref, g1_ref, w_ref, cos_ref, sin_ref,
                   aqg_ref, akg_ref, dqg_ref, dkg_ref, lcw_ref, lcb_ref, gcw_ref,
                   alog_ref, dtb_ref, ones_ref, *rest, tm, n_ctx_tiles, n_alias):
    (qa_ref, ka_ref, va_ref, lx_ref, lg_ref, gq_ref, gk_ref, gv_ref, gz_ref, ggb_ref,
     qd_ref, kd_ref, vd_ref, kac_ref, vac_ref, kdc_ref, vdc_ref) = rest[n_alias:]
    i = pl.program_id(0)
    is_ctx = i < n_ctx_tiles
    pick = lambda c_ref, d_ref: jnp.where(is_ctx, c_ref[...], d_ref[...])
    xfull = jnp.concatenate([pick(xpc_ref, xpd_ref), pick(xc_ref, xd_ref), pick(xnc_ref, xnd_ref)],
                            axis=0)
    h = _modulate(xfull, g1_ref[...], mod_ref[0, 0:1, :], mod_ref[0, 1:2, :])
    p = jnp.dot(h.astype(BF16), w_ref[...], preferred_element_type=F32)
    hb, hm = slice(0, tm + 2 * HALO), slice(HALO, HALO + tm)
    proj = lambda rows, c0, c1: p[rows, c0:c1]

    ones_bd = ones_ref[...]
    cos = cos_ref[...]
    sin = sin_ref[...]

    pa = proj(hm, 0, 512)
    qa_ref[...] = _rope(_head_rms(pa[:, 0:256], aqg_ref[...], ones_bd), cos, sin)
    ka = _rope(_head_rms(pa[:, 256:384], akg_ref[...], ones_bd[:128, :128]), cos, sin)
    ka_ref[...] = ka
    va_ref[...] = pa[:, 384:512]

    row = _row_iota((tm + 2 * HALO, 1))
    keep = jnp.logical_and(jnp.logical_or(row >= HALO, isstart_ref[i] == 0),
                           jnp.logical_or(row < HALO + tm, isend_ref[i] == 0))

    lxg = jnp.where(keep, proj(hb, 512, 768), 0.0)
    lx_ref[...] = _centred_conv(lxg, lcw_ref[...], tm) + lcb_ref[...]
    lg_ref[...] = proj(hm, 768, 1024)

    qkv = jnp.where(keep, proj(hb, 1024, 1792), 0.0)
    qkv = _silu(_centred_conv(qkv, gcw_ref[...], tm))
    gq = qkv[:, 0:256]
    gk = qkv[:, 256:512]
    gq_ref[...] = gq * lax.rsqrt(_group_sum(gq * gq, ones_bd) + EPS) * (GDN_DK ** -0.5)
    gk_ref[...] = gk * lax.rsqrt(_group_sum(gk * gk, ones_bd) + EPS)
    gv_ref[...] = qkv[:, 512:768]
    pz = proj(hm, 1792, 2176)
    gz_ref[...] = pz[:, 0:256]
    ab = pz[:, 256:384]
    g = -jnp.exp(alog_ref[...]) * _softplus(ab + dtb_ref[...])
    lane = _lane_iota(ab.shape)
    ggb_ref[...] = jnp.where(lane < 2 * GDN_H, g, _sigmoid(ab))

    pd = proj(hm, 2176, 2688)
    qd_ref[...] = _rope(_head_rms(pd[:, 0:256], dqg_ref[...], ones_bd), cos, sin)
    kd = _rope(_head_rms(pd[:, 256:384], dkg_ref[...], ones_bd[:128, :128]), cos, sin)
    kd_ref[...] = kd
    vd_ref[...] = pd[:, 384:512]

    @pl.when(is_ctx)
    def _():
        kac_ref[...] = ka
        vac_ref[...] = pa[:, 384:512]
        kdc_ref[...] = kd
        vdc_ref[...] = pd[:, 384:512]


def _x_pair_specs(x_pair, tm):
    x_c, x_d = x_pair
    d = x_c.shape[1]
    nct = x_c.shape[0] // tm
    hb = tm // HALO
    specs = []
    for arr, first in ((x_c, 0), (x_d, nct)):
        nblk = arr.shape[0] // tm
        last_hb = arr.shape[0] // HALO - 1
        tile = lambda i, first=first, nblk=nblk: jnp.clip(i - first, 0, nblk - 1)
        specs += [
            pl.BlockSpec((HALO, d), lambda i, *_, t=tile, m=last_hb: (jnp.clip(t(i) * hb - 1, 0, m), 0)),
            pl.BlockSpec((tm, d), lambda i, *_, t=tile: (t(i), 0)),
            pl.BlockSpec((HALO, d), lambda i, *_, t=tile, m=last_hb: (jnp.clip((t(i) + 1) * hb, 0, m), 0)),
        ]
    return specs


def _inproj_call(x_pair, meta, mod_all, g1, w_packed, cos_t, sin_t, vecs, ones_bd, kv_prev, *, l, depth, tm):
    x_c, x_d = x_pair
    n, d = x_c.shape[0] + x_d.shape[0], x_c.shape[1]
    nt = n // tm
    nct = x_c.shape[0] // tm
    row_tile = lambda w: pl.BlockSpec((tm, w), lambda i, *_: (i, 0))
    kv_spec = pl.BlockSpec((None, None, tm, 128), lambda i, *_: (jnp.minimum(i, nct - 1), l, 0, 0))
    kv_shape = jax.ShapeDtypeStruct((nct, depth, tm, 128), F32)
    alias_in = [] if kv_prev is None else list(kv_prev)
    in_specs = _x_pair_specs(x_pair, tm) + [
        pl.BlockSpec((None, 1, 6, d), lambda i, modrow, *_: (l, modrow[i], 0, 0)),
        _layer_spec(g1, l),
        _layer_spec(w_packed, l),
        pl.BlockSpec((tm, 128), lambda i, modrow, ropeblk, *_: (ropeblk[i], 0)),
        pl.BlockSpec((tm, 128), lambda i, modrow, ropeblk, *_: (ropeblk[i], 0)),
    ] + [_layer_spec(v, l) for v in vecs] + [pl.BlockSpec((256, 256), lambda i, *_: (0, 0))]
    in_specs += [pl.BlockSpec(memory_space=pl.ANY)] * len(alias_in)
    widths = (256, 128, 128, 256, 256, 256, 256, 256, 256, 128, 256, 128, 128)
    grid_spec = pltpu.PrefetchScalarGridSpec(
        num_scalar_prefetch=4,
        grid=(nt,),
        in_specs=in_specs,
        out_specs=[row_tile(w) for w in widths] + [kv_spec] * 4,
    )
    operands = (*meta, x_c, x_c, x_c, x_d, x_d, x_d, mod_all, g1, w_packed, cos_t, sin_t, *vecs, ones_bd)
    outs = pl.pallas_call(
        functools.partial(_inproj_kernel, tm=tm, n_ctx_tiles=nct, n_alias=len(alias_in)),
        grid_spec=grid_spec,
        out_shape=[jax.ShapeDtypeStruct((n, w), F32) for w in widths] + [kv_shape] * 4,
        input_output_aliases={len(operands) + u: len(widths) + u for u in range(len(alias_in))},
        compiler_params=_cparams(("arbitrary",)),
        name="in_projection",
    )(*operands, *alias_in)
    return outs[:len(widths)], outs[len(widths):]


def _head_q(q, j, g):
    qj = q[:, 128 * j:128 * (j + 1)]
    lane = _lane_iota(qj.shape)
    sel = (lane < HEAD_DIM) if g == 0 else (lane >= HEAD_DIM)
    return jnp.where(sel, qj, 0.0).astype(BF16)


def _place_heads(res, j):
    r0 = res[0] if j == 0 else pltpu.roll(res[0], HEAD_DIM, 1)
    r1 = res[1] if j == 1 else pltpu.roll(res[1], HEAD_DIM, 1)
    lane = _lane_iota(r0.shape)
    return jnp.where(lane < HEAD_DIM, r0, r1)


def _attend_many(problems, order="staged"):
    work = []
    for pi, (q, ks, vs, masks, sinks, ks_sw) in enumerate(problems):
        if ks_sw is None:
            ks_sw = [pltpu.roll(k, HEAD_DIM, 1) for k in ks]
        work += [(pi, j, g, q, ks, vs, masks, sinks, ks_sw) for j in range(2) for g in range(2)]
    scores, probs, res = {}, {}, {}

    def score(w):
        pi, j, g, q, ks, vs, masks, sinks, ks_sw = w
        qm = _head_q(q, j, g)
        ss = []
        for k, ksw, mk in zip(ks, ks_sw, masks):
            s = lax.dot_general(qm, k if g == j else ksw, (((1,), (1,)), ((), ())), preferred_element_type=F32)
            ss.append(s if mk is None else jnp.where(mk, s, NEG))
        scores[w[:3]] = ss

    def softmax_num(w):
        pi, j, g, q, ks, vs, masks, sinks, ks_sw = w
        ss = scores[w[:3]]
        m = ss[0].max(axis=-1, keepdims=True)
        for s in ss[1:]:
            m = jnp.maximum(m, s.max(axis=-1, keepdims=True))
        if sinks is not None:
            m = jnp.maximum(m, sinks[2 * j + g])
        es = [jnp.exp2(s - m) for s in ss]
        den = sum(e.sum(axis=-1, keepdims=True) for e in es)
        if sinks is not None:
            den = den + jnp.exp2(sinks[2 * j + g] - m)
        probs[w[:3]] = ([e.astype(BF16) for e in es], den)

    def values(w):
        es, den = probs[w[:3]]
        res[w[:3]] = sum(jnp.dot(e, v, preferred_element_type=F32) for e, v in zip(es, w[5])) / den

    if order == "staged":
        for step in (score, softmax_num, values):
            for w in work:
                step(w)
    else:
        score(work[0])
        for prev, nxt in zip(work, work[1:] + [None]):
            if nxt is not None:
                score(nxt)
            softmax_num(prev)
            values(prev)
    return [jnp.concatenate([_place_heads([res[(pi, j, 0)], res[(pi, j, 1)]], j) for j in range(2)], axis=1)
            for pi in range(len(problems))]


def _attend(q, ks, vs, masks, sinks, ks_sw=None, order="staged"):
    return _attend_many([(q, ks, vs, masks, sinks, ks_sw)], order)[0]


def _bf(x):
    return x.astype(BF16)


def _attn_ctx_kernel(sink_ref, qa_ref, ka_ref, va_ref, qd_ref, kd_ref, vd_ref, oa_ref, od_ref, *, l, t, n_par):
    sinks = [sink_ref[4 * l + u] * LOG2E for u in range(4)]
    problems = []
    for p in range(n_par):
        r = slice(p * t, (p + 1) * t)
        problems.append((qa_ref[r, :], [_bf(ka_ref[r, :])], [_bf(va_ref[r, :])], [None], sinks, None))
        problems.append((qd_ref[r, :], [_bf(kd_ref[r, :])], [_bf(vd_ref[r, :])], [None], None, None))
    outs = _attend_many(problems)
    for p in range(n_par):
        oa_ref[p * t:(p + 1) * t, :] = outs[2 * p]
        od_ref[p * t:(p + 1) * t, :] = outs[2 * p + 1]


def _attn_ctx_call(sink, qa, ka, va, qd, kd, vd, *, l, n_seq, t):
    n_par = 2 if n_seq % 2 == 0 else 1
    blk = lambda w: pl.BlockSpec((n_par * t, w), lambda b, *_: (b, 0))
    grid_spec = pltpu.PrefetchScalarGridSpec(
        num_scalar_prefetch=1, grid=(n_seq // n_par,),
        in_specs=[blk(256), blk(128), blk(128), blk(256), blk(128), blk(128)],
        out_specs=[blk(256), blk(256)])
    return pl.pallas_call(
        functools.partial(_attn_ctx_kernel, l=l, t=t, n_par=n_par), grid_spec=grid_spec,
        out_shape=[jax.ShapeDtypeStruct((n_seq * t, 256), F32)] * 2,
        compiler_params=_cparams(("parallel",)),
        name="attn_context",
    )(sink, qa, ka, va, qd, kd, vd)


def _attn_band_kernel(sink_ref, q_ref, k_ref, v_ref, ck_ref, cv_ref, o_ref, *, l, tq, t):
    i = pl.program_id(1)
    start = pl.multiple_of(i * tq, tq)
    prev = pl.multiple_of(jnp.maximum(start - WINDOW, 0), WINDOW)
    nxt = pl.multiple_of(jnp.minimum(start + tq, t - WINDOW), WINDOW)
    k_loc = jnp.concatenate([k_ref[pl.ds(prev, WINDOW), :], k_ref[pl.ds(start, tq), :],
                             k_ref[pl.ds(nxt, WINDOW), :]], axis=0)
    v_loc = jnp.concatenate([v_ref[pl.ds(prev, WINDOW), :], v_ref[pl.ds(start, tq), :],
                             v_ref[pl.ds(nxt, WINDOW), :]], axis=0)
    shape = (tq, tq + 2 * WINDOW)
    qpos = start + _row_iota(shape)
    kpos = start - WINDOW + _lane_iota(shape)
    mask = (kpos >= 0) & (kpos < t) & (jnp.abs(qpos - kpos) <= WINDOW)
    sinks = [sink_ref[4 * l + u] * LOG2E for u in range(4)]
    o_ref[...] = _attend(q_ref[...], [_bf(ck_ref[0]), _bf(k_loc)], [_bf(cv_ref[0]), _bf(v_loc)],
                         [None, mask], sinks)


def _cache_spec(c, l):
    return pl.BlockSpec((1, None) + c.shape[2:], lambda b, i, *_: (b, l, 0, 0))


def _attn_band_call(sink, q, k, v, ck, cv, *, l, n_seq, t, row0, tq):
    nq = t // tq
    qb0 = row0 // tq
    sb0 = row0 // t
    grid_spec = pltpu.PrefetchScalarGridSpec(
        num_scalar_prefetch=1, grid=(n_seq, nq),
        in_specs=[
            pl.BlockSpec((tq, 256), lambda b, i, *_: (qb0 + b * nq + i, 0)),
            pl.BlockSpec((t, 128), lambda b, i, *_: (sb0 + b, 0)),
            pl.BlockSpec((t, 128), lambda b, i, *_: (sb0 + b, 0)),
            _cache_spec(ck, l), _cache_spec(cv, l),
        ],
        out_specs=pl.BlockSpec((tq, 256), lambda b, i, *_: (b * nq + i, 0)))
    return pl.pallas_call(
        functools.partial(_attn_band_kernel, l=l, tq=tq, t=t), grid_spec=grid_spec,
        out_shape=jax.ShapeDtypeStruct((n_seq * t, 256), F32),
        compiler_params=_cparams(("parallel", "parallel")),
        name="attn_banded",
    )(sink, q, k, v, ck, cv)


def _attn_full_kernel(q_ref, k_ref, v_ref, ck_ref, cv_ref, o_ref, ks_ref, vs_ref, *, past, t):
    @pl.when(pl.program_id(1) == 0)
    def _():
        kc = _bf(ck_ref[0])
        kl = _bf(k_ref[...])
        ks_ref[0, 0:past, :] = kc
        ks_ref[0, past:past + t, :] = kl
        ks_ref[1, 0:past, :] = pltpu.roll(kc, HEAD_DIM, 1)
        ks_ref[1, past:past + t, :] = pltpu.roll(kl, HEAD_DIM, 1)
        vs_ref[0:past, :] = _bf(cv_ref[0])
        vs_ref[past:past + t, :] = _bf(v_ref[...])

    o_ref[...] = _attend(q_ref[...], [ks_ref[0]], [vs_ref[...]], [None], None, ks_sw=[ks_ref[1]], order="skewed")


def _attn_full_call(q, k, v, ck, cv, *, l, n_seq, t, row0, tq):
    nq = t // tq
    qb0 = row0 // tq
    sb0 = row0 // t
    past = ck.shape[2]
    assert past % 16 == 0
    return pl.pallas_call(
        functools.partial(_attn_full_kernel, past=past, t=t), grid=(n_seq, nq),
        scratch_shapes=[pltpu.VMEM((2, past + t, 128), BF16), pltpu.VMEM((past + t, 128), BF16)],
        in_specs=[
            pl.BlockSpec((tq, 256), lambda b, i: (qb0 + b * nq + i, 0)),
            pl.BlockSpec((t, 128), lambda b, i: (sb0 + b, 0)),
            pl.BlockSpec((t, 128), lambda b, i: (sb0 + b, 0)),
            _cache_spec(ck, l), _cache_spec(cv, l),
        ],
        out_specs=pl.BlockSpec((tq, 256), lambda b, i: (b * nq + i, 0)),
        out_shape=jax.ShapeDtypeStruct((n_seq * t, 256), F32),
        compiler_params=_cparams(("parallel", "arbitrary")),
        name="attn_full",
    )(q, k, v, ck, cv)


def _gelu_tanh(x):
    return 0.5 * x * (1.0 + jnp.tanh(0.7978845608028654 * (x + 0.044715 * (x * x * x))))


def _lru_kernel(lx_ref, lg_ref, w_ref, b_ref, lam_ref, h0_ref, y_ref, st_ref,
                hf_ref, hb_ref, af_ref, bf_ref, ab_ref, bb_ref, *, t, chunk):
    nc = t // chunk
    ng = chunk // 8
    sp = _softplus(-lam_ref[...])
    row8 = _row_iota((8, BRANCH_W))

    def gates(xc, d, a_ref, b2_ref):
        pre = _bdot(xc, w_ref[:, 512 * d:512 * (d + 1)]) + b_ref[:, 512 * d:512 * (d + 1)]
        r = _sigmoid(pre[:, 0:256])
        ig = _sigmoid(pre[:, 256:512])
        a = jnp.exp((-LRU_C) * r * sp[d:d + 1, :])
        a_ref[...] = a
        b2_ref[...] = jnp.sqrt(1.0 - a * a) * (ig * xc)

    def local_scan(a8, b8, reverse):
        for s in (1, 2, 4):
            sh = (8 - s) if reverse else s
            m = (row8 < 8 - s) if reverse else (row8 >= s)
            a_sh = pltpu.roll(a8, sh, 0)
            b_sh = pltpu.roll(b8, sh, 0)
            b8 = jnp.where(m, a8 * b_sh + b8, b8)
            a8 = jnp.where(m, a8 * a_sh, a8)
        return a8, b8

    def chunk_body(cc, carry):
        rf = pl.multiple_of(cc * chunk, chunk)
        rb = pl.multiple_of((nc - 1 - cc) * chunk, chunk)
        gates(lx_ref[pl.ds(rf, chunk), :], 0, af_ref, bf_ref)
        gates(lx_ref[pl.ds(rb, chunk), :], 1, ab_ref, bb_ref)

        def grp(gg, c2):
            hf, hb = c2
            gf = pl.multiple_of(gg * 8, 8)
            gb = pl.multiple_of((ng - 1 - gg) * 8, 8)
            a8, b8 = local_scan(af_ref[pl.ds(gf, 8), :], bf_ref[pl.ds(gf, 8), :], False)
            h8 = b8 + a8 * hf
            hf_ref[pl.ds(rf + gf, 8), :] = h8
            a8, b8 = local_scan(ab_ref[pl.ds(gb, 8), :], bb_ref[pl.ds(gb, 8), :], True)
            g8 = b8 + a8 * hb
            hb_ref[pl.ds(rb + gb, 8), :] = g8
            return (jnp.broadcast_to(h8[7:8, :], (8, BRANCH_W)), jnp.broadcast_to(g8[0:1, :], (8, BRANCH_W)))

        return lax.fori_loop(0, ng, grp, carry, unroll=4)

    init = (jnp.broadcast_to(h0_ref[0, 0:1, :], (8, BRANCH_W)), jnp.broadcast_to(h0_ref[0, 1:2, :], (8, BRANCH_W)))
    hf, hb = lax.fori_loop(0, nc, chunk_body, init)
    st_ref[0, 0:1, :] = hf[0:1, :]
    st_ref[0, 1:2, :] = hb[0:1, :]

    def combine(c, _):
        rows = pl.ds(pl.multiple_of(c * chunk, chunk), chunk)
        y_ref[rows, :] = (hf_ref[rows, :] + hb_ref[rows, :]) * _gelu_tanh(lg_ref[rows, :])
        return 0

    lax.fori_loop(0, nc, combine, 0)


def _lru_call(lx, lg, w, b, lam, h0, *, l, l_state, n_seq, t, row0):
    sb0 = row0 // t
    chunk = min(t, 256)
    seq = lambda: pl.BlockSpec((t, BRANCH_W), lambda s: (sb0 + s, 0))
    return pl.pallas_call(
        functools.partial(_lru_kernel, t=t, chunk=chunk),
        grid=(n_seq,),
        in_specs=[seq(), seq(), _layer_spec(w, l), _layer_spec(b, l), _layer_spec(lam, l),
                  pl.BlockSpec((1, None, 2, BRANCH_W), lambda s: (s, l_state, 0, 0))],
        out_specs=[pl.BlockSpec((t, BRANCH_W), lambda s: (s, 0)),
                   pl.BlockSpec((1, 2, BRANCH_W), lambda s: (s, 0, 0))],
        out_shape=[jax.ShapeDtypeStruct((n_seq * t, BRANCH_W), F32),
                   jax.ShapeDtypeStruct((n_seq, 2, BRANCH_W), F32)],
        scratch_shapes=[pltpu.VMEM((t, BRANCH_W), F32), pltpu.VMEM((t, BRANCH_W), F32)]
                       + [pltpu.VMEM((chunk, BRANCH_W), F32)] * 4,
        compiler_params=_cparams(("parallel",)),
        name="rglru",
    )(lx, lg, w, b, lam, h0)


def _rows_dot_exact(x, w01):
    r = x.shape[0]
    out = jnp.dot(jnp.concatenate(_split3(x), axis=0), w01, preferred_element_type=F32)
    return out[0:r] + out[r:2 * r] + out[2 * r:3 * r]


def _dot01_exact(m01, y):
    return sum(jnp.dot(m01, part, preferred_element_type=F32) for part in _split3(y))


def _gdn_kernel(q_ref, k_ref, v_ref, ggb_ref, s0_ref, o_ref, sT_ref, s_ref, *, d, reverse, n_par, n_chunk, n_tile):
    ti = pl.program_id(1)
    w4 = GDN_H * GDN_DK
    n_all = n_par * n_chunk
    tt = n_all * CHUNK
    blockmask = (_row_iota((w4, w4)) // CHUNK) == (_lane_iota((w4, w4)) // CHUNK)

    def expand_rows(y):
        yt = jnp.concatenate([y] * GDN_H, axis=0)
        zero = jnp.zeros((), y.dtype)
        parts = [jnp.where(blockmask, yt[:, w4 * u:w4 * (u + 1)], zero) for u in range(y.shape[1] // w4)]
        return parts[0] if len(parts) == 1 else jnp.concatenate(parts, axis=1)

    def heads_dot3(lhs, y):
        r = lhs.shape[0]
        lh, ll = _split(lhs)
        yh, yl = _split(y)
        out = jnp.dot(jnp.concatenate([lh, ll], axis=0), expand_rows(yh), preferred_element_type=F32)
        return out[0:r] + out[r:2 * r] + jnp.dot(lh, expand_rows(yl), preferred_element_type=F32)

    def heads_dot2(lhs, y):
        r = lhs.shape[0]
        lh, ll = _split(lhs)
        out = jnp.dot(jnp.concatenate([lh, ll], axis=0), expand_rows(y.astype(BF16)), preferred_element_type=F32)
        return out[0:r] + out[r:2 * r]

    def heads_dot1(lhs, y):
        return jnp.dot(lhs.astype(BF16), expand_rows(y.astype(BF16)), preferred_element_type=F32)

    @pl.when(ti == 0)
    def _():
        for p in range(n_par):
            s_ref[p] = expand_rows(s0_ref[p, 0])

    ri = _row_iota((CHUNK, w4))
    cj = _lane_iota((CHUNK, w4)) % CHUNK
    incl = (cj >= ri) if reverse else (cj <= ri)
    strict = (cj > ri) if reverse else (cj < ri)
    eye_sbs = (cj == ri)
    rep = eye_sbs.astype(BF16)

    er = _row_iota((128, 2 * w4))
    el = _lane_iota((128, 2 * w4))
    e_gb = (er == jnp.where(el < w4, d * GDN_H, 2 * GDN_H + d * GDN_H) + (el % w4) // CHUNK).astype(BF16)
    gbe = _rows_dot_exact(ggb_ref[...], e_gb)
    g_all = gbe[:, 0:w4]
    beta_all = gbe[:, w4:2 * w4]
    tr = _row_iota((tt, tt))
    tc = _lane_iota((tt, tt))
    same_chunk = (tr // CHUNK) == (tc // CHUNK)
    tri_bd = (same_chunk & ((tc >= tr) if reverse else (tc <= tr))).astype(BF16)
    ones_bd = same_chunk.astype(BF16)
    gc_all = _dot01_exact(tri_bd, g_all)
    diag_sel = (_row_iota((tt, w4)) % CHUNK) == (_lane_iota((tt, w4)) % CHUNK)
    gct_all = _dot01_exact(ones_bd, jnp.where(diag_sel, gc_all, 0.0))

    cs = range(n_all)
    rows = [slice(c * CHUNK, (c + 1) * CHUNK) for c in cs]
    q = [q_ref[r, :] for r in rows]
    k = [k_ref[r, :] for r in rows]
    v = [v_ref[r, :] for r in rows]
    beta = [beta_all[r, :] for r in rows]
    gc = [gc_all[r, :] for r in rows]
    decay = [jnp.where(incl, jnp.exp(jnp.where(incl, gc[c] - gct_all[rows[c], :], 0.0)), 0.0) for c in cs]
    kb = [k[c].astype(BF16) for c in cs]
    w_k = [jnp.where(blockmask,
                     lax.dot_general(kb[c], rep, (((0,), (0,)), ((), ())), preferred_element_type=F32),
                     0.0).astype(BF16) for c in cs]
    kq = [jnp.dot(jnp.concatenate([kb[c], q[c].astype(BF16)], axis=0), w_k[c], preferred_element_type=F32)
          for c in cs]
    qk = [(kq[c][CHUNK:2 * CHUNK] * decay[c]).astype(BF16) for c in cs]
    a = [jnp.where(strict, beta[c] * kq[c][0:CHUNK] * decay[c], 0.0) for c in cs]
    t_inv = [jnp.where(eye_sbs, 1.0, 0.0) - a[c] for c in cs]
    pw = [heads_dot3(a[c], a[c]) for c in cs]
    for stage in range(4):
        mm = heads_dot3 if stage < N_PRECISE else heads_dot1
        both = [mm(jnp.concatenate([t_inv[c], pw[c]], axis=0), pw[c]) for c in cs]
        t_inv = [t_inv[c] + both[c][0:CHUNK] for c in cs]
        pw = [both[c][CHUNK:2 * CHUNK] for c in cs]
    t_inv = [t_inv[c] + heads_dot1(t_inv[c], pw[c]) for c in cs]
    egc = [jnp.exp(gc[c]) for c in cs]
    sol = [heads_dot2(t_inv[c], jnp.concatenate([v[c] * beta[c], k[c] * (beta[c] * egc[c])], axis=1)) for c in cs]
    g_last = [gc[c][0:1, :] if reverse else gc[c][CHUNK - 1:CHUNK, :] for c in cs]
    wq = [jnp.concatenate([sol[c][:, w4:2 * w4], q[c] * egc[c]], axis=0).astype(BF16) for c in cs]
    kdec = [(k[c] * jnp.exp(g_last[c] - gc[c])).astype(BF16) for c in cs]

    for cc in range(n_chunk):
        for p in range(n_par):
            c = p * n_chunk + ((n_chunk - 1 - cc) if reverse else cc)
            s = s_ref[p]
            ws_qs = jnp.dot(wq[c], s.astype(BF16), preferred_element_type=F32)
            v_new = sol[c][:, 0:w4] - ws_qs[0:CHUNK]
            vb = v_new.astype(BF16)
            o_ref[rows[c], :] = (
                ws_qs[CHUNK:2 * CHUNK] + jnp.dot(qk[c], expand_rows(vb), preferred_element_type=F32))
            upd = lax.dot_general(kdec[c], vb, (((0,), (0,)), ((), ())), preferred_element_type=F32)
            s_ref[p] = s * jnp.exp(g_last[c]) + jnp.where(blockmask, upd, 0.0)

    @pl.when(ti == n_tile - 1)
    def _():
        for p in range(n_par):
            s = s_ref[p]
            sT_ref[p, 0] = s[0:64] + s[64:128] + s[128:192] + s[192:256]


def _gdn_call(q, k, v, ggb, s0, *, d, l_state, n_seq, t, row0, tt, n_par):
    reverse = d == 1
    n_tile = t // tt
    assert n_par == 1 or n_tile == 1
    b0 = row0 // (tt * n_par)
    tidx = (lambda i: n_tile - 1 - i) if reverse else (lambda i: i)
    blk = lambda w: pl.BlockSpec((tt * n_par, w), lambda s, i: (b0 + s * n_tile + tidx(i), 0))
    return pl.pallas_call(
        functools.partial(_gdn_kernel, d=d, reverse=reverse, n_par=n_par, n_chunk=tt // CHUNK, n_tile=n_tile),
        grid=(n_seq // n_par, n_tile),
        in_specs=[blk(256), blk(256), blk(256), blk(128),
                  pl.BlockSpec((n_par, None, 1, CHUNK, 256), lambda s, i: (s, l_state, d, 0, 0))],
        out_specs=[pl.BlockSpec((tt * n_par, 256), lambda s, i: (s * n_tile + tidx(i), 0)),
                   pl.BlockSpec((n_par, 1, CHUNK, 256), lambda s, i: (s, 0, 0, 0))],
        out_shape=[jax.ShapeDtypeStruct((n_seq * t, 256), F32),
                   jax.ShapeDtypeStruct((n_seq, 1, CHUNK, 256), F32)],
        scratch_shapes=[pltpu.VMEM((n_par, 256, 256), F32)],
        compiler_params=_cparams(("parallel", "arbitrary")),
        name="gdn_bwd" if reverse else "gdn_fwd",
    )(q, k, v, ggb, s0)


def _mix_mlp_kernel(modrow_ref, xc_ref, xd_ref, mod_ref, g1_ref,
                    oac_ref, oad_ref, obc_ref, obd_ref, ofc_ref, ofd_ref, orc_ref, ord_ref, odc_ref, odd_ref,
                    gz_ref, gng_ref, ones_ref, wm_ref, bm_ref, wb_ref, wo_ref, g2_ref, w1_ref, w2_ref,
                    yc_ref, yd_ref, *, n_ctx_tiles):
    is_ctx = pl.program_id(0) < n_ctx_tiles
    pick = lambda c_ref, d_ref: jnp.where(is_ctx, c_ref[...], d_ref[...])
    x = pick(xc_ref, xd_ref)
    h = _modulate(x, g1_ref[...], mod_ref[0, 0:1, :], mod_ref[0, 1:2, :]).astype(BF16)
    oc = pick(ofc_ref, ofd_ref) + pick(orc_ref, ord_ref)
    ms = _group_sum(oc * oc, ones_ref[...]) * (1.0 / GDN_DK)
    oc = (oc * lax.rsqrt(ms + EPS) * gng_ref[...]) * _silu(gz_ref[...])
    branches = (pick(oac_ref, oad_ref), pick(obc_ref, obd_ref), oc, pick(odc_ref, odd_ref))
    acc = None
    for m in range(N_BRANCH):
        cols = slice(D_MODEL * m, D_MODEL * (m + 1))
        gate = _sigmoid(jnp.dot(h, wm_ref[:, cols], preferred_element_type=F32) + bm_ref[:, cols])
        term = gate * jnp.dot(branches[m].astype(BF16), wb_ref[m], preferred_element_type=F32)
        acc = term if acc is None else acc + term
    x = x + mod_ref[0, 2:3, :] * jnp.dot(acc.astype(BF16), wo_ref[...], preferred_element_type=F32)

    h = _modulate(x, g2_ref[...], mod_ref[0, 3:4, :], mod_ref[0, 4:5, :]).astype(BF16)
    acc = None
    for j in range(D_FF // D_MODEL):
        cols = slice(D_MODEL * j, D_MODEL * (j + 1))
        a = jnp.maximum(jnp.dot(h, w1_ref[:, cols], preferred_element_type=F32), 0.0)
        term = jnp.dot((a * a).astype(BF16), w2_ref[cols, :], preferred_element_type=F32)
        acc = term if acc is None else acc + term
    y = x + mod_ref[0, 5:6, :] * acc

    @pl.when(is_ctx)
    def _():
        yc_ref[...] = y

    @pl.when(jnp.logical_not(is_ctx))
    def _():
        yd_ref[...] = y


def _mix_mlp_call(modrow, x_pair, mod_all, g1, pairs, gz, gng, ones_bd, wm, bm, wb, wo, g2, w1, w2, *, l, tm):
    x_c, x_d = x_pair
    n, d = x_c.shape[0] + x_d.shape[0], x_c.shape[1]
    nct = x_c.shape[0] // tm
    row_tile = lambda w: pl.BlockSpec((tm, w), lambda i, *_: (i, 0))
    pair = lambda w: [pl.BlockSpec((tm, w), lambda i, *_: (jnp.minimum(i, nct - 1), 0)),
                      pl.BlockSpec((tm, w), lambda i, *_: (jnp.maximum(i - nct, 0), 0))]
    pair_specs = [sp for _ in pairs for sp in pair(256)]
    resident = lambda arr: _layer_spec(arr, l, single_buffer=True)
    grid_spec = pltpu.PrefetchScalarGridSpec(
        num_scalar_prefetch=1, grid=(n // tm,),
        in_specs=pair(d) + [pl.BlockSpec((None, 1, 6, d), lambda i, modrow: (l, modrow[i], 0, 0)),
                            _layer_spec(g1, l)] + pair_specs
                 + [row_tile(256), _layer_spec(gng, l), pl.BlockSpec((256, 256), lambda i, *_: (0, 0)),
                    resident(wm), _layer_spec(bm, l), resident(wb), resident(wo),
                    _layer_spec(g2, l), resident(w1), resident(w2)],
        out_specs=pair(d))
    flat = [a for pr in pairs for a in pr]
    return pl.pallas_call(
        functools.partial(_mix_mlp_kernel, n_ctx_tiles=nct), grid_spec=grid_spec,
        out_shape=[jax.ShapeDtypeStruct(x_c.shape, F32), jax.ShapeDtypeStruct(x_d.shape, F32)],
        compiler_params=_cparams(("arbitrary",)),
        name="merge_mlp",
    )(modrow, x_c, x_d, mod_all, g1, *flat, gz, gng, ones_bd, wm, bm, wb, wo, g2, w1, w2)


def _rope_tables(t, tm):
    n_freq = HEAD_DIM // 4
    inv = np.float32(ROPE_BASE) ** (-np.arange(n_freq, dtype=np.float32) / np.float32(n_freq))
    pos = np.arange(t)
    row = (pos // GRID_W).astype(np.float32)[:, None]
    col = (pos % GRID_W).astype(np.float32)[:, None]
    ar = (row * inv).astype(np.float32)
    ac = (col * inv).astype(np.float32)
    cos64 = np.concatenate([np.cos(ar), np.cos(ar), np.cos(ac), np.cos(ac)], axis=1)
    sin64 = np.concatenate([-np.sin(ar), np.sin(ar), -np.sin(ac), np.sin(ac)], axis=1)
    cos = np.concatenate([np.ones((tm, 128), np.float32), np.tile(cos64, (1, 2)).astype(np.float32)], axis=0)
    sin = np.concatenate([np.zeros((tm, 128), np.float32), np.tile(sin64, (1, 2)).astype(np.float32)], axis=0)
    return jnp.asarray(cos), jnp.asarray(sin)


def _tile_meta(n_ctx, n_seq_dec, t_ctx, t_dec, tm):
    nct = n_ctx // tm
    per = t_dec // tm
    ndt = n_seq_dec * per
    idx = np.arange(nct + ndt)
    dec = idx >= nct
    di = np.maximum(idx - nct, 0)
    modrow = np.where(dec, 1 + di // per, 0)
    ropeblk = np.where(dec, 1 + di % per, 0)
    tiles_ctx = max(t_ctx // tm, 1)
    isstart = np.where(dec, di % per == 0, idx % tiles_ctx == 0)
    isend = np.where(dec, di % per == per - 1, idx % tiles_ctx == tiles_ctx - 1)
    as_i32 = lambda a: jnp.asarray(a.astype(np.int32))
    return as_i32(modrow), as_i32(ropeblk), as_i32(isstart), as_i32(isend)


def _block_diag_gates(w):
    depth, two, n, c, _ = w.shape
    eye = jnp.eye(n, dtype=w.dtype)
    return jnp.einsum("ldnij,nm->ldnimj", w, eye).reshape(depth, two, n * c, n * c)


def kernel(x_prompt, x_sample, c, cache_a_k, cache_a_v, cache_d_k, cache_d_v, state_lru, state_gdn, c_ctx, mod_w, mod_b, norm1_g, norm2_g, w_in, a_qn_g, a_kn_g, a_sink, lru_conv_w, lru_conv_b, lru_wr, lru_br, lru_wi, lru_bi, lru_lam, gdn_conv_w, gdn_a_log, gdn_dt_bias, gdn_norm_g, d_qn_g, d_kn_g, w_branch, w_merge, b_merge, w_out, mlp_w1, mlp_w2):
    batch, seq, d = x_prompt.shape
    dec_batch, dec_seq, _ = x_sample.shape
    depth = mod_w.shape[0]
    past = cache_a_k.shape[2]
    n_ctx = batch * seq
    n_dec = dec_batch * dec_seq
    tm = 256
    gdn_par = 2 if batch % 2 == 0 else 1
    gdn_tt = 512 if dec_seq % 512 == 0 else 256
    assert d == D_MODEL and seq % tm == 0 and dec_seq % tm == 0 and tm % seq == 0
    assert dec_batch + 1 <= 8 and n_ctx % dec_seq == 0

    cond8 = jnp.zeros((8, d), F32).at[0].set(c_ctx).at[1:1 + dec_batch].set(c)
    mod_all = _mod_call(cond8, mod_w, mod_b).reshape(depth, 8, 6, d)

    meta = _tile_meta(n_ctx, dec_batch, seq, dec_seq, tm)
    cos_t, sin_t = _rope_tables(dec_seq, tm)
    lane = np.arange(256)
    ones_bd = jnp.asarray((lane[:, None] // HEAD_DIM == lane[None, :] // HEAD_DIM).astype(np.float32)).astype(BF16)

    w_packed = jnp.concatenate([w_in[:, :, :2064], jnp.zeros((depth, d, GDN_PAD), F32), w_in[:, :, 2064:]],
                               axis=2).astype(BF16)
    pad128 = lambda v: jnp.pad(v.reshape(depth, 1, -1), ((0, 0), (0, 0), (0, 128 - v[0].size)))
    vecs = (
        (jnp.tile(a_qn_g, (1, 4)) * Q_SCALE)[:, None, :],
        jnp.tile(a_kn_g, (1, 2))[:, None, :],
        (jnp.tile(d_qn_g, (1, 4)) * Q_SCALE)[:, None, :],
        jnp.tile(d_kn_g, (1, 2))[:, None, :],
        lru_conv_w, lru_conv_b[:, None, :], gdn_conv_w,
        pad128(gdn_a_log), pad128(gdn_dt_bias),
    )
    g1 = norm1_g[:, None, :]
    g2 = norm2_g[:, None, :]
    wr_bd = _block_diag_gates(lru_wr)
    wi_bd = _block_diag_gates(lru_wi)
    w_lru = jnp.concatenate([wr_bd[:, 0], wi_bd[:, 0], wr_bd[:, 1], wi_bd[:, 1]], axis=-1).astype(BF16)
    b_lru = jnp.concatenate([lru_br[:, 0], lru_bi[:, 0], lru_br[:, 1], lru_bi[:, 1]], axis=-1)[:, None, :]
    gng = jnp.tile(gdn_norm_g, (1, 4))[:, None, :]
    wm = w_merge.astype(BF16)
    bm = b_merge[:, None, :]
    wb = w_branch.astype(BF16)
    wo = w_out.astype(BF16)
    w1 = mlp_w1.astype(BF16)
    w2 = mlp_w2.astype(BF16)
    sink = a_sink.reshape(-1)
    caches = [t.reshape(dec_batch, depth, past, 2 * HEAD_DIM) for t in (cache_a_k, cache_a_v, cache_d_k, cache_d_v)]
    s0_dec = state_gdn.transpose(0, 1, 2, 4, 3, 5).reshape(dec_batch, depth, 2, CHUNK, 256)
    zeros_lru = jnp.zeros((batch, 1, 2, BRANCH_W), F32)
    zeros_gdn = jnp.zeros((batch, 1, 2, CHUNK, 256), F32)

    x = (x_prompt.reshape(n_ctx, d), x_sample.reshape(n_dec, d))

    kv_ctx, lru_states, gdn_states = None, [], []
    for l in range(depth):
        (qa, ka, va, lx, lg, gq, gk, gv, gz, ggb, qd, kd, vd), kv_ctx = _inproj_call(
            x, meta, mod_all, g1, w_packed, cos_t, sin_t, vecs, ones_bd, kv_ctx, l=l, depth=depth, tm=tm)

        oa_c, od_c = _attn_ctx_call(sink, qa, ka, va, qd, kd, vd, l=l, n_seq=batch, t=seq)
        oa_d = _attn_band_call(sink, qa, ka, va, caches[0], caches[1], l=l, n_seq=dec_batch, t=dec_seq,
                               row0=n_ctx, tq=512)
        od_d = _attn_full_call(qd, kd, vd, caches[2], caches[3], l=l, n_seq=dec_batch, t=dec_seq,
                               row0=n_ctx, tq=256)

        ob_c, st_c = _lru_call(lx, lg, w_lru, b_lru, lru_lam, zeros_lru, l=l, l_state=0, n_seq=batch, t=seq,
                               row0=0)
        ob_d, _ = _lru_call(lx, lg, w_lru, b_lru, lru_lam, state_lru, l=l, l_state=l, n_seq=dec_batch,
                            t=dec_seq, row0=n_ctx)

        oc_pairs, sts = [], []
        for dd in range(2):
            o_c, s_c = _gdn_call(gq, gk, gv, ggb, zeros_gdn, d=dd, l_state=0, n_seq=batch, t=seq, row0=0,
                                 tt=seq, n_par=gdn_par)
            o_d, _ = _gdn_call(gq, gk, gv, ggb, s0_dec, d=dd, l_state=l, n_seq=dec_batch, t=dec_seq,
                               row0=n_ctx, tt=gdn_tt, n_par=1)
            oc_pairs.append((o_c, o_d))
            sts.append(s_c)

        x = _mix_mlp_call(meta[0], x, mod_all, g1,
                          [(oa_c, oa_d), (ob_c, ob_d), oc_pairs[0], oc_pairs[1], (od_c, od_d)],
                          gz, gng, ones_bd, wm, bm, wb, wo, g2, w1, w2, l=l, tm=tm)

        lru_states.append(st_c)
        gdn_states.append(jnp.concatenate(sts, axis=1))

    y_prompt = x[0].reshape(batch, seq, d)
    y_sample = x[1].reshape(dec_batch, dec_seq, d)
    new_kv = [t.reshape(batch, depth, seq, 2, HEAD_DIM) for t in kv_ctx]
    new_lru = jnp.stack(lru_states, axis=1)
    new_gdn = (jnp.stack(gdn_states, axis=1).reshape(batch, depth, 2, GDN_DK, GDN_H, GDN_DK)
               .transpose(0, 1, 2, 4, 3, 5))
    return (y_prompt, y_sample, new_kv[0], new_kv[1], new_kv[2], new_kv[3], new_lru, new_gdn)
```

```python
import functools

import numpy as np
import jax
import jax.numpy as jnp
from jax import lax
from jax.experimental import pallas as pl
from jax.experimental.pallas import tpu as pltpu

F32 = jnp.float32
BF16 = jnp.bfloat16
HIGHEST = lax.Precision.HIGHEST

D_MODEL = 1024
HEAD_DIM = 64
BRANCH_W = 256
N_BRANCH = 4
GRID_W = 64
WINDOW = 128
LRU_C = 8.0
CONV_W = 4
CONV_LEFT = 2
GDN_H = 4
GDN_DK = 64
CHUNK = 64
D_FF = 4 * D_MODEL
ROPE_BASE = 10000.0
EPS = 1e-6
NEG = -1e30
LOG2E = 1.4426950408889634
Q_SCALE = HEAD_DIM ** -0.5 * LOG2E
GDN_PAD = 112
IN_COLS_PACKED = 2688
HALO = 8
N_PRECISE = 4

V7X_VMEM_LIMIT = 56 * 1024 * 1024


def _cparams(sem, vmem=V7X_VMEM_LIMIT):
    return pltpu.CompilerParams(dimension_semantics=sem, vmem_limit_bytes=vmem)


def _layer_spec(arr, l, single_buffer=False):
    nd = arr.ndim - 1
    mode = dict(pipeline_mode=pl.Buffered(1)) if single_buffer else {}
    return pl.BlockSpec((None,) + arr.shape[1:], lambda *_: (l,) + (0,) * nd, **mode)


def _bdot(a, b):
    return jnp.dot(a.astype(BF16), b.astype(BF16), preferred_element_type=F32)


def _split(x):
    hi = x.astype(BF16)
    lo = (x - hi.astype(F32)).astype(BF16)
    return hi, lo


def _split3(x):
    hi = x.astype(BF16)
    r = x - hi.astype(F32)
    mid = r.astype(BF16)
    lo = (r - mid.astype(F32)).astype(BF16)
    return hi, mid, lo


def _group_sum(x, ones_bd):
    return jnp.dot(x.astype(BF16), ones_bd, preferred_element_type=F32)


def _sigmoid(x):
    return 0.5 * jnp.tanh(0.5 * x) + 0.5


def _silu(x):
    return x * _sigmoid(x)


def _softplus(x):
    return jnp.maximum(x, 0.0) + jnp.log1p(jnp.exp(-jnp.abs(x)))


def _modulate(x, g, shift, scale):
    ms = jnp.mean(x * x, axis=-1, keepdims=True)
    return (x * lax.rsqrt(ms + EPS)) * (g * (1.0 + scale)) + shift


def _lane_iota(shape):
    return lax.broadcasted_iota(jnp.int32, shape, len(shape) - 1)


def _row_iota(shape):
    return lax.broadcasted_iota(jnp.int32, shape, len(shape) - 2)


def _mod_kernel(cond_ref, w_ref, b_ref, o_ref):
    hi, lo = _split(_silu(cond_ref[...]))
    w = w_ref[0].astype(BF16)
    o_ref[0] = (jnp.dot(hi, w, preferred_element_type=F32) + jnp.dot(lo, w, preferred_element_type=F32)
                + b_ref[0])


def _mod_call(cond8, mod_w, mod_b):
    depth, d, n = mod_w.shape
    tn = 1536
    return pl.pallas_call(
        _mod_kernel,
        grid=(depth, n // tn),
        in_specs=[
            pl.BlockSpec((8, d), lambda l, j: (0, 0)),
            pl.BlockSpec((1, d, tn), lambda l, j: (l, 0, j)),
            pl.BlockSpec((1, 1, tn), lambda l, j: (l, 0, j)),
        ],
        out_specs=pl.BlockSpec((1, 8, tn), lambda l, j: (l, 0, j)),
        out_shape=jax.ShapeDtypeStruct((depth, 8, n), F32),
        compiler_params=_cparams(("parallel", "parallel")),
        name="mod_vectors",
    )(cond8, mod_w, mod_b.reshape(depth, 1, n))


def _rope(x, cos, sin):
    outs = []
    for j in range(x.shape[1] // 128):
        xb = x[:, 128 * j:128 * (j + 1)]
        lane = _lane_iota(xb.shape)
        sw = jnp.where((lane & 16) == 0, pltpu.roll(xb, 112, 1), pltpu.roll(xb, 16, 1))
        outs.append(xb * cos + sw * sin)
    return outs[0] if len(outs) == 1 else jnp.concatenate(outs, axis=1)


def _centred_conv(g, w, tm):
    rows = g.shape[0]
    acc = None
    for j in range(CONV_W):
        sh = (CONV_LEFT - j) % rows
        gj = g if sh == 0 else pltpu.roll(g, sh, 0)
        term = gj[HALO:HALO + tm] * w[j:j + 1, :]
        acc = term if acc is None else acc + term
    return acc


def _inproj_kernel(modrow_ref, ropeblk_ref, isstart_ref, isend_ref,
                   xpc_ref, xc_ref, xnc_ref, xpd_ref, xd_ref, xnd_ref, mod_ref, g1_ref, w_ref, cos_ref, sin_ref,
                   aqg_ref, akg_ref, dqg_ref, dkg_ref, lcw_ref, lcb_ref, gcw_ref,
                   alog_ref, dtb_ref, ones_ref, *rest, tm, n_ctx_tiles, n_alias):
    (qa_ref, ka_ref, va_ref, lx_ref, lg_ref, gq_ref, gk_ref, gv_ref, gz_ref, ggb_ref,
     qd_ref, kd_ref, vd_ref, kac_ref, vac_ref, kdc_ref, vdc_ref) = rest[n_alias:]
    i = pl.program_id(0)
    is_ctx = i < n_ctx_tiles
    pick = lambda c_ref, d_ref: jnp.where(is_ctx, c_ref[...], d_ref[...])
    xfull = jnp.concatenate([pick(xpc_ref, xpd_ref), pick(xc_ref, xd_ref), pick(xnc_ref, xnd_ref)],
                            axis=0)
    h = _modulate(xfull, g1_ref[...], mod_ref[0, 0:1, :], mod_ref[0, 1:2, :])
    p = jnp.dot(h.astype(BF16), w_ref[...], preferred_element_type=F32)
    hb, hm = slice(0, tm + 2 * HALO), slice(HALO, HALO + tm)
    proj = lambda rows, c0, c1: p[rows, c0:c1]

    ones_bd = ones_ref[...]
    cos = cos_ref[...]
    sin = sin_ref[...]

    pa = proj(hm, 0, 512)
    pd = proj(hm, 2176, 2688)
    va_ref[...] = pa[:, 384:512]
    vd_ref[...] = pd[:, 384:512]

    row = _row_iota((tm + 2 * HALO, 1))
    keep = jnp.logical_and(jnp.logical_or(row >= HALO, isstart_ref[i] == 0),
                           jnp.logical_or(row < HALO + tm, isend_ref[i] == 0))

    lxg = jnp.where(keep, proj(hb, 512, 768), 0.0)
    lx_ref[...] = _centred_conv(lxg, lcw_ref[...], tm) + lcb_ref[...]
    lg_ref[...] = proj(hm, 768, 1024)

    qkv = jnp.where(keep, proj(hb, 1024, 1792), 0.0)
    qkv = _silu(_centred_conv(qkv, gcw_ref[...], tm))
    gq = qkv[:, 0:256]
    gk = qkv[:, 256:512]
    gv_ref[...] = qkv[:, 512:768]
    pz = proj(hm, 1792, 2176)
    gz_ref[...] = pz[:, 0:256]
    ab = pz[:, 256:384]
    g = -jnp.exp(alog_ref[...]) * _softplus(ab + dtb_ref[...])
    lane = _lane_iota(ab.shape)
    ggb_ref[...] = jnp.where(lane < 2 * GDN_H, g, _sigmoid(ab))

    qa_raw, ka_raw, qd_raw, kd_raw = pa[:, 0:256], pa[:, 256:384], pd[:, 0:256], pd[:, 256:384]
    ss256 = _group_sum(jnp.concatenate([qa_raw * qa_raw, qd_raw * qd_raw, gq * gq, gk * gk], axis=0), ones_bd)
    ss128 = _group_sum(jnp.concatenate([ka_raw * ka_raw, kd_raw * kd_raw], axis=0), ones_bd[:128, :128])
    rms = lambda x, ssq, gain: x * lax.rsqrt(ssq * (1.0 / HEAD_DIM) + EPS) * gain

    qa_ref[...] = _rope(rms(qa_raw, ss256[0:tm], aqg_ref[...]), cos, sin)
    qd_ref[...] = _rope(rms(qd_raw, ss256[tm:2 * tm], dqg_ref[...]), cos, sin)
    ka = _rope(rms(ka_raw, ss128[0:tm], akg_ref[...]), cos, sin)
    kd = _rope(rms(kd_raw, ss128[tm:2 * tm], dkg_ref[...]), cos, sin)
    ka_ref[...] = ka
    kd_ref[...] = kd
    gq_ref[...] = gq * lax.rsqrt(ss256[2 * tm:3 * tm] + EPS) * (GDN_DK ** -0.5)
    gk_ref[...] = gk * lax.rsqrt(ss256[3 * tm:4 * tm] + EPS)

    @pl.when(is_ctx)
    def _():
        kac_ref[...] = ka
        vac_ref[...] = pa[:, 384:512]
        kdc_ref[...] = kd
        vdc_ref[...] = pd[:, 384:512]


def _x_pair_specs(x_pair, tm):
    x_c, x_d = x_pair
    d = x_c.shape[1]
    nct = x_c.shape[0] // tm
    hb = tm // HALO
    specs = []
    for arr, first in ((x_c, 0), (x_d, nct)):
        nblk = arr.shape[0] // tm
        last_hb = arr.shape[0] // HALO - 1
        tile = lambda i, first=first, nblk=nblk: jnp.clip(i - first, 0, nblk - 1)
        specs += [
            pl.BlockSpec((HALO, d), lambda i, *_, t=tile, m=last_hb: (jnp.clip(t(i) * hb - 1, 0, m), 0)),
            pl.BlockSpec((tm, d), lambda i, *_, t=tile: (t(i), 0)),
            pl.BlockSpec((HALO, d), lambda i, *_, t=tile, m=last_hb: (jnp.clip((t(i) + 1) * hb, 0, m), 0)),
        ]
    return specs


def _inproj_call(x_pair, meta, mod_all, g1, w_packed, cos_t, sin_t, vecs, ones_bd, kv_prev, *, l, depth, tm):
    x_c, x_d = x_pair
    n, d = x_c.shape[0] + x_d.shape[0], x_c.shape[1]
    nt = n // tm
    nct = x_c.shape[0] // tm
    row_tile = lambda w: pl.BlockSpec((tm, w), lambda i, *_: (i, 0))
    kv_spec = pl.BlockSpec((None, None, tm, 128), lambda i, *_: (jnp.minimum(i, nct - 1), l, 0, 0))
    kv_shape = jax.ShapeDtypeStruct((nct, depth, tm, 128), F32)
    alias_in = [] if kv_prev is None else list(kv_prev)
    in_specs = _x_pair_specs(x_pair, tm) + [
        pl.BlockSpec((None, 1, 6, d), lambda i, modrow, *_: (l, modrow[i], 0, 0)),
        _layer_spec(g1, l),
        _layer_spec(w_packed, l),
        pl.BlockSpec((tm, 128), lambda i, modrow, ropeblk, *_: (ropeblk[i], 0)),
        pl.BlockSpec((tm, 128), lambda i, modrow, ropeblk, *_: (ropeblk[i], 0)),
    ] + [_layer_spec(v, l) for v in vecs] + [pl.BlockSpec((256, 256), lambda i, *_: (0, 0))]
    in_specs += [pl.BlockSpec(memory_space=pl.ANY)] * len(alias_in)
    widths = (256, 128, 128, 256, 256, 256, 256, 256, 256, 128, 256, 128, 128)
    grid_spec = pltpu.PrefetchScalarGridSpec(
        num_scalar_prefetch=4,
        grid=(nt,),
        in_specs=in_specs,
        out_specs=[row_tile(w) for w in widths] + [kv_spec] * 4,
    )
    operands = (*meta, x_c, x_c, x_c, x_d, x_d, x_d, mod_all, g1, w_packed, cos_t, sin_t, *vecs, ones_bd)
    outs = pl.pallas_call(
        functools.partial(_inproj_kernel, tm=tm, n_ctx_tiles=nct, n_alias=len(alias_in)),
        grid_spec=grid_spec,
        out_shape=[jax.ShapeDtypeStruct((n, w), F32) for w in widths] + [kv_shape] * 4,
        input_output_aliases={len(operands) + u: len(widths) + u for u in range(len(alias_in))},
        compiler_params=_cparams(("arbitrary",)),
        name="in_projection",
    )(*operands, *alias_in)
    return outs[:len(widths)], outs[len(widths):]


def _head_q(q, j, g):
    qj = q[:, 128 * j:128 * (j + 1)]
    lane = _lane_iota(qj.shape)
    sel = (lane < HEAD_DIM) if g == 0 else (lane >= HEAD_DIM)
    return jnp.where(sel, qj, 0.0).astype(BF16)


def _place_heads(res, j):
    r0 = res[0] if j == 0 else pltpu.roll(res[0], HEAD_DIM, 1)
    r1 = res[1] if j == 1 else pltpu.roll(res[1], HEAD_DIM, 1)
    lane = _lane_iota(r0.shape)
    return jnp.where(lane < HEAD_DIM, r0, r1)


def _attend_many(problems, order="staged"):
    work = []
    for pi, (q, ks, vs, masks, sinks, ks_sw) in enumerate(problems):
        if ks_sw is None:
            ks_sw = [pltpu.roll(k, HEAD_DIM, 1) for k in ks]
        work += [(pi, j, g, q, ks, vs, masks, sinks, ks_sw) for j in range(2) for g in range(2)]
    scores, probs, res = {}, {}, {}

    def score(w):
        pi, j, g, q, ks, vs, masks, sinks, ks_sw = w
        qm = _head_q(q, j, g)
        ss = []
        for k, ksw, mk in zip(ks, ks_sw, masks):
            s = lax.dot_general(qm, k if g == j else ksw, (((1,), (1,)), ((), ())), preferred_element_type=F32)
            ss.append(s if mk is None else jnp.where(mk, s, NEG))
        scores[w[:3]] = ss

    def softmax_num(w):
        pi, j, g, q, ks, vs, masks, sinks, ks_sw = w
        ss = scores[w[:3]]
        m = ss[0].max(axis=-1, keepdims=True)
        for s in ss[1:]:
            m = jnp.maximum(m, s.max(axis=-1, keepdims=True))
        if sinks is not None:
            m = jnp.maximum(m, sinks[2 * j + g])
        es = [jnp.exp2(s - m) for s in ss]
        den = sum(e.sum(axis=-1, keepdims=True) for e in es)
        if sinks is not None:
            den = den + jnp.exp2(sinks[2 * j + g] - m)
        probs[w[:3]] = ([e.astype(BF16) for e in es], den)

    def values(w):
        es, den = probs[w[:3]]
        res[w[:3]] = sum(jnp.dot(e, v, preferred_element_type=F32) for e, v in zip(es, w[5])) / den

    if order == "staged":
        for step in (score, softmax_num, values):
            for w in work:
                step(w)
    else:
        score(work[0])
        for prev, nxt in zip(work, work[1:] + [None]):
            if nxt is not None:
                score(nxt)
            softmax_num(prev)
            values(prev)
    return [jnp.concatenate([_place_heads([res[(pi, j, 0)], res[(pi, j, 1)]], j) for j in range(2)], axis=1)
            for pi in range(len(problems))]


def _attend(q, ks, vs, masks, sinks, ks_sw=None, order="staged"):
    return _attend_many([(q, ks, vs, masks, sinks, ks_sw)], order)[0]


def _bf(x):
    return x.astype(BF16)


def _attn_ctx_kernel(sink_ref, qa_ref, ka_ref, va_ref, qd_ref, kd_ref, vd_ref, oa_ref, od_ref, *, l, t, n_par):
    sinks = [sink_ref[4 * l + u] * LOG2E for u in range(4)]
    problems = []
    for p in range(n_par):
        r = slice(p * t, (p + 1) * t)
        problems.append((qa_ref[r, :], [_bf(ka_ref[r, :])], [_bf(va_ref[r, :])], [None], sinks, None))
        problems.append((qd_ref[r, :], [_bf(kd_ref[r, :])], [_bf(vd_ref[r, :])], [None], None, None))
    outs = _attend_many(problems)
    for p in range(n_par):
        oa_ref[p * t:(p + 1) * t, :] = outs[2 * p]
        od_ref[p * t:(p + 1) * t, :] = outs[2 * p + 1]


def _attn_ctx_call(sink, qa, ka, va, qd, kd, vd, *, l, n_seq, t):
    n_par = 2 if n_seq % 2 == 0 else 1
    blk = lambda w: pl.BlockSpec((n_par * t, w), lambda b, *_: (b, 0))
    grid_spec = pltpu.PrefetchScalarGridSpec(
        num_scalar_prefetch=1, grid=(n_seq // n_par,),
        in_specs=[blk(256), blk(128), blk(128), blk(256), blk(128), blk(128)],
        out_specs=[blk(256), blk(256)])
    return pl.pallas_call(
        functools.partial(_attn_ctx_kernel, l=l, t=t, n_par=n_par), grid_spec=grid_spec,
        out_shape=[jax.ShapeDtypeStruct((n_seq * t, 256), F32)] * 2,
        compiler_params=_cparams(("parallel",)),
        name="attn_context",
    )(sink, qa, ka, va, qd, kd, vd)


def _attn_band_kernel(sink_ref, q_ref, k_ref, v_ref, ck_ref, cv_ref, o_ref, *, l, tq, t):
    i = pl.program_id(1)
    start = pl.multiple_of(i * tq, tq)
    prev = pl.multiple_of(jnp.maximum(start - WINDOW, 0), WINDOW)
    nxt = pl.multiple_of(jnp.minimum(start + tq, t - WINDOW), WINDOW)
    k_loc = jnp.concatenate([k_ref[pl.ds(prev, WINDOW), :], k_ref[pl.ds(start, tq), :],
                             k_ref[pl.ds(nxt, WINDOW), :]], axis=0)
    v_loc = jnp.concatenate([v_ref[pl.ds(prev, WINDOW), :], v_ref[pl.ds(start, tq), :],
                             v_ref[pl.ds(nxt, WINDOW), :]], axis=0)
    shape = (tq, tq + 2 * WINDOW)
    qpos = start + _row_iota(shape)
    kpos = start - WINDOW + _lane_iota(shape)
    mask = (kpos >= 0) & (kpos < t) & (jnp.abs(qpos - kpos) <= WINDOW)
    sinks = [sink_ref[4 * l + u] * LOG2E for u in range(4)]
    o_ref[...] = _attend(q_ref[...], [_bf(ck_ref[0]), _bf(k_loc)], [_bf(cv_ref[0]), _bf(v_loc)],
                         [None, mask], sinks)


def _cache_spec(c, l):
    return pl.BlockSpec((1, None) + c.shape[2:], lambda b, i, *_: (b, l, 0, 0))


def _attn_band_call(sink, q, k, v, ck, cv, *, l, n_seq, t, row0, tq):
    nq = t // tq
    qb0 = row0 // tq
    sb0 = row0 // t
    grid_spec = pltpu.PrefetchScalarGridSpec(
        num_scalar_prefetch=1, grid=(n_seq, nq),
        in_specs=[
            pl.BlockSpec((tq, 256), lambda b, i, *_: (qb0 + b * nq + i, 0)),
            pl.BlockSpec((t, 128), lambda b, i, *_: (sb0 + b, 0)),
            pl.BlockSpec((t, 128), lambda b, i, *_: (sb0 + b, 0)),
            _cache_spec(ck, l), _cache_spec(cv, l),
        ],
        out_specs=pl.BlockSpec((tq, 256), lambda b, i, *_: (b * nq + i, 0)))
    return pl.pallas_call(
        functools.partial(_attn_band_kernel, l=l, tq=tq, t=t), grid_spec=grid_spec,
        out_shape=jax.ShapeDtypeStruct((n_seq * t, 256), F32),
        compiler_params=_cparams(("parallel", "parallel")),
        name="attn_banded",
    )(sink, q, k, v, ck, cv)


def _attn_full_kernel(q_ref, k_ref, v_ref, ck_ref, cv_ref, o_ref, ks_ref, vs_ref, *, past, t):
    @pl.when(pl.program_id(1) == 0)
    def _():
        kc = _bf(ck_ref[0])
        kl = _bf(k_ref[...])
        ks_ref[0, 0:past, :] = kc
        ks_ref[0, past:past + t, :] = kl
        ks_ref[1, 0:past, :] = pltpu.roll(kc, HEAD_DIM, 1)
        ks_ref[1, past:past + t, :] = pltpu.roll(kl, HEAD_DIM, 1)
        vs_ref[0:past, :] = _bf(cv_ref[0])
        vs_ref[past:past + t, :] = _bf(v_ref[...])

    o_ref[...] = _attend(q_ref[...], [ks_ref[0]], [vs_ref[...]], [None], None, ks_sw=[ks_ref[1]], order="skewed")


def _attn_full_call(q, k, v, ck, cv, *, l, n_seq, t, row0, tq):
    nq = t // tq
    qb0 = row0 // tq
    sb0 = row0 // t
    past = ck.shape[2]
    assert past % 16 == 0
    return pl.pallas_call(
        functools.partial(_attn_full_kernel, past=past, t=t), grid=(n_seq, nq),
        scratch_shapes=[pltpu.VMEM((2, past + t, 128), BF16), pltpu.VMEM((past + t, 128), BF16)],
        in_specs=[
            pl.BlockSpec((tq, 256), lambda b, i: (qb0 + b * nq + i, 0)),
            pl.BlockSpec((t, 128), lambda b, i: (sb0 + b, 0)),
            pl.BlockSpec((t, 128), lambda b, i: (sb0 + b, 0)),
            _cache_spec(ck, l), _cache_spec(cv, l),
        ],
        out_specs=pl.BlockSpec((tq, 256), lambda b, i: (b * nq + i, 0)),
        out_shape=jax.ShapeDtypeStruct((n_seq * t, 256), F32),
        compiler_params=_cparams(("parallel", "arbitrary")),
        name="attn_full",
    )(q, k, v, ck, cv)


def _gelu_tanh(x):
    return 0.5 * x * (1.0 + jnp.tanh(0.7978845608028654 * (x + 0.044715 * (x * x * x))))


def _lru_kernel(lx_ref, lg_ref, w_ref, b_ref, lam_ref, h0_ref, y_ref, st_ref,
                hf_ref, hb_ref, af_ref, bf_ref, ab_ref, bb_ref, *, t, chunk):
    nc = t // chunk
    ng = chunk // 8
    sp = _softplus(-lam_ref[...])
    row8 = _row_iota((8, BRANCH_W))

    def gates(xc, d, a_ref, b2_ref):
        pre = _bdot(xc, w_ref[:, 512 * d:512 * (d + 1)]) + b_ref[:, 512 * d:512 * (d + 1)]
        r = _sigmoid(pre[:, 0:256])
        ig = _sigmoid(pre[:, 256:512])
        a = jnp.exp((-LRU_C) * r * sp[d:d + 1, :])
        a_ref[...] = a
        om = 1.0 - a * a
        b2_ref[...] = jnp.where(om > 0.0, om * lax.rsqrt(om), 0.0) * (ig * xc)

    def local_scan(a8, b8, reverse):
        for s in (1, 2, 4):
            sh = (8 - s) if reverse else s
            m = (row8 < 8 - s) if reverse else (row8 >= s)
            a_sh = pltpu.roll(a8, sh, 0)
            b_sh = pltpu.roll(b8, sh, 0)
            b8 = jnp.where(m, a8 * b_sh + b8, b8)
            a8 = jnp.where(m, a8 * a_sh, a8)
        return a8, b8

    def chunk_body(cc, carry):
        rf = pl.multiple_of(cc * chunk, chunk)
        rb = pl.multiple_of((nc - 1 - cc) * chunk, chunk)
        gates(lx_ref[pl.ds(rf, chunk), :], 0, af_ref, bf_ref)
        gates(lx_ref[pl.ds(rb, chunk), :], 1, ab_ref, bb_ref)

        def grp(gg, c2):
            hf, hb = c2
            gf = pl.multiple_of(gg * 8, 8)
            gb = pl.multiple_of((ng - 1 - gg) * 8, 8)
            a8, b8 = local_scan(af_ref[pl.ds(gf, 8), :], bf_ref[pl.ds(gf, 8), :], False)
            h8 = b8 + a8 * hf
            hf_ref[pl.ds(rf + gf, 8), :] = h8
            a8, b8 = local_scan(ab_ref[pl.ds(gb, 8), :], bb_ref[pl.ds(gb, 8), :], True)
            g8 = b8 + a8 * hb
            hb_ref[pl.ds(rb + gb, 8), :] = g8
            return (jnp.broadcast_to(h8[7:8, :], (8, BRANCH_W)), jnp.broadcast_to(g8[0:1, :], (8, BRANCH_W)))

        return lax.fori_loop(0, ng, grp, carry, unroll=4)

    init = (jnp.broadcast_to(h0_ref[0, 0:1, :], (8, BRANCH_W)), jnp.broadcast_to(h0_ref[0, 1:2, :], (8, BRANCH_W)))
    hf, hb = lax.fori_loop(0, nc, chunk_body, init)
    st_ref[0, 0:1, :] = hf[0:1, :]
    st_ref[0, 1:2, :] = hb[0:1, :]

    def combine(c, _):
        rows = pl.ds(pl.multiple_of(c * chunk, chunk), chunk)
        y_ref[rows, :] = (hf_ref[rows, :] + hb_ref[rows, :]) * _gelu_tanh(lg_ref[rows, :])
        return 0

    lax.fori_loop(0, nc, combine, 0)


def _lru_call(lx, lg, w, b, lam, h0, *, l, l_state, n_seq, t, row0):
    sb0 = row0 // t
    chunk = min(t, 256)
    seq = lambda: pl.BlockSpec((t, BRANCH_W), lambda s: (sb0 + s, 0))
    return pl.pallas_call(
        functools.partial(_lru_kernel, t=t, chunk=chunk),
        grid=(n_seq,),
        in_specs=[seq(), seq(), _layer_spec(w, l), _layer_spec(b, l), _layer_spec(lam, l),
                  pl.BlockSpec((1, None, 2, BRANCH_W), lambda s: (s, l_state, 0, 0))],
        out_specs=[pl.BlockSpec((t, BRANCH_W), lambda s: (s, 0)),
                   pl.BlockSpec((1, 2, BRANCH_W), lambda s: (s, 0, 0))],
        out_shape=[jax.ShapeDtypeStruct((n_seq * t, BRANCH_W), F32),
                   jax.ShapeDtypeStruct((n_seq, 2, BRANCH_W), F32)],
        scratch_shapes=[pltpu.VMEM((t, BRANCH_W), F32), pltpu.VMEM((t, BRANCH_W), F32)]
                       + [pltpu.VMEM((chunk, BRANCH_W), F32)] * 4,
        compiler_params=_cparams(("parallel",)),
        name="rglru",
    )(lx, lg, w, b, lam, h0)


def _rows_dot_exact(x, w01):
    r = x.shape[0]
    out = jnp.dot(jnp.concatenate(_split3(x), axis=0), w01, preferred_element_type=F32)
    return out[0:r] + out[r:2 * r] + out[2 * r:3 * r]


def _dot01_exact(m01, y):
    return sum(jnp.dot(m01, part, preferred_element_type=F32) for part in _split3(y))


def _gdn_kernel(q_ref, k_ref, v_ref, ggb_ref, s0_ref, o_ref, sT_ref, s_ref, *, d, reverse, n_par, n_chunk, n_tile):
    ti = pl.program_id(1)
    w4 = GDN_H * GDN_DK
    n_all = n_par * n_chunk
    tt = n_all * CHUNK
    blockmask = (_row_iota((w4, w4)) // CHUNK) == (_lane_iota((w4, w4)) // CHUNK)

    def expand_rows(y):
        yt = jnp.concatenate([y] * GDN_H, axis=0)
        zero = jnp.zeros((), y.dtype)
        parts = [jnp.where(blockmask, yt[:, w4 * u:w4 * (u + 1)], zero) for u in range(y.shape[1] // w4)]
        return parts[0] if len(parts) == 1 else jnp.concatenate(parts, axis=1)

    def heads_dot3(lhs, y):
        r = lhs.shape[0]
        lh, ll = _split(lhs)
        yh, yl = _split(y)
        out = jnp.dot(jnp.concatenate([lh, ll], axis=0), expand_rows(yh), preferred_element_type=F32)
        return out[0:r] + out[r:2 * r] + jnp.dot(lh, expand_rows(yl), preferred_element_type=F32)

    def heads_dot2(lhs, y):
        r = lhs.shape[0]
        lh, ll = _split(lhs)
        out = jnp.dot(jnp.concatenate([lh, ll], axis=0), expand_rows(y.astype(BF16)), preferred_element_type=F32)
        return out[0:r] + out[r:2 * r]

    def heads_dot1(lhs, y):
        return jnp.dot(lhs.astype(BF16), expand_rows(y.astype(BF16)), preferred_element_type=F32)

    @pl.when(ti == 0)
    def _():
        for p in range(n_par):
            s_ref[p] = expand_rows(s0_ref[p, 0])

    ri = _row_iota((CHUNK, w4))
    cj = _lane_iota((CHUNK, w4)) % CHUNK
    incl = (cj >= ri) if reverse else (cj <= ri)
    strict = (cj > ri) if reverse else (cj < ri)
    eye_sbs = (cj == ri)
    rep = eye_sbs.astype(BF16)

    er = _row_iota((128, 2 * w4))
    el = _lane_iota((128, 2 * w4))
    e_gb = (er == jnp.where(el < w4, d * GDN_H, 2 * GDN_H + d * GDN_H) + (el % w4) // CHUNK).astype(BF16)
    gbe = _rows_dot_exact(ggb_ref[...], e_gb)
    g_all = gbe[:, 0:w4]
    beta_all = gbe[:, w4:2 * w4]
    tr = _row_iota((tt, tt))
    tc = _lane_iota((tt, tt))
    same_chunk = (tr // CHUNK) == (tc // CHUNK)
    tri_bd = (same_chunk & ((tc >= tr) if reverse else (tc <= tr))).astype(BF16)
    ones_bd = same_chunk.astype(BF16)
    gc_all = _dot01_exact(tri_bd, g_all)
    diag_sel = (_row_iota((tt, w4)) % CHUNK) == (_lane_iota((tt, w4)) % CHUNK)
    gct_all = _dot01_exact(ones_bd, jnp.where(diag_sel, gc_all, 0.0))

    cs = range(n_all)
    rows = [slice(c * CHUNK, (c + 1) * CHUNK) for c in cs]
    q = [q_ref[r, :] for r in rows]
    k = [k_ref[r, :] for r in rows]
    v = [v_ref[r, :] for r in rows]
    beta = [beta_all[r, :] for r in rows]
    gc = [gc_all[r, :] for r in rows]
    decay = [jnp.where(incl, jnp.exp(jnp.where(incl, gc[c] - gct_all[rows[c], :], 0.0)), 0.0) for c in cs]
    kb = [k[c].astype(BF16) for c in cs]
    w_k = [jnp.where(blockmask,
                     lax.dot_general(kb[c], rep, (((0,), (0,)), ((), ())), preferred_element_type=F32),
                     0.0).astype(BF16) for c in cs]
    kq = [jnp.dot(jnp.concatenate([kb[c], q[c].astype(BF16)], axis=0), w_k[c], preferred_element_type=F32)
          for c in cs]
    qk = [(kq[c][CHUNK:2 * CHUNK] * decay[c]).astype(BF16) for c in cs]
    a = [jnp.where(strict, beta[c] * kq[c][0:CHUNK] * decay[c], 0.0) for c in cs]
    t_inv = [jnp.where(eye_sbs, 1.0, 0.0) - a[c] for c in cs]
    pw = [heads_dot3(a[c], a[c]) for c in cs]
    for stage in range(4):
        mm = heads_dot3 if stage < N_PRECISE else heads_dot1
        both = [mm(jnp.concatenate([t_inv[c], pw[c]], axis=0), pw[c]) for c in cs]
        t_inv = [t_inv[c] + both[c][0:CHUNK] for c in cs]
        pw = [both[c][CHUNK:2 * CHUNK] for c in cs]
    t_inv = [t_inv[c] + heads_dot1(t_inv[c], pw[c]) for c in cs]
    egc = [jnp.exp(gc[c]) for c in cs]
    sol = [heads_dot2(t_inv[c], jnp.concatenate([v[c] * beta[c], k[c] * (beta[c] * egc[c])], axis=1)) for c in cs]
    g_last = [gc[c][0:1, :] if reverse else gc[c][CHUNK - 1:CHUNK, :] for c in cs]
    wq = [jnp.concatenate([sol[c][:, w4:2 * w4], q[c] * egc[c]], axis=0).astype(BF16) for c in cs]
    kdec = [(k[c] * jnp.exp(g_last[c] - gc[c])).astype(BF16) for c in cs]

    for cc in range(n_chunk):
        for p in range(n_par):
            c = p * n_chunk + ((n_chunk - 1 - cc) if reverse else cc)
            s = s_ref[p]
            ws_qs = jnp.dot(wq[c], s.astype(BF16), preferred_element_type=F32)
            v_new = sol[c][:, 0:w4] - ws_qs[0:CHUNK]
            vb = v_new.astype(BF16)
            o_ref[rows[c], :] = (
                ws_qs[CHUNK:2 * CHUNK] + jnp.dot(qk[c], expand_rows(vb), preferred_element_type=F32))
            upd = lax.dot_general(kdec[c], vb, (((0,), (0,)), ((), ())), preferred_element_type=F32)
            s_ref[p] = s * jnp.exp(g_last[c]) + jnp.where(blockmask, upd, 0.0)

    @pl.when(ti == n_tile - 1)
    def _():
        for p in range(n_par):
            s = s_ref[p]
            sT_ref[p, 0] = s[0:64] + s[64:128] + s[128:192] + s[192:256]


def _gdn_call(q, k, v, ggb, s0, *, d, l_state, n_seq, t, row0, tt, n_par):
    reverse = d == 1
    n_tile = t // tt
    assert n_par == 1 or n_tile == 1
    b0 = row0 // (tt * n_par)
    tidx = (lambda i: n_tile - 1 - i) if reverse else (lambda i: i)
    blk = lambda w: pl.BlockSpec((tt * n_par, w), lambda s, i: (b0 + s * n_tile + tidx(i), 0))
    return pl.pallas_call(
        functools.partial(_gdn_kernel, d=d, reverse=reverse, n_par=n_par, n_chunk=tt // CHUNK, n_tile=n_tile),
        grid=(n_seq // n_par, n_tile),
        in_specs=[blk(256), blk(256), blk(256), blk(128),
                  pl.BlockSpec((n_par, None, 1, CHUNK, 256), lambda s, i: (s, l_state, d, 0, 0))],
        out_specs=[pl.BlockSpec((tt * n_par, 256), lambda s, i: (s * n_tile + tidx(i), 0)),
                   pl.BlockSpec((n_par, 1, CHUNK, 256), lambda s, i: (s, 0, 0, 0))],
        out_shape=[jax.ShapeDtypeStruct((n_seq * t, 256), F32),
                   jax.ShapeDtypeStruct((n_seq, 1, CHUNK, 256), F32)],
        scratch_shapes=[pltpu.VMEM((n_par, 256, 256), F32)],
        compiler_params=_cparams(("parallel", "arbitrary")),
        name="gdn_bwd" if reverse else "gdn_fwd",
    )(q, k, v, ggb, s0)


def _mix_mlp_kernel(modrow_ref, xc_ref, xd_ref, mod_ref, g1_ref,
                    oac_ref, oad_ref, obc_ref, obd_ref, ofc_ref, ofd_ref, orc_ref, ord_ref, odc_ref, odd_ref,
                    gz_ref, gng_ref, ones_ref, wm_ref, bm_ref, wb_ref, wo_ref, g2_ref, w1_ref, w2_ref,
                    yc_ref, yd_ref, *, n_ctx_tiles):
    is_ctx = pl.program_id(0) < n_ctx_tiles
    pick = lambda c_ref, d_ref: jnp.where(is_ctx, c_ref[...], d_ref[...])
    x = pick(xc_ref, xd_ref)
    h = _modulate(x, g1_ref[...], mod_ref[0, 0:1, :], mod_ref[0, 1:2, :]).astype(BF16)
    oc = pick(ofc_ref, ofd_ref) + pick(orc_ref, ord_ref)
    ms = _group_sum(oc * oc, ones_ref[...]) * (1.0 / GDN_DK)
    oc = (oc * lax.rsqrt(ms + EPS) * gng_ref[...]) * _silu(gz_ref[...])
    branches = (pick(oac_ref, oad_ref), pick(obc_ref, obd_ref), oc, pick(odc_ref, odd_ref))
    acc = None
    for m in range(N_BRANCH):
        cols = slice(D_MODEL * m, D_MODEL * (m + 1))
        gate = _sigmoid(jnp.dot(h, wm_ref[:, cols], preferred_element_type=F32) + bm_ref[:, cols])
        term = gate * jnp.dot(branches[m].astype(BF16), wb_ref[m], preferred_element_type=F32)
        acc = term if acc is None else acc + term
    x = x + mod_ref[0, 2:3, :] * jnp.dot(acc.astype(BF16), wo_ref[...], preferred_element_type=F32)

    h = _modulate(x, g2_ref[...], mod_ref[0, 3:4, :], mod_ref[0, 4:5, :]).astype(BF16)
    acc = None
    for j in range(D_FF // D_MODEL):
        cols = slice(D_MODEL * j, D_MODEL * (j + 1))
        a = jnp.maximum(jnp.dot(h, w1_ref[:, cols], preferred_element_type=F32), 0.0)
        term = jnp.dot((a * a).astype(BF16), w2_ref[cols, :], preferred_element_type=F32)
        acc = term if acc is None else acc + term
    y = x + mod_ref[0, 5:6, :] * acc

    @pl.when(is_ctx)
    def _():
        yc_ref[...] = y

    @pl.when(jnp.logical_not(is_ctx))
    def _():
        yd_ref[...] = y


def _mix_mlp_call(modrow, x_pair, mod_all, g1, pairs, gz, gng, ones_bd, wm, bm, wb, wo, g2, w1, w2, *, l, tm):
    x_c, x_d = x_pair
    n, d = x_c.shape[0] + x_d.shape[0], x_c.shape[1]
    nct = x_c.shape[0] // tm
    row_tile = lambda w: pl.BlockSpec((tm, w), lambda i, *_: (i, 0))
    pair = lambda w: [pl.BlockSpec((tm, w), lambda i, *_: (jnp.minimum(i, nct - 1), 0)),
                      pl.BlockSpec((tm, w), lambda i, *_: (jnp.maximum(i - nct, 0), 0))]
    pair_specs = [sp for _ in pairs for sp in pair(256)]
    resident = lambda arr: _layer_spec(arr, l, single_buffer=True)
    grid_spec = pltpu.PrefetchScalarGridSpec(
        num_scalar_prefetch=1, grid=(n // tm,),
        in_specs=pair(d) + [pl.BlockSpec((None, 1, 6, d), lambda i, modrow: (l, modrow[i], 0, 0)),
                            _layer_spec(g1, l)] + pair_specs
                 + [row_tile(256), _layer_spec(gng, l), pl.BlockSpec((256, 256), lambda i, *_: (0, 0)),
                    resident(wm), _layer_spec(bm, l), resident(wb), resident(wo),
                    _layer_spec(g2, l), resident(w1), resident(w2)],
        out_specs=pair(d))
    flat = [a for pr in pairs for a in pr]
    return pl.pallas_call(
        functools.partial(_mix_mlp_kernel, n_ctx_tiles=nct), grid_spec=grid_spec,
        out_shape=[jax.ShapeDtypeStruct(x_c.shape, F32), jax.ShapeDtypeStruct(x_d.shape, F32)],
        compiler_params=_cparams(("arbitrary",)),
        name="merge_mlp",
    )(modrow, x_c, x_d, mod_all, g1, *flat, gz, gng, ones_bd, wm, bm, wb, wo, g2, w1, w2)


def _rope_tables(t, tm):
    n_freq = HEAD_DIM // 4
    inv = np.float32(ROPE_BASE) ** (-np.arange(n_freq, dtype=np.float32) / np.float32(n_freq))
    pos = np.arange(t)
    row = (pos // GRID_W).astype(np.float32)[:, None]
    col = (pos % GRID_W).astype(np.float32)[:, None]
    ar = (row * inv).astype(np.float32)
    ac = (col * inv).astype(np.float32)
    cos64 = np.concatenate([np.cos(ar), np.cos(ar), np.cos(ac), np.cos(ac)], axis=1)
    sin64 = np.concatenate([-np.sin(ar), np.sin(ar), -np.sin(ac), np.sin(ac)], axis=1)
    cos = np.concatenate([np.ones((tm, 128), np.float32), np.tile(cos64, (1, 2)).astype(np.float32)], axis=0)
    sin = np.concatenate([np.zeros((tm, 128), np.float32), np.tile(sin64, (1, 2)).astype(np.float32)], axis=0)
    return jnp.asarray(cos), jnp.asarray(sin)


def _tile_meta(n_ctx, n_seq_dec, t_ctx, t_dec, tm):
    nct = n_ctx // tm
    per = t_dec // tm
    ndt = n_seq_dec * per
    idx = np.arange(nct + ndt)
    dec = idx >= nct
    di = np.maximum(idx - nct, 0)
    modrow = np.where(dec, 1 + di // per, 0)
    ropeblk = np.where(dec, 1 + di % per, 0)
    tiles_ctx = max(t_ctx // tm, 1)
    isstart = np.where(dec, di % per == 0, idx % tiles_ctx == 0)
    isend = np.where(dec, di % per == per - 1, idx % tiles_ctx == tiles_ctx - 1)
    as_i32 = lambda a: jnp.asarray(a.astype(np.int32))
    return as_i32(modrow), as_i32(ropeblk), as_i32(isstart), as_i32(isend)


def _block_diag_gates(w):
    depth, two, n, c, _ = w.shape
    eye = jnp.eye(n, dtype=w.dtype)
    return jnp.einsum("ldnij,nm->ldnimj", w, eye).reshape(depth, two, n * c, n * c)


def kernel(x_prompt, x_sample, c, cache_a_k, cache_a_v, cache_d_k, cache_d_v, state_lru, state_gdn, c_ctx, mod_w, mod_b, norm1_g, norm2_g, w_in, a_qn_g, a_kn_g, a_sink, lru_conv_w, lru_conv_b, lru_wr, lru_br, lru_wi, lru_bi, lru_lam, gdn_conv_w, gdn_a_log, gdn_dt_bias, gdn_norm_g, d_qn_g, d_kn_g, w_branch, w_merge, b_merge, w_out, mlp_w1, mlp_w2):
    batch, seq, d = x_prompt.shape
    dec_batch, dec_seq, _ = x_sample.shape
    depth = mod_w.shape[0]
    past = cache_a_k.shape[2]
    n_ctx = batch * seq
    n_dec = dec_batch * dec_seq
    tm = 256
    gdn_par = 2 if batch % 2 == 0 else 1
    gdn_tt = 512 if dec_seq % 512 == 0 else 256
    assert d == D_MODEL and seq % tm == 0 and dec_seq % tm == 0 and tm % seq == 0
    assert dec_batch + 1 <= 8 and n_ctx % dec_seq == 0

    cond8 = jnp.zeros((8, d), F32).at[0].set(c_ctx).at[1:1 + dec_batch].set(c)
    mod_all = _mod_call(cond8, mod_w, mod_b).reshape(depth, 8, 6, d)

    meta = _tile_meta(n_ctx, dec_batch, seq, dec_seq, tm)
    cos_t, sin_t = _rope_tables(dec_seq, tm)
    lane = np.arange(256)
    ones_bd = jnp.asarray((lane[:, None] // HEAD_DIM == lane[None, :] // HEAD_DIM).astype(np.float32)).astype(BF16)

    w_packed = jnp.concatenate([w_in[:, :, :2064], jnp.zeros((depth, d, GDN_PAD), F32), w_in[:, :, 2064:]],
                               axis=2).astype(BF16)
    pad128 = lambda v: jnp.pad(v.reshape(depth, 1, -1), ((0, 0), (0, 0), (0, 128 - v[0].size)))
    vecs = (
        (jnp.tile(a_qn_g, (1, 4)) * Q_SCALE)[:, None, :],
        jnp.tile(a_kn_g, (1, 2))[:, None, :],
        (jnp.tile(d_qn_g, (1, 4)) * Q_SCALE)[:, None, :],
        jnp.tile(d_kn_g, (1, 2))[:, None, :],
        lru_conv_w, lru_conv_b[:, None, :], gdn_conv_w,
        pad128(gdn_a_log), pad128(gdn_dt_bias),
    )
    g1 = norm1_g[:, None, :]
    g2 = norm2_g[:, None, :]
    wr_bd = _block_diag_gates(lru_wr)
    wi_bd = _block_diag_gates(lru_wi)
    w_lru = jnp.concatenate([wr_bd[:, 0], wi_bd[:, 0], wr_bd[:, 1], wi_bd[:, 1]], axis=-1).astype(BF16)
    b_lru = jnp.concatenate([lru_br[:, 0], lru_bi[:, 0], lru_br[:, 1], lru_bi[:, 1]], axis=-1)[:, None, :]
    gng = jnp.tile(gdn_norm_g, (1, 4))[:, None, :]
    wm = w_merge.astype(BF16)
    bm = b_merge[:, None, :]
    wb = w_branch.astype(BF16)
    wo = w_out.astype(BF16)
    w1 = mlp_w1.astype(BF16)
    w2 = mlp_w2.astype(BF16)
    sink = a_sink.reshape(-1)
    caches = [t.reshape(dec_batch, depth, past, 2 * HEAD_DIM) for t in (cache_a_k, cache_a_v, cache_d_k, cache_d_v)]
    s0_dec = state_gdn.transpose(0, 1, 2, 4, 3, 5).reshape(dec_batch, depth, 2, CHUNK, 256)
    zeros_lru = jnp.zeros((batch, 1, 2, BRANCH_W), F32)
    zeros_gdn = jnp.zeros((batch, 1, 2, CHUNK, 256), F32)

    x = (x_prompt.reshape(n_ctx, d), x_sample.reshape(n_dec, d))

    kv_ctx, lru_states, gdn_states = None, [], []
    for l in range(depth):
        (qa, ka, va, lx, lg, gq, gk, gv, gz, ggb, qd, kd, vd), kv_ctx = _inproj_call(
            x, meta, mod_all, g1, w_packed, cos_t, sin_t, vecs, ones_bd, kv_ctx, l=l, depth=depth, tm=tm)

        oa_c, od_c = _attn_ctx_call(sink, qa, ka, va, qd, kd, vd, l=l, n_seq=batch, t=seq)
        oa_d = _attn_band_call(sink, qa, ka, va, caches[0], caches[1], l=l, n_seq=dec_batch, t=dec_seq,
                               row0=n_ctx, tq=256)
        od_d = _attn_full_call(qd, kd, vd, caches[2], caches[3], l=l, n_seq=dec_batch, t=dec_seq,
                               row0=n_ctx, tq=256)

        ob_c, st_c = _lru_call(lx, lg, w_lru, b_lru, lru_lam, zeros_lru, l=l, l_state=0, n_seq=batch, t=seq,
                               row0=0)
        ob_d, _ = _lru_call(lx, lg, w_lru, b_lru, lru_lam, state_lru, l=l, l_state=l, n_seq=dec_batch,
                            t=dec_seq, row0=n_ctx)

        oc_pairs, sts = [], []
        for dd in range(2):
            o_c, s_c = _gdn_call(gq, gk, gv, ggb, zeros_gdn, d=dd, l_state=0, n_seq=batch, t=seq, row0=0,
                                 tt=seq, n_par=gdn_par)
            o_d, _ = _gdn_call(gq, gk, gv, ggb, s0_dec, d=dd, l_state=l, n_seq=dec_batch, t=dec_seq,
                               row0=n_ctx, tt=gdn_tt, n_par=1)
            oc_pairs.append((o_c, o_d))
            sts.append(s_c)

        x = _mix_mlp_call(meta[0], x, mod_all, g1,
                          [(oa_c, oa_d), (ob_c, ob_d), oc_pairs[0], oc_pairs[1], (od_c, od_d)],
                          gz, gng, ones_bd, wm, bm, wb, wo, g2, w1, w2, l=l, tm=tm)

        lru_states.append(st_c)
        gdn_states.append(jnp.concatenate(sts, axis=1))

    y_prompt = x[0].reshape(batch, seq, d)
    y_sample = x[1].reshape(dec_batch, dec_seq, d)
    new_kv = [t.reshape(batch, depth, seq, 2, HEAD_DIM) for t in kv_ctx]
    new_lru = jnp.stack(lru_states, axis=1)
    new_gdn = (jnp.stack(gdn_states, axis=1).reshape(batch, depth, 2, GDN_DK, GDN_H, GDN_DK)
               .transpose(0, 1, 2, 4, 3, 5))
    return (y_prompt, y_sample, new_kv[0], new_kv[1], new_kv[2], new_kv[3], new_lru, new_gdn)
```

```python
import functools

import numpy as np
import jax
import jax.numpy as jnp
from jax import lax
from jax.experimental import pallas as pl
from jax.experimental.pallas import tpu as pltpu

F32 = jnp.float32
BF16 = jnp.bfloat16
HIGHEST = lax.Precision.HIGHEST

D_MODEL = 1024
HEAD_DIM = 64
BRANCH_W = 256
N_BRANCH = 4
GRID_W = 64
WINDOW = 128
LRU_C = 8.0
CONV_W = 4
CONV_LEFT = 2
GDN_H = 4
GDN_DK = 64
CHUNK = 64
D_FF = 4 * D_MODEL
ROPE_BASE = 10000.0
EPS = 1e-6
NEG = -1e30
LOG2E = 1.4426950408889634
Q_SCALE = HEAD_DIM ** -0.5 * LOG2E
GDN_PAD = 112
IN_COLS_PACKED = 2688
HALO = 8
N_PRECISE = 4

V7X_VMEM_LIMIT = 56 * 1024 * 1024


def _cparams(sem, vmem=V7X_VMEM_LIMIT):
    return pltpu.CompilerParams(dimension_semantics=sem, vmem_limit_bytes=vmem)


def _layer_spec(arr, l, single_buffer=False):
    nd = arr.ndim - 1
    mode = dict(pipeline_mode=pl.Buffered(1)) if single_buffer else {}
    return pl.BlockSpec((None,) + arr.shape[1:], lambda *_: (l,) + (0,) * nd, **mode)


def _bdot(a, b):
    return jnp.dot(a.astype(BF16), b.astype(BF16), preferred_element_type=F32)


def _split(x):
    hi = x.astype(BF16)
    lo = (x - hi.astype(F32)).astype(BF16)
    return hi, lo


def _split3(x):
    hi = x.astype(BF16)
    r = x - hi.astype(F32)
    mid = r.astype(BF16)
    lo = (r - mid.astype(F32)).astype(BF16)
    return hi, mid, lo


def _group_sum(x, ones_bd):
    return jnp.dot(x.astype(BF16), ones_bd, preferred_element_type=F32)


def _sigmoid(x):
    return 0.5 * jnp.tanh(0.5 * x) + 0.5


def _silu(x):
    return x * _sigmoid(x)


def _softplus(x):
    return jnp.maximum(x, 0.0) + jnp.log1p(jnp.exp(-jnp.abs(x)))


def _modulate(x, g, shift, scale):
    ms = jnp.mean(x * x, axis=-1, keepdims=True)
    return (x * lax.rsqrt(ms + EPS)) * (g * (1.0 + scale)) + shift


def _lane_iota(shape):
    return lax.broadcasted_iota(jnp.int32, shape, len(shape) - 1)


def _row_iota(shape):
    return lax.broadcasted_iota(jnp.int32, shape, len(shape) - 2)


def _mod_kernel(cond_ref, w_ref, b_ref, o_ref):
    hi, lo = _split(_silu(cond_ref[...]))
    w = w_ref[0].astype(BF16)
    o_ref[0] = (jnp.dot(hi, w, preferred_element_type=F32) + jnp.dot(lo, w, preferred_element_type=F32)
                + b_ref[0])


def _mod_call(cond8, mod_w, mod_b):
    depth, d, n = mod_w.shape
    tn = 1536
    return pl.pallas_call(
        _mod_kernel,
        grid=(depth, n // tn),
        in_specs=[
            pl.BlockSpec((8, d), lambda l, j: (0, 0)),
            pl.BlockSpec((1, d, tn), lambda l, j: (l, 0, j)),
            pl.BlockSpec((1, 1, tn), lambda l, j: (l, 0, j)),
        ],
        out_specs=pl.BlockSpec((1, 8, tn), lambda l, j: (l, 0, j)),
        out_shape=jax.ShapeDtypeStruct((depth, 8, n), F32),
        compiler_params=_cparams(("parallel", "parallel")),
        name="mod_vectors",
    )(cond8, mod_w, mod_b.reshape(depth, 1, n))


def _rope(x, cos, sin):
    outs = []
    for j in range(x.shape[1] // 128):
        xb = x[:, 128 * j:128 * (j + 1)]
        lane = _lane_iota(xb.shape)
        sw = jnp.where((lane & 16) == 0, pltpu.roll(xb, 112, 1), pltpu.roll(xb, 16, 1))
        outs.append(xb * cos + sw * sin)
    return outs[0] if len(outs) == 1 else jnp.concatenate(outs, axis=1)


def _centred_conv(g, w, tm):
    rows = g.shape[0]
    acc = None
    for j in range(CONV_W):
        sh = (CONV_LEFT - j) % rows
        gj = g if sh == 0 else pltpu.roll(g, sh, 0)
        term = gj[HALO:HALO + tm] * w[j:j + 1, :]
        acc = term if acc is None else acc + term
    return acc


def _inproj_kernel(modrow_ref, ropeblk_ref, isstart_ref, isend_ref,
                   xpc_ref, xc_ref, xnc_ref, xpd_ref, xd_ref, xnd_ref, mod_ref, g1_ref, w_ref, cos_ref, sin_ref,
                   aqg_ref, akg_ref, dqg_ref, dkg_ref, lcw_ref, lcb_ref, gcw_ref,
                   alog_ref, dtb_ref, ones_ref, *rest, tm, n_ctx_tiles, n_alias):
    (qa_ref, ka_ref, va_ref, lx_ref, lg_ref, gq_ref, gk_ref, gv_ref, gz_ref, ggb_ref,
     qd_ref, kd_ref, vd_ref, kac_ref, vac_ref, kdc_ref, vdc_ref) = rest[n_alias:]
    i = pl.program_id(0)
    is_ctx = i < n_ctx_tiles
    pick = lambda c_ref, d_ref: jnp.where(is_ctx, c_ref[...], d_ref[...])
    xfull = jnp.concatenate([pick(xpc_ref, xpd_ref), pick(xc_ref, xd_ref), pick(xnc_ref, xnd_ref)],
                            axis=0)
    h = _modulate(xfull, g1_ref[...], mod_ref[0, 0:1, :], mod_ref[0, 1:2, :])
    p = jnp.dot(h.astype(BF16), w_ref[...], preferred_element_type=F32)
    hb, hm = slice(0, tm + 2 * HALO), slice(HALO, HALO + tm)
    proj = lambda rows, c0, c1: p[rows, c0:c1]

    ones_bd = ones_ref[...]
    cos = cos_ref[...]
    sin = sin_ref[...]

    pa = proj(hm, 0, 512)
    pd = proj(hm, 2176, 2688)
    va_ref[...] = pa[:, 384:512]
    vd_ref[...] = pd[:, 384:512]

    row = _row_iota((tm + 2 * HALO, 1))
    keep = jnp.logical_and(jnp.logical_or(row >= HALO, isstart_ref[i] == 0),
                           jnp.logical_or(row < HALO + tm, isend_ref[i] == 0))

    lxg = jnp.where(keep, proj(hb, 512, 768), 0.0)
    lx_ref[...] = _centred_conv(lxg, lcw_ref[...], tm) + lcb_ref[...]
    lg_ref[...] = proj(hm, 768, 1024)

    qkv = jnp.where(keep, proj(hb, 1024, 1792), 0.0)
    qkv = _silu(_centred_conv(qkv, gcw_ref[...], tm))
    gq = qkv[:, 0:256]
    gk = qkv[:, 256:512]
    gv_ref[...] = qkv[:, 512:768]
    pz = proj(hm, 1792, 2176)
    gz_ref[...] = pz[:, 0:256]
    ab = pz[:, 256:384]
    g = -jnp.exp(alog_ref[...]) * _softplus(ab + dtb_ref[...])
    lane = _lane_iota(ab.shape)
    ggb_ref[...] = jnp.where(lane < 2 * GDN_H, g, _sigmoid(ab))

    qa_raw, ka_raw, qd_raw, kd_raw = pa[:, 0:256], pa[:, 256:384], pd[:, 0:256], pd[:, 256:384]
    ss256 = _group_sum(jnp.concatenate([qa_raw * qa_raw, qd_raw * qd_raw, gq * gq, gk * gk], axis=0), ones_bd)
    ss128 = _group_sum(jnp.concatenate([ka_raw * ka_raw, kd_raw * kd_raw], axis=0), ones_bd[:128, :128])
    rms = lambda x, ssq, gain: x * lax.rsqrt(ssq * (1.0 / HEAD_DIM) + EPS) * gain

    qa_ref[...] = _rope(rms(qa_raw, ss256[0:tm], aqg_ref[...]), cos, sin)
    qd_ref[...] = _rope(rms(qd_raw, ss256[tm:2 * tm], dqg_ref[...]), cos, sin)
    ka = _rope(rms(ka_raw, ss128[0:tm], akg_ref[...]), cos, sin)
    kd = _rope(rms(kd_raw, ss128[tm:2 * tm], dkg_ref[...]), cos, sin)
    ka_ref[...] = ka
    kd_ref[...] = kd
    gq_ref[...] = gq * lax.rsqrt(ss256[2 * tm:3 * tm] + EPS) * (GDN_DK ** -0.5)
    gk_ref[...] = gk * lax.rsqrt(ss256[3 * tm:4 * tm] + EPS)

    @pl.when(is_ctx)
    def _():
        for ref, val in ((kac_ref, ka), (vac_ref, pa[:, 384:512]), (kdc_ref, kd), (vdc_ref, pd[:, 384:512])):
            if n_alias:
                ref[...] = val
            else:
                for slot in range(ref.shape[0]):
                    ref[slot] = val


def _x_pair_specs(x_pair, tm):
    x_c, x_d = x_pair
    d = x_c.shape[1]
    nct = x_c.shape[0] // tm
    hb = tm // HALO
    specs = []
    for arr, first in ((x_c, 0), (x_d, nct)):
        nblk = arr.shape[0] // tm
        last_hb = arr.shape[0] // HALO - 1
        tile = lambda i, first=first, nblk=nblk: jnp.clip(i - first, 0, nblk - 1)
        specs += [
            pl.BlockSpec((HALO, d), lambda i, *_, t=tile, m=last_hb: (jnp.clip(t(i) * hb - 1, 0, m), 0)),
            pl.BlockSpec((tm, d), lambda i, *_, t=tile: (t(i), 0)),
            pl.BlockSpec((HALO, d), lambda i, *_, t=tile, m=last_hb: (jnp.clip((t(i) + 1) * hb, 0, m), 0)),
        ]
    return specs


def _inproj_call(x_pair, meta, mod_all, g1, w_packed, cos_t, sin_t, vecs, ones_bd, kv_prev, *, l, depth, tm):
    x_c, x_d = x_pair
    n, d = x_c.shape[0] + x_d.shape[0], x_c.shape[1]
    nt = n // tm
    nct = x_c.shape[0] // tm
    row_tile = lambda w: pl.BlockSpec((tm, w), lambda i, *_: (i, 0))
    alias_in = [] if kv_prev is None else list(kv_prev)
    if alias_in:
        kv_spec = pl.BlockSpec((None, None, tm, 128), lambda i, *_: (jnp.minimum(i, nct - 1), l, 0, 0))
    else:
        kv_spec = pl.BlockSpec((None, depth, tm, 128), lambda i, *_: (jnp.minimum(i, nct - 1), 0, 0, 0))
    kv_shape = jax.ShapeDtypeStruct((nct, depth, tm, 128), F32)
    in_specs = _x_pair_specs(x_pair, tm) + [
        pl.BlockSpec((None, 1, 6, d), lambda i, modrow, *_: (l, modrow[i], 0, 0)),
        _layer_spec(g1, l),
        _layer_spec(w_packed, l),
        pl.BlockSpec((tm, 128), lambda i, modrow, ropeblk, *_: (ropeblk[i], 0)),
        pl.BlockSpec((tm, 128), lambda i, modrow, ropeblk, *_: (ropeblk[i], 0)),
    ] + [_layer_spec(v, l) for v in vecs] + [pl.BlockSpec((256, 256), lambda i, *_: (0, 0))]
    in_specs += [pl.BlockSpec(memory_space=pl.ANY)] * len(alias_in)
    widths = (256, 128, 128, 256, 256, 256, 256, 256, 256, 128, 256, 128, 128)
    grid_spec = pltpu.PrefetchScalarGridSpec(
        num_scalar_prefetch=4,
        grid=(nt,),
        in_specs=in_specs,
        out_specs=[row_tile(w) for w in widths] + [kv_spec] * 4,
    )
    operands = (*meta, x_c, x_c, x_c, x_d, x_d, x_d, mod_all, g1, w_packed, cos_t, sin_t, *vecs, ones_bd)
    outs = pl.pallas_call(
        functools.partial(_inproj_kernel, tm=tm, n_ctx_tiles=nct, n_alias=len(alias_in)),
        grid_spec=grid_spec,
        out_shape=[jax.ShapeDtypeStruct((n, w), F32) for w in widths] + [kv_shape] * 4,
        input_output_aliases={len(operands) + u: len(widths) + u for u in range(len(alias_in))},
        compiler_params=_cparams(("arbitrary",)),
        name="in_projection",
    )(*operands, *alias_in)
    return outs[:len(widths)], outs[len(widths):]


def _head_q(q, j, g):
    qj = q[:, 128 * j:128 * (j + 1)]
    lane = _lane_iota(qj.shape)
    sel = (lane < HEAD_DIM) if g == 0 else (lane >= HEAD_DIM)
    return jnp.where(sel, qj, 0.0).astype(BF16)


def _place_heads(res, j):
    r0 = res[0] if j == 0 else pltpu.roll(res[0], HEAD_DIM, 1)
    r1 = res[1] if j == 1 else pltpu.roll(res[1], HEAD_DIM, 1)
    lane = _lane_iota(r0.shape)
    return jnp.where(lane < HEAD_DIM, r0, r1)


def _attend_many(problems, order="staged"):
    work = []
    for pi, (q, ks, vs, masks, sinks, ks_sw) in enumerate(problems):
        if ks_sw is None:
            ks_sw = [pltpu.roll(k, HEAD_DIM, 1) for k in ks]
        work += [(pi, j, g, q, ks, vs, masks, sinks, ks_sw) for j in range(2) for g in range(2)]
    scores, probs, res = {}, {}, {}

    def score(w):
        pi, j, g, q, ks, vs, masks, sinks, ks_sw = w
        qm = _head_q(q, j, g)
        ss = []
        for k, ksw, mk in zip(ks, ks_sw, masks):
            s = lax.dot_general(qm, k if g == j else ksw, (((1,), (1,)), ((), ())), preferred_element_type=F32)
            ss.append(s if mk is None else jnp.where(mk, s, NEG))
        scores[w[:3]] = ss

    def softmax_num(w):
        pi, j, g, q, ks, vs, masks, sinks, ks_sw = w
        ss = scores[w[:3]]
        m = ss[0].max(axis=-1, keepdims=True)
        for s in ss[1:]:
            m = jnp.maximum(m, s.max(axis=-1, keepdims=True))
        if sinks is not None:
            m = jnp.maximum(m, sinks[2 * j + g])
        es = [jnp.exp2(s - m) for s in ss]
        den = sum(e.sum(axis=-1, keepdims=True) for e in es)
        if sinks is not None:
            den = den + jnp.exp2(sinks[2 * j + g] - m)
        probs[w[:3]] = ([e.astype(BF16) for e in es], den)

    def values(w):
        es, den = probs[w[:3]]
        res[w[:3]] = sum(jnp.dot(e, v, preferred_element_type=F32) for e, v in zip(es, w[5])) / den

    if order == "staged":
        for step in (score, softmax_num, values):
            for w in work:
                step(w)
    else:
        score(work[0])
        for prev, nxt in zip(work, work[1:] + [None]):
            if nxt is not None:
                score(nxt)
            softmax_num(prev)
            values(prev)
    return [jnp.concatenate([_place_heads([res[(pi, j, 0)], res[(pi, j, 1)]], j) for j in range(2)], axis=1)
            for pi in range(len(problems))]


def _attend(q, ks, vs, masks, sinks, ks_sw=None, order="staged"):
    return _attend_many([(q, ks, vs, masks, sinks, ks_sw)], order)[0]


def _bf(x):
    return x.astype(BF16)


def _attn_ctx_kernel(sink_ref, qa_ref, ka_ref, va_ref, qd_ref, kd_ref, vd_ref, oa_ref, od_ref, *, l, t, n_par):
    sinks = [sink_ref[4 * l + u] * LOG2E for u in range(4)]
    problems = []
    for p in range(n_par):
        r = slice(p * t, (p + 1) * t)
        problems.append((qa_ref[r, :], [_bf(ka_ref[r, :])], [_bf(va_ref[r, :])], [None], sinks, None))
        problems.append((qd_ref[r, :], [_bf(kd_ref[r, :])], [_bf(vd_ref[r, :])], [None], None, None))
    outs = _attend_many(problems)
    for p in range(n_par):
        oa_ref[p * t:(p + 1) * t, :] = outs[2 * p]
        od_ref[p * t:(p + 1) * t, :] = outs[2 * p + 1]


def _attn_ctx_call(sink, qa, ka, va, qd, kd, vd, *, l, n_seq, t):
    n_par = 2 if n_seq % 2 == 0 else 1
    blk = lambda w: pl.BlockSpec((n_par * t, w), lambda b, *_: (b, 0))
    grid_spec = pltpu.PrefetchScalarGridSpec(
        num_scalar_prefetch=1, grid=(n_seq // n_par,),
        in_specs=[blk(256), blk(128), blk(128), blk(256), blk(128), blk(128)],
        out_specs=[blk(256), blk(256)])
    return pl.pallas_call(
        functools.partial(_attn_ctx_kernel, l=l, t=t, n_par=n_par), grid_spec=grid_spec,
        out_shape=[jax.ShapeDtypeStruct((n_seq * t, 256), F32)] * 2,
        compiler_params=_cparams(("parallel",)),
        name="attn_context",
    )(sink, qa, ka, va, qd, kd, vd)


def _attn_band_kernel(sink_ref, q_ref, k_ref, v_ref, ck_ref, cv_ref, o_ref, *, l, tq, t):
    i = pl.program_id(1)
    start = pl.multiple_of(i * tq, tq)
    prev = pl.multiple_of(jnp.maximum(start - WINDOW, 0), WINDOW)
    nxt = pl.multiple_of(jnp.minimum(start + tq, t - WINDOW), WINDOW)
    k_loc = jnp.concatenate([k_ref[pl.ds(prev, WINDOW), :], k_ref[pl.ds(start, tq), :],
                             k_ref[pl.ds(nxt, WINDOW), :]], axis=0)
    v_loc = jnp.concatenate([v_ref[pl.ds(prev, WINDOW), :], v_ref[pl.ds(start, tq), :],
                             v_ref[pl.ds(nxt, WINDOW), :]], axis=0)
    shape = (tq, tq + 2 * WINDOW)
    qpos = start + _row_iota(shape)
    kpos = start - WINDOW + _lane_iota(shape)
    mask = (kpos >= 0) & (kpos < t) & (jnp.abs(qpos - kpos) <= WINDOW)
    sinks = [sink_ref[4 * l + u] * LOG2E for u in range(4)]
    o_ref[...] = _attend(q_ref[...], [_bf(ck_ref[0]), _bf(k_loc)], [_bf(cv_ref[0]), _bf(v_loc)],
                         [None, mask], sinks)


def _cache_spec(c, l):
    return pl.BlockSpec((1, None) + c.shape[2:], lambda b, i, *_: (b, l, 0, 0))


def _attn_band_call(sink, q, k, v, ck, cv, *, l, n_seq, t, row0, tq):
    nq = t // tq
    qb0 = row0 // tq
    sb0 = row0 // t
    grid_spec = pltpu.PrefetchScalarGridSpec(
        num_scalar_prefetch=1, grid=(n_seq, nq),
        in_specs=[
            pl.BlockSpec((tq, 256), lambda b, i, *_: (qb0 + b * nq + i, 0)),
            pl.BlockSpec((t, 128), lambda b, i, *_: (sb0 + b, 0)),
            pl.BlockSpec((t, 128), lambda b, i, *_: (sb0 + b, 0)),
            _cache_spec(ck, l), _cache_spec(cv, l),
        ],
        out_specs=pl.BlockSpec((tq, 256), lambda b, i, *_: (b * nq + i, 0)))
    return pl.pallas_call(
        functools.partial(_attn_band_kernel, l=l, tq=tq, t=t), grid_spec=grid_spec,
        out_shape=jax.ShapeDtypeStruct((n_seq * t, 256), F32),
        compiler_params=_cparams(("parallel", "parallel")),
        name="attn_banded",
    )(sink, q, k, v, ck, cv)


def _attn_full_kernel(q_ref, k_ref, v_ref, ck_ref, cv_ref, o_ref, ks_ref, vs_ref, *, past, t):
    @pl.when(pl.program_id(1) == 0)
    def _():
        kc = _bf(ck_ref[0])
        kl = _bf(k_ref[...])
        ks_ref[0, 0:past, :] = kc
        ks_ref[0, past:past + t, :] = kl
        ks_ref[1, 0:past, :] = pltpu.roll(kc, HEAD_DIM, 1)
        ks_ref[1, past:past + t, :] = pltpu.roll(kl, HEAD_DIM, 1)
        vs_ref[0:past, :] = _bf(cv_ref[0])
        vs_ref[past:past + t, :] = _bf(v_ref[...])

    o_ref[...] = _attend(q_ref[...], [ks_ref[0]], [vs_ref[...]], [None], None, ks_sw=[ks_ref[1]], order="skewed")


def _attn_full_call(q, k, v, ck, cv, *, l, n_seq, t, row0, tq):
    nq = t // tq
    qb0 = row0 // tq
    sb0 = row0 // t
    past = ck.shape[2]
    assert past % 16 == 0
    return pl.pallas_call(
        functools.partial(_attn_full_kernel, past=past, t=t), grid=(n_seq, nq),
        scratch_shapes=[pltpu.VMEM((2, past + t, 128), BF16), pltpu.VMEM((past + t, 128), BF16)],
        in_specs=[
            pl.BlockSpec((tq, 256), lambda b, i: (qb0 + b * nq + i, 0)),
            pl.BlockSpec((t, 128), lambda b, i: (sb0 + b, 0)),
            pl.BlockSpec((t, 128), lambda b, i: (sb0 + b, 0)),
            _cache_spec(ck, l), _cache_spec(cv, l),
        ],
        out_specs=pl.BlockSpec((tq, 256), lambda b, i: (b * nq + i, 0)),
        out_shape=jax.ShapeDtypeStruct((n_seq * t, 256), F32),
        compiler_params=_cparams(("parallel", "arbitrary")),
        name="attn_full",
    )(q, k, v, ck, cv)


def _gelu_tanh(x):
    return 0.5 * x * (1.0 + jnp.tanh(0.7978845608028654 * (x + 0.044715 * (x * x * x))))


def _lru_kernel(lx_ref, lg_ref, w_ref, b_ref, lam_ref, h0_ref, y_ref, st_ref,
                hf_ref, hb_ref, af_ref, bf_ref, ab_ref, bb_ref, *, t, chunk):
    nc = t // chunk
    ng = chunk // 8
    sp = _softplus(-lam_ref[...])
    row8 = _row_iota((8, BRANCH_W))

    def gates(xc, d, a_ref, b2_ref):
        pre = _bdot(xc, w_ref[:, 512 * d:512 * (d + 1)]) + b_ref[:, 512 * d:512 * (d + 1)]
        r = _sigmoid(pre[:, 0:256])
        ig = _sigmoid(pre[:, 256:512])
        a = jnp.exp((-LRU_C) * r * sp[d:d + 1, :])
        a_ref[...] = a
        om = 1.0 - a * a
        b2_ref[...] = jnp.where(om > 0.0, om * lax.rsqrt(om), 0.0) * (ig * xc)

    def local_scan(a8, b8, reverse):
        for s in (1, 2, 4):
            sh = (8 - s) if reverse else s
            m = (row8 < 8 - s) if reverse else (row8 >= s)
            a_sh = pltpu.roll(a8, sh, 0)
            b_sh = pltpu.roll(b8, sh, 0)
            b8 = jnp.where(m, a8 * b_sh + b8, b8)
            a8 = jnp.where(m, a8 * a_sh, a8)
        return a8, b8

    def chunk_body(cc, carry):
        rf = pl.multiple_of(cc * chunk, chunk)
        rb = pl.multiple_of((nc - 1 - cc) * chunk, chunk)
        gates(lx_ref[pl.ds(rf, chunk), :], 0, af_ref, bf_ref)
        gates(lx_ref[pl.ds(rb, chunk), :], 1, ab_ref, bb_ref)

        def grp(gg, c2):
            hf, hb = c2
            gf = pl.multiple_of(gg * 8, 8)
            gb = pl.multiple_of((ng - 1 - gg) * 8, 8)
            a8, b8 = local_scan(af_ref[pl.ds(gf, 8), :], bf_ref[pl.ds(gf, 8), :], False)
            h8 = b8 + a8 * hf
            hf_ref[pl.ds(rf + gf, 8), :] = h8
            a8, b8 = local_scan(ab_ref[pl.ds(gb, 8), :], bb_ref[pl.ds(gb, 8), :], True)
            g8 = b8 + a8 * hb
            hb_ref[pl.ds(rb + gb, 8), :] = g8
            return (jnp.broadcast_to(h8[7:8, :], (8, BRANCH_W)), jnp.broadcast_to(g8[0:1, :], (8, BRANCH_W)))

        return lax.fori_loop(0, ng, grp, carry, unroll=4)

    init = (jnp.broadcast_to(h0_ref[0, 0:1, :], (8, BRANCH_W)), jnp.broadcast_to(h0_ref[0, 1:2, :], (8, BRANCH_W)))
    hf, hb = lax.fori_loop(0, nc, chunk_body, init)
    st_ref[0, 0:1, :] = hf[0:1, :]
    st_ref[0, 1:2, :] = hb[0:1, :]

    def combine(c, _):
        rows = pl.ds(pl.multiple_of(c * chunk, chunk), chunk)
        y_ref[rows, :] = (hf_ref[rows, :] + hb_ref[rows, :]) * _gelu_tanh(lg_ref[rows, :])
        return 0

    lax.fori_loop(0, nc, combine, 0)


def _lru_call(lx, lg, w, b, lam, h0, *, l, l_state, n_seq, t, row0):
    sb0 = row0 // t
    chunk = min(t, 256)
    seq = lambda: pl.BlockSpec((t, BRANCH_W), lambda s: (sb0 + s, 0))
    return pl.pallas_call(
        functools.partial(_lru_kernel, t=t, chunk=chunk),
        grid=(n_seq,),
        in_specs=[seq(), seq(), _layer_spec(w, l), _layer_spec(b, l), _layer_spec(lam, l),
                  pl.BlockSpec((1, None, 2, BRANCH_W), lambda s: (s, l_state, 0, 0))],
        out_specs=[pl.BlockSpec((t, BRANCH_W), lambda s: (s, 0)),
                   pl.BlockSpec((1, 2, BRANCH_W), lambda s: (s, 0, 0))],
        out_shape=[jax.ShapeDtypeStruct((n_seq * t, BRANCH_W), F32),
                   jax.ShapeDtypeStruct((n_seq, 2, BRANCH_W), F32)],
        scratch_shapes=[pltpu.VMEM((t, BRANCH_W), F32), pltpu.VMEM((t, BRANCH_W), F32)]
                       + [pltpu.VMEM((chunk, BRANCH_W), F32)] * 4,
        compiler_params=_cparams(("parallel",)),
        name="rglru",
    )(lx, lg, w, b, lam, h0)


def _rows_dot_exact(x, w01):
    r = x.shape[0]
    out = jnp.dot(jnp.concatenate(_split3(x), axis=0), w01, preferred_element_type=F32)
    return out[0:r] + out[r:2 * r] + out[2 * r:3 * r]


def _dot01_exact(m01, y):
    return sum(jnp.dot(m01, part, preferred_element_type=F32) for part in _split3(y))


def _gdn_kernel(q_ref, k_ref, v_ref, ggb_ref, s0_ref, o_ref, sT_ref, s_ref, *, d, reverse, n_par, n_chunk, n_tile):
    ti = pl.program_id(1)
    w4 = GDN_H * GDN_DK
    n_all = n_par * n_chunk
    tt = n_all * CHUNK
    blockmask = (_row_iota((w4, w4)) // CHUNK) == (_lane_iota((w4, w4)) // CHUNK)

    def expand_rows(y):
        yt = jnp.concatenate([y] * GDN_H, axis=0)
        zero = jnp.zeros((), y.dtype)
        parts = [jnp.where(blockmask, yt[:, w4 * u:w4 * (u + 1)], zero) for u in range(y.shape[1] // w4)]
        return parts[0] if len(parts) == 1 else jnp.concatenate(parts, axis=1)

    def heads_dot3(lhs, y):
        r = lhs.shape[0]
        lh, ll = _split(lhs)
        yh, yl = _split(y)
        out = jnp.dot(jnp.concatenate([lh, ll], axis=0), expand_rows(yh), preferred_element_type=F32)
        return out[0:r] + out[r:2 * r] + jnp.dot(lh, expand_rows(yl), preferred_element_type=F32)

    def heads_dot2(lhs, y):
        r = lhs.shape[0]
        lh, ll = _split(lhs)
        out = jnp.dot(jnp.concatenate([lh, ll], axis=0), expand_rows(y.astype(BF16)), preferred_element_type=F32)
        return out[0:r] + out[r:2 * r]

    def heads_dot1(lhs, y):
        return jnp.dot(lhs.astype(BF16), expand_rows(y.astype(BF16)), preferred_element_type=F32)

    @pl.when(ti == 0)
    def _():
        for p in range(n_par):
            s_ref[p] = expand_rows(s0_ref[p, 0])

    ri = _row_iota((CHUNK, w4))
    cj = _lane_iota((CHUNK, w4)) % CHUNK
    incl = (cj >= ri) if reverse else (cj <= ri)
    strict = (cj > ri) if reverse else (cj < ri)
    eye_sbs = (cj == ri)
    rep = eye_sbs.astype(BF16)

    er = _row_iota((128, 2 * w4))
    el = _lane_iota((128, 2 * w4))
    e_gb = (er == jnp.where(el < w4, d * GDN_H, 2 * GDN_H + d * GDN_H) + (el % w4) // CHUNK).astype(BF16)
    gbe = _rows_dot_exact(ggb_ref[...], e_gb)
    g_all = gbe[:, 0:w4]
    beta_all = gbe[:, w4:2 * w4]
    tr = _row_iota((tt, tt))
    tc = _lane_iota((tt, tt))
    same_chunk = (tr // CHUNK) == (tc // CHUNK)
    tri_bd = (same_chunk & ((tc >= tr) if reverse else (tc <= tr))).astype(BF16)
    ones_bd = same_chunk.astype(BF16)
    gc_all = _dot01_exact(tri_bd, g_all)
    diag_sel = (_row_iota((tt, w4)) % CHUNK) == (_lane_iota((tt, w4)) % CHUNK)
    gct_all = _dot01_exact(ones_bd, jnp.where(diag_sel, gc_all, 0.0))

    cs = range(n_all)
    rows = [slice(c * CHUNK, (c + 1) * CHUNK) for c in cs]
    q = [q_ref[r, :] for r in rows]
    k = [k_ref[r, :] for r in rows]
    v = [v_ref[r, :] for r in rows]
    beta = [beta_all[r, :] for r in rows]
    gc = [gc_all[r, :] for r in rows]
    decay = [jnp.where(incl, jnp.exp(jnp.where(incl, gc[c] - gct_all[rows[c], :], 0.0)), 0.0) for c in cs]
    kb = [k[c].astype(BF16) for c in cs]
    w_k = [jnp.where(blockmask,
                     lax.dot_general(kb[c], rep, (((0,), (0,)), ((), ())), preferred_element_type=F32),
                     0.0).astype(BF16) for c in cs]
    kq = [jnp.dot(jnp.concatenate([kb[c], q[c].astype(BF16)], axis=0), w_k[c], preferred_element_type=F32)
          for c in cs]
    qk = [(kq[c][CHUNK:2 * CHUNK] * decay[c]).astype(BF16) for c in cs]
    a = [jnp.where(strict, beta[c] * kq[c][0:CHUNK] * decay[c], 0.0) for c in cs]
    t_inv = [jnp.where(eye_sbs, 1.0, 0.0) - a[c] for c in cs]
    pw = [heads_dot3(a[c], a[c]) for c in cs]
    for stage in range(4):
        mm = heads_dot3 if stage < N_PRECISE else heads_dot1
        both = [mm(jnp.concatenate([t_inv[c], pw[c]], axis=0), pw[c]) for c in cs]
        t_inv = [t_inv[c] + both[c][0:CHUNK] for c in cs]
        pw = [both[c][CHUNK:2 * CHUNK] for c in cs]
    t_inv = [t_inv[c] + heads_dot1(t_inv[c], pw[c]) for c in cs]
    egc = [jnp.exp(gc[c]) for c in cs]
    sol = [heads_dot2(t_inv[c], jnp.concatenate([v[c] * beta[c], k[c] * (beta[c] * egc[c])], axis=1)) for c in cs]
    g_last = [gc[c][0:1, :] if reverse else gc[c][CHUNK - 1:CHUNK, :] for c in cs]
    wq = [jnp.concatenate([sol[c][:, w4:2 * w4], q[c] * egc[c]], axis=0).astype(BF16) for c in cs]
    kdec = [(k[c] * jnp.exp(g_last[c] - gc[c])).astype(BF16) for c in cs]

    for cc in range(n_chunk):
        for p in range(n_par):
            c = p * n_chunk + ((n_chunk - 1 - cc) if reverse else cc)
            s = s_ref[p]
            ws_qs = jnp.dot(wq[c], s.astype(BF16), preferred_element_type=F32)
            v_new = sol[c][:, 0:w4] - ws_qs[0:CHUNK]
            vb = v_new.astype(BF16)
            o_ref[rows[c], :] = (
                ws_qs[CHUNK:2 * CHUNK] + jnp.dot(qk[c], expand_rows(vb), preferred_element_type=F32))
            upd = lax.dot_general(kdec[c], vb, (((0,), (0,)), ((), ())), preferred_element_type=F32)
            s_ref[p] = s * jnp.exp(g_last[c]) + jnp.where(blockmask, upd, 0.0)

    @pl.when(ti == n_tile - 1)
    def _():
        for p in range(n_par):
            s = s_ref[p]
            sT_ref[p, 0] = s[0:64] + s[64:128] + s[128:192] + s[192:256]


def _gdn_call(q, k, v, ggb, s0, *, d, l_state, n_seq, t, row0, tt, n_par):
    reverse = d == 1
    n_tile = t // tt
    assert n_par == 1 or n_tile == 1
    b0 = row0 // (tt * n_par)
    tidx = (lambda i: n_tile - 1 - i) if reverse else (lambda i: i)
    blk = lambda w: pl.BlockSpec((tt * n_par, w), lambda s, i: (b0 + s * n_tile + tidx(i), 0))
    return pl.pallas_call(
        functools.partial(_gdn_kernel, d=d, reverse=reverse, n_par=n_par, n_chunk=tt // CHUNK, n_tile=n_tile),
        grid=(n_seq // n_par, n_tile),
        in_specs=[blk(256), blk(256), blk(256), blk(128),
                  pl.BlockSpec((n_par, None, 1, CHUNK, 256), lambda s, i: (s, l_state, d, 0, 0))],
        out_specs=[pl.BlockSpec((tt * n_par, 256), lambda s, i: (s * n_tile + tidx(i), 0)),
                   pl.BlockSpec((n_par, 1, CHUNK, 256), lambda s, i: (s, 0, 0, 0))],
        out_shape=[jax.ShapeDtypeStruct((n_seq * t, 256), F32),
                   jax.ShapeDtypeStruct((n_seq, 1, CHUNK, 256), F32)],
        scratch_shapes=[pltpu.VMEM((n_par, 256, 256), F32)],
        compiler_params=_cparams(("parallel", "arbitrary")),
        name="gdn_bwd" if reverse else "gdn_fwd",
    )(q, k, v, ggb, s0)


def _mix_mlp_kernel(modrow_ref, xc_ref, xd_ref, mod_ref, g1_ref,
                    oac_ref, oad_ref, obc_ref, obd_ref, ofc_ref, ofd_ref, orc_ref, ord_ref, odc_ref, odd_ref,
                    gz_ref, gng_ref, ones_ref, wm_ref, bm_ref, wb_ref, wo_ref, g2_ref, w1_ref, w2_ref,
                    yc_ref, yd_ref, *, n_ctx_tiles):
    is_ctx = pl.program_id(0) < n_ctx_tiles
    pick = lambda c_ref, d_ref: jnp.where(is_ctx, c_ref[...], d_ref[...])
    x = pick(xc_ref, xd_ref)
    h = _modulate(x, g1_ref[...], mod_ref[0, 0:1, :], mod_ref[0, 1:2, :]).astype(BF16)
    oc = pick(ofc_ref, ofd_ref) + pick(orc_ref, ord_ref)
    ms = _group_sum(oc * oc, ones_ref[...]) * (1.0 / GDN_DK)
    oc = (oc * lax.rsqrt(ms + EPS) * gng_ref[...]) * _silu(gz_ref[...])
    branches = (pick(oac_ref, oad_ref), pick(obc_ref, obd_ref), oc, pick(odc_ref, odd_ref))
    acc = None
    for m in range(N_BRANCH):
        cols = slice(D_MODEL * m, D_MODEL * (m + 1))
        gate = _sigmoid(jnp.dot(h, wm_ref[:, cols], preferred_element_type=F32) + bm_ref[:, cols])
        term = gate * jnp.dot(branches[m].astype(BF16), wb_ref[m], preferred_element_type=F32)
        acc = term if acc is None else acc + term
    x = x + mod_ref[0, 2:3, :] * jnp.dot(acc.astype(BF16), wo_ref[...], preferred_element_type=F32)

    h = _modulate(x, g2_ref[...], mod_ref[0, 3:4, :], mod_ref[0, 4:5, :]).astype(BF16)
    acc = None
    for j in range(D_FF // D_MODEL):
        cols = slice(D_MODEL * j, D_MODEL * (j + 1))
        a = jnp.maximum(jnp.dot(h, w1_ref[:, cols], preferred_element_type=F32), 0.0)
        term = jnp.dot((a * a).astype(BF16), w2_ref[cols, :], preferred_element_type=F32)
        acc = term if acc is None else acc + term
    y = x + mod_ref[0, 5:6, :] * acc

    @pl.when(is_ctx)
    def _():
        yc_ref[...] = y

    @pl.when(jnp.logical_not(is_ctx))
    def _():
        yd_ref[...] = y


def _mix_mlp_call(modrow, x_pair, mod_all, g1, pairs, gz, gng, ones_bd, wm, bm, wb, wo, g2, w1, w2, *, l, tm):
    x_c, x_d = x_pair
    n, d = x_c.shape[0] + x_d.shape[0], x_c.shape[1]
    nct = x_c.shape[0] // tm
    row_tile = lambda w: pl.BlockSpec((tm, w), lambda i, *_: (i, 0))
    pair = lambda w: [pl.BlockSpec((tm, w), lambda i, *_: (jnp.minimum(i, nct - 1), 0)),
                      pl.BlockSpec((tm, w), lambda i, *_: (jnp.maximum(i - nct, 0), 0))]
    pair_specs = [sp for _ in pairs for sp in pair(256)]
    resident = lambda arr: _layer_spec(arr, l, single_buffer=True)
    grid_spec = pltpu.PrefetchScalarGridSpec(
        num_scalar_prefetch=1, grid=(n // tm,),
        in_specs=pair(d) + [pl.BlockSpec((None, 1, 6, d), lambda i, modrow: (l, modrow[i], 0, 0)),
                            _layer_spec(g1, l)] + pair_specs
                 + [row_tile(256), _layer_spec(gng, l), pl.BlockSpec((256, 256), lambda i, *_: (0, 0)),
                    resident(wm), _layer_spec(bm, l), resident(wb), resident(wo),
                    _layer_spec(g2, l), resident(w1), resident(w2)],
        out_specs=pair(d))
    flat = [a for pr in pairs for a in pr]
    return pl.pallas_call(
        functools.partial(_mix_mlp_kernel, n_ctx_tiles=nct), grid_spec=grid_spec,
        out_shape=[jax.ShapeDtypeStruct(x_c.shape, F32), jax.ShapeDtypeStruct(x_d.shape, F32)],
        compiler_params=_cparams(("arbitrary",)),
        name="merge_mlp",
    )(modrow, x_c, x_d, mod_all, g1, *flat, gz, gng, ones_bd, wm, bm, wb, wo, g2, w1, w2)


def _rope_tables(t, tm):
    n_freq = HEAD_DIM // 4
    inv = np.float32(ROPE_BASE) ** (-np.arange(n_freq, dtype=np.float32) / np.float32(n_freq))
    pos = np.arange(t)
    row = (pos // GRID_W).astype(np.float32)[:, None]
    col = (pos % GRID_W).astype(np.float32)[:, None]
    ar = (row * inv).astype(np.float32)
    ac = (col * inv).astype(np.float32)
    cos64 = np.concatenate([np.cos(ar), np.cos(ar), np.cos(ac), np.cos(ac)], axis=1)
    sin64 = np.concatenate([-np.sin(ar), np.sin(ar), -np.sin(ac), np.sin(ac)], axis=1)
    cos = np.concatenate([np.ones((tm, 128), np.float32), np.tile(cos64, (1, 2)).astype(np.float32)], axis=0)
    sin = np.concatenate([np.zeros((tm, 128), np.float32), np.tile(sin64, (1, 2)).astype(np.float32)], axis=0)
    return jnp.asarray(cos), jnp.asarray(sin)


def _tile_meta(n_ctx, n_seq_dec, t_ctx, t_dec, tm):
    nct = n_ctx // tm
    per = t_dec // tm
    ndt = n_seq_dec * per
    idx = np.arange(nct + ndt)
    dec = idx >= nct
    di = np.maximum(idx - nct, 0)
    modrow = np.where(dec, 1 + di // per, 0)
    ropeblk = np.where(dec, 1 + di % per, 0)
    tiles_ctx = max(t_ctx // tm, 1)
    isstart = np.where(dec, di % per == 0, idx % tiles_ctx == 0)
    isend = np.where(dec, di % per == per - 1, idx % tiles_ctx == tiles_ctx - 1)
    as_i32 = lambda a: jnp.asarray(a.astype(np.int32))
    return as_i32(modrow), as_i32(ropeblk), as_i32(isstart), as_i32(isend)


def _block_diag_gates(w):
    depth, two, n, c, _ = w.shape
    eye = jnp.eye(n, dtype=w.dtype)
    return jnp.einsum("ldnij,nm->ldnimj", w, eye).reshape(depth, two, n * c, n * c)


def kernel(x_prompt, x_sample, c, cache_a_k, cache_a_v, cache_d_k, cache_d_v, state_lru, state_gdn, c_ctx, mod_w, mod_b, norm1_g, norm2_g, w_in, a_qn_g, a_kn_g, a_sink, lru_conv_w, lru_conv_b, lru_wr, lru_br, lru_wi, lru_bi, lru_lam, gdn_conv_w, gdn_a_log, gdn_dt_bias, gdn_norm_g, d_qn_g, d_kn_g, w_branch, w_merge, b_merge, w_out, mlp_w1, mlp_w2):
    batch, seq, d = x_prompt.shape
    dec_batch, dec_seq, _ = x_sample.shape
    depth = mod_w.shape[0]
    past = cache_a_k.shape[2]
    n_ctx = batch * seq
    n_dec = dec_batch * dec_seq
    tm = 256
    gdn_par = 2 if batch % 2 == 0 else 1
    gdn_tt = 512 if dec_seq % 512 == 0 else 256
    assert d == D_MODEL and seq % tm == 0 and dec_seq % tm == 0 and tm % seq == 0
    assert dec_batch + 1 <= 8 and n_ctx % dec_seq == 0

    cond8 = jnp.zeros((8, d), F32).at[0].set(c_ctx).at[1:1 + dec_batch].set(c)
    mod_all = _mod_call(cond8, mod_w, mod_b).reshape(depth, 8, 6, d)

    meta = _tile_meta(n_ctx, dec_batch, seq, dec_seq, tm)
    cos_t, sin_t = _rope_tables(dec_seq, tm)
    lane = np.arange(256)
    ones_bd = jnp.asarray((lane[:, None] // HEAD_DIM == lane[None, :] // HEAD_DIM).astype(np.float32)).astype(BF16)

    w_packed = jnp.concatenate([w_in[:, :, :2064], jnp.zeros((depth, d, GDN_PAD), F32), w_in[:, :, 2064:]],
                               axis=2).astype(BF16)
    pad128 = lambda v: jnp.pad(v.reshape(depth, 1, -1), ((0, 0), (0, 0), (0, 128 - v[0].size)))
    vecs = (
        (jnp.tile(a_qn_g, (1, 4)) * Q_SCALE)[:, None, :],
        jnp.tile(a_kn_g, (1, 2))[:, None, :],
        (jnp.tile(d_qn_g, (1, 4)) * Q_SCALE)[:, None, :],
        jnp.tile(d_kn_g, (1, 2))[:, None, :],
        lru_conv_w, lru_conv_b[:, None, :], gdn_conv_w,
        pad128(gdn_a_log), pad128(gdn_dt_bias),
    )
    g1 = norm1_g[:, None, :]
    g2 = norm2_g[:, None, :]
    wr_bd = _block_diag_gates(lru_wr)
    wi_bd = _block_diag_gates(lru_wi)
    w_lru = jnp.concatenate([wr_bd[:, 0], wi_bd[:, 0], wr_bd[:, 1], wi_bd[:, 1]], axis=-1).astype(BF16)
    b_lru = jnp.concatenate([lru_br[:, 0], lru_bi[:, 0], lru_br[:, 1], lru_bi[:, 1]], axis=-1)[:, None, :]
    gng = jnp.tile(gdn_norm_g, (1, 4))[:, None, :]
    wm = w_merge.astype(BF16)
    bm = b_merge[:, None, :]
    wb = w_branch.astype(BF16)
    wo = w_out.astype(BF16)
    w1 = mlp_w1.astype(BF16)
    w2 = mlp_w2.astype(BF16)
    sink = a_sink.reshape(-1)
    caches = [t.reshape(dec_batch, depth, past, 2 * HEAD_DIM) for t in (cache_a_k, cache_a_v, cache_d_k, cache_d_v)]
    s0_dec = state_gdn.transpose(0, 1, 2, 4, 3, 5).reshape(dec_batch, depth, 2, CHUNK, 256)
    zeros_lru = jnp.zeros((batch, 1, 2, BRANCH_W), F32)
    zeros_gdn = jnp.zeros((batch, 1, 2, CHUNK, 256), F32)

    x = (x_prompt.reshape(n_ctx, d), x_sample.reshape(n_dec, d))

    kv_ctx, lru_states, gdn_states = None, [], []
    for l in range(depth):
        (qa, ka, va, lx, lg, gq, gk, gv, gz, ggb, qd, kd, vd), kv_ctx = _inproj_call(
            x, meta, mod_all, g1, w_packed, cos_t, sin_t, vecs, ones_bd, kv_ctx, l=l, depth=depth, tm=tm)

        oa_c, od_c = _attn_ctx_call(sink, qa, ka, va, qd, kd, vd, l=l, n_seq=batch, t=seq)
        oa_d = _attn_band_call(sink, qa, ka, va, caches[0], caches[1], l=l, n_seq=dec_batch, t=dec_seq,
                               row0=n_ctx, tq=256)
        od_d = _attn_full_call(qd, kd, vd, caches[2], caches[3], l=l, n_seq=dec_batch, t=dec_seq,
                               row0=n_ctx, tq=256)

        ob_c, st_c = _lru_call(lx, lg, w_lru, b_lru, lru_lam, zeros_lru, l=l, l_state=0, n_seq=batch, t=seq,
                               row0=0)
        ob_d, _ = _lru_call(lx, lg, w_lru, b_lru, lru_lam, state_lru, l=l, l_state=l, n_seq=dec_batch,
                            t=dec_seq, row0=n_ctx)

        oc_pairs, sts = [], []
        for dd in range(2):
            o_c, s_c = _gdn_call(gq, gk, gv, ggb, zeros_gdn, d=dd, l_state=0, n_seq=batch, t=seq, row0=0,
                                 tt=seq, n_par=gdn_par)
            o_d, _ = _gdn_call(gq, gk, gv, ggb, s0_dec, d=dd, l_state=l, n_seq=dec_batch, t=dec_seq,
                               row0=n_ctx, tt=gdn_tt, n_par=1)
            oc_pairs.append((o_c, o_d))
            sts.append(s_c)

        x = _mix_mlp_call(meta[0], x, mod_all, g1,
                          [(oa_c, oa_d), (ob_c, ob_d), oc_pairs[0], oc_pairs[1], (od_c, od_d)],
                          gz, gng, ones_bd, wm, bm, wb, wo, g2, w1, w2, l=l, tm=tm)

        lru_states.append(st_c)
        gdn_states.append(jnp.concatenate(sts, axis=1))

    y_prompt = x[0].reshape(batch, seq, d)
    y_sample = x[1].reshape(dec_batch, dec_seq, d)
    new_kv = [t.reshape(batch, depth, seq, 2, HEAD_DIM) for t in kv_ctx]
    new_lru = jnp.stack(lru_states, axis=1)
    new_gdn = (jnp.stack(gdn_states, axis=1).reshape(batch, depth, 2, GDN_DK, GDN_H, GDN_DK)
               .transpose(0, 1, 2, 4, 3, 5))
    return (y_prompt, y_sample, new_kv[0], new_kv[1], new_kv[2], new_kv[3], new_lru, new_gdn)
```

```python
import functools

import numpy as np
import jax
import jax.numpy as jnp
from jax import lax
from jax.experimental import pallas as pl
from jax.experimental.pallas import tpu as pltpu

F32 = jnp.float32
BF16 = jnp.bfloat16
HIGHEST = lax.Precision.HIGHEST

D_MODEL = 1024
HEAD_DIM = 64
BRANCH_W = 256
N_BRANCH = 4
GRID_W = 64
WINDOW = 128
LRU_C = 8.0
CONV_W = 4
CONV_LEFT = 2
GDN_H = 4
GDN_DK = 64
CHUNK = 64
D_FF = 4 * D_MODEL
ROPE_BASE = 10000.0
EPS = 1e-6
NEG = -1e30
LOG2E = 1.4426950408889634
Q_SCALE = HEAD_DIM ** -0.5 * LOG2E
GDN_PAD = 112
IN_COLS_PACKED = 2688
HALO = 8
N_PRECISE = 4

V7X_VMEM_LIMIT = 56 * 1024 * 1024


def _cparams(sem, vmem=V7X_VMEM_LIMIT):
    return pltpu.CompilerParams(dimension_semantics=sem, vmem_limit_bytes=vmem)


def _layer_spec(arr, l, single_buffer=False):
    nd = arr.ndim - 1
    mode = dict(pipeline_mode=pl.Buffered(1)) if single_buffer else {}
    return pl.BlockSpec((None,) + arr.shape[1:], lambda *_: (l,) + (0,) * nd, **mode)


def _bdot(a, b):
    return jnp.dot(a.astype(BF16), b.astype(BF16), preferred_element_type=F32)


def _split(x):
    hi = x.astype(BF16)
    lo = (x - hi.astype(F32)).astype(BF16)
    return hi, lo


def _split3(x):
    hi = x.astype(BF16)
    r = x - hi.astype(F32)
    mid = r.astype(BF16)
    lo = (r - mid.astype(F32)).astype(BF16)
    return hi, mid, lo


def _group_sum(x, ones_bd):
    return jnp.dot(x.astype(BF16), ones_bd, preferred_element_type=F32)


def _sigmoid(x):
    return 0.5 * jnp.tanh(0.5 * x) + 0.5


def _silu(x):
    return x * _sigmoid(x)


def _softplus(x):
    return jnp.maximum(x, 0.0) + jnp.log1p(jnp.exp(-jnp.abs(x)))


def _modulate(x, g, shift, scale):
    ms = jnp.mean(x * x, axis=-1, keepdims=True)
    return (x * lax.rsqrt(ms + EPS)) * (g * (1.0 + scale)) + shift


def _lane_iota(shape):
    return lax.broadcasted_iota(jnp.int32, shape, len(shape) - 1)


def _row_iota(shape):
    return lax.broadcasted_iota(jnp.int32, shape, len(shape) - 2)


def _mod_kernel(cond_ref, w_ref, b_ref, o_ref):
    hi, lo = _split(_silu(cond_ref[...]))
    w = w_ref[0].astype(BF16)
    o_ref[0] = (jnp.dot(hi, w, preferred_element_type=F32) + jnp.dot(lo, w, preferred_element_type=F32)
                + b_ref[0])


def _mod_call(cond8, mod_w, mod_b):
    depth, d, n = mod_w.shape
    tn = 1536
    return pl.pallas_call(
        _mod_kernel,
        grid=(depth, n // tn),
        in_specs=[
            pl.BlockSpec((8, d), lambda l, j: (0, 0)),
            pl.BlockSpec((1, d, tn), lambda l, j: (l, 0, j)),
            pl.BlockSpec((1, 1, tn), lambda l, j: (l, 0, j)),
        ],
        out_specs=pl.BlockSpec((1, 8, tn), lambda l, j: (l, 0, j)),
        out_shape=jax.ShapeDtypeStruct((depth, 8, n), F32),
        compiler_params=_cparams(("parallel", "parallel")),
        name="mod_vectors",
    )(cond8, mod_w, mod_b.reshape(depth, 1, n))


def _rope(x, cos, sin):
    outs = []
    for j in range(x.shape[1] // 128):
        xb = x[:, 128 * j:128 * (j + 1)]
        lane = _lane_iota(xb.shape)
        sw = jnp.where((lane & 16) == 0, pltpu.roll(xb, 112, 1), pltpu.roll(xb, 16, 1))
        outs.append(xb * cos + sw * sin)
    return outs[0] if len(outs) == 1 else jnp.concatenate(outs, axis=1)


def _centred_conv(g, w, tm):
    rows = g.shape[0]
    acc = None
    for j in range(CONV_W):
        sh = (CONV_LEFT - j) % rows
        gj = g if sh == 0 else pltpu.roll(g, sh, 0)
        term = gj[HALO:HALO + tm] * w[j:j + 1, :]
        acc = term if acc is None else acc + term
    return acc


def _inproj_kernel(modrow_ref, ropeblk_ref, isstart_ref, isend_ref,
                   xpc_ref, xc_ref, xnc_ref, xpd_ref, xd_ref, xnd_ref, mod_ref, g1_ref, w_ref, cos_ref, sin_ref,
                   aqg_ref, akg_ref, dqg_ref, dkg_ref, lcw_ref, lcb_ref, gcw_ref,
                   alog_ref, dtb_ref, ones_ref, *rest, tm, n_ctx_tiles, n_alias):
    (qa_ref, ka_ref, va_ref, lx_ref, lg_ref, gq_ref, gk_ref, gv_ref, gz_ref, ggb_ref,
     qd_ref, kd_ref, vd_ref, kac_ref, vac_ref, kdc_ref, vdc_ref) = rest[n_alias:]
    i = pl.program_id(0)
    is_ctx = i < n_ctx_tiles
    pick = lambda c_ref, d_ref: jnp.where(is_ctx, c_ref[...], d_ref[...])
    xfull = jnp.concatenate([pick(xpc_ref, xpd_ref), pick(xc_ref, xd_ref), pick(xnc_ref, xnd_ref)],
                            axis=0)
    h = _modulate(xfull, g1_ref[...], mod_ref[0, 0:1, :], mod_ref[0, 1:2, :])
    p = jnp.dot(h.astype(BF16), w_ref[...], preferred_element_type=F32)
    hb, hm = slice(0, tm + 2 * HALO), slice(HALO, HALO + tm)
    proj = lambda rows, c0, c1: p[rows, c0:c1]

    ones_bd = ones_ref[...]
    cos = cos_ref[...]
    sin = sin_ref[...]

    pa = proj(hm, 0, 512)
    pd = proj(hm, 2176, 2688)
    va_ref[...] = pa[:, 384:512]
    vd_ref[...] = pd[:, 384:512]

    row = _row_iota((tm + 2 * HALO, 1))
    keep = jnp.logical_and(jnp.logical_or(row >= HALO, isstart_ref[i] == 0),
                           jnp.logical_or(row < HALO + tm, isend_ref[i] == 0))

    lxg = jnp.where(keep, proj(hb, 512, 768), 0.0)
    lx_ref[...] = _centred_conv(lxg, lcw_ref[...], tm) + lcb_ref[...]
    lg_ref[...] = proj(hm, 768, 1024)

    qkv = jnp.where(keep, proj(hb, 1024, 1792), 0.0)
    qkv = _silu(_centred_conv(qkv, gcw_ref[...], tm))
    gq = qkv[:, 0:256]
    gk = qkv[:, 256:512]
    gv_ref[...] = qkv[:, 512:768]
    pz = proj(hm, 1792, 2176)
    gz_ref[...] = pz[:, 0:256]
    ab = pz[:, 256:384]
    g = -jnp.exp(alog_ref[...]) * _softplus(ab + dtb_ref[...])
    lane = _lane_iota(ab.shape)
    ggb_ref[...] = jnp.where(lane < 2 * GDN_H, g, _sigmoid(ab))

    qa_raw, ka_raw, qd_raw, kd_raw = pa[:, 0:256], pa[:, 256:384], pd[:, 0:256], pd[:, 256:384]
    ss256 = _group_sum(jnp.concatenate([qa_raw * qa_raw, qd_raw * qd_raw, gq * gq, gk * gk], axis=0), ones_bd)
    ss128 = _group_sum(jnp.concatenate([ka_raw * ka_raw, kd_raw * kd_raw], axis=0), ones_bd[:128, :128])
    rms = lambda x, ssq, gain: x * lax.rsqrt(ssq * (1.0 / HEAD_DIM) + EPS) * gain

    qa_ref[...] = _rope(rms(qa_raw, ss256[0:tm], aqg_ref[...]), cos, sin)
    qd_ref[...] = _rope(rms(qd_raw, ss256[tm:2 * tm], dqg_ref[...]), cos, sin)
    ka = _rope(rms(ka_raw, ss128[0:tm], akg_ref[...]), cos, sin)
    kd = _rope(rms(kd_raw, ss128[tm:2 * tm], dkg_ref[...]), cos, sin)
    ka_ref[...] = ka
    kd_ref[...] = kd
    gq_ref[...] = gq * lax.rsqrt(ss256[2 * tm:3 * tm] + EPS) * (GDN_DK ** -0.5)
    gk_ref[...] = gk * lax.rsqrt(ss256[3 * tm:4 * tm] + EPS)

    @pl.when(is_ctx)
    def _():
        for ref, val in ((kac_ref, ka), (vac_ref, pa[:, 384:512]), (kdc_ref, kd), (vdc_ref, pd[:, 384:512])):
            if n_alias:
                ref[...] = val
            else:
                for slot in range(ref.shape[0]):
                    ref[slot] = val


def _x_pair_specs(x_pair, tm):
    x_c, x_d = x_pair
    d = x_c.shape[1]
    nct = x_c.shape[0] // tm
    hb = tm // HALO
    specs = []
    for arr, first in ((x_c, 0), (x_d, nct)):
        nblk = arr.shape[0] // tm
        last_hb = arr.shape[0] // HALO - 1
        tile = lambda i, first=first, nblk=nblk: jnp.clip(i - first, 0, nblk - 1)
        specs += [
            pl.BlockSpec((HALO, d), lambda i, *_, t=tile, m=last_hb: (jnp.clip(t(i) * hb - 1, 0, m), 0)),
            pl.BlockSpec((tm, d), lambda i, *_, t=tile: (t(i), 0)),
            pl.BlockSpec((HALO, d), lambda i, *_, t=tile, m=last_hb: (jnp.clip((t(i) + 1) * hb, 0, m), 0)),
        ]
    return specs


def _inproj_call(x_pair, meta, mod_all, g1, w_packed, cos_t, sin_t, vecs, ones_bd, kv_prev, *, l, depth, tm):
    x_c, x_d = x_pair
    n, d = x_c.shape[0] + x_d.shape[0], x_c.shape[1]
    nt = n // tm
    nct = x_c.shape[0] // tm
    row_tile = lambda w: pl.BlockSpec((tm, w), lambda i, *_: (i, 0))
    alias_in = [] if kv_prev is None else list(kv_prev)
    if alias_in:
        kv_spec = pl.BlockSpec((None, None, tm, 128), lambda i, *_: (jnp.minimum(i, nct - 1), l, 0, 0))
    else:
        kv_spec = pl.BlockSpec((None, depth, tm, 128), lambda i, *_: (jnp.minimum(i, nct - 1), 0, 0, 0))
    kv_shape = jax.ShapeDtypeStruct((nct, depth, tm, 128), F32)
    in_specs = _x_pair_specs(x_pair, tm) + [
        pl.BlockSpec((None, 1, 6, d), lambda i, modrow, *_: (l, modrow[i], 0, 0)),
        _layer_spec(g1, l),
        _layer_spec(w_packed, l),
        pl.BlockSpec((tm, 128), lambda i, modrow, ropeblk, *_: (ropeblk[i], 0)),
        pl.BlockSpec((tm, 128), lambda i, modrow, ropeblk, *_: (ropeblk[i], 0)),
    ] + [_layer_spec(v, l) for v in vecs] + [pl.BlockSpec((256, 256), lambda i, *_: (0, 0))]
    in_specs += [pl.BlockSpec(memory_space=pl.ANY)] * len(alias_in)
    widths = (256, 128, 128, 256, 256, 256, 256, 256, 256, 128, 256, 128, 128)
    grid_spec = pltpu.PrefetchScalarGridSpec(
        num_scalar_prefetch=4,
        grid=(nt,),
        in_specs=in_specs,
        out_specs=[row_tile(w) for w in widths] + [kv_spec] * 4,
    )
    operands = (*meta, x_c, x_c, x_c, x_d, x_d, x_d, mod_all, g1, w_packed, cos_t, sin_t, *vecs, ones_bd)
    outs = pl.pallas_call(
        functools.partial(_inproj_kernel, tm=tm, n_ctx_tiles=nct, n_alias=len(alias_in)),
        grid_spec=grid_spec,
        out_shape=[jax.ShapeDtypeStruct((n, w), F32) for w in widths] + [kv_shape] * 4,
        input_output_aliases={len(operands) + u: len(widths) + u for u in range(len(alias_in))},
        compiler_params=_cparams(("arbitrary",)),
        name="in_projection",
    )(*operands, *alias_in)
    return outs[:len(widths)], outs[len(widths):]


def _head_q(q, j, g):
    qj = q[:, 128 * j:128 * (j + 1)]
    lane = _lane_iota(qj.shape)
    sel = (lane < HEAD_DIM) if g == 0 else (lane >= HEAD_DIM)
    return jnp.where(sel, qj, 0.0).astype(BF16)


def _place_heads(res, j):
    r0 = res[0] if j == 0 else pltpu.roll(res[0], HEAD_DIM, 1)
    r1 = res[1] if j == 1 else pltpu.roll(res[1], HEAD_DIM, 1)
    lane = _lane_iota(r0.shape)
    return jnp.where(lane < HEAD_DIM, r0, r1)


def _attend_many(problems, order="staged"):
    work = []
    for pi, (q, ks, vs, masks, sinks, ks_sw) in enumerate(problems):
        if ks_sw is None:
            ks_sw = [pltpu.roll(k, HEAD_DIM, 1) for k in ks]
        work += [(pi, j, g, q, ks, vs, masks, sinks, ks_sw) for j in range(2) for g in range(2)]
    scores, probs, res = {}, {}, {}

    def score(w):
        pi, j, g, q, ks, vs, masks, sinks, ks_sw = w
        qm = _head_q(q, j, g)
        ss = []
        for k, ksw, mk in zip(ks, ks_sw, masks):
            s = lax.dot_general(qm, k if g == j else ksw, (((1,), (1,)), ((), ())), preferred_element_type=F32)
            ss.append(s if mk is None else jnp.where(mk, s, NEG))
        scores[w[:3]] = ss

    def softmax_num(w):
        pi, j, g, q, ks, vs, masks, sinks, ks_sw = w
        ss = scores[w[:3]]
        m = ss[0].max(axis=-1, keepdims=True)
        for s in ss[1:]:
            m = jnp.maximum(m, s.max(axis=-1, keepdims=True))
        if sinks is not None:
            m = jnp.maximum(m, sinks[2 * j + g])
        es = [jnp.exp2(s - m) for s in ss]
        den = sum(e.sum(axis=-1, keepdims=True) for e in es)
        if sinks is not None:
            den = den + jnp.exp2(sinks[2 * j + g] - m)
        probs[w[:3]] = ([e.astype(BF16) for e in es], den)

    def values(w):
        es, den = probs[w[:3]]
        res[w[:3]] = sum(jnp.dot(e, v, preferred_element_type=F32) for e, v in zip(es, w[5])) / den

    if order == "staged":
        for step in (score, softmax_num, values):
            for w in work:
                step(w)
    else:
        score(work[0])
        for prev, nxt in zip(work, work[1:] + [None]):
            if nxt is not None:
                score(nxt)
            softmax_num(prev)
            values(prev)
    return [jnp.concatenate([_place_heads([res[(pi, j, 0)], res[(pi, j, 1)]], j) for j in range(2)], axis=1)
            for pi in range(len(problems))]


def _attend(q, ks, vs, masks, sinks, ks_sw=None, order="staged"):
    return _attend_many([(q, ks, vs, masks, sinks, ks_sw)], order)[0]


def _bf(x):
    return x.astype(BF16)


def _attn_ctx_kernel(sink_ref, qa_ref, ka_ref, va_ref, qd_ref, kd_ref, vd_ref, oa_ref, od_ref, *, l, t, n_par):
    sinks = [sink_ref[4 * l + u] * LOG2E for u in range(4)]
    problems = []
    for p in range(n_par):
        r = slice(p * t, (p + 1) * t)
        problems.append((qa_ref[r, :], [_bf(ka_ref[r, :])], [_bf(va_ref[r, :])], [None], sinks, None))
        problems.append((qd_ref[r, :], [_bf(kd_ref[r, :])], [_bf(vd_ref[r, :])], [None], None, None))
    outs = _attend_many(problems)
    for p in range(n_par):
        oa_ref[p * t:(p + 1) * t, :] = outs[2 * p]
        od_ref[p * t:(p + 1) * t, :] = outs[2 * p + 1]


def _attn_ctx_call(sink, qa, ka, va, qd, kd, vd, *, l, n_seq, t):
    n_par = 2 if n_seq % 2 == 0 else 1
    blk = lambda w: pl.BlockSpec((n_par * t, w), lambda b, *_: (b, 0))
    grid_spec = pltpu.PrefetchScalarGridSpec(
        num_scalar_prefetch=1, grid=(n_seq // n_par,),
        in_specs=[blk(256), blk(128), blk(128), blk(256), blk(128), blk(128)],
        out_specs=[blk(256), blk(256)])
    return pl.pallas_call(
        functools.partial(_attn_ctx_kernel, l=l, t=t, n_par=n_par), grid_spec=grid_spec,
        out_shape=[jax.ShapeDtypeStruct((n_seq * t, 256), F32)] * 2,
        compiler_params=_cparams(("parallel",)),
        name="attn_context",
    )(sink, qa, ka, va, qd, kd, vd)


def _attn_band_kernel(sink_ref, q_ref, k_ref, v_ref, ck_ref, cv_ref, o_ref, *, l, tq, t):
    i = pl.program_id(1)
    start = pl.multiple_of(i * tq, tq)
    prev = pl.multiple_of(jnp.maximum(start - WINDOW, 0), WINDOW)
    nxt = pl.multiple_of(jnp.minimum(start + tq, t - WINDOW), WINDOW)
    k_loc = jnp.concatenate([k_ref[pl.ds(prev, WINDOW), :], k_ref[pl.ds(start, tq), :],
                             k_ref[pl.ds(nxt, WINDOW), :]], axis=0)
    v_loc = jnp.concatenate([v_ref[pl.ds(prev, WINDOW), :], v_ref[pl.ds(start, tq), :],
                             v_ref[pl.ds(nxt, WINDOW), :]], axis=0)
    shape = (tq, tq + 2 * WINDOW)
    qpos = start + _row_iota(shape)
    kpos = start - WINDOW + _lane_iota(shape)
    mask = (kpos >= 0) & (kpos < t) & (jnp.abs(qpos - kpos) <= WINDOW)
    sinks = [sink_ref[4 * l + u] * LOG2E for u in range(4)]
    o_ref[...] = _attend(q_ref[...], [_bf(ck_ref[0]), _bf(k_loc)], [_bf(cv_ref[0]), _bf(v_loc)],
                         [None, mask], sinks)


def _cache_spec(c, l):
    return pl.BlockSpec((1, None) + c.shape[2:], lambda b, i, *_: (b, l, 0, 0))


def _attn_band_call(sink, q, k, v, ck, cv, *, l, n_seq, t, row0, tq):
    nq = t // tq
    qb0 = row0 // tq
    sb0 = row0 // t
    grid_spec = pltpu.PrefetchScalarGridSpec(
        num_scalar_prefetch=1, grid=(n_seq, nq),
        in_specs=[
            pl.BlockSpec((tq, 256), lambda b, i, *_: (qb0 + b * nq + i, 0)),
            pl.BlockSpec((t, 128), lambda b, i, *_: (sb0 + b, 0)),
            pl.BlockSpec((t, 128), lambda b, i, *_: (sb0 + b, 0)),
            _cache_spec(ck, l), _cache_spec(cv, l),
        ],
        out_specs=pl.BlockSpec((tq, 256), lambda b, i, *_: (b * nq + i, 0)))
    return pl.pallas_call(
        functools.partial(_attn_band_kernel, l=l, tq=tq, t=t), grid_spec=grid_spec,
        out_shape=jax.ShapeDtypeStruct((n_seq * t, 256), F32),
        compiler_params=_cparams(("parallel", "parallel")),
        name="attn_banded",
    )(sink, q, k, v, ck, cv)


def _attn_full_kernel(q_ref, k_ref, v_ref, ck_ref, cv_ref, o_ref, ks_ref, vs_ref, *, past, t):
    @pl.when(pl.program_id(1) == 0)
    def _():
        kc = _bf(ck_ref[0])
        kl = _bf(k_ref[...])
        ks_ref[0, 0:past, :] = kc
        ks_ref[0, past:past + t, :] = kl
        ks_ref[1, 0:past, :] = pltpu.roll(kc, HEAD_DIM, 1)
        ks_ref[1, past:past + t, :] = pltpu.roll(kl, HEAD_DIM, 1)
        vs_ref[0:past, :] = _bf(cv_ref[0])
        vs_ref[past:past + t, :] = _bf(v_ref[...])

    o_ref[...] = _attend(q_ref[...], [ks_ref[0]], [vs_ref[...]], [None], None, ks_sw=[ks_ref[1]], order="skewed")


def _attn_full_call(q, k, v, ck, cv, *, l, n_seq, t, row0, tq):
    nq = t // tq
    qb0 = row0 // tq
    sb0 = row0 // t
    past = ck.shape[2]
    assert past % 16 == 0
    return pl.pallas_call(
        functools.partial(_attn_full_kernel, past=past, t=t), grid=(n_seq, nq),
        scratch_shapes=[pltpu.VMEM((2, past + t, 128), BF16), pltpu.VMEM((past + t, 128), BF16)],
        in_specs=[
            pl.BlockSpec((tq, 256), lambda b, i: (qb0 + b * nq + i, 0)),
            pl.BlockSpec((t, 128), lambda b, i: (sb0 + b, 0)),
            pl.BlockSpec((t, 128), lambda b, i: (sb0 + b, 0)),
            _cache_spec(ck, l), _cache_spec(cv, l),
        ],
        out_specs=pl.BlockSpec((tq, 256), lambda b, i: (b * nq + i, 0)),
        out_shape=jax.ShapeDtypeStruct((n_seq * t, 256), F32),
        compiler_params=_cparams(("parallel", "arbitrary")),
        name="attn_full",
    )(q, k, v, ck, cv)


def _gelu_tanh(x):
    return 0.5 * x * (1.0 + jnp.tanh(0.7978845608028654 * (x + 0.044715 * (x * x * x))))


def _lru_kernel(lx_ref, lg_ref, w_ref, b_ref, lam_ref, h0_ref, y_ref, st_ref,
                hf_ref, hb_ref, af_ref, bf_ref, ab_ref, bb_ref, *, t, chunk):
    nc = t // chunk
    ng = chunk // 8
    sp = _softplus(-lam_ref[...])
    row8 = _row_iota((8, BRANCH_W))

    def gates(xc, d, a_ref, b2_ref):
        pre = _bdot(xc, w_ref[:, 512 * d:512 * (d + 1)]) + b_ref[:, 512 * d:512 * (d + 1)]
        r = _sigmoid(pre[:, 0:256])
        ig = _sigmoid(pre[:, 256:512])
        a = jnp.exp((-LRU_C) * r * sp[d:d + 1, :])
        a_ref[...] = a
        om = 1.0 - a * a
        b2_ref[...] = jnp.where(om > 0.0, om * lax.rsqrt(om), 0.0) * (ig * xc)

    def local_scan(a8, b8, reverse):
        for s in (1, 2, 4):
            sh = (8 - s) if reverse else s
            m = (row8 < 8 - s) if reverse else (row8 >= s)
            a_sh = pltpu.roll(a8, sh, 0)
            b_sh = pltpu.roll(b8, sh, 0)
            b8 = jnp.where(m, a8 * b_sh + b8, b8)
            a8 = jnp.where(m, a8 * a_sh, a8)
        return a8, b8

    def chunk_body(cc, carry):
        rf = pl.multiple_of(cc * chunk, chunk)
        rb = pl.multiple_of((nc - 1 - cc) * chunk, chunk)
        gates(lx_ref[pl.ds(rf, chunk), :], 0, af_ref, bf_ref)
        gates(lx_ref[pl.ds(rb, chunk), :], 1, ab_ref, bb_ref)

        def grp(gg, c2):
            hf, hb = c2
            gf = pl.multiple_of(gg * 8, 8)
            gb = pl.multiple_of((ng - 1 - gg) * 8, 8)
            a8, b8 = local_scan(af_ref[pl.ds(gf, 8), :], bf_ref[pl.ds(gf, 8), :], False)
            h8 = b8 + a8 * hf
            hf_ref[pl.ds(rf + gf, 8), :] = h8
            a8, b8 = local_scan(ab_ref[pl.ds(gb, 8), :], bb_ref[pl.ds(gb, 8), :], True)
            g8 = b8 + a8 * hb
            hb_ref[pl.ds(rb + gb, 8), :] = g8
            return (jnp.broadcast_to(h8[7:8, :], (8, BRANCH_W)), jnp.broadcast_to(g8[0:1, :], (8, BRANCH_W)))

        return lax.fori_loop(0, ng, grp, carry, unroll=4)

    init = (jnp.broadcast_to(h0_ref[0, 0:1, :], (8, BRANCH_W)), jnp.broadcast_to(h0_ref[0, 1:2, :], (8, BRANCH_W)))
    hf, hb = lax.fori_loop(0, nc, chunk_body, init)
    st_ref[0, 0:1, :] = hf[0:1, :]
    st_ref[0, 1:2, :] = hb[0:1, :]

    def combine(c, _):
        rows = pl.ds(pl.multiple_of(c * chunk, chunk), chunk)
        y_ref[rows, :] = (hf_ref[rows, :] + hb_ref[rows, :]) * _gelu_tanh(lg_ref[rows, :])
        return 0

    lax.fori_loop(0, nc, combine, 0)


def _lru_call(lx, lg, w, b, lam, h0, *, l, l_state, n_seq, t, row0):
    sb0 = row0 // t
    chunk = min(t, 256)
    seq = lambda: pl.BlockSpec((t, BRANCH_W), lambda s: (sb0 + s, 0))
    return pl.pallas_call(
        functools.partial(_lru_kernel, t=t, chunk=chunk),
        grid=(n_seq,),
        in_specs=[seq(), seq(), _layer_spec(w, l), _layer_spec(b, l), _layer_spec(lam, l),
                  pl.BlockSpec((1, None, 2, BRANCH_W), lambda s: (s, l_state, 0, 0))],
        out_specs=[pl.BlockSpec((t, BRANCH_W), lambda s: (s, 0)),
                   pl.BlockSpec((1, 2, BRANCH_W), lambda s: (s, 0, 0))],
        out_shape=[jax.ShapeDtypeStruct((n_seq * t, BRANCH_W), F32),
                   jax.ShapeDtypeStruct((n_seq, 2, BRANCH_W), F32)],
        scratch_shapes=[pltpu.VMEM((t, BRANCH_W), F32), pltpu.VMEM((t, BRANCH_W), F32)]
                       + [pltpu.VMEM((chunk, BRANCH_W), F32)] * 4,
        compiler_params=_cparams(("parallel",)),
        name="rglru",
    )(lx, lg, w, b, lam, h0)


def _rows_dot_exact(x, w01, terms=3):
    r = x.shape[0]
    parts = _split3(x)[:terms] if terms == 3 else _split(x)
    out = jnp.dot(jnp.concatenate(parts, axis=0), w01, preferred_element_type=F32)
    return sum(out[u * r:(u + 1) * r] for u in range(terms))


def _dot01_exact(m01, y):
    return sum(jnp.dot(m01, part, preferred_element_type=F32) for part in _split3(y))


def _gdn_kernel(q_ref, k_ref, v_ref, ggb_ref, s0_ref, o_ref, sT_ref, s_ref, *, d, reverse, n_par, n_chunk, n_tile):
    ti = pl.program_id(1)
    w4 = GDN_H * GDN_DK
    n_all = n_par * n_chunk
    tt = n_all * CHUNK
    blockmask = (_row_iota((w4, w4)) // CHUNK) == (_lane_iota((w4, w4)) // CHUNK)

    def expand_rows(y):
        yt = jnp.concatenate([y] * GDN_H, axis=0)
        zero = jnp.zeros((), y.dtype)
        parts = [jnp.where(blockmask, yt[:, w4 * u:w4 * (u + 1)], zero) for u in range(y.shape[1] // w4)]
        return parts[0] if len(parts) == 1 else jnp.concatenate(parts, axis=1)

    def heads_dot3(lhs, y):
        r = lhs.shape[0]
        lh, ll = _split(lhs)
        yh, yl = _split(y)
        out = jnp.dot(jnp.concatenate([lh, ll], axis=0), expand_rows(yh), preferred_element_type=F32)
        return out[0:r] + out[r:2 * r] + jnp.dot(lh, expand_rows(yl), preferred_element_type=F32)

    def heads_dot2(lhs, y):
        r = lhs.shape[0]
        lh, ll = _split(lhs)
        out = jnp.dot(jnp.concatenate([lh, ll], axis=0), expand_rows(y.astype(BF16)), preferred_element_type=F32)
        return out[0:r] + out[r:2 * r]

    def heads_dot1(lhs, y):
        return jnp.dot(lhs.astype(BF16), expand_rows(y.astype(BF16)), preferred_element_type=F32)

    @pl.when(ti == 0)
    def _():
        for p in range(n_par):
            s_ref[p] = expand_rows(s0_ref[p, 0])

    ri = _row_iota((CHUNK, w4))
    cj = _lane_iota((CHUNK, w4)) % CHUNK
    incl = (cj >= ri) if reverse else (cj <= ri)
    strict = (cj > ri) if reverse else (cj < ri)
    eye_sbs = (cj == ri)
    rep = eye_sbs.astype(BF16)

    t_r = _row_iota((CHUNK, CHUNK))
    t_c = _lane_iota((CHUNK, CHUNK))
    tri = ((t_c >= t_r) if reverse else (t_c <= t_r)).astype(BF16)
    ones_cc = jnp.ones((CHUNK, CHUNK), BF16)
    gb_all = ggb_ref[...]
    cs = range(n_all)
    rows = [slice(c * CHUNK, (c + 1) * CHUNK) for c in cs]
    gcum = jnp.concatenate([_dot01_exact(tri, gb_all[r, :]) for r in rows], axis=0)
    er = _row_iota((128, w4))
    head = _lane_iota((128, w4)) // CHUNK
    e_g = (er == d * GDN_H + head).astype(BF16)
    e_b = (er == 2 * GDN_H + d * GDN_H + head).astype(BF16)
    gc_all = _rows_dot_exact(gcum, e_g)
    beta_all = _rows_dot_exact(gb_all, e_b, terms=2)
    gct = [_dot01_exact(ones_cc, jnp.where(eye_sbs, gc_all[r, :], 0.0)) for r in rows]

    q = [q_ref[r, :] for r in rows]
    k = [k_ref[r, :] for r in rows]
    v = [v_ref[r, :] for r in rows]
    beta = [beta_all[r, :] for r in rows]
    gc = [gc_all[r, :] for r in rows]
    decay = [jnp.where(incl, jnp.exp(jnp.where(incl, gc[c] - gct[c], 0.0)), 0.0) for c in cs]
    kb = [k[c].astype(BF16) for c in cs]
    w_k = [jnp.where(blockmask,
                     lax.dot_general(kb[c], rep, (((0,), (0,)), ((), ())), preferred_element_type=F32),
                     0.0).astype(BF16) for c in cs]
    kq = [jnp.dot(jnp.concatenate([kb[c], q[c].astype(BF16)], axis=0), w_k[c], preferred_element_type=F32)
          for c in cs]
    qk = [(kq[c][CHUNK:2 * CHUNK] * decay[c]).astype(BF16) for c in cs]
    a = [jnp.where(strict, beta[c] * kq[c][0:CHUNK] * decay[c], 0.0) for c in cs]
    t_inv = [jnp.where(eye_sbs, 1.0, 0.0) - a[c] for c in cs]
    pw = [heads_dot3(a[c], a[c]) for c in cs]
    for stage in range(4):
        mm = heads_dot3 if stage < N_PRECISE else heads_dot1
        both = [mm(jnp.concatenate([t_inv[c], pw[c]], axis=0), pw[c]) for c in cs]
        t_inv = [t_inv[c] + both[c][0:CHUNK] for c in cs]
        pw = [both[c][CHUNK:2 * CHUNK] for c in cs]
    t_inv = [t_inv[c] + heads_dot1(t_inv[c], pw[c]) for c in cs]
    egc = [jnp.exp(gc[c]) for c in cs]
    sol = [heads_dot2(t_inv[c], jnp.concatenate([v[c] * beta[c], k[c] * (beta[c] * egc[c])], axis=1)) for c in cs]
    g_last = [gc[c][0:1, :] if reverse else gc[c][CHUNK - 1:CHUNK, :] for c in cs]
    wq = [jnp.concatenate([sol[c][:, w4:2 * w4], q[c] * egc[c]], axis=0).astype(BF16) for c in cs]
    kdec = [(k[c] * jnp.exp(g_last[c] - gc[c])).astype(BF16) for c in cs]

    for cc in range(n_chunk):
        for p in range(n_par):
            c = p * n_chunk + ((n_chunk - 1 - cc) if reverse else cc)
            s = s_ref[p]
            ws_qs = jnp.dot(wq[c], s.astype(BF16), preferred_element_type=F32)
            v_new = sol[c][:, 0:w4] - ws_qs[0:CHUNK]
            vb = v_new.astype(BF16)
            o_ref[rows[c], :] = (
                ws_qs[CHUNK:2 * CHUNK] + jnp.dot(qk[c], expand_rows(vb), preferred_element_type=F32))
            upd = lax.dot_general(kdec[c], vb, (((0,), (0,)), ((), ())), preferred_element_type=F32)
            s_ref[p] = s * jnp.exp(g_last[c]) + jnp.where(blockmask, upd, 0.0)

    @pl.when(ti == n_tile - 1)
    def _():
        for p in range(n_par):
            s = s_ref[p]
            sT_ref[p, 0] = s[0:64] + s[64:128] + s[128:192] + s[192:256]


def _gdn_call(q, k, v, ggb, s0, *, d, l_state, n_seq, t, row0, tt, n_par):
    reverse = d == 1
    n_tile = t // tt
    assert n_par == 1 or n_tile == 1
    b0 = row0 // (tt * n_par)
    tidx = (lambda i: n_tile - 1 - i) if reverse else (lambda i: i)
    blk = lambda w: pl.BlockSpec((tt * n_par, w), lambda s, i: (b0 + s * n_tile + tidx(i), 0))
    return pl.pallas_call(
        functools.partial(_gdn_kernel, d=d, reverse=reverse, n_par=n_par, n_chunk=tt // CHUNK, n_tile=n_tile),
        grid=(n_seq // n_par, n_tile),
        in_specs=[blk(256), blk(256), blk(256), blk(128),
                  pl.BlockSpec((n_par, None, 1, CHUNK, 256), lambda s, i: (s, l_state, d, 0, 0))],
        out_specs=[pl.BlockSpec((tt * n_par, 256), lambda s, i: (s * n_tile + tidx(i), 0)),
                   pl.BlockSpec((n_par, 1, CHUNK, 256), lambda s, i: (s, 0, 0, 0))],
        out_shape=[jax.ShapeDtypeStruct((n_seq * t, 256), F32),
                   jax.ShapeDtypeStruct((n_seq, 1, CHUNK, 256), F32)],
        scratch_shapes=[pltpu.VMEM((n_par, 256, 256), F32)],
        compiler_params=_cparams(("parallel", "arbitrary")),
        name="gdn_bwd" if reverse else "gdn_fwd",
    )(q, k, v, ggb, s0)


def _mix_mlp_kernel(modrow_ref, xc_ref, xd_ref, mod_ref, g1_ref,
                    oac_ref, oad_ref, obc_ref, obd_ref, ofc_ref, ofd_ref, orc_ref, ord_ref, odc_ref, odd_ref,
                    gz_ref, gng_ref, ones_ref, wm_ref, bm_ref, wb_ref, wo_ref, g2_ref, w1_ref, w2_ref,
                    yc_ref, yd_ref, *, n_ctx_tiles):
    is_ctx = pl.program_id(0) < n_ctx_tiles
    pick = lambda c_ref, d_ref: jnp.where(is_ctx, c_ref[...], d_ref[...])
    x = pick(xc_ref, xd_ref)
    h = _modulate(x, g1_ref[...], mod_ref[0, 0:1, :], mod_ref[0, 1:2, :]).astype(BF16)
    oc = pick(ofc_ref, ofd_ref) + pick(orc_ref, ord_ref)
    ms = _group_sum(oc * oc, ones_ref[...]) * (1.0 / GDN_DK)
    oc = (oc * lax.rsqrt(ms + EPS) * gng_ref[...]) * _silu(gz_ref[...])
    branches = (pick(oac_ref, oad_ref), pick(obc_ref, obd_ref), oc, pick(odc_ref, odd_ref))
    acc = None
    for m in range(N_BRANCH):
        cols = slice(D_MODEL * m, D_MODEL * (m + 1))
        gate = _sigmoid(jnp.dot(h, wm_ref[:, cols], preferred_element_type=F32) + bm_ref[:, cols])
        term = gate * jnp.dot(branches[m].astype(BF16), wb_ref[m], preferred_element_type=F32)
        acc = term if acc is None else acc + term
    x = x + mod_ref[0, 2:3, :] * jnp.dot(acc.astype(BF16), wo_ref[...], preferred_element_type=F32)

    h = _modulate(x, g2_ref[...], mod_ref[0, 3:4, :], mod_ref[0, 4:5, :]).astype(BF16)
    acc = None
    for j in range(D_FF // D_MODEL):
        cols = slice(D_MODEL * j, D_MODEL * (j + 1))
        a = jnp.maximum(jnp.dot(h, w1_ref[:, cols], preferred_element_type=F32), 0.0)
        term = jnp.dot((a * a).astype(BF16), w2_ref[cols, :], preferred_element_type=F32)
        acc = term if acc is None else acc + term
    y = x + mod_ref[0, 5:6, :] * acc

    @pl.when(is_ctx)
    def _():
        yc_ref[...] = y

    @pl.when(jnp.logical_not(is_ctx))
    def _():
        yd_ref[...] = y


def _mix_mlp_call(modrow, x_pair, mod_all, g1, pairs, gz, gng, ones_bd, wm, bm, wb, wo, g2, w1, w2, *, l, tm):
    x_c, x_d = x_pair
    n, d = x_c.shape[0] + x_d.shape[0], x_c.shape[1]
    nct = x_c.shape[0] // tm
    row_tile = lambda w: pl.BlockSpec((tm, w), lambda i, *_: (i, 0))
    pair = lambda w: [pl.BlockSpec((tm, w), lambda i, *_: (jnp.minimum(i, nct - 1), 0)),
                      pl.BlockSpec((tm, w), lambda i, *_: (jnp.maximum(i - nct, 0), 0))]
    pair_specs = [sp for _ in pairs for sp in pair(256)]
    resident = lambda arr: _layer_spec(arr, l, single_buffer=True)
    grid_spec = pltpu.PrefetchScalarGridSpec(
        num_scalar_prefetch=1, grid=(n // tm,),
        in_specs=pair(d) + [pl.BlockSpec((None, 1, 6, d), lambda i, modrow: (l, modrow[i], 0, 0)),
                            _layer_spec(g1, l)] + pair_specs
                 + [row_tile(256), _layer_spec(gng, l), pl.BlockSpec((256, 256), lambda i, *_: (0, 0)),
                    resident(wm), _layer_spec(bm, l), resident(wb), resident(wo),
                    _layer_spec(g2, l), resident(w1), resident(w2)],
        out_specs=pair(d))
    flat = [a for pr in pairs for a in pr]
    return pl.pallas_call(
        functools.partial(_mix_mlp_kernel, n_ctx_tiles=nct), grid_spec=grid_spec,
        out_shape=[jax.ShapeDtypeStruct(x_c.shape, F32), jax.ShapeDtypeStruct(x_d.shape, F32)],
        compiler_params=_cparams(("arbitrary",)),
        name="merge_mlp",
    )(modrow, x_c, x_d, mod_all, g1, *flat, gz, gng, ones_bd, wm, bm, wb, wo, g2, w1, w2)


def _rope_tables(t, tm):
    n_freq = HEAD_DIM // 4
    inv = np.float32(ROPE_BASE) ** (-np.arange(n_freq, dtype=np.float32) / np.float32(n_freq))
    pos = np.arange(t)
    row = (pos // GRID_W).astype(np.float32)[:, None]
    col = (pos % GRID_W).astype(np.float32)[:, None]
    ar = (row * inv).astype(np.float32)
    ac = (col * inv).astype(np.float32)
    cos64 = np.concatenate([np.cos(ar), np.cos(ar), np.cos(ac), np.cos(ac)], axis=1)
    sin64 = np.concatenate([-np.sin(ar), np.sin(ar), -np.sin(ac), np.sin(ac)], axis=1)
    cos = np.concatenate([np.ones((tm, 128), np.float32), np.tile(cos64, (1, 2)).astype(np.float32)], axis=0)
    sin = np.concatenate([np.zeros((tm, 128), np.float32), np.tile(sin64, (1, 2)).astype(np.float32)], axis=0)
    return jnp.asarray(cos), jnp.asarray(sin)


def _tile_meta(n_ctx, n_seq_dec, t_ctx, t_dec, tm):
    nct = n_ctx // tm
    per = t_dec // tm
    ndt = n_seq_dec * per
    idx = np.arange(nct + ndt)
    dec = idx >= nct
    di = np.maximum(idx - nct, 0)
    modrow = np.where(dec, 1 + di // per, 0)
    ropeblk = np.where(dec, 1 + di % per, 0)
    tiles_ctx = max(t_ctx // tm, 1)
    isstart = np.where(dec, di % per == 0, idx % tiles_ctx == 0)
    isend = np.where(dec, di % per == per - 1, idx % tiles_ctx == tiles_ctx - 1)
    as_i32 = lambda a: jnp.asarray(a.astype(np.int32))
    return as_i32(modrow), as_i32(ropeblk), as_i32(isstart), as_i32(isend)


def _block_diag_gates(w):
    depth, two, n, c, _ = w.shape
    eye = jnp.eye(n, dtype=w.dtype)
    return jnp.einsum("ldnij,nm->ldnimj", w, eye).reshape(depth, two, n * c, n * c)


def kernel(x_prompt, x_sample, c, cache_a_k, cache_a_v, cache_d_k, cache_d_v, state_lru, state_gdn, c_ctx, mod_w, mod_b, norm1_g, norm2_g, w_in, a_qn_g, a_kn_g, a_sink, lru_conv_w, lru_conv_b, lru_wr, lru_br, lru_wi, lru_bi, lru_lam, gdn_conv_w, gdn_a_log, gdn_dt_bias, gdn_norm_g, d_qn_g, d_kn_g, w_branch, w_merge, b_merge, w_out, mlp_w1, mlp_w2):
    batch, seq, d = x_prompt.shape
    dec_batch, dec_seq, _ = x_sample.shape
    depth = mod_w.shape[0]
    past = cache_a_k.shape[2]
    n_ctx = batch * seq
    n_dec = dec_batch * dec_seq
    tm = 256
    gdn_par = 2 if batch % 2 == 0 else 1
    gdn_tt = 512 if dec_seq % 512 == 0 else 256
    assert d == D_MODEL and seq % tm == 0 and dec_seq % tm == 0 and tm % seq == 0
    assert dec_batch + 1 <= 8 and n_ctx % dec_seq == 0

    cond8 = jnp.zeros((8, d), F32).at[0].set(c_ctx).at[1:1 + dec_batch].set(c)
    mod_all = _mod_call(cond8, mod_w, mod_b).reshape(depth, 8, 6, d)

    meta = _tile_meta(n_ctx, dec_batch, seq, dec_seq, tm)
    cos_t, sin_t = _rope_tables(dec_seq, tm)
    lane = np.arange(256)
    ones_bd = jnp.asarray((lane[:, None] // HEAD_DIM == lane[None, :] // HEAD_DIM).astype(np.float32)).astype(BF16)

    w_packed = jnp.concatenate([w_in[:, :, :2064], jnp.zeros((depth, d, GDN_PAD), F32), w_in[:, :, 2064:]],
                               axis=2).astype(BF16)
    pad128 = lambda v: jnp.pad(v.reshape(depth, 1, -1), ((0, 0), (0, 0), (0, 128 - v[0].size)))
    vecs = (
        (jnp.tile(a_qn_g, (1, 4)) * Q_SCALE)[:, None, :],
        jnp.tile(a_kn_g, (1, 2))[:, None, :],
        (jnp.tile(d_qn_g, (1, 4)) * Q_SCALE)[:, None, :],
        jnp.tile(d_kn_g, (1, 2))[:, None, :],
        lru_conv_w, lru_conv_b[:, None, :], gdn_conv_w,
        pad128(gdn_a_log), pad128(gdn_dt_bias),
    )
    g1 = norm1_g[:, None, :]
    g2 = norm2_g[:, None, :]
    wr_bd = _block_diag_gates(lru_wr)
    wi_bd = _block_diag_gates(lru_wi)
    w_lru = jnp.concatenate([wr_bd[:, 0], wi_bd[:, 0], wr_bd[:, 1], wi_bd[:, 1]], axis=-1).astype(BF16)
    b_lru = jnp.concatenate([lru_br[:, 0], lru_bi[:, 0], lru_br[:, 1], lru_bi[:, 1]], axis=-1)[:, None, :]
    gng = jnp.tile(gdn_norm_g, (1, 4))[:, None, :]
    wm = w_merge.astype(BF16)
    bm = b_merge[:, None, :]
    wb = w_branch.astype(BF16)
    wo = w_out.astype(BF16)
    w1 = mlp_w1.astype(BF16)
    w2 = mlp_w2.astype(BF16)
    sink = a_sink.reshape(-1)
    caches = [t.reshape(dec_batch, depth, past, 2 * HEAD_DIM) for t in (cache_a_k, cache_a_v, cache_d_k, cache_d_v)]
    s0_dec = state_gdn.transpose(0, 1, 2, 4, 3, 5).reshape(dec_batch, depth, 2, CHUNK, 256)
    zeros_lru = jnp.zeros((batch, 1, 2, BRANCH_W), F32)
    zeros_gdn = jnp.zeros((batch, 1, 2, CHUNK, 256), F32)

    x = (x_prompt.reshape(n_ctx, d), x_sample.reshape(n_dec, d))

    kv_ctx, lru_states, gdn_states = None, [], []
    for l in range(depth):
        (qa, ka, va, lx, lg, gq, gk, gv, gz, ggb, qd, kd, vd), kv_ctx = _inproj_call(
            x, meta, mod_all, g1, w_packed, cos_t, sin_t, vecs, ones_bd, kv_ctx, l=l, depth=depth, tm=tm)

        oa_c, od_c = _attn_ctx_call(sink, qa, ka, va, qd, kd, vd, l=l, n_seq=batch, t=seq)
        oa_d = _attn_band_call(sink, qa, ka, va, caches[0], caches[1], l=l, n_seq=dec_batch, t=dec_seq,
                               row0=n_ctx, tq=256)
        od_d = _attn_full_call(qd, kd, vd, caches[2], caches[3], l=l, n_seq=dec_batch, t=dec_seq,
                               row0=n_ctx, tq=256)

        ob_c, st_c = _lru_call(lx, lg, w_lru, b_lru, lru_lam, zeros_lru, l=l, l_state=0, n_seq=batch, t=seq,
                               row0=0)
        ob_d, _ = _lru_call(lx, lg, w_lru, b_lru, lru_lam, state_lru, l=l, l_state=l, n_seq=dec_batch,
                            t=dec_seq, row0=n_ctx)

        oc_pairs, sts = [], []
        for dd in range(2):
            o_c, s_c = _gdn_call(gq, gk, gv, ggb, zeros_gdn, d=dd, l_state=0, n_seq=batch, t=seq, row0=0,
                                 tt=seq, n_par=gdn_par)
            o_d, _ = _gdn_call(gq, gk, gv, ggb, s0_dec, d=dd, l_state=l, n_seq=dec_batch, t=dec_seq,
                               row0=n_ctx, tt=gdn_tt, n_par=1)
            oc_pairs.append((o_c, o_d))
            sts.append(s_c)

        x = _mix_mlp_call(meta[0], x, mod_all, g1,
                          [(oa_c, oa_d), (ob_c, ob_d), oc_pairs[0], oc_pairs[1], (od_c, od_d)],
                          gz, gng, ones_bd, wm, bm, wb, wo, g2, w1, w2, l=l, tm=tm)

        lru_states.append(st_c)
        gdn_states.append(jnp.concatenate(sts, axis=1))

    y_prompt = x[0].reshape(batch, seq, d)
    y_sample = x[1].reshape(dec_batch, dec_seq, d)
    new_kv = [t.reshape(batch, depth, seq, 2, HEAD_DIM) for t in kv_ctx]
    new_lru = jnp.stack(lru_states, axis=1)
    new_gdn = (jnp.stack(gdn_states, axis=1).reshape(batch, depth, 2, GDN_DK, GDN_H, GDN_DK)
               .transpose(0, 1, 2, 4, 3, 5))
    return (y_prompt, y_sample, new_kv[0], new_kv[1], new_kv[2], new_kv[3], new_lru, new_gdn)
```

```python
import functools

import numpy as np
import jax
import jax.numpy as jnp
from jax import lax
from jax.experimental import pallas as pl
from jax.experimental.pallas import tpu as pltpu

F32 = jnp.float32
BF16 = jnp.bfloat16
HIGHEST = lax.Precision.HIGHEST

D_MODEL = 1024
HEAD_DIM = 64
BRANCH_W = 256
N_BRANCH = 4
GRID_W = 64
WINDOW = 128
LRU_C = 8.0
CONV_W = 4
CONV_LEFT = 2
GDN_H = 4
GDN_DK = 64
CHUNK = 64
D_FF = 4 * D_MODEL
ROPE_BASE = 10000.0
EPS = 1e-6
NEG = -1e30
LOG2E = 1.4426950408889634
Q_SCALE = HEAD_DIM ** -0.5 * LOG2E
GDN_PAD = 112
IN_COLS_PACKED = 2688
HALO = 8
N_PRECISE = 4

V7X_VMEM_LIMIT = 56 * 1024 * 1024


def _cparams(sem, vmem=V7X_VMEM_LIMIT):
    return pltpu.CompilerParams(dimension_semantics=sem, vmem_limit_bytes=vmem)


def _layer_spec(arr, l, single_buffer=False):
    nd = arr.ndim - 1
    mode = dict(pipeline_mode=pl.Buffered(1)) if single_buffer else {}
    return pl.BlockSpec((None,) + arr.shape[1:], lambda *_: (l,) + (0,) * nd, **mode)


def _bdot(a, b):
    return jnp.dot(a.astype(BF16), b.astype(BF16), preferred_element_type=F32)


def _split(x):
    hi = x.astype(BF16)
    lo = (x - hi.astype(F32)).astype(BF16)
    return hi, lo


def _split3(x):
    hi = x.astype(BF16)
    r = x - hi.astype(F32)
    mid = r.astype(BF16)
    lo = (r - mid.astype(F32)).astype(BF16)
    return hi, mid, lo


def _group_sum(x, ones_bd):
    return jnp.dot(x.astype(BF16), ones_bd, preferred_element_type=F32)


def _sigmoid(x):
    return 0.5 * jnp.tanh(0.5 * x) + 0.5


def _silu(x):
    return x * _sigmoid(x)


def _softplus(x):
    return jnp.maximum(x, 0.0) + jnp.log1p(jnp.exp(-jnp.abs(x)))


def _modulate(x, g, shift, scale):
    ms = jnp.mean(x * x, axis=-1, keepdims=True)
    return (x * lax.rsqrt(ms + EPS)) * (g * (1.0 + scale)) + shift


def _lane_iota(shape):
    return lax.broadcasted_iota(jnp.int32, shape, len(shape) - 1)


def _row_iota(shape):
    return lax.broadcasted_iota(jnp.int32, shape, len(shape) - 2)


def _mod_kernel(cond_ref, w_ref, b_ref, o_ref):
    hi, lo = _split(_silu(cond_ref[...]))
    w = w_ref[0].astype(BF16)
    o_ref[0] = (jnp.dot(hi, w, preferred_element_type=F32) + jnp.dot(lo, w, preferred_element_type=F32)
                + b_ref[0])


def _mod_call(cond8, mod_w, mod_b):
    depth, d, n = mod_w.shape
    tn = 1536
    return pl.pallas_call(
        _mod_kernel,
        grid=(depth, n // tn),
        in_specs=[
            pl.BlockSpec((8, d), lambda l, j: (0, 0)),
            pl.BlockSpec((1, d, tn), lambda l, j: (l, 0, j)),
            pl.BlockSpec((1, 1, tn), lambda l, j: (l, 0, j)),
        ],
        out_specs=pl.BlockSpec((1, 8, tn), lambda l, j: (l, 0, j)),
        out_shape=jax.ShapeDtypeStruct((depth, 8, n), F32),
        compiler_params=_cparams(("parallel", "parallel")),
        name="mod_vectors",
    )(cond8, mod_w, mod_b.reshape(depth, 1, n))


def _rope(x, cos, sin):
    outs = []
    for j in range(x.shape[1] // 128):
        xb = x[:, 128 * j:128 * (j + 1)]
        lane = _lane_iota(xb.shape)
        sw = jnp.where((lane & 16) == 0, pltpu.roll(xb, 112, 1), pltpu.roll(xb, 16, 1))
        outs.append(xb * cos + sw * sin)
    return outs[0] if len(outs) == 1 else jnp.concatenate(outs, axis=1)


def _centred_conv(g, w, tm):
    rows = g.shape[0]
    acc = None
    for j in range(CONV_W):
        sh = (CONV_LEFT - j) % rows
        gj = g if sh == 0 else pltpu.roll(g, sh, 0)
        term = gj[HALO:HALO + tm] * w[j:j + 1, :]
        acc = term if acc is None else acc + term
    return acc


def _inproj_kernel(modrow_ref, ropeblk_ref, isstart_ref, isend_ref,
                   xpc_ref, xc_ref, xnc_ref, xpd_ref, xd_ref, xnd_ref, mod_ref, g1_ref, w_ref, cos_ref, sin_ref,
                   aqg_ref, akg_ref, dqg_ref, dkg_ref, lcw_ref, lcb_ref, gcw_ref,
                   alog_ref, dtb_ref, ones_ref, *rest, tm, n_ctx_tiles, n_alias):
    (qa_ref, ka_ref, va_ref, lx_ref, lg_ref, gq_ref, gk_ref, gv_ref, gz_ref, ggb_ref,
     qd_ref, kd_ref, vd_ref, kac_ref, vac_ref, kdc_ref, vdc_ref) = rest[n_alias:]
    i = pl.program_id(0)
    is_ctx = i < n_ctx_tiles
    pick = lambda c_ref, d_ref: jnp.where(is_ctx, c_ref[...], d_ref[...])
    xfull = jnp.concatenate([pick(xpc_ref, xpd_ref), pick(xc_ref, xd_ref), pick(xnc_ref, xnd_ref)],
                            axis=0)
    h = _modulate(xfull, g1_ref[...], mod_ref[0, 0:1, :], mod_ref[0, 1:2, :])
    p = jnp.dot(h.astype(BF16), w_ref[...], preferred_element_type=F32)
    hb, hm = slice(0, tm + 2 * HALO), slice(HALO, HALO + tm)
    proj = lambda rows, c0, c1: p[rows, c0:c1]

    ones_bd = ones_ref[...]
    cos = cos_ref[...]
    sin = sin_ref[...]

    pa = proj(hm, 0, 512)
    pd = proj(hm, 2176, 2688)
    va_ref[...] = pa[:, 384:512]
    vd_ref[...] = pd[:, 384:512]

    row = _row_iota((tm + 2 * HALO, 1))
    keep = jnp.logical_and(jnp.logical_or(row >= HALO, isstart_ref[i] == 0),
                           jnp.logical_or(row < HALO + tm, isend_ref[i] == 0))

    lxg = jnp.where(keep, proj(hb, 512, 768), 0.0)
    lx_ref[...] = _centred_conv(lxg, lcw_ref[...], tm) + lcb_ref[...]
    lg_ref[...] = proj(hm, 768, 1024)

    qkv = jnp.where(keep, proj(hb, 1024, 1792), 0.0)
    qkv = _silu(_centred_conv(qkv, gcw_ref[...], tm))
    gq = qkv[:, 0:256]
    gk = qkv[:, 256:512]
    gv_ref[...] = qkv[:, 512:768]
    pz = proj(hm, 1792, 2176)
    gz_ref[...] = pz[:, 0:256]
    ab = pz[:, 256:384]
    g = -jnp.exp(alog_ref[...]) * _softplus(ab + dtb_ref[...])
    lane = _lane_iota(ab.shape)
    ggb_ref[...] = jnp.where(lane < 2 * GDN_H, g, _sigmoid(ab))

    qa_raw, ka_raw, qd_raw, kd_raw = pa[:, 0:256], pa[:, 256:384], pd[:, 0:256], pd[:, 256:384]
    ss256 = _group_sum(jnp.concatenate([qa_raw * qa_raw, qd_raw * qd_raw, gq * gq, gk * gk], axis=0), ones_bd)
    ss128 = _group_sum(jnp.concatenate([ka_raw * ka_raw, kd_raw * kd_raw], axis=0), ones_bd[:128, :128])
    rms = lambda x, ssq, gain: x * lax.rsqrt(ssq * (1.0 / HEAD_DIM) + EPS) * gain

    qa_ref[...] = _rope(rms(qa_raw, ss256[0:tm], aqg_ref[...]), cos, sin)
    qd_ref[...] = _rope(rms(qd_raw, ss256[tm:2 * tm], dqg_ref[...]), cos, sin)
    ka = _rope(rms(ka_raw, ss128[0:tm], akg_ref[...]), cos, sin)
    kd = _rope(rms(kd_raw, ss128[tm:2 * tm], dkg_ref[...]), cos, sin)
    ka_ref[...] = ka
    kd_ref[...] = kd
    gq_ref[...] = gq * lax.rsqrt(ss256[2 * tm:3 * tm] + EPS) * (GDN_DK ** -0.5)
    gk_ref[...] = gk * lax.rsqrt(ss256[3 * tm:4 * tm] + EPS)

    @pl.when(is_ctx)
    def _():
        for ref, val in ((kac_ref, ka), (vac_ref, pa[:, 384:512]), (kdc_ref, kd), (vdc_ref, pd[:, 384:512])):
            if n_alias:
                ref[...] = val
            else:
                for slot in range(ref.shape[0]):
                    ref[slot] = val


def _x_pair_specs(x_pair, tm):
    x_c, x_d = x_pair
    d = x_c.shape[1]
    nct = x_c.shape[0] // tm
    hb = tm // HALO
    specs = []
    for arr, first in ((x_c, 0), (x_d, nct)):
        nblk = arr.shape[0] // tm
        last_hb = arr.shape[0] // HALO - 1
        tile = lambda i, first=first, nblk=nblk: jnp.clip(i - first, 0, nblk - 1)
        specs += [
            pl.BlockSpec((HALO, d), lambda i, *_, t=tile, m=last_hb: (jnp.clip(t(i) * hb - 1, 0, m), 0)),
            pl.BlockSpec((tm, d), lambda i, *_, t=tile: (t(i), 0)),
            pl.BlockSpec((HALO, d), lambda i, *_, t=tile, m=last_hb: (jnp.clip((t(i) + 1) * hb, 0, m), 0)),
        ]
    return specs


def _inproj_call(x_pair, meta, mod_all, g1, w_packed, cos_t, sin_t, vecs, ones_bd, kv_prev, *, l, depth, tm):
    x_c, x_d = x_pair
    n, d = x_c.shape[0] + x_d.shape[0], x_c.shape[1]
    nt = n // tm
    nct = x_c.shape[0] // tm
    row_tile = lambda w: pl.BlockSpec((tm, w), lambda i, *_: (i, 0))
    alias_in = [] if kv_prev is None else list(kv_prev)
    if alias_in:
        kv_spec = pl.BlockSpec((None, None, tm, 128), lambda i, *_: (jnp.minimum(i, nct - 1), l, 0, 0))
    else:
        kv_spec = pl.BlockSpec((None, depth, tm, 128), lambda i, *_: (jnp.minimum(i, nct - 1), 0, 0, 0))
    kv_shape = jax.ShapeDtypeStruct((nct, depth, tm, 128), F32)
    in_specs = _x_pair_specs(x_pair, tm) + [
        pl.BlockSpec((None, 1, 6, d), lambda i, modrow, *_: (l, modrow[i], 0, 0)),
        _layer_spec(g1, l),
        _layer_spec(w_packed, l),
        pl.BlockSpec((tm, 128), lambda i, modrow, ropeblk, *_: (ropeblk[i], 0)),
        pl.BlockSpec((tm, 128), lambda i, modrow, ropeblk, *_: (ropeblk[i], 0)),
    ] + [_layer_spec(v, l) for v in vecs] + [pl.BlockSpec((256, 256), lambda i, *_: (0, 0))]
    in_specs += [pl.BlockSpec(memory_space=pl.ANY)] * len(alias_in)
    widths = (256, 128, 128, 256, 256, 256, 256, 256, 256, 128, 256, 128, 128)
    grid_spec = pltpu.PrefetchScalarGridSpec(
        num_scalar_prefetch=4,
        grid=(nt,),
        in_specs=in_specs,
        out_specs=[row_tile(w) for w in widths] + [kv_spec] * 4,
    )
    operands = (*meta, x_c, x_c, x_c, x_d, x_d, x_d, mod_all, g1, w_packed, cos_t, sin_t, *vecs, ones_bd)
    outs = pl.pallas_call(
        functools.partial(_inproj_kernel, tm=tm, n_ctx_tiles=nct, n_alias=len(alias_in)),
        grid_spec=grid_spec,
        out_shape=[jax.ShapeDtypeStruct((n, w), F32) for w in widths] + [kv_shape] * 4,
        input_output_aliases={len(operands) + u: len(widths) + u for u in range(len(alias_in))},
        compiler_params=_cparams(("arbitrary",)),
        name="in_projection",
    )(*operands, *alias_in)
    return outs[:len(widths)], outs[len(widths):]


def _head_q(q, j, g):
    qj = q[:, 128 * j:128 * (j + 1)]
    lane = _lane_iota(qj.shape)
    sel = (lane < HEAD_DIM) if g == 0 else (lane >= HEAD_DIM)
    return jnp.where(sel, qj, 0.0).astype(BF16)


def _place_heads(res, j):
    r0 = res[0] if j == 0 else pltpu.roll(res[0], HEAD_DIM, 1)
    r1 = res[1] if j == 1 else pltpu.roll(res[1], HEAD_DIM, 1)
    lane = _lane_iota(r0.shape)
    return jnp.where(lane < HEAD_DIM, r0, r1)


def _attend_many(problems, order="staged"):
    work = []
    for pi, (q, ks, vs, masks, sinks, ks_sw) in enumerate(problems):
        if ks_sw is None:
            ks_sw = [pltpu.roll(k, HEAD_DIM, 1) for k in ks]
        work += [(pi, j, g, q, ks, vs, masks, sinks, ks_sw) for j in range(2) for g in range(2)]
    scores, probs, res = {}, {}, {}

    def score(w):
        pi, j, g, q, ks, vs, masks, sinks, ks_sw = w
        qm = _head_q(q, j, g)
        ss = []
        for k, ksw, mk in zip(ks, ks_sw, masks):
            s = lax.dot_general(qm, k if g == j else ksw, (((1,), (1,)), ((), ())), preferred_element_type=F32)
            ss.append(s if mk is None else jnp.where(mk, s, NEG))
        scores[w[:3]] = ss

    def softmax_num(w):
        pi, j, g, q, ks, vs, masks, sinks, ks_sw = w
        ss = scores[w[:3]]
        m = ss[0].max(axis=-1, keepdims=True)
        for s in ss[1:]:
            m = jnp.maximum(m, s.max(axis=-1, keepdims=True))
        if sinks is not None:
            m = jnp.maximum(m, sinks[2 * j + g])
        es = [jnp.exp2(s - m) for s in ss]
        den = sum(e.sum(axis=-1, keepdims=True) for e in es)
        if sinks is not None:
            den = den + jnp.exp2(sinks[2 * j + g] - m)
        probs[w[:3]] = ([e.astype(BF16) for e in es], den)

    def values(w):
        es, den = probs[w[:3]]
        res[w[:3]] = sum(jnp.dot(e, v, preferred_element_type=F32) for e, v in zip(es, w[5])) / den

    if order == "staged":
        for step in (score, softmax_num, values):
            for w in work:
                step(w)
    else:
        score(work[0])
        for prev, nxt in zip(work, work[1:] + [None]):
            if nxt is not None:
                score(nxt)
            softmax_num(prev)
            values(prev)
    return [jnp.concatenate([_place_heads([res[(pi, j, 0)], res[(pi, j, 1)]], j) for j in range(2)], axis=1)
            for pi in range(len(problems))]


def _attend(q, ks, vs, masks, sinks, ks_sw=None, order="staged"):
    return _attend_many([(q, ks, vs, masks, sinks, ks_sw)], order)[0]


def _bf(x):
    return x.astype(BF16)


def _attn_ctx_kernel(sink_ref, qa_ref, ka_ref, va_ref, qd_ref, kd_ref, vd_ref, oa_ref, od_ref, *, l, t, n_par):
    sinks = [sink_ref[4 * l + u] * LOG2E for u in range(4)]
    problems = []
    for p in range(n_par):
        r = slice(p * t, (p + 1) * t)
        problems.append((qa_ref[r, :], [_bf(ka_ref[r, :])], [_bf(va_ref[r, :])], [None], sinks, None))
        problems.append((qd_ref[r, :], [_bf(kd_ref[r, :])], [_bf(vd_ref[r, :])], [None], None, None))
    outs = _attend_many(problems)
    for p in range(n_par):
        oa_ref[p * t:(p + 1) * t, :] = outs[2 * p]
        od_ref[p * t:(p + 1) * t, :] = outs[2 * p + 1]


def _attn_ctx_call(sink, qa, ka, va, qd, kd, vd, *, l, n_seq, t):
    n_par = 2 if n_seq % 2 == 0 else 1
    blk = lambda w: pl.BlockSpec((n_par * t, w), lambda b, *_: (b, 0))
    grid_spec = pltpu.PrefetchScalarGridSpec(
        num_scalar_prefetch=1, grid=(n_seq // n_par,),
        in_specs=[blk(256), blk(128), blk(128), blk(256), blk(128), blk(128)],
        out_specs=[blk(256), blk(256)])
    return pl.pallas_call(
        functools.partial(_attn_ctx_kernel, l=l, t=t, n_par=n_par), grid_spec=grid_spec,
        out_shape=[jax.ShapeDtypeStruct((n_seq * t, 256), F32)] * 2,
        compiler_params=_cparams(("parallel",)),
        name="attn_context",
    )(sink, qa, ka, va, qd, kd, vd)


def _attn_band_kernel(sink_ref, q_ref, k_ref, v_ref, ck_ref, cv_ref, o_ref, *, l, tq, t):
    i = pl.program_id(1)
    start = pl.multiple_of(i * tq, tq)
    prev = pl.multiple_of(jnp.maximum(start - WINDOW, 0), WINDOW)
    nxt = pl.multiple_of(jnp.minimum(start + tq, t - WINDOW), WINDOW)
    k_loc = jnp.concatenate([k_ref[pl.ds(prev, WINDOW), :], k_ref[pl.ds(start, tq), :],
                             k_ref[pl.ds(nxt, WINDOW), :]], axis=0)
    v_loc = jnp.concatenate([v_ref[pl.ds(prev, WINDOW), :], v_ref[pl.ds(start, tq), :],
                             v_ref[pl.ds(nxt, WINDOW), :]], axis=0)
    shape = (tq, tq + 2 * WINDOW)
    qpos = start + _row_iota(shape)
    kpos = start - WINDOW + _lane_iota(shape)
    mask = (kpos >= 0) & (kpos < t) & (jnp.abs(qpos - kpos) <= WINDOW)
    sinks = [sink_ref[4 * l + u] * LOG2E for u in range(4)]
    o_ref[...] = _attend(q_ref[...], [_bf(ck_ref[0]), _bf(k_loc)], [_bf(cv_ref[0]), _bf(v_loc)],
                         [None, mask], sinks)


def _cache_spec(c, l):
    return pl.BlockSpec((1, None) + c.shape[2:], lambda b, i, *_: (b, l, 0, 0))


def _attn_band_call(sink, q, k, v, ck, cv, *, l, n_seq, t, row0, tq):
    nq = t // tq
    qb0 = row0 // tq
    sb0 = row0 // t
    grid_spec = pltpu.PrefetchScalarGridSpec(
        num_scalar_prefetch=1, grid=(n_seq, nq),
        in_specs=[
            pl.BlockSpec((tq, 256), lambda b, i, *_: (qb0 + b * nq + i, 0)),
            pl.BlockSpec((t, 128), lambda b, i, *_: (sb0 + b, 0)),
            pl.BlockSpec((t, 128), lambda b, i, *_: (sb0 + b, 0)),
            _cache_spec(ck, l), _cache_spec(cv, l),
        ],
        out_specs=pl.BlockSpec((tq, 256), lambda b, i, *_: (b * nq + i, 0)))
    return pl.pallas_call(
        functools.partial(_attn_band_kernel, l=l, tq=tq, t=t), grid_spec=grid_spec,
        out_shape=jax.ShapeDtypeStruct((n_seq * t, 256), F32),
        compiler_params=_cparams(("parallel", "parallel")),
        name="attn_banded",
    )(sink, q, k, v, ck, cv)


def _attn_full_kernel(q_ref, k_ref, v_ref, ck_ref, cv_ref, o_ref, ks_ref, vs_ref, *, past, t):
    @pl.when(pl.program_id(1) == 0)
    def _():
        kc = _bf(ck_ref[0])
        kl = _bf(k_ref[...])
        ks_ref[0, 0:past, :] = kc
        ks_ref[0, past:past + t, :] = kl
        ks_ref[1, 0:past, :] = pltpu.roll(kc, HEAD_DIM, 1)
        ks_ref[1, past:past + t, :] = pltpu.roll(kl, HEAD_DIM, 1)
        vs_ref[0:past, :] = _bf(cv_ref[0])
        vs_ref[past:past + t, :] = _bf(v_ref[...])

    o_ref[...] = _attend(q_ref[...], [ks_ref[0]], [vs_ref[...]], [None], None, ks_sw=[ks_ref[1]], order="skewed")


def _attn_full_call(q, k, v, ck, cv, *, l, n_seq, t, row0, tq):
    nq = t // tq
    qb0 = row0 // tq
    sb0 = row0 // t
    past = ck.shape[2]
    assert past % 16 == 0
    return pl.pallas_call(
        functools.partial(_attn_full_kernel, past=past, t=t), grid=(n_seq, nq),
        scratch_shapes=[pltpu.VMEM((2, past + t, 128), BF16), pltpu.VMEM((past + t, 128), BF16)],
        in_specs=[
            pl.BlockSpec((tq, 256), lambda b, i: (qb0 + b * nq + i, 0)),
            pl.BlockSpec((t, 128), lambda b, i: (sb0 + b, 0)),
            pl.BlockSpec((t, 128), lambda b, i: (sb0 + b, 0)),
            _cache_spec(ck, l), _cache_spec(cv, l),
        ],
        out_specs=pl.BlockSpec((tq, 256), lambda b, i: (b * nq + i, 0)),
        out_shape=jax.ShapeDtypeStruct((n_seq * t, 256), F32),
        compiler_params=_cparams(("parallel", "arbitrary")),
        name="attn_full",
    )(q, k, v, ck, cv)


def _gelu_tanh(x):
    return 0.5 * x * (1.0 + jnp.tanh(0.7978845608028654 * (x + 0.044715 * (x * x * x))))


def _lru_kernel(lx_ref, lg_ref, w_ref, b_ref, lam_ref, h0_ref, y_ref, st_ref,
                hf_ref, hb_ref, af_ref, bf_ref, ab_ref, bb_ref, *, t, chunk):
    nc = t // chunk
    ng = chunk // 8
    sp = _softplus(-lam_ref[...])
    row8 = _row_iota((8, BRANCH_W))

    def gates(xc, d, a_ref, b2_ref):
        pre = _bdot(xc, w_ref[:, 512 * d:512 * (d + 1)]) + b_ref[:, 512 * d:512 * (d + 1)]
        r = _sigmoid(pre[:, 0:256])
        ig = _sigmoid(pre[:, 256:512])
        a = jnp.exp((-LRU_C) * r * sp[d:d + 1, :])
        a_ref[...] = a
        om = 1.0 - a * a
        b2_ref[...] = jnp.where(om > 0.0, om * lax.rsqrt(om), 0.0) * (ig * xc)

    def local_scan(a8, b8, reverse):
        for s in (1, 2, 4):
            sh = (8 - s) if reverse else s
            m = (row8 < 8 - s) if reverse else (row8 >= s)
            a_sh = pltpu.roll(a8, sh, 0)
            b_sh = pltpu.roll(b8, sh, 0)
            b8 = jnp.where(m, a8 * b_sh + b8, b8)
            a8 = jnp.where(m, a8 * a_sh, a8)
        return a8, b8

    def chunk_body(cc, carry):
        rf = pl.multiple_of(cc * chunk, chunk)
        rb = pl.multiple_of((nc - 1 - cc) * chunk, chunk)
        gates(lx_ref[pl.ds(rf, chunk), :], 0, af_ref, bf_ref)
        gates(lx_ref[pl.ds(rb, chunk), :], 1, ab_ref, bb_ref)

        def grp(gg, c2):
            hf, hb = c2
            gf = pl.multiple_of(gg * 8, 8)
            gb = pl.multiple_of((ng - 1 - gg) * 8, 8)
            a8, b8 = local_scan(af_ref[pl.ds(gf, 8), :], bf_ref[pl.ds(gf, 8), :], False)
            h8 = b8 + a8 * hf
            hf_ref[pl.ds(rf + gf, 8), :] = h8
            a8, b8 = local_scan(ab_ref[pl.ds(gb, 8), :], bb_ref[pl.ds(gb, 8), :], True)
            g8 = b8 + a8 * hb
            hb_ref[pl.ds(rb + gb, 8), :] = g8
            return (jnp.broadcast_to(h8[7:8, :], (8, BRANCH_W)), jnp.broadcast_to(g8[0:1, :], (8, BRANCH_W)))

        return lax.fori_loop(0, ng, grp, carry, unroll=4)

    init = (jnp.broadcast_to(h0_ref[0, 0:1, :], (8, BRANCH_W)), jnp.broadcast_to(h0_ref[0, 1:2, :], (8, BRANCH_W)))
    hf, hb = lax.fori_loop(0, nc, chunk_body, init)
    st_ref[0, 0:1, :] = hf[0:1, :]
    st_ref[0, 1:2, :] = hb[0:1, :]

    def combine(c, _):
        rows = pl.ds(pl.multiple_of(c * chunk, chunk), chunk)
        y_ref[rows, :] = (hf_ref[rows, :] + hb_ref[rows, :]) * _gelu_tanh(lg_ref[rows, :])
        return 0

    lax.fori_loop(0, nc, combine, 0)


def _lru_call(lx, lg, w, b, lam, h0, *, l, l_state, n_seq, t, row0):
    sb0 = row0 // t
    chunk = min(t, 256)
    seq = lambda: pl.BlockSpec((t, BRANCH_W), lambda s: (sb0 + s, 0))
    return pl.pallas_call(
        functools.partial(_lru_kernel, t=t, chunk=chunk),
        grid=(n_seq,),
        in_specs=[seq(), seq(), _layer_spec(w, l), _layer_spec(b, l), _layer_spec(lam, l),
                  pl.BlockSpec((1, None, 2, BRANCH_W), lambda s: (s, l_state, 0, 0))],
        out_specs=[pl.BlockSpec((t, BRANCH_W), lambda s: (s, 0)),
                   pl.BlockSpec((1, 2, BRANCH_W), lambda s: (s, 0, 0))],
        out_shape=[jax.ShapeDtypeStruct((n_seq * t, BRANCH_W), F32),
                   jax.ShapeDtypeStruct((n_seq, 2, BRANCH_W), F32)],
        scratch_shapes=[pltpu.VMEM((t, BRANCH_W), F32), pltpu.VMEM((t, BRANCH_W), F32)]
                       + [pltpu.VMEM((chunk, BRANCH_W), F32)] * 4,
        compiler_params=_cparams(("parallel",)),
        name="rglru",
    )(lx, lg, w, b, lam, h0)


def _rows_dot_exact(x, w01, terms=3):
    r = x.shape[0]
    parts = _split3(x)[:terms] if terms == 3 else _split(x)
    out = jnp.dot(jnp.concatenate(parts, axis=0), w01, preferred_element_type=F32)
    return sum(out[u * r:(u + 1) * r] for u in range(terms))


def _dot01_exact(m01, y):
    return sum(jnp.dot(m01, part, preferred_element_type=F32) for part in _split3(y))


def _gdn_kernel(q_ref, k_ref, v_ref, ggb_ref, s0_ref, o_ref, sT_ref, s_ref, *, d, reverse, n_par, n_chunk, n_tile):
    ti = pl.program_id(1)
    w4 = GDN_H * GDN_DK
    n_all = n_par * n_chunk
    tt = n_all * CHUNK
    blockmask = (_row_iota((w4, w4)) // CHUNK) == (_lane_iota((w4, w4)) // CHUNK)

    def expand_rows(y):
        yt = jnp.concatenate([y] * GDN_H, axis=0)
        zero = jnp.zeros((), y.dtype)
        parts = [jnp.where(blockmask, yt[:, w4 * u:w4 * (u + 1)], zero) for u in range(y.shape[1] // w4)]
        return parts[0] if len(parts) == 1 else jnp.concatenate(parts, axis=1)

    def heads_dot3(lhs, y):
        r = lhs.shape[0]
        lh, ll = _split(lhs)
        yh, yl = _split(y)
        out = jnp.dot(jnp.concatenate([lh, ll], axis=0), expand_rows(yh), preferred_element_type=F32)
        return out[0:r] + out[r:2 * r] + jnp.dot(lh, expand_rows(yl), preferred_element_type=F32)

    def heads_dot2(lhs, y):
        r = lhs.shape[0]
        lh, ll = _split(lhs)
        out = jnp.dot(jnp.concatenate([lh, ll], axis=0), expand_rows(y.astype(BF16)), preferred_element_type=F32)
        return out[0:r] + out[r:2 * r]

    def heads_dot1(lhs, y):
        return jnp.dot(lhs.astype(BF16), expand_rows(y.astype(BF16)), preferred_element_type=F32)

    @pl.when(ti == 0)
    def _():
        for p in range(n_par):
            s_ref[p] = expand_rows(s0_ref[p, 0])

    ri = _row_iota((CHUNK, w4))
    cj = _lane_iota((CHUNK, w4)) % CHUNK
    incl = (cj >= ri) if reverse else (cj <= ri)
    strict = (cj > ri) if reverse else (cj < ri)
    eye_sbs = (cj == ri)

    t_r = _row_iota((CHUNK, CHUNK))
    t_c = _lane_iota((CHUNK, CHUNK))
    tri = ((t_c >= t_r) if reverse else (t_c <= t_r)).astype(BF16)
    ones_cc = jnp.ones((CHUNK, CHUNK), BF16)
    gb_all = ggb_ref[...]
    cs = range(n_all)
    rows = [slice(c * CHUNK, (c + 1) * CHUNK) for c in cs]
    gcum = jnp.concatenate([_dot01_exact(tri, gb_all[r, :]) for r in rows], axis=0)
    er = _row_iota((128, w4))
    head = _lane_iota((128, w4)) // CHUNK
    e_g = (er == d * GDN_H + head).astype(BF16)
    e_b = (er == 2 * GDN_H + d * GDN_H + head).astype(BF16)
    gc_all = _rows_dot_exact(gcum, e_g)
    beta_all = _rows_dot_exact(gb_all, e_b, terms=2)
    gct = [_dot01_exact(ones_cc, jnp.where(eye_sbs, gc_all[r, :], 0.0)) for r in rows]

    q = [q_ref[r, :] for r in rows]
    k = [k_ref[r, :] for r in rows]
    v = [v_ref[r, :] for r in rows]
    beta = [beta_all[r, :] for r in rows]
    gc = [gc_all[r, :] for r in rows]
    decay = [jnp.where(incl, jnp.exp(jnp.where(incl, gc[c] - gct[c], 0.0)), 0.0) for c in cs]
    kb = [k[c].astype(BF16) for c in cs]
    kq = [lax.dot_general(jnp.concatenate([kb[c], q[c].astype(BF16)], axis=0), expand_rows(kb[c]),
                          (((1,), (1,)), ((), ())), preferred_element_type=F32) for c in cs]
    qk = [(kq[c][CHUNK:2 * CHUNK] * decay[c]).astype(BF16) for c in cs]
    a = [jnp.where(strict, beta[c] * kq[c][0:CHUNK] * decay[c], 0.0) for c in cs]
    t_inv = [jnp.where(eye_sbs, 1.0, 0.0) - a[c] for c in cs]
    pw = [heads_dot3(a[c], a[c]) for c in cs]
    for stage in range(4):
        mm = heads_dot3 if stage < N_PRECISE else heads_dot1
        both = [mm(jnp.concatenate([t_inv[c], pw[c]], axis=0), pw[c]) for c in cs]
        t_inv = [t_inv[c] + both[c][0:CHUNK] for c in cs]
        pw = [both[c][CHUNK:2 * CHUNK] for c in cs]
    t_inv = [t_inv[c] + heads_dot1(t_inv[c], pw[c]) for c in cs]
    egc = [jnp.exp(gc[c]) for c in cs]
    sol = [heads_dot2(t_inv[c], jnp.concatenate([v[c] * beta[c], k[c] * (beta[c] * egc[c])], axis=1)) for c in cs]
    g_last = [gc[c][0:1, :] if reverse else gc[c][CHUNK - 1:CHUNK, :] for c in cs]
    wq = [jnp.concatenate([sol[c][:, w4:2 * w4], q[c] * egc[c]], axis=0).astype(BF16) for c in cs]
    kdec = [(k[c] * jnp.exp(g_last[c] - gc[c])).astype(BF16) for c in cs]

    for cc in range(n_chunk):
        for p in range(n_par):
            c = p * n_chunk + ((n_chunk - 1 - cc) if reverse else cc)
            s = s_ref[p]
            ws_qs = jnp.dot(wq[c], s.astype(BF16), preferred_element_type=F32)
            v_new = sol[c][:, 0:w4] - ws_qs[0:CHUNK]
            vb = v_new.astype(BF16)
            o_ref[rows[c], :] = (
                ws_qs[CHUNK:2 * CHUNK] + jnp.dot(qk[c], expand_rows(vb), preferred_element_type=F32))
            upd = lax.dot_general(kdec[c], vb, (((0,), (0,)), ((), ())), preferred_element_type=F32)
            s_ref[p] = s * jnp.exp(g_last[c]) + jnp.where(blockmask, upd, 0.0)

    @pl.when(ti == n_tile - 1)
    def _():
        for p in range(n_par):
            s = s_ref[p]
            sT_ref[p, 0] = s[0:64] + s[64:128] + s[128:192] + s[192:256]


def _gdn_call(q, k, v, ggb, s0, *, d, l_state, n_seq, t, row0, tt, n_par):
    reverse = d == 1
    n_tile = t // tt
    assert n_par == 1 or n_tile == 1
    b0 = row0 // (tt * n_par)
    tidx = (lambda i: n_tile - 1 - i) if reverse else (lambda i: i)
    blk = lambda w: pl.BlockSpec((tt * n_par, w), lambda s, i: (b0 + s * n_tile + tidx(i), 0))
    return pl.pallas_call(
        functools.partial(_gdn_kernel, d=d, reverse=reverse, n_par=n_par, n_chunk=tt // CHUNK, n_tile=n_tile),
        grid=(n_seq // n_par, n_tile),
        in_specs=[blk(256), blk(256), blk(256), blk(128),
                  pl.BlockSpec((n_par, None, 1, CHUNK, 256), lambda s, i: (s, l_state, d, 0, 0))],
        out_specs=[pl.BlockSpec((tt * n_par, 256), lambda s, i: (s * n_tile + tidx(i), 0)),
                   pl.BlockSpec((n_par, 1, CHUNK, 256), lambda s, i: (s, 0, 0, 0))],
        out_shape=[jax.ShapeDtypeStruct((n_seq * t, 256), F32),
                   jax.ShapeDtypeStruct((n_seq, 1, CHUNK, 256), F32)],
        scratch_shapes=[pltpu.VMEM((n_par, 256, 256), F32)],
        compiler_params=_cparams(("parallel", "arbitrary")),
        name="gdn_bwd" if reverse else "gdn_fwd",
    )(q, k, v, ggb, s0)


def _mix_mlp_kernel(modrow_ref, xc_ref, xd_ref, mod_ref, g1_ref,
                    oac_ref, oad_ref, obc_ref, obd_ref, ofc_ref, ofd_ref, orc_ref, ord_ref, odc_ref, odd_ref,
                    gz_ref, gng_ref, ones_ref, wm_ref, bm_ref, wb_ref, wo_ref, g2_ref, w1_ref, w2_ref,
                    yc_ref, yd_ref, *, n_ctx_tiles):
    is_ctx = pl.program_id(0) < n_ctx_tiles
    pick = lambda c_ref, d_ref: jnp.where(is_ctx, c_ref[...], d_ref[...])
    x = pick(xc_ref, xd_ref)
    h = _modulate(x, g1_ref[...], mod_ref[0, 0:1, :], mod_ref[0, 1:2, :]).astype(BF16)
    oc = pick(ofc_ref, ofd_ref) + pick(orc_ref, ord_ref)
    ms = _group_sum(oc * oc, ones_ref[...]) * (1.0 / GDN_DK)
    oc = (oc * lax.rsqrt(ms + EPS) * gng_ref[...]) * _silu(gz_ref[...])
    branches = (pick(oac_ref, oad_ref), pick(obc_ref, obd_ref), oc, pick(odc_ref, odd_ref))
    acc = None
    for m in range(N_BRANCH):
        cols = slice(D_MODEL * m, D_MODEL * (m + 1))
        gate = _sigmoid(jnp.dot(h, wm_ref[:, cols], preferred_element_type=F32) + bm_ref[:, cols])
        term = gate * jnp.dot(branches[m].astype(BF16), wb_ref[m], preferred_element_type=F32)
        acc = term if acc is None else acc + term
    x = x + mod_ref[0, 2:3, :] * jnp.dot(acc.astype(BF16), wo_ref[...], preferred_element_type=F32)

    h = _modulate(x, g2_ref[...], mod_ref[0, 3:4, :], mod_ref[0, 4:5, :]).astype(BF16)
    acc = None
    for j in range(D_FF // D_MODEL):
        cols = slice(D_MODEL * j, D_MODEL * (j + 1))
        a = jnp.maximum(jnp.dot(h, w1_ref[:, cols], preferred_element_type=F32), 0.0)
        term = jnp.dot((a * a).astype(BF16), w2_ref[cols, :], preferred_element_type=F32)
        acc = term if acc is None else acc + term
    y = x + mod_ref[0, 5:6, :] * acc

    @pl.when(is_ctx)
    def _():
        yc_ref[...] = y

    @pl.when(jnp.logical_not(is_ctx))
    def _():
        yd_ref[...] = y


def _mix_mlp_call(modrow, x_pair, mod_all, g1, pairs, gz, gng, ones_bd, wm, bm, wb, wo, g2, w1, w2, *, l, tm):
    x_c, x_d = x_pair
    n, d = x_c.shape[0] + x_d.shape[0], x_c.shape[1]
    nct = x_c.shape[0] // tm
    row_tile = lambda w: pl.BlockSpec((tm, w), lambda i, *_: (i, 0))
    pair = lambda w: [pl.BlockSpec((tm, w), lambda i, *_: (jnp.minimum(i, nct - 1), 0)),
                      pl.BlockSpec((tm, w), lambda i, *_: (jnp.maximum(i - nct, 0), 0))]
    pair_specs = [sp for _ in pairs for sp in pair(256)]
    resident = lambda arr: _layer_spec(arr, l, single_buffer=True)
    grid_spec = pltpu.PrefetchScalarGridSpec(
        num_scalar_prefetch=1, grid=(n // tm,),
        in_specs=pair(d) + [pl.BlockSpec((None, 1, 6, d), lambda i, modrow: (l, modrow[i], 0, 0)),
                            _layer_spec(g1, l)] + pair_specs
                 + [row_tile(256), _layer_spec(gng, l), pl.BlockSpec((256, 256), lambda i, *_: (0, 0)),
                    resident(wm), _layer_spec(bm, l), resident(wb), resident(wo),
                    _layer_spec(g2, l), resident(w1), resident(w2)],
        out_specs=pair(d))
    flat = [a for pr in pairs for a in pr]
    return pl.pallas_call(
        functools.partial(_mix_mlp_kernel, n_ctx_tiles=nct), grid_spec=grid_spec,
        out_shape=[jax.ShapeDtypeStruct(x_c.shape, F32), jax.ShapeDtypeStruct(x_d.shape, F32)],
        compiler_params=_cparams(("arbitrary",)),
        name="merge_mlp",
    )(modrow, x_c, x_d, mod_all, g1, *flat, gz, gng, ones_bd, wm, bm, wb, wo, g2, w1, w2)


def _rope_tables(t, tm):
    n_freq = HEAD_DIM // 4
    inv = np.float32(ROPE_BASE) ** (-np.arange(n_freq, dtype=np.float32) / np.float32(n_freq))
    pos = np.arange(t)
    row = (pos // GRID_W).astype(np.float32)[:, None]
    col = (pos % GRID_W).astype(np.float32)[:, None]
    ar = (row * inv).astype(np.float32)
    ac = (col * inv).astype(np.float32)
    cos64 = np.concatenate([np.cos(ar), np.cos(ar), np.cos(ac), np.cos(ac)], axis=1)
    sin64 = np.concatenate([-np.sin(ar), np.sin(ar), -np.sin(ac), np.sin(ac)], axis=1)
    cos = np.concatenate([np.ones((tm, 128), np.float32), np.tile(cos64, (1, 2)).astype(np.float32)], axis=0)
    sin = np.concatenate([np.zeros((tm, 128), np.float32), np.tile(sin64, (1, 2)).astype(np.float32)], axis=0)
    return jnp.asarray(cos), jnp.asarray(sin)


def _tile_meta(n_ctx, n_seq_dec, t_ctx, t_dec, tm):
    nct = n_ctx // tm
    per = t_dec // tm
    ndt = n_seq_dec * per
    idx = np.arange(nct + ndt)
    dec = idx >= nct
    di = np.maximum(idx - nct, 0)
    modrow = np.where(dec, 1 + di // per, 0)
    ropeblk = np.where(dec, 1 + di % per, 0)
    tiles_ctx = max(t_ctx // tm, 1)
    isstart = np.where(dec, di % per == 0, idx % tiles_ctx == 0)
    isend = np.where(dec, di % per == per - 1, idx % tiles_ctx == tiles_ctx - 1)
    as_i32 = lambda a: jnp.asarray(a.astype(np.int32))
    return as_i32(modrow), as_i32(ropeblk), as_i32(isstart), as_i32(isend)


def _block_diag_gates(w):
    depth, two, n, c, _ = w.shape
    eye = jnp.eye(n, dtype=w.dtype)
    return jnp.einsum("ldnij,nm->ldnimj", w, eye).reshape(depth, two, n * c, n * c)


def kernel(x_prompt, x_sample, c, cache_a_k, cache_a_v, cache_d_k, cache_d_v, state_lru, state_gdn, c_ctx, mod_w, mod_b, norm1_g, norm2_g, w_in, a_qn_g, a_kn_g, a_sink, lru_conv_w, lru_conv_b, lru_wr, lru_br, lru_wi, lru_bi, lru_lam, gdn_conv_w, gdn_a_log, gdn_dt_bias, gdn_norm_g, d_qn_g, d_kn_g, w_branch, w_merge, b_merge, w_out, mlp_w1, mlp_w2):
    batch, seq, d = x_prompt.shape
    dec_batch, dec_seq, _ = x_sample.shape
    depth = mod_w.shape[0]
    past = cache_a_k.shape[2]
    n_ctx = batch * seq
    n_dec = dec_batch * dec_seq
    tm = 256
    gdn_par = 2 if batch % 2 == 0 else 1
    gdn_tt = 512 if dec_seq % 512 == 0 else 256
    assert d == D_MODEL and seq % tm == 0 and dec_seq % tm == 0 and tm % seq == 0
    assert dec_batch + 1 <= 8 and n_ctx % dec_seq == 0

    cond8 = jnp.zeros((8, d), F32).at[0].set(c_ctx).at[1:1 + dec_batch].set(c)
    mod_all = _mod_call(cond8, mod_w, mod_b).reshape(depth, 8, 6, d)

    meta = _tile_meta(n_ctx, dec_batch, seq, dec_seq, tm)
    cos_t, sin_t = _rope_tables(dec_seq, tm)
    lane = np.arange(256)
    ones_bd = jnp.asarray((lane[:, None] // HEAD_DIM == lane[None, :] // HEAD_DIM).astype(np.float32)).astype(BF16)

    w_packed = jnp.concatenate([w_in[:, :, :2064], jnp.zeros((depth, d, GDN_PAD), F32), w_in[:, :, 2064:]],
                               axis=2).astype(BF16)
    pad128 = lambda v: jnp.pad(v.reshape(depth, 1, -1), ((0, 0), (0, 0), (0, 128 - v[0].size)))
    vecs = (
        (jnp.tile(a_qn_g, (1, 4)) * Q_SCALE)[:, None, :],
        jnp.tile(a_kn_g, (1, 2))[:, None, :],
        (jnp.tile(d_qn_g, (1, 4)) * Q_SCALE)[:, None, :],
        jnp.tile(d_kn_g, (1, 2))[:, None, :],
        lru_conv_w, lru_conv_b[:, None, :], gdn_conv_w,
        pad128(gdn_a_log), pad128(gdn_dt_bias),
    )
    g1 = norm1_g[:, None, :]
    g2 = norm2_g[:, None, :]
    wr_bd = _block_diag_gates(lru_wr)
    wi_bd = _block_diag_gates(lru_wi)
    w_lru = jnp.concatenate([wr_bd[:, 0], wi_bd[:, 0], wr_bd[:, 1], wi_bd[:, 1]], axis=-1).astype(BF16)
    b_lru = jnp.concatenate([lru_br[:, 0], lru_bi[:, 0], lru_br[:, 1], lru_bi[:, 1]], axis=-1)[:, None, :]
    gng = jnp.tile(gdn_norm_g, (1, 4))[:, None, :]
    wm = w_merge.astype(BF16)
    bm = b_merge[:, None, :]
    wb = w_branch.astype(BF16)
    wo = w_out.astype(BF16)
    w1 = mlp_w1.astype(BF16)
    w2 = mlp_w2.astype(BF16)
    sink = a_sink.reshape(-1)
    caches = [t.reshape(dec_batch, depth, past, 2 * HEAD_DIM) for t in (cache_a_k, cache_a_v, cache_d_k, cache_d_v)]
    s0_dec = state_gdn.transpose(0, 1, 2, 4, 3, 5).reshape(dec_batch, depth, 2, CHUNK, 256)
    zeros_lru = jnp.zeros((batch, 1, 2, BRANCH_W), F32)
    zeros_gdn = jnp.zeros((batch, 1, 2, CHUNK, 256), F32)

    x = (x_prompt.reshape(n_ctx, d), x_sample.reshape(n_dec, d))

    kv_ctx, lru_states, gdn_states = None, [], []
    for l in range(depth):
        (qa, ka, va, lx, lg, gq, gk, gv, gz, ggb, qd, kd, vd), kv_ctx = _inproj_call(
            x, meta, mod_all, g1, w_packed, cos_t, sin_t, vecs, ones_bd, kv_ctx, l=l, depth=depth, tm=tm)

        oa_c, od_c = _attn_ctx_call(sink, qa, ka, va, qd, kd, vd, l=l, n_seq=batch, t=seq)
        oa_d = _attn_band_call(sink, qa, ka, va, caches[0], caches[1], l=l, n_seq=dec_batch, t=dec_seq,
                               row0=n_ctx, tq=256)
        od_d = _attn_full_call(qd, kd, vd, caches[2], caches[3], l=l, n_seq=dec_batch, t=dec_seq,
                               row0=n_ctx, tq=256)

        ob_c, st_c = _lru_call(lx, lg, w_lru, b_lru, lru_lam, zeros_lru, l=l, l_state=0, n_seq=batch, t=seq,
                               row0=0)
        ob_d, _ = _lru_call(lx, lg, w_lru, b_lru, lru_lam, state_lru, l=l, l_state=l, n_seq=dec_batch,
                            t=dec_seq, row0=n_ctx)

        oc_pairs, sts = [], []
        for dd in range(2):
            o_c, s_c = _gdn_call(gq, gk, gv, ggb, zeros_gdn, d=dd, l_state=0, n_seq=batch, t=seq, row0=0,
                                 tt=seq, n_par=gdn_par)
            o_d, _ = _gdn_call(gq, gk, gv, ggb, s0_dec, d=dd, l_state=l, n_seq=dec_batch, t=dec_seq,
                               row0=n_ctx, tt=gdn_tt, n_par=1)
            oc_pairs.append((o_c, o_d))
            sts.append(s_c)

        x = _mix_mlp_call(meta[0], x, mod_all, g1,
                          [(oa_c, oa_d), (ob_c, ob_d), oc_pairs[0], oc_pairs[1], (od_c, od_d)],
                          gz, gng, ones_bd, wm, bm, wb, wo, g2, w1, w2, l=l, tm=tm)

        lru_states.append(st_c)
        gdn_states.append(jnp.concatenate(sts, axis=1))

    y_prompt = x[0].reshape(batch, seq, d)
    y_sample = x[1].reshape(dec_batch, dec_seq, d)
    new_kv = [t.reshape(batch, depth, seq, 2, HEAD_DIM) for t in kv_ctx]
    new_lru = jnp.stack(lru_states, axis=1)
    new_gdn = (jnp.stack(gdn_states, axis=1).reshape(batch, depth, 2, GDN_DK, GDN_H, GDN_DK)
               .transpose(0, 1, 2, 4, 3, 5))
    return (y_prompt, y_sample, new_kv[0], new_kv[1], new_kv[2], new_kv[3], new_lru, new_gdn)
```

```python
import functools

import numpy as np
import jax
import jax.numpy as jnp
from jax import lax
from jax.experimental import pallas as pl
from jax.experimental.pallas import tpu as pltpu

F32 = jnp.float32
BF16 = jnp.bfloat16
HIGHEST = lax.Precision.HIGHEST

D_MODEL = 1024
HEAD_DIM = 64
BRANCH_W = 256
N_BRANCH = 4
GRID_W = 64
WINDOW = 128
LRU_C = 8.0
CONV_W = 4
CONV_LEFT = 2
GDN_H = 4
GDN_DK = 64
CHUNK = 64
D_FF = 4 * D_MODEL
ROPE_BASE = 10000.0
EPS = 1e-6
NEG = -1e30
LOG2E = 1.4426950408889634
Q_SCALE = HEAD_DIM ** -0.5 * LOG2E
GDN_PAD = 112
IN_COLS_PACKED = 2688
HALO = 8
N_PRECISE = 4

V7X_VMEM_LIMIT = 56 * 1024 * 1024


def _cparams(sem, vmem=V7X_VMEM_LIMIT):
    return pltpu.CompilerParams(dimension_semantics=sem, vmem_limit_bytes=vmem)


def _layer_spec(arr, l, single_buffer=False):
    nd = arr.ndim - 1
    mode = dict(pipeline_mode=pl.Buffered(1)) if single_buffer else {}
    return pl.BlockSpec((None,) + arr.shape[1:], lambda *_: (l,) + (0,) * nd, **mode)


def _bdot(a, b):
    return jnp.dot(a.astype(BF16), b.astype(BF16), preferred_element_type=F32)


def _split(x):
    hi = x.astype(BF16)
    lo = (x - hi.astype(F32)).astype(BF16)
    return hi, lo


def _split3(x):
    hi = x.astype(BF16)
    r = x - hi.astype(F32)
    mid = r.astype(BF16)
    lo = (r - mid.astype(F32)).astype(BF16)
    return hi, mid, lo


def _group_sum(x, ones_bd):
    return jnp.dot(x.astype(BF16), ones_bd, preferred_element_type=F32)


def _sigmoid(x):
    return 0.5 * jnp.tanh(0.5 * x) + 0.5


def _silu(x):
    return x * _sigmoid(x)


def _softplus(x):
    return jnp.maximum(x, 0.0) + jnp.log1p(jnp.exp(-jnp.abs(x)))


def _modulate(x, g, shift, scale):
    ms = jnp.mean(x * x, axis=-1, keepdims=True)
    return (x * lax.rsqrt(ms + EPS)) * (g * (1.0 + scale)) + shift


def _lane_iota(shape):
    return lax.broadcasted_iota(jnp.int32, shape, len(shape) - 1)


def _row_iota(shape):
    return lax.broadcasted_iota(jnp.int32, shape, len(shape) - 2)


def _mod_kernel(cond_ref, w_ref, b_ref, o_ref):
    hi, lo = _split(_silu(cond_ref[...]))
    w = w_ref[0].astype(BF16)
    o_ref[0] = (jnp.dot(hi, w, preferred_element_type=F32) + jnp.dot(lo, w, preferred_element_type=F32)
                + b_ref[0])


def _mod_call(cond8, mod_w, mod_b):
    depth, d, n = mod_w.shape
    tn = 1536
    return pl.pallas_call(
        _mod_kernel,
        grid=(depth, n // tn),
        in_specs=[
            pl.BlockSpec((8, d), lambda l, j: (0, 0)),
            pl.BlockSpec((1, d, tn), lambda l, j: (l, 0, j)),
            pl.BlockSpec((1, 1, tn), lambda l, j: (l, 0, j)),
        ],
        out_specs=pl.BlockSpec((1, 8, tn), lambda l, j: (l, 0, j)),
        out_shape=jax.ShapeDtypeStruct((depth, 8, n), F32),
        compiler_params=_cparams(("parallel", "parallel")),
        name="mod_vectors",
    )(cond8, mod_w, mod_b.reshape(depth, 1, n))


def _rope(x, cos, sin):
    outs = []
    for j in range(x.shape[1] // 128):
        xb = x[:, 128 * j:128 * (j + 1)]
        lane = _lane_iota(xb.shape)
        sw = jnp.where((lane & 16) == 0, pltpu.roll(xb, 112, 1), pltpu.roll(xb, 16, 1))
        outs.append(xb * cos + sw * sin)
    return outs[0] if len(outs) == 1 else jnp.concatenate(outs, axis=1)


def _centred_conv(g, w, tm):
    rows = g.shape[0]
    acc = None
    for j in range(CONV_W):
        sh = (CONV_LEFT - j) % rows
        gj = g if sh == 0 else pltpu.roll(g, sh, 0)
        term = gj[HALO:HALO + tm] * w[j:j + 1, :]
        acc = term if acc is None else acc + term
    return acc


def _inproj_kernel(modrow_ref, ropeblk_ref, isstart_ref, isend_ref,
                   xpc_ref, xc_ref, xnc_ref, xpd_ref, xd_ref, xnd_ref, mod_ref, g1_ref, w_ref, cos_ref, sin_ref,
                   aqg_ref, akg_ref, dqg_ref, dkg_ref, lcw_ref, lcb_ref, gcw_ref,
                   alog_ref, dtb_ref, ones_ref, *rest, tm, n_ctx_tiles, n_alias):
    (qa_ref, ka_ref, va_ref, lx_ref, lg_ref, gq_ref, gk_ref, gv_ref, gz_ref, ggb_ref,
     qd_ref, kd_ref, vd_ref, kac_ref, vac_ref, kdc_ref, vdc_ref) = rest[n_alias:]
    i = pl.program_id(0)
    is_ctx = i < n_ctx_tiles
    pick = lambda c_ref, d_ref: jnp.where(is_ctx, c_ref[...], d_ref[...])
    xfull = jnp.concatenate([pick(xpc_ref, xpd_ref), pick(xc_ref, xd_ref), pick(xnc_ref, xnd_ref)],
                            axis=0)
    h = _modulate(xfull, g1_ref[...], mod_ref[0, 0:1, :], mod_ref[0, 1:2, :])
    p = jnp.dot(h.astype(BF16), w_ref[...], preferred_element_type=F32)
    hb, hm = slice(0, tm + 2 * HALO), slice(HALO, HALO + tm)
    proj = lambda rows, c0, c1: p[rows, c0:c1]

    ones_bd = ones_ref[...]
    cos = cos_ref[...]
    sin = sin_ref[...]

    pa = proj(hm, 0, 512)
    pd = proj(hm, 2176, 2688)
    va_ref[...] = pa[:, 384:512]
    vd_ref[...] = pd[:, 384:512]

    row = _row_iota((tm + 2 * HALO, 1))
    keep = jnp.logical_and(jnp.logical_or(row >= HALO, isstart_ref[i] == 0),
                           jnp.logical_or(row < HALO + tm, isend_ref[i] == 0))

    lxg = jnp.where(keep, proj(hb, 512, 768), 0.0)
    lx_ref[...] = _centred_conv(lxg, lcw_ref[...], tm) + lcb_ref[...]
    lg_ref[...] = proj(hm, 768, 1024)

    qkv = jnp.where(keep, proj(hb, 1024, 1792), 0.0)
    qkv = _silu(_centred_conv(qkv, gcw_ref[...], tm))
    gq = qkv[:, 0:256]
    gk = qkv[:, 256:512]
    gv_ref[...] = qkv[:, 512:768]
    pz = proj(hm, 1792, 2176)
    gz_ref[...] = pz[:, 0:256]
    ab = pz[:, 256:384]
    g = -jnp.exp(alog_ref[...]) * _softplus(ab + dtb_ref[...])
    lane = _lane_iota(ab.shape)
    ggb_ref[...] = jnp.where(lane < 2 * GDN_H, g, _sigmoid(ab))

    qa_raw, ka_raw, qd_raw, kd_raw = pa[:, 0:256], pa[:, 256:384], pd[:, 0:256], pd[:, 256:384]
    ss256 = _group_sum(jnp.concatenate([qa_raw * qa_raw, qd_raw * qd_raw, gq * gq, gk * gk], axis=0), ones_bd)
    ss128 = _group_sum(jnp.concatenate([ka_raw * ka_raw, kd_raw * kd_raw], axis=0), ones_bd[:128, :128])
    rms = lambda x, ssq, gain: x * lax.rsqrt(ssq * (1.0 / HEAD_DIM) + EPS) * gain

    qa_ref[...] = _rope(rms(qa_raw, ss256[0:tm], aqg_ref[...]), cos, sin)
    qd_ref[...] = _rope(rms(qd_raw, ss256[tm:2 * tm], dqg_ref[...]), cos, sin)
    ka = _rope(rms(ka_raw, ss128[0:tm], akg_ref[...]), cos, sin)
    kd = _rope(rms(kd_raw, ss128[tm:2 * tm], dkg_ref[...]), cos, sin)
    ka_ref[...] = ka
    kd_ref[...] = kd
    gq_ref[...] = gq * lax.rsqrt(ss256[2 * tm:3 * tm] + EPS) * (GDN_DK ** -0.5)
    gk_ref[...] = gk * lax.rsqrt(ss256[3 * tm:4 * tm] + EPS)

    @pl.when(is_ctx)
    def _():
        for ref, val in ((kac_ref, ka), (vac_ref, pa[:, 384:512]), (kdc_ref, kd), (vdc_ref, pd[:, 384:512])):
            if n_alias:
                ref[...] = val
            else:
                for slot in range(ref.shape[0]):
                    ref[slot] = val


def _x_pair_specs(x_pair, tm):
    x_c, x_d = x_pair
    d = x_c.shape[1]
    nct = x_c.shape[0] // tm
    hb = tm // HALO
    specs = []
    for arr, first in ((x_c, 0), (x_d, nct)):
        nblk = arr.shape[0] // tm
        last_hb = arr.shape[0] // HALO - 1
        tile = lambda i, first=first, nblk=nblk: jnp.clip(i - first, 0, nblk - 1)
        specs += [
            pl.BlockSpec((HALO, d), lambda i, *_, t=tile, m=last_hb: (jnp.clip(t(i) * hb - 1, 0, m), 0)),
            pl.BlockSpec((tm, d), lambda i, *_, t=tile: (t(i), 0)),
            pl.BlockSpec((HALO, d), lambda i, *_, t=tile, m=last_hb: (jnp.clip((t(i) + 1) * hb, 0, m), 0)),
        ]
    return specs


def _inproj_call(x_pair, meta, mod_all, g1, w_packed, cos_t, sin_t, vecs, ones_bd, kv_prev, *, l, depth, tm):
    x_c, x_d = x_pair
    n, d = x_c.shape[0] + x_d.shape[0], x_c.shape[1]
    nt = n // tm
    nct = x_c.shape[0] // tm
    row_tile = lambda w: pl.BlockSpec((tm, w), lambda i, *_: (i, 0))
    alias_in = [] if kv_prev is None else list(kv_prev)
    if alias_in:
        kv_spec = pl.BlockSpec((None, None, tm, 128), lambda i, *_: (jnp.minimum(i, nct - 1), l, 0, 0))
    else:
        kv_spec = pl.BlockSpec((None, depth, tm, 128), lambda i, *_: (jnp.minimum(i, nct - 1), 0, 0, 0))
    kv_shape = jax.ShapeDtypeStruct((nct, depth, tm, 128), F32)
    in_specs = _x_pair_specs(x_pair, tm) + [
        pl.BlockSpec((None, 1, 6, d), lambda i, modrow, *_: (l, modrow[i], 0, 0)),
        _layer_spec(g1, l),
        _layer_spec(w_packed, l),
        pl.BlockSpec((tm, 128), lambda i, modrow, ropeblk, *_: (ropeblk[i], 0)),
        pl.BlockSpec((tm, 128), lambda i, modrow, ropeblk, *_: (ropeblk[i], 0)),
    ] + [_layer_spec(v, l) for v in vecs] + [pl.BlockSpec((256, 256), lambda i, *_: (0, 0))]
    in_specs += [pl.BlockSpec(memory_space=pl.ANY)] * len(alias_in)
    widths = (256, 128, 128, 256, 256, 256, 256, 256, 256, 128, 256, 128, 128)
    grid_spec = pltpu.PrefetchScalarGridSpec(
        num_scalar_prefetch=4,
        grid=(nt,),
        in_specs=in_specs,
        out_specs=[row_tile(w) for w in widths] + [kv_spec] * 4,
    )
    operands = (*meta, x_c, x_c, x_c, x_d, x_d, x_d, mod_all, g1, w_packed, cos_t, sin_t, *vecs, ones_bd)
    outs = pl.pallas_call(
        functools.partial(_inproj_kernel, tm=tm, n_ctx_tiles=nct, n_alias=len(alias_in)),
        grid_spec=grid_spec,
        out_shape=[jax.ShapeDtypeStruct((n, w), F32) for w in widths] + [kv_shape] * 4,
        input_output_aliases={len(operands) + u: len(widths) + u for u in range(len(alias_in))},
        compiler_params=_cparams(("arbitrary",)),
        name="in_projection",
    )(*operands, *alias_in)
    return outs[:len(widths)], outs[len(widths):]


def _head_q(q, j, g):
    qj = q[:, 128 * j:128 * (j + 1)]
    lane = _lane_iota(qj.shape)
    sel = (lane < HEAD_DIM) if g == 0 else (lane >= HEAD_DIM)
    return jnp.where(sel, qj, 0.0).astype(BF16)


def _place_heads(res, j):
    r0 = res[0] if j == 0 else pltpu.roll(res[0], HEAD_DIM, 1)
    r1 = res[1] if j == 1 else pltpu.roll(res[1], HEAD_DIM, 1)
    lane = _lane_iota(r0.shape)
    return jnp.where(lane < HEAD_DIM, r0, r1)


def _attend_many(problems, order="staged"):
    work = []
    for pi, (q, ks, vs, masks, sinks, ks_sw) in enumerate(problems):
        if ks_sw is None:
            ks_sw = [pltpu.roll(k, HEAD_DIM, 1) for k in ks]
        work += [(pi, j, g, q, ks, vs, masks, sinks, ks_sw) for j in range(2) for g in range(2)]
    scores, probs, res = {}, {}, {}

    def score(w):
        pi, j, g, q, ks, vs, masks, sinks, ks_sw = w
        qm = _head_q(q, j, g)
        ss = []
        for k, ksw, mk in zip(ks, ks_sw, masks):
            s = lax.dot_general(qm, k if g == j else ksw, (((1,), (1,)), ((), ())), preferred_element_type=F32)
            ss.append(s if mk is None else jnp.where(mk, s, NEG))
        scores[w[:3]] = ss

    def softmax_num(w):
        pi, j, g, q, ks, vs, masks, sinks, ks_sw = w
        ss = scores[w[:3]]
        m = ss[0].max(axis=-1, keepdims=True)
        for s in ss[1:]:
            m = jnp.maximum(m, s.max(axis=-1, keepdims=True))
        if sinks is not None:
            m = jnp.maximum(m, sinks[2 * j + g])
        es = [jnp.exp2(s - m) for s in ss]
        den = sum(e.sum(axis=-1, keepdims=True) for e in es)
        if sinks is not None:
            den = den + jnp.exp2(sinks[2 * j + g] - m)
        probs[w[:3]] = ([e.astype(BF16) for e in es], den)

    def values(w):
        es, den = probs[w[:3]]
        res[w[:3]] = sum(jnp.dot(e, v, preferred_element_type=F32) for e, v in zip(es, w[5])) / den

    if order == "staged":
        for step in (score, softmax_num, values):
            for w in work:
                step(w)
    else:
        score(work[0])
        for prev, nxt in zip(work, work[1:] + [None]):
            if nxt is not None:
                score(nxt)
            softmax_num(prev)
            values(prev)
    return [jnp.concatenate([_place_heads([res[(pi, j, 0)], res[(pi, j, 1)]], j) for j in range(2)], axis=1)
            for pi in range(len(problems))]


def _attend(q, ks, vs, masks, sinks, ks_sw=None, order="staged"):
    return _attend_many([(q, ks, vs, masks, sinks, ks_sw)], order)[0]


def _bf(x):
    return x.astype(BF16)


def _attn_ctx_kernel(sink_ref, qa_ref, ka_ref, va_ref, qd_ref, kd_ref, vd_ref, oa_ref, od_ref, *, l, t, n_par):
    sinks = [sink_ref[4 * l + u] * LOG2E for u in range(4)]
    problems = []
    for p in range(n_par):
        r = slice(p * t, (p + 1) * t)
        problems.append((qa_ref[r, :], [_bf(ka_ref[r, :])], [_bf(va_ref[r, :])], [None], sinks, None))
        problems.append((qd_ref[r, :], [_bf(kd_ref[r, :])], [_bf(vd_ref[r, :])], [None], None, None))
    outs = _attend_many(problems)
    for p in range(n_par):
        oa_ref[p * t:(p + 1) * t, :] = outs[2 * p]
        od_ref[p * t:(p + 1) * t, :] = outs[2 * p + 1]


def _attn_ctx_call(sink, qa, ka, va, qd, kd, vd, *, l, n_seq, t):
    n_par = 2 if n_seq % 2 == 0 else 1
    blk = lambda w: pl.BlockSpec((n_par * t, w), lambda b, *_: (b, 0))
    grid_spec = pltpu.PrefetchScalarGridSpec(
        num_scalar_prefetch=1, grid=(n_seq // n_par,),
        in_specs=[blk(256), blk(128), blk(128), blk(256), blk(128), blk(128)],
        out_specs=[blk(256), blk(256)])
    return pl.pallas_call(
        functools.partial(_attn_ctx_kernel, l=l, t=t, n_par=n_par), grid_spec=grid_spec,
        out_shape=[jax.ShapeDtypeStruct((n_seq * t, 256), F32)] * 2,
        compiler_params=_cparams(("parallel",)),
        name="attn_context",
    )(sink, qa, ka, va, qd, kd, vd)


def _attn_band_kernel(sink_ref, q_ref, k_ref, v_ref, ck_ref, cv_ref, o_ref, *, l, tq, t):
    i = pl.program_id(1)
    start = pl.multiple_of(i * tq, tq)
    prev = pl.multiple_of(jnp.maximum(start - WINDOW, 0), WINDOW)
    nxt = pl.multiple_of(jnp.minimum(start + tq, t - WINDOW), WINDOW)
    k_loc = jnp.concatenate([k_ref[pl.ds(prev, WINDOW), :], k_ref[pl.ds(start, tq), :],
                             k_ref[pl.ds(nxt, WINDOW), :]], axis=0)
    v_loc = jnp.concatenate([v_ref[pl.ds(prev, WINDOW), :], v_ref[pl.ds(start, tq), :],
                             v_ref[pl.ds(nxt, WINDOW), :]], axis=0)
    shape = (tq, tq + 2 * WINDOW)
    qpos = start + _row_iota(shape)
    kpos = start - WINDOW + _lane_iota(shape)
    mask = (kpos >= 0) & (kpos < t) & (jnp.abs(qpos - kpos) <= WINDOW)
    sinks = [sink_ref[4 * l + u] * LOG2E for u in range(4)]
    o_ref[...] = _attend(q_ref[...], [_bf(ck_ref[0]), _bf(k_loc)], [_bf(cv_ref[0]), _bf(v_loc)],
                         [None, mask], sinks)


def _cache_spec(c, l):
    return pl.BlockSpec((1, None) + c.shape[2:], lambda b, i, *_: (b, l, 0, 0))


def _attn_band_call(sink, q, k, v, ck, cv, *, l, n_seq, t, row0, tq):
    nq = t // tq
    qb0 = row0 // tq
    sb0 = row0 // t
    grid_spec = pltpu.PrefetchScalarGridSpec(
        num_scalar_prefetch=1, grid=(n_seq, nq),
        in_specs=[
            pl.BlockSpec((tq, 256), lambda b, i, *_: (qb0 + b * nq + i, 0)),
            pl.BlockSpec((t, 128), lambda b, i, *_: (sb0 + b, 0)),
            pl.BlockSpec((t, 128), lambda b, i, *_: (sb0 + b, 0)),
            _cache_spec(ck, l), _cache_spec(cv, l),
        ],
        out_specs=pl.BlockSpec((tq, 256), lambda b, i, *_: (b * nq + i, 0)))
    return pl.pallas_call(
        functools.partial(_attn_band_kernel, l=l, tq=tq, t=t), grid_spec=grid_spec,
        out_shape=jax.ShapeDtypeStruct((n_seq * t, 256), F32),
        compiler_params=_cparams(("parallel", "parallel")),
        name="attn_banded",
    )(sink, q, k, v, ck, cv)


def _attn_full_kernel(q_ref, k_ref, v_ref, ck_ref, cv_ref, o_ref, ks_ref, vs_ref, *, past, t):
    @pl.when(pl.program_id(1) == 0)
    def _():
        kc = _bf(ck_ref[0])
        kl = _bf(k_ref[...])
        ks_ref[0, 0:past, :] = kc
        ks_ref[0, past:past + t, :] = kl
        ks_ref[1, 0:past, :] = pltpu.roll(kc, HEAD_DIM, 1)
        ks_ref[1, past:past + t, :] = pltpu.roll(kl, HEAD_DIM, 1)
        vs_ref[0:past, :] = _bf(cv_ref[0])
        vs_ref[past:past + t, :] = _bf(v_ref[...])

    o_ref[...] = _attend(q_ref[...], [ks_ref[0]], [vs_ref[...]], [None], None, ks_sw=[ks_ref[1]], order="skewed")


def _attn_full_call(q, k, v, ck, cv, *, l, n_seq, t, row0, tq):
    nq = t // tq
    qb0 = row0 // tq
    sb0 = row0 // t
    past = ck.shape[2]
    assert past % 16 == 0
    return pl.pallas_call(
        functools.partial(_attn_full_kernel, past=past, t=t), grid=(n_seq, nq),
        scratch_shapes=[pltpu.VMEM((2, past + t, 128), BF16), pltpu.VMEM((past + t, 128), BF16)],
        in_specs=[
            pl.BlockSpec((tq, 256), lambda b, i: (qb0 + b * nq + i, 0)),
            pl.BlockSpec((t, 128), lambda b, i: (sb0 + b, 0)),
            pl.BlockSpec((t, 128), lambda b, i: (sb0 + b, 0)),
            _cache_spec(ck, l), _cache_spec(cv, l),
        ],
        out_specs=pl.BlockSpec((tq, 256), lambda b, i: (b * nq + i, 0)),
        out_shape=jax.ShapeDtypeStruct((n_seq * t, 256), F32),
        compiler_params=_cparams(("parallel", "arbitrary")),
        name="attn_full",
    )(q, k, v, ck, cv)


def _gelu_tanh(x):
    return 0.5 * x * (1.0 + jnp.tanh(0.7978845608028654 * (x + 0.044715 * (x * x * x))))


def _lru_kernel(lx_ref, lg_ref, w_ref, b_ref, lam_ref, h0_ref, y_ref, st_ref,
                hf_ref, hb_ref, af_ref, bf_ref, ab_ref, bb_ref, *, t, chunk):
    nc = t // chunk
    ng = chunk // 8
    sp = _softplus(-lam_ref[...])
    row8 = _row_iota((8, BRANCH_W))

    def gates(xc, d, a_ref, b2_ref):
        pre = _bdot(xc, w_ref[:, 512 * d:512 * (d + 1)]) + b_ref[:, 512 * d:512 * (d + 1)]
        r = _sigmoid(pre[:, 0:256])
        ig = _sigmoid(pre[:, 256:512])
        a = jnp.exp((-LRU_C) * r * sp[d:d + 1, :])
        a_ref[...] = a
        om = 1.0 - a * a
        b2_ref[...] = jnp.where(om > 0.0, om * lax.rsqrt(om), 0.0) * (ig * xc)

    def local_scan(a8, b8, reverse):
        for s in (1, 2, 4):
            sh = (8 - s) if reverse else s
            m = (row8 < 8 - s) if reverse else (row8 >= s)
            a_sh = pltpu.roll(a8, sh, 0)
            b_sh = pltpu.roll(b8, sh, 0)
            b8 = jnp.where(m, a8 * b_sh + b8, b8)
            a8 = jnp.where(m, a8 * a_sh, a8)
        return a8, b8

    def chunk_body(cc, carry):
        rf = pl.multiple_of(cc * chunk, chunk)
        rb = pl.multiple_of((nc - 1 - cc) * chunk, chunk)
        gates(lx_ref[pl.ds(rf, chunk), :], 0, af_ref, bf_ref)
        gates(lx_ref[pl.ds(rb, chunk), :], 1, ab_ref, bb_ref)

        def grp(gg, c2):
            hf, hb = c2
            gf = pl.multiple_of(gg * 8, 8)
            gb = pl.multiple_of((ng - 1 - gg) * 8, 8)
            a8, b8 = local_scan(af_ref[pl.ds(gf, 8), :], bf_ref[pl.ds(gf, 8), :], False)
            h8 = b8 + a8 * hf
            hf_ref[pl.ds(rf + gf, 8), :] = h8
            a8, b8 = local_scan(ab_ref[pl.ds(gb, 8), :], bb_ref[pl.ds(gb, 8), :], True)
            g8 = b8 + a8 * hb
            hb_ref[pl.ds(rb + gb, 8), :] = g8
            return (jnp.broadcast_to(h8[7:8, :], (8, BRANCH_W)), jnp.broadcast_to(g8[0:1, :], (8, BRANCH_W)))

        return lax.fori_loop(0, ng, grp, carry, unroll=4)

    init = (jnp.broadcast_to(h0_ref[0, 0:1, :], (8, BRANCH_W)), jnp.broadcast_to(h0_ref[0, 1:2, :], (8, BRANCH_W)))
    hf, hb = lax.fori_loop(0, nc, chunk_body, init)
    st_ref[0, 0:1, :] = hf[0:1, :]
    st_ref[0, 1:2, :] = hb[0:1, :]

    def combine(c, _):
        rows = pl.ds(pl.multiple_of(c * chunk, chunk), chunk)
        y_ref[rows, :] = (hf_ref[rows, :] + hb_ref[rows, :]) * _gelu_tanh(lg_ref[rows, :])
        return 0

    lax.fori_loop(0, nc, combine, 0)


def _lru_call(lx, lg, w, b, lam, h0, *, l, l_state, n_seq, t, row0):
    sb0 = row0 // t
    chunk = min(t, 256)
    seq = lambda: pl.BlockSpec((t, BRANCH_W), lambda s: (sb0 + s, 0))
    return pl.pallas_call(
        functools.partial(_lru_kernel, t=t, chunk=chunk),
        grid=(n_seq,),
        in_specs=[seq(), seq(), _layer_spec(w, l), _layer_spec(b, l), _layer_spec(lam, l),
                  pl.BlockSpec((1, None, 2, BRANCH_W), lambda s: (s, l_state, 0, 0))],
        out_specs=[pl.BlockSpec((t, BRANCH_W), lambda s: (s, 0)),
                   pl.BlockSpec((1, 2, BRANCH_W), lambda s: (s, 0, 0))],
        out_shape=[jax.ShapeDtypeStruct((n_seq * t, BRANCH_W), F32),
                   jax.ShapeDtypeStruct((n_seq, 2, BRANCH_W), F32)],
        scratch_shapes=[pltpu.VMEM((t, BRANCH_W), F32), pltpu.VMEM((t, BRANCH_W), F32)]
                       + [pltpu.VMEM((chunk, BRANCH_W), F32)] * 4,
        compiler_params=_cparams(("parallel",)),
        name="rglru",
    )(lx, lg, w, b, lam, h0)


def _rows_dot_exact(x, w01, terms=3):
    r = x.shape[0]
    parts = _split3(x)[:terms] if terms == 3 else _split(x)
    out = jnp.dot(jnp.concatenate(parts, axis=0), w01, preferred_element_type=F32)
    return sum(out[u * r:(u + 1) * r] for u in range(terms))


def _dot01_exact(m01, y):
    return sum(jnp.dot(m01, part, preferred_element_type=F32) for part in _split3(y))


def _gdn_kernel(q_ref, k_ref, v_ref, ggb_ref, s0_ref, o_ref, sT_ref, s_ref, *, d, reverse, n_par, n_chunk, n_tile):
    ti = pl.program_id(1)
    w4 = GDN_H * GDN_DK
    n_all = n_par * n_chunk
    tt = n_all * CHUNK
    blockmask = (_row_iota((w4, w4)) // CHUNK) == (_lane_iota((w4, w4)) // CHUNK)

    def expand_rows(y):
        yt = jnp.concatenate([y] * GDN_H, axis=0)
        zero = jnp.zeros((), y.dtype)
        parts = [jnp.where(blockmask, yt[:, w4 * u:w4 * (u + 1)], zero) for u in range(y.shape[1] // w4)]
        return parts[0] if len(parts) == 1 else jnp.concatenate(parts, axis=1)

    def heads_dot3(lhs, y):
        r = lhs.shape[0]
        lh, ll = _split(lhs)
        yh, yl = _split(y)
        out = jnp.dot(jnp.concatenate([lh, ll], axis=0), expand_rows(yh), preferred_element_type=F32)
        return out[0:r] + out[r:2 * r] + jnp.dot(lh, expand_rows(yl), preferred_element_type=F32)

    def heads_dot2(lhs, y):
        r = lhs.shape[0]
        lh, ll = _split(lhs)
        out = jnp.dot(jnp.concatenate([lh, ll], axis=0), expand_rows(y.astype(BF16)), preferred_element_type=F32)
        return out[0:r] + out[r:2 * r]

    def heads_dot1(lhs, y):
        return jnp.dot(lhs.astype(BF16), expand_rows(y.astype(BF16)), preferred_element_type=F32)

    @pl.when(ti == 0)
    def _():
        for p in range(n_par):
            s_ref[p] = expand_rows(s0_ref[p, 0])

    ri = _row_iota((CHUNK, w4))
    cj = _lane_iota((CHUNK, w4)) % CHUNK
    incl = (cj >= ri) if reverse else (cj <= ri)
    strict = (cj > ri) if reverse else (cj < ri)
    eye_sbs = (cj == ri)

    t_r = _row_iota((CHUNK, CHUNK))
    t_c = _lane_iota((CHUNK, CHUNK))
    tri = ((t_c >= t_r) if reverse else (t_c <= t_r)).astype(BF16)
    ones_cc = jnp.ones((CHUNK, CHUNK), BF16)
    gb_all = ggb_ref[...].reshape(tt, 128)
    cs = range(n_all)
    rows = [slice(c * CHUNK, (c + 1) * CHUNK) for c in cs]
    gcum = jnp.concatenate([_dot01_exact(tri, gb_all[r, :]) for r in rows], axis=0)
    er = _row_iota((128, w4))
    head = _lane_iota((128, w4)) // CHUNK
    e_g = (er == d * GDN_H + head).astype(BF16)
    e_b = (er == 2 * GDN_H + d * GDN_H + head).astype(BF16)
    gc_all = _rows_dot_exact(gcum, e_g)
    beta_all = _rows_dot_exact(gb_all, e_b, terms=2)
    gct = [_dot01_exact(ones_cc, jnp.where(eye_sbs, gc_all[r, :], 0.0)) for r in rows]

    at = lambda c: (c // n_chunk, slice((c % n_chunk) * CHUNK, (c % n_chunk + 1) * CHUNK))
    q = [q_ref[at(c)[0], at(c)[1], :] for c in cs]
    k = [k_ref[at(c)[0], at(c)[1], :] for c in cs]
    v = [v_ref[at(c)[0], at(c)[1], :] for c in cs]
    beta = [beta_all[r, :] for r in rows]
    gc = [gc_all[r, :] for r in rows]
    decay = [jnp.where(incl, jnp.exp(jnp.where(incl, gc[c] - gct[c], 0.0)), 0.0) for c in cs]
    kb = [k[c].astype(BF16) for c in cs]
    kq = [lax.dot_general(jnp.concatenate([kb[c], q[c].astype(BF16)], axis=0), expand_rows(kb[c]),
                          (((1,), (1,)), ((), ())), preferred_element_type=F32) for c in cs]
    qk = [(kq[c][CHUNK:2 * CHUNK] * decay[c]).astype(BF16) for c in cs]
    a = [jnp.where(strict, beta[c] * kq[c][0:CHUNK] * decay[c], 0.0) for c in cs]
    t_inv = [jnp.where(eye_sbs, 1.0, 0.0) - a[c] for c in cs]
    pw = [heads_dot3(a[c], a[c]) for c in cs]
    for stage in range(4):
        mm = heads_dot3 if stage < N_PRECISE else heads_dot1
        both = [mm(jnp.concatenate([t_inv[c], pw[c]], axis=0), pw[c]) for c in cs]
        t_inv = [t_inv[c] + both[c][0:CHUNK] for c in cs]
        pw = [both[c][CHUNK:2 * CHUNK] for c in cs]
    t_inv = [t_inv[c] + heads_dot1(t_inv[c], pw[c]) for c in cs]
    egc = [jnp.exp(gc[c]) for c in cs]
    sol = [heads_dot2(t_inv[c], jnp.concatenate([v[c] * beta[c], k[c] * (beta[c] * egc[c])], axis=1)) for c in cs]
    g_last = [gc[c][0:1, :] if reverse else gc[c][CHUNK - 1:CHUNK, :] for c in cs]
    wq = [jnp.concatenate([sol[c][:, w4:2 * w4], q[c] * egc[c]], axis=0).astype(BF16) for c in cs]
    kdec = [(k[c] * jnp.exp(g_last[c] - gc[c])).astype(BF16) for c in cs]

    for cc in range(n_chunk):
        for p in range(n_par):
            c = p * n_chunk + ((n_chunk - 1 - cc) if reverse else cc)
            s = s_ref[p]
            ws_qs = jnp.dot(wq[c], s.astype(BF16), preferred_element_type=F32)
            v_new = sol[c][:, 0:w4] - ws_qs[0:CHUNK]
            vb = v_new.astype(BF16)
            o_ref[at(c)[0], at(c)[1], :] = (
                ws_qs[CHUNK:2 * CHUNK] + jnp.dot(qk[c], expand_rows(vb), preferred_element_type=F32))
            upd = lax.dot_general(kdec[c], vb, (((0,), (0,)), ((), ())), preferred_element_type=F32)
            s_ref[p] = s * jnp.exp(g_last[c]) + jnp.where(blockmask, upd, 0.0)

    @pl.when(ti == n_tile - 1)
    def _():
        for p in range(n_par):
            s = s_ref[p]
            sT_ref[p, 0] = s[0:64] + s[64:128] + s[128:192] + s[192:256]


def _gdn_call(q, k, v, ggb, s0, *, d, l_state, n_seq, t, row0, tt, n_par):
    reverse = d == 1
    n_tile = t // tt
    assert row0 % (t * n_par) == 0 and n_seq % n_par == 0
    g0 = row0 // (t * n_par)
    tidx = (lambda i: n_tile - 1 - i) if reverse else (lambda i: i)
    blk = lambda w: pl.BlockSpec((n_par, tt, w), lambda s, i: (g0 + s, tidx(i), 0))
    view = lambda a: a.reshape(a.shape[0] // t, t, a.shape[1])
    o, s_final = pl.pallas_call(
        functools.partial(_gdn_kernel, d=d, reverse=reverse, n_par=n_par, n_chunk=tt // CHUNK, n_tile=n_tile),
        grid=(n_seq // n_par, n_tile),
        in_specs=[blk(256), blk(256), blk(256), blk(128),
                  pl.BlockSpec((n_par, None, 1, CHUNK, 256), lambda s, i: (s, l_state, d, 0, 0))],
        out_specs=[pl.BlockSpec((n_par, tt, 256), lambda s, i: (s, tidx(i), 0)),
                   pl.BlockSpec((n_par, 1, CHUNK, 256), lambda s, i: (s, 0, 0, 0))],
        out_shape=[jax.ShapeDtypeStruct((n_seq, t, 256), F32),
                   jax.ShapeDtypeStruct((n_seq, 1, CHUNK, 256), F32)],
        scratch_shapes=[pltpu.VMEM((n_par, 256, 256), F32)],
        compiler_params=_cparams(("parallel", "arbitrary")),
        name="gdn_bwd" if reverse else "gdn_fwd",
    )(view(q), view(k), view(v), view(ggb), s0)
    return o.reshape(n_seq * t, 256), s_final


def _mix_mlp_kernel(modrow_ref, xc_ref, xd_ref, mod_ref, g1_ref,
                    oac_ref, oad_ref, obc_ref, obd_ref, ofc_ref, ofd_ref, orc_ref, ord_ref, odc_ref, odd_ref,
                    gz_ref, gng_ref, ones_ref, wm_ref, bm_ref, wb_ref, wo_ref, g2_ref, w1_ref, w2_ref,
                    yc_ref, yd_ref, *, n_ctx_tiles):
    is_ctx = pl.program_id(0) < n_ctx_tiles
    pick = lambda c_ref, d_ref: jnp.where(is_ctx, c_ref[...], d_ref[...])
    x = pick(xc_ref, xd_ref)
    h = _modulate(x, g1_ref[...], mod_ref[0, 0:1, :], mod_ref[0, 1:2, :]).astype(BF16)
    oc = pick(ofc_ref, ofd_ref) + pick(orc_ref, ord_ref)
    ms = _group_sum(oc * oc, ones_ref[...]) * (1.0 / GDN_DK)
    oc = (oc * lax.rsqrt(ms + EPS) * gng_ref[...]) * _silu(gz_ref[...])
    branches = (pick(oac_ref, oad_ref), pick(obc_ref, obd_ref), oc, pick(odc_ref, odd_ref))
    acc = None
    for m in range(N_BRANCH):
        cols = slice(D_MODEL * m, D_MODEL * (m + 1))
        gate = _sigmoid(jnp.dot(h, wm_ref[:, cols], preferred_element_type=F32) + bm_ref[:, cols])
        term = gate * jnp.dot(branches[m].astype(BF16), wb_ref[m], preferred_element_type=F32)
        acc = term if acc is None else acc + term
    x = x + mod_ref[0, 2:3, :] * jnp.dot(acc.astype(BF16), wo_ref[...], preferred_element_type=F32)

    h = _modulate(x, g2_ref[...], mod_ref[0, 3:4, :], mod_ref[0, 4:5, :]).astype(BF16)
    acc = None
    for j in range(D_FF // D_MODEL):
        cols = slice(D_MODEL * j, D_MODEL * (j + 1))
        a = jnp.maximum(jnp.dot(h, w1_ref[:, cols], preferred_element_type=F32), 0.0)
        term = jnp.dot((a * a).astype(BF16), w2_ref[cols, :], preferred_element_type=F32)
        acc = term if acc is None else acc + term
    y = x + mod_ref[0, 5:6, :] * acc

    @pl.when(is_ctx)
    def _():
        yc_ref[...] = y

    @pl.when(jnp.logical_not(is_ctx))
    def _():
        yd_ref[...] = y


def _mix_mlp_call(modrow, x_pair, mod_all, g1, pairs, gz, gng, ones_bd, wm, bm, wb, wo, g2, w1, w2, *, l, tm):
    x_c, x_d = x_pair
    n, d = x_c.shape[0] + x_d.shape[0], x_c.shape[1]
    nct = x_c.shape[0] // tm
    row_tile = lambda w: pl.BlockSpec((tm, w), lambda i, *_: (i, 0))
    pair = lambda w: [pl.BlockSpec((tm, w), lambda i, *_: (jnp.minimum(i, nct - 1), 0)),
                      pl.BlockSpec((tm, w), lambda i, *_: (jnp.maximum(i - nct, 0), 0))]
    pair_specs = [sp for _ in pairs for sp in pair(256)]
    resident = lambda arr: _layer_spec(arr, l, single_buffer=True)
    grid_spec = pltpu.PrefetchScalarGridSpec(
        num_scalar_prefetch=1, grid=(n // tm,),
        in_specs=pair(d) + [pl.BlockSpec((None, 1, 6, d), lambda i, modrow: (l, modrow[i], 0, 0)),
                            _layer_spec(g1, l)] + pair_specs
                 + [row_tile(256), _layer_spec(gng, l), pl.BlockSpec((256, 256), lambda i, *_: (0, 0)),
                    resident(wm), _layer_spec(bm, l), resident(wb), resident(wo),
                    _layer_spec(g2, l), resident(w1), resident(w2)],
        out_specs=pair(d))
    flat = [a for pr in pairs for a in pr]
    return pl.pallas_call(
        functools.partial(_mix_mlp_kernel, n_ctx_tiles=nct), grid_spec=grid_spec,
        out_shape=[jax.ShapeDtypeStruct(x_c.shape, F32), jax.ShapeDtypeStruct(x_d.shape, F32)],
        compiler_params=_cparams(("arbitrary",)),
        name="merge_mlp",
    )(modrow, x_c, x_d, mod_all, g1, *flat, gz, gng, ones_bd, wm, bm, wb, wo, g2, w1, w2)


def _rope_tables(t, tm):
    n_freq = HEAD_DIM // 4
    inv = np.float32(ROPE_BASE) ** (-np.arange(n_freq, dtype=np.float32) / np.float32(n_freq))
    pos = np.arange(t)
    row = (pos // GRID_W).astype(np.float32)[:, None]
    col = (pos % GRID_W).astype(np.float32)[:, None]
    ar = (row * inv).astype(np.float32)
    ac = (col * inv).astype(np.float32)
    cos64 = np.concatenate([np.cos(ar), np.cos(ar), np.cos(ac), np.cos(ac)], axis=1)
    sin64 = np.concatenate([-np.sin(ar), np.sin(ar), -np.sin(ac), np.sin(ac)], axis=1)
    cos = np.concatenate([np.ones((tm, 128), np.float32), np.tile(cos64, (1, 2)).astype(np.float32)], axis=0)
    sin = np.concatenate([np.zeros((tm, 128), np.float32), np.tile(sin64, (1, 2)).astype(np.float32)], axis=0)
    return jnp.asarray(cos), jnp.asarray(sin)


def _tile_meta(n_ctx, n_seq_dec, t_ctx, t_dec, tm):
    nct = n_ctx // tm
    per = t_dec // tm
    ndt = n_seq_dec * per
    idx = np.arange(nct + ndt)
    dec = idx >= nct
    di = np.maximum(idx - nct, 0)
    modrow = np.where(dec, 1 + di // per, 0)
    ropeblk = np.where(dec, 1 + di % per, 0)
    tiles_ctx = max(t_ctx // tm, 1)
    isstart = np.where(dec, di % per == 0, idx % tiles_ctx == 0)
    isend = np.where(dec, di % per == per - 1, idx % tiles_ctx == tiles_ctx - 1)
    as_i32 = lambda a: jnp.asarray(a.astype(np.int32))
    return as_i32(modrow), as_i32(ropeblk), as_i32(isstart), as_i32(isend)


def _block_diag_gates(w):
    depth, two, n, c, _ = w.shape
    eye = jnp.eye(n, dtype=w.dtype)
    return jnp.einsum("ldnij,nm->ldnimj", w, eye).reshape(depth, two, n * c, n * c)


def kernel(x_prompt, x_sample, c, cache_a_k, cache_a_v, cache_d_k, cache_d_v, state_lru, state_gdn, c_ctx, mod_w, mod_b, norm1_g, norm2_g, w_in, a_qn_g, a_kn_g, a_sink, lru_conv_w, lru_conv_b, lru_wr, lru_br, lru_wi, lru_bi, lru_lam, gdn_conv_w, gdn_a_log, gdn_dt_bias, gdn_norm_g, d_qn_g, d_kn_g, w_branch, w_merge, b_merge, w_out, mlp_w1, mlp_w2):
    batch, seq, d = x_prompt.shape
    dec_batch, dec_seq, _ = x_sample.shape
    depth = mod_w.shape[0]
    past = cache_a_k.shape[2]
    n_ctx = batch * seq
    n_dec = dec_batch * dec_seq
    tm = 256
    gdn_par = 2 if batch % 2 == 0 else 1
    gdn_par_dec = 2 if (dec_batch % 2 == 0 and n_ctx % (2 * dec_seq) == 0) else 1
    gdn_tt = 256 if gdn_par_dec == 2 else (512 if dec_seq % 512 == 0 else 256)
    assert d == D_MODEL and seq % tm == 0 and dec_seq % tm == 0 and tm % seq == 0
    assert dec_batch + 1 <= 8 and n_ctx % dec_seq == 0

    cond8 = jnp.zeros((8, d), F32).at[0].set(c_ctx).at[1:1 + dec_batch].set(c)
    mod_all = _mod_call(cond8, mod_w, mod_b).reshape(depth, 8, 6, d)

    meta = _tile_meta(n_ctx, dec_batch, seq, dec_seq, tm)
    cos_t, sin_t = _rope_tables(dec_seq, tm)
    lane = np.arange(256)
    ones_bd = jnp.asarray((lane[:, None] // HEAD_DIM == lane[None, :] // HEAD_DIM).astype(np.float32)).astype(BF16)

    w_packed = jnp.concatenate([w_in[:, :, :2064], jnp.zeros((depth, d, GDN_PAD), F32), w_in[:, :, 2064:]],
                               axis=2).astype(BF16)
    pad128 = lambda v: jnp.pad(v.reshape(depth, 1, -1), ((0, 0), (0, 0), (0, 128 - v[0].size)))
    vecs = (
        (jnp.tile(a_qn_g, (1, 4)) * Q_SCALE)[:, None, :],
        jnp.tile(a_kn_g, (1, 2))[:, None, :],
        (jnp.tile(d_qn_g, (1, 4)) * Q_SCALE)[:, None, :],
        jnp.tile(d_kn_g, (1, 2))[:, None, :],
        lru_conv_w, lru_conv_b[:, None, :], gdn_conv_w,
        pad128(gdn_a_log), pad128(gdn_dt_bias),
    )
    g1 = norm1_g[:, None, :]
    g2 = norm2_g[:, None, :]
    wr_bd = _block_diag_gates(lru_wr)
    wi_bd = _block_diag_gates(lru_wi)
    w_lru = jnp.concatenate([wr_bd[:, 0], wi_bd[:, 0], wr_bd[:, 1], wi_bd[:, 1]], axis=-1).astype(BF16)
    b_lru = jnp.concatenate([lru_br[:, 0], lru_bi[:, 0], lru_br[:, 1], lru_bi[:, 1]], axis=-1)[:, None, :]
    gng = jnp.tile(gdn_norm_g, (1, 4))[:, None, :]
    wm = w_merge.astype(BF16)
    bm = b_merge[:, None, :]
    wb = w_branch.astype(BF16)
    wo = w_out.astype(BF16)
    w1 = mlp_w1.astype(BF16)
    w2 = mlp_w2.astype(BF16)
    sink = a_sink.reshape(-1)
    caches = [t.reshape(dec_batch, depth, past, 2 * HEAD_DIM) for t in (cache_a_k, cache_a_v, cache_d_k, cache_d_v)]
    s0_dec = state_gdn.transpose(0, 1, 2, 4, 3, 5).reshape(dec_batch, depth, 2, CHUNK, 256)
    zeros_lru = jnp.zeros((batch, 1, 2, BRANCH_W), F32)
    zeros_gdn = jnp.zeros((batch, 1, 2, CHUNK, 256), F32)

    x = (x_prompt.reshape(n_ctx, d), x_sample.reshape(n_dec, d))

    kv_ctx, lru_states, gdn_states = None, [], []
    for l in range(depth):
        (qa, ka, va, lx, lg, gq, gk, gv, gz, ggb, qd, kd, vd), kv_ctx = _inproj_call(
            x, meta, mod_all, g1, w_packed, cos_t, sin_t, vecs, ones_bd, kv_ctx, l=l, depth=depth, tm=tm)

        oa_c, od_c = _attn_ctx_call(sink, qa, ka, va, qd, kd, vd, l=l, n_seq=batch, t=seq)
        oa_d = _attn_band_call(sink, qa, ka, va, caches[0], caches[1], l=l, n_seq=dec_batch, t=dec_seq,
                               row0=n_ctx, tq=256)
        od_d = _attn_full_call(qd, kd, vd, caches[2], caches[3], l=l, n_seq=dec_batch, t=dec_seq,
                               row0=n_ctx, tq=256)

        ob_c, st_c = _lru_call(lx, lg, w_lru, b_lru, lru_lam, zeros_lru, l=l, l_state=0, n_seq=batch, t=seq,
                               row0=0)
        ob_d, _ = _lru_call(lx, lg, w_lru, b_lru, lru_lam, state_lru, l=l, l_state=l, n_seq=dec_batch,
                            t=dec_seq, row0=n_ctx)

        oc_pairs, sts = [], []
        for dd in range(2):
            o_c, s_c = _gdn_call(gq, gk, gv, ggb, zeros_gdn, d=dd, l_state=0, n_seq=batch, t=seq, row0=0,
                                 tt=seq, n_par=gdn_par)
            o_d, _ = _gdn_call(gq, gk, gv, ggb, s0_dec, d=dd, l_state=l, n_seq=dec_batch, t=dec_seq,
                               row0=n_ctx, tt=gdn_tt, n_par=gdn_par_dec)
            oc_pairs.append((o_c, o_d))
            sts.append(s_c)

        x = _mix_mlp_call(meta[0], x, mod_all, g1,
                          [(oa_c, oa_d), (ob_c, ob_d), oc_pairs[0], oc_pairs[1], (od_c, od_d)],
                          gz, gng, ones_bd, wm, bm, wb, wo, g2, w1, w2, l=l, tm=tm)

        lru_states.append(st_c)
        gdn_states.append(jnp.concatenate(sts, axis=1))

    y_prompt = x[0].reshape(batch, seq, d)
    y_sample = x[1].reshape(dec_batch, dec_seq, d)
    new_kv = [t.reshape(batch, depth, seq, 2, HEAD_DIM) for t in kv_ctx]
    new_lru = jnp.stack(lru_states, axis=1)
    new_gdn = (jnp.stack(gdn_states, axis=1).reshape(batch, depth, 2, GDN_DK, GDN_H, GDN_DK)
               .transpose(0, 1, 2, 4, 3, 5))
    return (y_prompt, y_sample, new_kv[0], new_kv[1], new_kv[2], new_kv[3], new_lru, new_gdn)
```

```python
import functools

import numpy as np
import jax
import jax.numpy as jnp
from jax import lax
from jax.experimental import pallas as pl
from jax.experimental.pallas import tpu as pltpu

F32 = jnp.float32
BF16 = jnp.bfloat16
HIGHEST = lax.Precision.HIGHEST

D_MODEL = 1024
HEAD_DIM = 64
BRANCH_W = 256
N_BRANCH = 4
GRID_W = 64
WINDOW = 128
LRU_C = 8.0
CONV_W = 4
CONV_LEFT = 2
GDN_H = 4
GDN_DK = 64
CHUNK = 64
D_FF = 4 * D_MODEL
ROPE_BASE = 10000.0
EPS = 1e-6
NEG = -1e30
LOG2E = 1.4426950408889634
Q_SCALE = HEAD_DIM ** -0.5 * LOG2E
GDN_PAD = 112
IN_COLS_PACKED = 2688
HALO = 8
N_PRECISE = 4

V7X_VMEM_LIMIT = 56 * 1024 * 1024


def _cparams(sem, vmem=V7X_VMEM_LIMIT):
    return pltpu.CompilerParams(dimension_semantics=sem, vmem_limit_bytes=vmem)


def _layer_spec(arr, l, single_buffer=False):
    nd = arr.ndim - 1
    mode = dict(pipeline_mode=pl.Buffered(1)) if single_buffer else {}
    return pl.BlockSpec((None,) + arr.shape[1:], lambda *_: (l,) + (0,) * nd, **mode)


def _bdot(a, b):
    return jnp.dot(a.astype(BF16), b.astype(BF16), preferred_element_type=F32)


def _split(x):
    hi = x.astype(BF16)
    lo = (x - hi.astype(F32)).astype(BF16)
    return hi, lo


def _split3(x):
    hi = x.astype(BF16)
    r = x - hi.astype(F32)
    mid = r.astype(BF16)
    lo = (r - mid.astype(F32)).astype(BF16)
    return hi, mid, lo


def _group_sum(x, ones_bd):
    return jnp.dot(x.astype(BF16), ones_bd, preferred_element_type=F32)


def _sigmoid(x):
    return 0.5 * jnp.tanh(0.5 * x) + 0.5


def _silu(x):
    return x * _sigmoid(x)


def _softplus(x):
    return jnp.maximum(x, 0.0) + jnp.log1p(jnp.exp(-jnp.abs(x)))


def _modulate(x, g, shift, scale):
    ms = jnp.mean(x * x, axis=-1, keepdims=True)
    return (x * lax.rsqrt(ms + EPS)) * (g * (1.0 + scale)) + shift


def _lane_iota(shape):
    return lax.broadcasted_iota(jnp.int32, shape, len(shape) - 1)


def _row_iota(shape):
    return lax.broadcasted_iota(jnp.int32, shape, len(shape) - 2)


def _mod_kernel(cond_ref, w_ref, b_ref, o_ref):
    hi, lo = _split(_silu(cond_ref[...]))
    w = w_ref[0].astype(BF16)
    o_ref[0] = (jnp.dot(hi, w, preferred_element_type=F32) + jnp.dot(lo, w, preferred_element_type=F32)
                + b_ref[0])


def _mod_call(cond8, mod_w, mod_b):
    depth, d, n = mod_w.shape
    tn = 1536
    return pl.pallas_call(
        _mod_kernel,
        grid=(depth, n // tn),
        in_specs=[
            pl.BlockSpec((8, d), lambda l, j: (0, 0)),
            pl.BlockSpec((1, d, tn), lambda l, j: (l, 0, j)),
            pl.BlockSpec((1, 1, tn), lambda l, j: (l, 0, j)),
        ],
        out_specs=pl.BlockSpec((1, 8, tn), lambda l, j: (l, 0, j)),
        out_shape=jax.ShapeDtypeStruct((depth, 8, n), F32),
        compiler_params=_cparams(("parallel", "parallel")),
        name="mod_vectors",
    )(cond8, mod_w, mod_b.reshape(depth, 1, n))


def _rope(x, cos, sin):
    outs = []
    for j in range(x.shape[1] // 128):
        xb = x[:, 128 * j:128 * (j + 1)]
        lane = _lane_iota(xb.shape)
        sw = jnp.where((lane & 16) == 0, pltpu.roll(xb, 112, 1), pltpu.roll(xb, 16, 1))
        outs.append(xb * cos + sw * sin)
    return outs[0] if len(outs) == 1 else jnp.concatenate(outs, axis=1)


def _centred_conv(g, w, tm):
    rows = g.shape[0]
    acc = None
    for j in range(CONV_W):
        sh = (CONV_LEFT - j) % rows
        gj = g if sh == 0 else pltpu.roll(g, sh, 0)
        term = gj[HALO:HALO + tm] * w[j:j + 1, :]
        acc = term if acc is None else acc + term
    return acc


def _inproj_kernel(modrow_ref, ropeblk_ref, isstart_ref, isend_ref,
                   xpc_ref, xc_ref, xnc_ref, xpd_ref, xd_ref, xnd_ref, mod_ref, g1_ref, w_ref, cos_ref, sin_ref,
                   aqg_ref, akg_ref, dqg_ref, dkg_ref, lcw_ref, lcb_ref, gcw_ref,
                   alog_ref, dtb_ref, ones_ref, *rest, tm, n_ctx_tiles, n_alias):
    (qa_ref, ka_ref, va_ref, lx_ref, lg_ref, gq_ref, gk_ref, gv_ref, gz_ref, ggb_ref,
     qd_ref, kd_ref, vd_ref, kac_ref, vac_ref, kdc_ref, vdc_ref) = rest[n_alias:]
    i = pl.program_id(0)
    is_ctx = i < n_ctx_tiles
    pick = lambda c_ref, d_ref: jnp.where(is_ctx, c_ref[...], d_ref[...])
    xfull = jnp.concatenate([pick(xpc_ref, xpd_ref), pick(xc_ref, xd_ref), pick(xnc_ref, xnd_ref)],
                            axis=0)
    h = _modulate(xfull, g1_ref[...], mod_ref[0, 0:1, :], mod_ref[0, 1:2, :])
    p = jnp.dot(h.astype(BF16), w_ref[...], preferred_element_type=F32)
    hb, hm = slice(0, tm + 2 * HALO), slice(HALO, HALO + tm)
    proj = lambda rows, c0, c1: p[rows, c0:c1]

    ones_bd = ones_ref[...]
    cos = cos_ref[...]
    sin = sin_ref[...]

    pa = proj(hm, 0, 512)
    pd = proj(hm, 2176, 2688)
    va_ref[...] = pa[:, 384:512]
    vd_ref[...] = pd[:, 384:512]

    row = _row_iota((tm + 2 * HALO, 1))
    keep = jnp.logical_and(jnp.logical_or(row >= HALO, isstart_ref[i] == 0),
                           jnp.logical_or(row < HALO + tm, isend_ref[i] == 0))

    lxg = jnp.where(keep, proj(hb, 512, 768), 0.0)
    lx_ref[...] = _centred_conv(lxg, lcw_ref[...], tm) + lcb_ref[...]
    lg_ref[...] = proj(hm, 768, 1024)

    qkv = jnp.where(keep, proj(hb, 1024, 1792), 0.0)
    qkv = _silu(_centred_conv(qkv, gcw_ref[...], tm))
    gq = qkv[:, 0:256]
    gk = qkv[:, 256:512]
    gv_ref[...] = qkv[:, 512:768]
    pz = proj(hm, 1792, 2176)
    gz_ref[...] = pz[:, 0:256]
    ab = pz[:, 256:384]
    g = -jnp.exp(alog_ref[...]) * _softplus(ab + dtb_ref[...])
    lane = _lane_iota(ab.shape)
    ggb_ref[...] = jnp.where(lane < 2 * GDN_H, g, _sigmoid(ab))

    qa_raw, ka_raw, qd_raw, kd_raw = pa[:, 0:256], pa[:, 256:384], pd[:, 0:256], pd[:, 256:384]
    ss256 = _group_sum(jnp.concatenate([qa_raw * qa_raw, qd_raw * qd_raw, gq * gq, gk * gk], axis=0), ones_bd)
    ss128 = _group_sum(jnp.concatenate([ka_raw * ka_raw, kd_raw * kd_raw], axis=0), ones_bd[:128, :128])
    rms = lambda x, ssq, gain: x * lax.rsqrt(ssq * (1.0 / HEAD_DIM) + EPS) * gain

    qa_ref[...] = _rope(rms(qa_raw, ss256[0:tm], aqg_ref[...]), cos, sin)
    qd_ref[...] = _rope(rms(qd_raw, ss256[tm:2 * tm], dqg_ref[...]), cos, sin)
    ka = _rope(rms(ka_raw, ss128[0:tm], akg_ref[...]), cos, sin)
    kd = _rope(rms(kd_raw, ss128[tm:2 * tm], dkg_ref[...]), cos, sin)
    ka_ref[...] = ka
    kd_ref[...] = kd
    gq_ref[...] = gq * lax.rsqrt(ss256[2 * tm:3 * tm] + EPS) * (GDN_DK ** -0.5)
    gk_ref[...] = gk * lax.rsqrt(ss256[3 * tm:4 * tm] + EPS)

    @pl.when(is_ctx)
    def _():
        for ref, val in ((kac_ref, ka), (vac_ref, pa[:, 384:512]), (kdc_ref, kd), (vdc_ref, pd[:, 384:512])):
            if n_alias:
                ref[...] = val
            else:
                for slot in range(ref.shape[0]):
                    ref[slot] = val


def _x_pair_specs(x_pair, tm):
    x_c, x_d = x_pair
    d = x_c.shape[1]
    nct = x_c.shape[0] // tm
    hb = tm // HALO
    specs = []
    for arr, first in ((x_c, 0), (x_d, nct)):
        nblk = arr.shape[0] // tm
        last_hb = arr.shape[0] // HALO - 1
        tile = lambda i, first=first, nblk=nblk: jnp.clip(i - first, 0, nblk - 1)
        specs += [
            pl.BlockSpec((HALO, d), lambda i, *_, t=tile, m=last_hb: (jnp.clip(t(i) * hb - 1, 0, m), 0)),
            pl.BlockSpec((tm, d), lambda i, *_, t=tile: (t(i), 0)),
            pl.BlockSpec((HALO, d), lambda i, *_, t=tile, m=last_hb: (jnp.clip((t(i) + 1) * hb, 0, m), 0)),
        ]
    return specs


def _inproj_call(x_pair, meta, mod_all, g1, w_packed, cos_t, sin_t, vecs, ones_bd, kv_prev, *, l, depth, tm):
    x_c, x_d = x_pair
    n, d = x_c.shape[0] + x_d.shape[0], x_c.shape[1]
    nt = n // tm
    nct = x_c.shape[0] // tm
    row_tile = lambda w: pl.BlockSpec((tm, w), lambda i, *_: (i, 0))
    alias_in = [] if kv_prev is None else list(kv_prev)
    if alias_in:
        kv_spec = pl.BlockSpec((None, None, tm, 128), lambda i, *_: (jnp.minimum(i, nct - 1), l, 0, 0))
    else:
        kv_spec = pl.BlockSpec((None, depth, tm, 128), lambda i, *_: (jnp.minimum(i, nct - 1), 0, 0, 0))
    kv_shape = jax.ShapeDtypeStruct((nct, depth, tm, 128), F32)
    in_specs = _x_pair_specs(x_pair, tm) + [
        pl.BlockSpec((None, 1, 6, d), lambda i, modrow, *_: (l, modrow[i], 0, 0)),
        _layer_spec(g1, l),
        _layer_spec(w_packed, l),
        pl.BlockSpec((tm, 128), lambda i, modrow, ropeblk, *_: (ropeblk[i], 0)),
        pl.BlockSpec((tm, 128), lambda i, modrow, ropeblk, *_: (ropeblk[i], 0)),
    ] + [_layer_spec(v, l) for v in vecs] + [pl.BlockSpec((256, 256), lambda i, *_: (0, 0))]
    in_specs += [pl.BlockSpec(memory_space=pl.ANY)] * len(alias_in)
    widths = (256, 128, 128, 256, 256, 256, 256, 256, 256, 128, 256, 128, 128)
    grid_spec = pltpu.PrefetchScalarGridSpec(
        num_scalar_prefetch=4,
        grid=(nt,),
        in_specs=in_specs,
        out_specs=[row_tile(w) for w in widths] + [kv_spec] * 4,
    )
    operands = (*meta, x_c, x_c, x_c, x_d, x_d, x_d, mod_all, g1, w_packed, cos_t, sin_t, *vecs, ones_bd)
    outs = pl.pallas_call(
        functools.partial(_inproj_kernel, tm=tm, n_ctx_tiles=nct, n_alias=len(alias_in)),
        grid_spec=grid_spec,
        out_shape=[jax.ShapeDtypeStruct((n, w), F32) for w in widths] + [kv_shape] * 4,
        input_output_aliases={len(operands) + u: len(widths) + u for u in range(len(alias_in))},
        compiler_params=_cparams(("arbitrary",)),
        name="in_projection",
    )(*operands, *alias_in)
    return outs[:len(widths)], outs[len(widths):]


def _head_q(q, j, g):
    qj = q[:, 128 * j:128 * (j + 1)]
    lane = _lane_iota(qj.shape)
    sel = (lane < HEAD_DIM) if g == 0 else (lane >= HEAD_DIM)
    return jnp.where(sel, qj, 0.0).astype(BF16)


def _place_heads(res, j):
    r0 = res[0] if j == 0 else pltpu.roll(res[0], HEAD_DIM, 1)
    r1 = res[1] if j == 1 else pltpu.roll(res[1], HEAD_DIM, 1)
    lane = _lane_iota(r0.shape)
    return jnp.where(lane < HEAD_DIM, r0, r1)


def _attend_many(problems, order="staged"):
    work = []
    for pi, (q, ks, vs, masks, sinks, ks_sw) in enumerate(problems):
        if ks_sw is None:
            ks_sw = [pltpu.roll(k, HEAD_DIM, 1) for k in ks]
        work += [(pi, j, g, q, ks, vs, masks, sinks, ks_sw) for j in range(2) for g in range(2)]
    scores, probs, res = {}, {}, {}

    def score(w):
        pi, j, g, q, ks, vs, masks, sinks, ks_sw = w
        if g == 1:
            return
        r = q.shape[0]
        qj = q[:, 128 * j:128 * (j + 1)]
        lane = _lane_iota(qj.shape)
        on_j = (lane >= HEAD_DIM * j) & (lane < HEAD_DIM * (j + 1))
        stacked = jnp.concatenate(
            [jnp.where(on_j, qj if gg == j else pltpu.roll(qj, HEAD_DIM, 1), 0.0) for gg in range(2)],
            axis=0).astype(BF16)
        both = [[], []]
        for k, mk in zip(ks, masks):
            s = lax.dot_general(stacked, k, (((1,), (1,)), ((), ())), preferred_element_type=F32)
            for gg in range(2):
                sg = s[gg * r:(gg + 1) * r]
                both[gg].append(sg if mk is None else jnp.where(mk, sg, NEG))
        scores[(pi, j, 0)], scores[(pi, j, 1)] = both

    def softmax_num(w):
        pi, j, g, q, ks, vs, masks, sinks, ks_sw = w
        ss = scores[w[:3]]
        m = ss[0].max(axis=-1, keepdims=True)
        for s in ss[1:]:
            m = jnp.maximum(m, s.max(axis=-1, keepdims=True))
        if sinks is not None:
            m = jnp.maximum(m, sinks[2 * j + g])
        es = [jnp.exp2(s - m) for s in ss]
        den = sum(e.sum(axis=-1, keepdims=True) for e in es)
        if sinks is not None:
            den = den + jnp.exp2(sinks[2 * j + g] - m)
        probs[w[:3]] = ([e.astype(BF16) for e in es], den)

    def values(w):
        es, den = probs[w[:3]]
        res[w[:3]] = sum(jnp.dot(e, v, preferred_element_type=F32) for e, v in zip(es, w[5])) / den

    if order == "staged":
        for step in (score, softmax_num, values):
            for w in work:
                step(w)
    else:
        score(work[0])
        for prev, nxt in zip(work, work[1:] + [None]):
            if nxt is not None:
                score(nxt)
            softmax_num(prev)
            values(prev)
    return [jnp.concatenate([_place_heads([res[(pi, j, 0)], res[(pi, j, 1)]], j) for j in range(2)], axis=1)
            for pi in range(len(problems))]


def _attend(q, ks, vs, masks, sinks, ks_sw=None, order="staged"):
    return _attend_many([(q, ks, vs, masks, sinks, ks_sw)], order)[0]


def _bf(x):
    return x.astype(BF16)


def _attn_ctx_kernel(sink_ref, qa_ref, ka_ref, va_ref, qd_ref, kd_ref, vd_ref, oa_ref, od_ref, *, l, t, n_par):
    sinks = [sink_ref[4 * l + u] * LOG2E for u in range(4)]
    problems = []
    for p in range(n_par):
        r = slice(p * t, (p + 1) * t)
        problems.append((qa_ref[r, :], [_bf(ka_ref[r, :])], [_bf(va_ref[r, :])], [None], sinks, None))
        problems.append((qd_ref[r, :], [_bf(kd_ref[r, :])], [_bf(vd_ref[r, :])], [None], None, None))
    outs = _attend_many(problems)
    for p in range(n_par):
        oa_ref[p * t:(p + 1) * t, :] = outs[2 * p]
        od_ref[p * t:(p + 1) * t, :] = outs[2 * p + 1]


def _attn_ctx_call(sink, qa, ka, va, qd, kd, vd, *, l, n_seq, t):
    n_par = 2 if n_seq % 2 == 0 else 1
    blk = lambda w: pl.BlockSpec((n_par * t, w), lambda b, *_: (b, 0))
    grid_spec = pltpu.PrefetchScalarGridSpec(
        num_scalar_prefetch=1, grid=(n_seq // n_par,),
        in_specs=[blk(256), blk(128), blk(128), blk(256), blk(128), blk(128)],
        out_specs=[blk(256), blk(256)])
    return pl.pallas_call(
        functools.partial(_attn_ctx_kernel, l=l, t=t, n_par=n_par), grid_spec=grid_spec,
        out_shape=[jax.ShapeDtypeStruct((n_seq * t, 256), F32)] * 2,
        compiler_params=_cparams(("parallel",)),
        name="attn_context",
    )(sink, qa, ka, va, qd, kd, vd)


def _attn_band_kernel(sink_ref, q_ref, k_ref, v_ref, ck_ref, cv_ref, o_ref, *, l, tq, t):
    i = pl.program_id(1)
    start = pl.multiple_of(i * tq, tq)
    prev = pl.multiple_of(jnp.maximum(start - WINDOW, 0), WINDOW)
    nxt = pl.multiple_of(jnp.minimum(start + tq, t - WINDOW), WINDOW)
    k_loc = jnp.concatenate([k_ref[pl.ds(prev, WINDOW), :], k_ref[pl.ds(start, tq), :],
                             k_ref[pl.ds(nxt, WINDOW), :]], axis=0)
    v_loc = jnp.concatenate([v_ref[pl.ds(prev, WINDOW), :], v_ref[pl.ds(start, tq), :],
                             v_ref[pl.ds(nxt, WINDOW), :]], axis=0)
    shape = (tq, tq + 2 * WINDOW)
    qpos = start + _row_iota(shape)
    kpos = start - WINDOW + _lane_iota(shape)
    mask = (kpos >= 0) & (kpos < t) & (jnp.abs(qpos - kpos) <= WINDOW)
    sinks = [sink_ref[4 * l + u] * LOG2E for u in range(4)]
    o_ref[...] = _attend(q_ref[...], [_bf(ck_ref[0]), _bf(k_loc)], [_bf(cv_ref[0]), _bf(v_loc)],
                         [None, mask], sinks)


def _cache_spec(c, l):
    return pl.BlockSpec((1, None) + c.shape[2:], lambda b, i, *_: (b, l, 0, 0))


def _attn_band_call(sink, q, k, v, ck, cv, *, l, n_seq, t, row0, tq):
    nq = t // tq
    qb0 = row0 // tq
    sb0 = row0 // t
    grid_spec = pltpu.PrefetchScalarGridSpec(
        num_scalar_prefetch=1, grid=(n_seq, nq),
        in_specs=[
            pl.BlockSpec((tq, 256), lambda b, i, *_: (qb0 + b * nq + i, 0)),
            pl.BlockSpec((t, 128), lambda b, i, *_: (sb0 + b, 0)),
            pl.BlockSpec((t, 128), lambda b, i, *_: (sb0 + b, 0)),
            _cache_spec(ck, l), _cache_spec(cv, l),
        ],
        out_specs=pl.BlockSpec((tq, 256), lambda b, i, *_: (b * nq + i, 0)))
    return pl.pallas_call(
        functools.partial(_attn_band_kernel, l=l, tq=tq, t=t), grid_spec=grid_spec,
        out_shape=jax.ShapeDtypeStruct((n_seq * t, 256), F32),
        compiler_params=_cparams(("parallel", "parallel")),
        name="attn_banded",
    )(sink, q, k, v, ck, cv)


def _attn_full_kernel(q_ref, k_ref, v_ref, ck_ref, cv_ref, o_ref, ks_ref, vs_ref, *, past, t):
    @pl.when(pl.program_id(1) == 0)
    def _():
        kc = _bf(ck_ref[0])
        kl = _bf(k_ref[...])
        ks_ref[0, 0:past, :] = kc
        ks_ref[0, past:past + t, :] = kl
        ks_ref[1, 0:past, :] = pltpu.roll(kc, HEAD_DIM, 1)
        ks_ref[1, past:past + t, :] = pltpu.roll(kl, HEAD_DIM, 1)
        vs_ref[0:past, :] = _bf(cv_ref[0])
        vs_ref[past:past + t, :] = _bf(v_ref[...])

    o_ref[...] = _attend(q_ref[...], [ks_ref[0]], [vs_ref[...]], [None], None, ks_sw=[ks_ref[1]], order="skewed")


def _attn_full_call(q, k, v, ck, cv, *, l, n_seq, t, row0, tq):
    nq = t // tq
    qb0 = row0 // tq
    sb0 = row0 // t
    past = ck.shape[2]
    assert past % 16 == 0
    return pl.pallas_call(
        functools.partial(_attn_full_kernel, past=past, t=t), grid=(n_seq, nq),
        scratch_shapes=[pltpu.VMEM((2, past + t, 128), BF16), pltpu.VMEM((past + t, 128), BF16)],
        in_specs=[
            pl.BlockSpec((tq, 256), lambda b, i: (qb0 + b * nq + i, 0)),
            pl.BlockSpec((t, 128), lambda b, i: (sb0 + b, 0)),
            pl.BlockSpec((t, 128), lambda b, i: (sb0 + b, 0)),
            _cache_spec(ck, l), _cache_spec(cv, l),
        ],
        out_specs=pl.BlockSpec((tq, 256), lambda b, i: (b * nq + i, 0)),
        out_shape=jax.ShapeDtypeStruct((n_seq * t, 256), F32),
        compiler_params=_cparams(("parallel", "arbitrary")),
        name="attn_full",
    )(q, k, v, ck, cv)


def _gelu_tanh(x):
    return 0.5 * x * (1.0 + jnp.tanh(0.7978845608028654 * (x + 0.044715 * (x * x * x))))


def _lru_kernel(lx_ref, lg_ref, w_ref, b_ref, lam_ref, h0_ref, y_ref, st_ref,
                hf_ref, hb_ref, af_ref, bf_ref, ab_ref, bb_ref, *, t, chunk):
    nc = t // chunk
    ng = chunk // 8
    sp = _softplus(-lam_ref[...])
    row8 = _row_iota((8, BRANCH_W))

    def gates(xc, d, a_ref, b2_ref):
        pre = _bdot(xc, w_ref[:, 512 * d:512 * (d + 1)]) + b_ref[:, 512 * d:512 * (d + 1)]
        r = _sigmoid(pre[:, 0:256])
        ig = _sigmoid(pre[:, 256:512])
        a = jnp.exp((-LRU_C) * r * sp[d:d + 1, :])
        a_ref[...] = a
        om = 1.0 - a * a
        b2_ref[...] = jnp.where(om > 0.0, om * lax.rsqrt(om), 0.0) * (ig * xc)

    def local_scan(a8, b8, reverse):
        for s in (1, 2, 4):
            sh = (8 - s) if reverse else s
            m = (row8 < 8 - s) if reverse else (row8 >= s)
            a_sh = pltpu.roll(a8, sh, 0)
            b_sh = pltpu.roll(b8, sh, 0)
            b8 = jnp.where(m, a8 * b_sh + b8, b8)
            a8 = jnp.where(m, a8 * a_sh, a8)
        return a8, b8

    def chunk_body(cc, carry):
        rf = pl.multiple_of(cc * chunk, chunk)
        rb = pl.multiple_of((nc - 1 - cc) * chunk, chunk)
        gates(lx_ref[pl.ds(rf, chunk), :], 0, af_ref, bf_ref)
        gates(lx_ref[pl.ds(rb, chunk), :], 1, ab_ref, bb_ref)

        def grp(gg, c2):
            hf, hb = c2
            gf = pl.multiple_of(gg * 8, 8)
            gb = pl.multiple_of((ng - 1 - gg) * 8, 8)
            a8, b8 = local_scan(af_ref[pl.ds(gf, 8), :], bf_ref[pl.ds(gf, 8), :], False)
            h8 = b8 + a8 * hf
            hf_ref[pl.ds(rf + gf, 8), :] = h8
            a8, b8 = local_scan(ab_ref[pl.ds(gb, 8), :], bb_ref[pl.ds(gb, 8), :], True)
            g8 = b8 + a8 * hb
            hb_ref[pl.ds(rb + gb, 8), :] = g8
            return (jnp.broadcast_to(h8[7:8, :], (8, BRANCH_W)), jnp.broadcast_to(g8[0:1, :], (8, BRANCH_W)))

        return lax.fori_loop(0, ng, grp, carry, unroll=4)

    init = (jnp.broadcast_to(h0_ref[0, 0:1, :], (8, BRANCH_W)), jnp.broadcast_to(h0_ref[0, 1:2, :], (8, BRANCH_W)))
    hf, hb = lax.fori_loop(0, nc, chunk_body, init)
    st_ref[0, 0:1, :] = hf[0:1, :]
    st_ref[0, 1:2, :] = hb[0:1, :]

    def combine(c, _):
        rows = pl.ds(pl.multiple_of(c * chunk, chunk), chunk)
        y_ref[rows, :] = (hf_ref[rows, :] + hb_ref[rows, :]) * _gelu_tanh(lg_ref[rows, :])
        return 0

    lax.fori_loop(0, nc, combine, 0)


def _lru_call(lx, lg, w, b, lam, h0, *, l, l_state, n_seq, t, row0):
    sb0 = row0 // t
    chunk = min(t, 256)
    seq = lambda: pl.BlockSpec((t, BRANCH_W), lambda s: (sb0 + s, 0))
    return pl.pallas_call(
        functools.partial(_lru_kernel, t=t, chunk=chunk),
        grid=(n_seq,),
        in_specs=[seq(), seq(), _layer_spec(w, l), _layer_spec(b, l), _layer_spec(lam, l),
                  pl.BlockSpec((1, None, 2, BRANCH_W), lambda s: (s, l_state, 0, 0))],
        out_specs=[pl.BlockSpec((t, BRANCH_W), lambda s: (s, 0)),
                   pl.BlockSpec((1, 2, BRANCH_W), lambda s: (s, 0, 0))],
        out_shape=[jax.ShapeDtypeStruct((n_seq * t, BRANCH_W), F32),
                   jax.ShapeDtypeStruct((n_seq, 2, BRANCH_W), F32)],
        scratch_shapes=[pltpu.VMEM((t, BRANCH_W), F32), pltpu.VMEM((t, BRANCH_W), F32)]
                       + [pltpu.VMEM((chunk, BRANCH_W), F32)] * 4,
        compiler_params=_cparams(("parallel",)),
        name="rglru",
    )(lx, lg, w, b, lam, h0)


def _rows_dot_exact(x, w01, terms=3):
    r = x.shape[0]
    parts = _split3(x)[:terms] if terms == 3 else _split(x)
    out = jnp.dot(jnp.concatenate(parts, axis=0), w01, preferred_element_type=F32)
    return sum(out[u * r:(u + 1) * r] for u in range(terms))


def _dot01_exact(m01, y):
    return sum(jnp.dot(m01, part, preferred_element_type=F32) for part in _split3(y))


def _gdn_kernel(q_ref, k_ref, v_ref, ggb_ref, s0_ref, o_ref, sT_ref, s_ref, *, d, reverse, n_par, n_chunk, n_tile):
    ti = pl.program_id(1)
    w4 = GDN_H * GDN_DK
    n_all = n_par * n_chunk
    tt = n_all * CHUNK
    blockmask = (_row_iota((w4, w4)) // CHUNK) == (_lane_iota((w4, w4)) // CHUNK)

    def expand_rows(y):
        yt = jnp.concatenate([y] * GDN_H, axis=0)
        zero = jnp.zeros((), y.dtype)
        parts = [jnp.where(blockmask, yt[:, w4 * u:w4 * (u + 1)], zero) for u in range(y.shape[1] // w4)]
        return parts[0] if len(parts) == 1 else jnp.concatenate(parts, axis=1)

    def heads_dot3(lhs, y):
        r = lhs.shape[0]
        lh, ll = _split(lhs)
        yh, yl = _split(y)
        out = jnp.dot(jnp.concatenate([lh, ll], axis=0), expand_rows(yh), preferred_element_type=F32)
        return out[0:r] + out[r:2 * r] + jnp.dot(lh, expand_rows(yl), preferred_element_type=F32)

    def heads_dot2(lhs, y):
        r = lhs.shape[0]
        lh, ll = _split(lhs)
        out = jnp.dot(jnp.concatenate([lh, ll], axis=0), expand_rows(y.astype(BF16)), preferred_element_type=F32)
        return out[0:r] + out[r:2 * r]

    def heads_dot1(lhs, y):
        return jnp.dot(lhs.astype(BF16), expand_rows(y.astype(BF16)), preferred_element_type=F32)

    @pl.when(ti == 0)
    def _():
        for p in range(n_par):
            s_ref[p] = expand_rows(s0_ref[p, 0])

    ri = _row_iota((CHUNK, w4))
    cj = _lane_iota((CHUNK, w4)) % CHUNK
    incl = (cj >= ri) if reverse else (cj <= ri)
    strict = (cj > ri) if reverse else (cj < ri)
    eye_sbs = (cj == ri)

    t_r = _row_iota((CHUNK, CHUNK))
    t_c = _lane_iota((CHUNK, CHUNK))
    tri = ((t_c >= t_r) if reverse else (t_c <= t_r)).astype(BF16)
    ones_cc = jnp.ones((CHUNK, CHUNK), BF16)
    gb_all = ggb_ref[...].reshape(tt, 128)
    cs = range(n_all)
    rows = [slice(c * CHUNK, (c + 1) * CHUNK) for c in cs]
    gcum = jnp.concatenate([_dot01_exact(tri, gb_all[r, :]) for r in rows], axis=0)
    er = _row_iota((128, w4))
    head = _lane_iota((128, w4)) // CHUNK
    e_g = (er == d * GDN_H + head).astype(BF16)
    e_b = (er == 2 * GDN_H + d * GDN_H + head).astype(BF16)
    gc_all = _rows_dot_exact(gcum, e_g)
    beta_all = _rows_dot_exact(gb_all, e_b, terms=2)
    gct = [_dot01_exact(ones_cc, jnp.where(eye_sbs, gc_all[r, :], 0.0)) for r in rows]

    at = lambda c: (c // n_chunk, slice((c % n_chunk) * CHUNK, (c % n_chunk + 1) * CHUNK))
    q = [q_ref[at(c)[0], at(c)[1], :] for c in cs]
    k = [k_ref[at(c)[0], at(c)[1], :] for c in cs]
    v = [v_ref[at(c)[0], at(c)[1], :] for c in cs]
    beta = [beta_all[r, :] for r in rows]
    gc = [gc_all[r, :] for r in rows]
    decay = [jnp.where(incl, jnp.exp(jnp.where(incl, gc[c] - gct[c], 0.0)), 0.0) for c in cs]
    kb = [k[c].astype(BF16) for c in cs]
    kq = [lax.dot_general(jnp.concatenate([kb[c], q[c].astype(BF16)], axis=0), expand_rows(kb[c]),
                          (((1,), (1,)), ((), ())), preferred_element_type=F32) for c in cs]
    qk = [(kq[c][CHUNK:2 * CHUNK] * decay[c]).astype(BF16) for c in cs]
    a = [jnp.where(strict, beta[c] * kq[c][0:CHUNK] * decay[c], 0.0) for c in cs]
    t_inv = [jnp.where(eye_sbs, 1.0, 0.0) - a[c] for c in cs]
    pw = [heads_dot3(a[c], a[c]) for c in cs]
    for stage in range(4):
        mm = heads_dot3 if stage < N_PRECISE else heads_dot1
        both = [mm(jnp.concatenate([t_inv[c], pw[c]], axis=0), pw[c]) for c in cs]
        t_inv = [t_inv[c] + both[c][0:CHUNK] for c in cs]
        pw = [both[c][CHUNK:2 * CHUNK] for c in cs]
    t_inv = [t_inv[c] + heads_dot1(t_inv[c], pw[c]) for c in cs]
    egc = [jnp.exp(gc[c]) for c in cs]
    sol = [heads_dot2(t_inv[c], jnp.concatenate([v[c] * beta[c], k[c] * (beta[c] * egc[c])], axis=1)) for c in cs]
    g_last = [gc[c][0:1, :] if reverse else gc[c][CHUNK - 1:CHUNK, :] for c in cs]
    wq = [jnp.concatenate([sol[c][:, w4:2 * w4], q[c] * egc[c]], axis=0).astype(BF16) for c in cs]
    kdec = [(k[c] * jnp.exp(g_last[c] - gc[c])).astype(BF16) for c in cs]

    for cc in range(n_chunk):
        for p in range(n_par):
            c = p * n_chunk + ((n_chunk - 1 - cc) if reverse else cc)
            s = s_ref[p]
            ws_qs = jnp.dot(wq[c], s.astype(BF16), preferred_element_type=F32)
            v_new = sol[c][:, 0:w4] - ws_qs[0:CHUNK]
            vb = v_new.astype(BF16)
            o_ref[at(c)[0], at(c)[1], :] = (
                ws_qs[CHUNK:2 * CHUNK] + jnp.dot(qk[c], expand_rows(vb), preferred_element_type=F32))
            upd = lax.dot_general(kdec[c], vb, (((0,), (0,)), ((), ())), preferred_element_type=F32)
            s_ref[p] = s * jnp.exp(g_last[c]) + jnp.where(blockmask, upd, 0.0)

    @pl.when(ti == n_tile - 1)
    def _():
        for p in range(n_par):
            s = s_ref[p]
            sT_ref[p, 0] = s[0:64] + s[64:128] + s[128:192] + s[192:256]


def _gdn_call(q, k, v, ggb, s0, *, d, l_state, n_seq, t, row0, tt, n_par):
    reverse = d == 1
    n_tile = t // tt
    assert row0 % (t * n_par) == 0 and n_seq % n_par == 0
    g0 = row0 // (t * n_par)
    tidx = (lambda i: n_tile - 1 - i) if reverse else (lambda i: i)
    blk = lambda w: pl.BlockSpec((n_par, tt, w), lambda s, i: (g0 + s, tidx(i), 0))
    view = lambda a: a.reshape(a.shape[0] // t, t, a.shape[1])
    o, s_final = pl.pallas_call(
        functools.partial(_gdn_kernel, d=d, reverse=reverse, n_par=n_par, n_chunk=tt // CHUNK, n_tile=n_tile),
        grid=(n_seq // n_par, n_tile),
        in_specs=[blk(256), blk(256), blk(256), blk(128),
                  pl.BlockSpec((n_par, None, 1, CHUNK, 256), lambda s, i: (s, l_state, d, 0, 0))],
        out_specs=[pl.BlockSpec((n_par, tt, 256), lambda s, i: (s, tidx(i), 0)),
                   pl.BlockSpec((n_par, 1, CHUNK, 256), lambda s, i: (s, 0, 0, 0))],
        out_shape=[jax.ShapeDtypeStruct((n_seq, t, 256), F32),
                   jax.ShapeDtypeStruct((n_seq, 1, CHUNK, 256), F32)],
        scratch_shapes=[pltpu.VMEM((n_par, 256, 256), F32)],
        compiler_params=_cparams(("parallel", "arbitrary")),
        name="gdn_bwd" if reverse else "gdn_fwd",
    )(view(q), view(k), view(v), view(ggb), s0)
    return o.reshape(n_seq * t, 256), s_final


def _mix_mlp_kernel(modrow_ref, xc_ref, xd_ref, mod_ref, g1_ref,
                    oac_ref, oad_ref, obc_ref, obd_ref, ofc_ref, ofd_ref, orc_ref, ord_ref, odc_ref, odd_ref,
                    gz_ref, gng_ref, ones_ref, wm_ref, bm_ref, wb_ref, wo_ref, g2_ref, w1_ref, w2_ref,
                    yc_ref, yd_ref, *, n_ctx_tiles):
    is_ctx = pl.program_id(0) < n_ctx_tiles
    pick = lambda c_ref, d_ref: jnp.where(is_ctx, c_ref[...], d_ref[...])
    x = pick(xc_ref, xd_ref)
    h = _modulate(x, g1_ref[...], mod_ref[0, 0:1, :], mod_ref[0, 1:2, :]).astype(BF16)
    oc = pick(ofc_ref, ofd_ref) + pick(orc_ref, ord_ref)
    ms = _group_sum(oc * oc, ones_ref[...]) * (1.0 / GDN_DK)
    oc = (oc * lax.rsqrt(ms + EPS) * gng_ref[...]) * _silu(gz_ref[...])
    branches = (pick(oac_ref, oad_ref), pick(obc_ref, obd_ref), oc, pick(odc_ref, odd_ref))
    acc = None
    for m in range(N_BRANCH):
        cols = slice(D_MODEL * m, D_MODEL * (m + 1))
        gate = _sigmoid(jnp.dot(h, wm_ref[:, cols], preferred_element_type=F32) + bm_ref[:, cols])
        term = gate * jnp.dot(branches[m].astype(BF16), wb_ref[m], preferred_element_type=F32)
        acc = term if acc is None else acc + term
    x = x + mod_ref[0, 2:3, :] * jnp.dot(acc.astype(BF16), wo_ref[...], preferred_element_type=F32)

    h = _modulate(x, g2_ref[...], mod_ref[0, 3:4, :], mod_ref[0, 4:5, :]).astype(BF16)
    acc = None
    for j in range(D_FF // D_MODEL):
        cols = slice(D_MODEL * j, D_MODEL * (j + 1))
        a = jnp.maximum(jnp.dot(h, w1_ref[:, cols], preferred_element_type=F32), 0.0)
        term = jnp.dot((a * a).astype(BF16), w2_ref[cols, :], preferred_element_type=F32)
        acc = term if acc is None else acc + term
    y = x + mod_ref[0, 5:6, :] * acc

    @pl.when(is_ctx)
    def _():
        yc_ref[...] = y

    @pl.when(jnp.logical_not(is_ctx))
    def _():
        yd_ref[...] = y


def _mix_mlp_call(modrow, x_pair, mod_all, g1, pairs, gz, gng, ones_bd, wm, bm, wb, wo, g2, w1, w2, *, l, tm):
    x_c, x_d = x_pair
    n, d = x_c.shape[0] + x_d.shape[0], x_c.shape[1]
    nct = x_c.shape[0] // tm
    row_tile = lambda w: pl.BlockSpec((tm, w), lambda i, *_: (i, 0))
    pair = lambda w: [pl.BlockSpec((tm, w), lambda i, *_: (jnp.minimum(i, nct - 1), 0)),
                      pl.BlockSpec((tm, w), lambda i, *_: (jnp.maximum(i - nct, 0), 0))]
    pair_specs = [sp for _ in pairs for sp in pair(256)]
    resident = lambda arr: _layer_spec(arr, l, single_buffer=True)
    grid_spec = pltpu.PrefetchScalarGridSpec(
        num_scalar_prefetch=1, grid=(n // tm,),
        in_specs=pair(d) + [pl.BlockSpec((None, 1, 6, d), lambda i, modrow: (l, modrow[i], 0, 0)),
                            _layer_spec(g1, l)] + pair_specs
                 + [row_tile(256), _layer_spec(gng, l), pl.BlockSpec((256, 256), lambda i, *_: (0, 0)),
                    resident(wm), _layer_spec(bm, l), resident(wb), resident(wo),
                    _layer_spec(g2, l), resident(w1), resident(w2)],
        out_specs=pair(d))
    flat = [a for pr in pairs for a in pr]
    return pl.pallas_call(
        functools.partial(_mix_mlp_kernel, n_ctx_tiles=nct), grid_spec=grid_spec,
        out_shape=[jax.ShapeDtypeStruct(x_c.shape, F32), jax.ShapeDtypeStruct(x_d.shape, F32)],
        compiler_params=_cparams(("arbitrary",)),
        name="merge_mlp",
    )(modrow, x_c, x_d, mod_all, g1, *flat, gz, gng, ones_bd, wm, bm, wb, wo, g2, w1, w2)


def _rope_tables(t, tm):
    n_freq = HEAD_DIM // 4
    inv = np.float32(ROPE_BASE) ** (-np.arange(n_freq, dtype=np.float32) / np.float32(n_freq))
    pos = np.arange(t)
    row = (pos // GRID_W).astype(np.float32)[:, None]
    col = (pos % GRID_W).astype(np.float32)[:, None]
    ar = (row * inv).astype(np.float32)
    ac = (col * inv).astype(np.float32)
    cos64 = np.concatenate([np.cos(ar), np.cos(ar), np.cos(ac), np.cos(ac)], axis=1)
    sin64 = np.concatenate([-np.sin(ar), np.sin(ar), -np.sin(ac), np.sin(ac)], axis=1)
    cos = np.concatenate([np.ones((tm, 128), np.float32), np.tile(cos64, (1, 2)).astype(np.float32)], axis=0)
    sin = np.concatenate([np.zeros((tm, 128), np.float32), np.tile(sin64, (1, 2)).astype(np.float32)], axis=0)
    return jnp.asarray(cos), jnp.asarray(sin)


def _tile_meta(n_ctx, n_seq_dec, t_ctx, t_dec, tm):
    nct = n_ctx // tm
    per = t_dec // tm
    ndt = n_seq_dec * per
    idx = np.arange(nct + ndt)
    dec = idx >= nct
    di = np.maximum(idx - nct, 0)
    modrow = np.where(dec, 1 + di // per, 0)
    ropeblk = np.where(dec, 1 + di % per, 0)
    tiles_ctx = max(t_ctx // tm, 1)
    isstart = np.where(dec, di % per == 0, idx % tiles_ctx == 0)
    isend = np.where(dec, di % per == per - 1, idx % tiles_ctx == tiles_ctx - 1)
    as_i32 = lambda a: jnp.asarray(a.astype(np.int32))
    return as_i32(modrow), as_i32(ropeblk), as_i32(isstart), as_i32(isend)


def _block_diag_gates(w):
    depth, two, n, c, _ = w.shape
    eye = jnp.eye(n, dtype=w.dtype)
    return jnp.einsum("ldnij,nm->ldnimj", w, eye).reshape(depth, two, n * c, n * c)


def kernel(x_prompt, x_sample, c, cache_a_k, cache_a_v, cache_d_k, cache_d_v, state_lru, state_gdn, c_ctx, mod_w, mod_b, norm1_g, norm2_g, w_in, a_qn_g, a_kn_g, a_sink, lru_conv_w, lru_conv_b, lru_wr, lru_br, lru_wi, lru_bi, lru_lam, gdn_conv_w, gdn_a_log, gdn_dt_bias, gdn_norm_g, d_qn_g, d_kn_g, w_branch, w_merge, b_merge, w_out, mlp_w1, mlp_w2):
    batch, seq, d = x_prompt.shape
    dec_batch, dec_seq, _ = x_sample.shape
    depth = mod_w.shape[0]
    past = cache_a_k.shape[2]
    n_ctx = batch * seq
    n_dec = dec_batch * dec_seq
    tm = 256
    gdn_par = 2 if batch % 2 == 0 else 1
    gdn_par_dec = 2 if (dec_batch % 2 == 0 and n_ctx % (2 * dec_seq) == 0) else 1
    gdn_tt = 256 if gdn_par_dec == 2 else (512 if dec_seq % 512 == 0 else 256)
    assert d == D_MODEL and seq % tm == 0 and dec_seq % tm == 0 and tm % seq == 0
    assert dec_batch + 1 <= 8 and n_ctx % dec_seq == 0

    cond8 = jnp.zeros((8, d), F32).at[0].set(c_ctx).at[1:1 + dec_batch].set(c)
    mod_all = _mod_call(cond8, mod_w, mod_b).reshape(depth, 8, 6, d)

    meta = _tile_meta(n_ctx, dec_batch, seq, dec_seq, tm)
    cos_t, sin_t = _rope_tables(dec_seq, tm)
    lane = np.arange(256)
    ones_bd = jnp.asarray((lane[:, None] // HEAD_DIM == lane[None, :] // HEAD_DIM).astype(np.float32)).astype(BF16)

    w_packed = jnp.concatenate([w_in[:, :, :2064], jnp.zeros((depth, d, GDN_PAD), F32), w_in[:, :, 2064:]],
                               axis=2).astype(BF16)
    pad128 = lambda v: jnp.pad(v.reshape(depth, 1, -1), ((0, 0), (0, 0), (0, 128 - v[0].size)))
    vecs = (
        (jnp.tile(a_qn_g, (1, 4)) * Q_SCALE)[:, None, :],
        jnp.tile(a_kn_g, (1, 2))[:, None, :],
        (jnp.tile(d_qn_g, (1, 4)) * Q_SCALE)[:, None, :],
        jnp.tile(d_kn_g, (1, 2))[:, None, :],
        lru_conv_w, lru_conv_b[:, None, :], gdn_conv_w,
        pad128(gdn_a_log), pad128(gdn_dt_bias),
    )
    g1 = norm1_g[:, None, :]
    g2 = norm2_g[:, None, :]
    wr_bd = _block_diag_gates(lru_wr)
    wi_bd = _block_diag_gates(lru_wi)
    w_lru = jnp.concatenate([wr_bd[:, 0], wi_bd[:, 0], wr_bd[:, 1], wi_bd[:, 1]], axis=-1).astype(BF16)
    b_lru = jnp.concatenate([lru_br[:, 0], lru_bi[:, 0], lru_br[:, 1], lru_bi[:, 1]], axis=-1)[:, None, :]
    gng = jnp.tile(gdn_norm_g, (1, 4))[:, None, :]
    wm = w_merge.astype(BF16)
    bm = b_merge[:, None, :]
    wb = w_branch.astype(BF16)
    wo = w_out.astype(BF16)
    w1 = mlp_w1.astype(BF16)
    w2 = mlp_w2.astype(BF16)
    sink = a_sink.reshape(-1)
    caches = [t.reshape(dec_batch, depth, past, 2 * HEAD_DIM) for t in (cache_a_k, cache_a_v, cache_d_k, cache_d_v)]
    s0_dec = state_gdn.transpose(0, 1, 2, 4, 3, 5).reshape(dec_batch, depth, 2, CHUNK, 256)
    zeros_lru = jnp.zeros((batch, 1, 2, BRANCH_W), F32)
    zeros_gdn = jnp.zeros((batch, 1, 2, CHUNK, 256), F32)

    x = (x_prompt.reshape(n_ctx, d), x_sample.reshape(n_dec, d))

    kv_ctx, lru_states, gdn_states = None, [], []
    for l in range(depth):
        (qa, ka, va, lx, lg, gq, gk, gv, gz, ggb, qd, kd, vd), kv_ctx = _inproj_call(
            x, meta, mod_all, g1, w_packed, cos_t, sin_t, vecs, ones_bd, kv_ctx, l=l, depth=depth, tm=tm)

        oa_c, od_c = _attn_ctx_call(sink, qa, ka, va, qd, kd, vd, l=l, n_seq=batch, t=seq)
        oa_d = _attn_band_call(sink, qa, ka, va, caches[0], caches[1], l=l, n_seq=dec_batch, t=dec_seq,
                               row0=n_ctx, tq=256)
        od_d = _attn_full_call(qd, kd, vd, caches[2], caches[3], l=l, n_seq=dec_batch, t=dec_seq,
                               row0=n_ctx, tq=256)

        ob_c, st_c = _lru_call(lx, lg, w_lru, b_lru, lru_lam, zeros_lru, l=l, l_state=0, n_seq=batch, t=seq,
                               row0=0)
        ob_d, _ = _lru_call(lx, lg, w_lru, b_lru, lru_lam, state_lru, l=l, l_state=l, n_seq=dec_batch,
                            t=dec_seq, row0=n_ctx)

        oc_pairs, sts = [], []
        for dd in range(2):
            o_c, s_c = _gdn_call(gq, gk, gv, ggb, zeros_gdn, d=dd, l_state=0, n_seq=batch, t=seq, row0=0,
                                 tt=seq, n_par=gdn_par)
            o_d, _ = _gdn_call(gq, gk, gv, ggb, s0_dec, d=dd, l_state=l, n_seq=dec_batch, t=dec_seq,
                               row0=n_ctx, tt=gdn_tt, n_par=gdn_par_dec)
            oc_pairs.append((o_c, o_d))
            sts.append(s_c)

        x = _mix_mlp_call(meta[0], x, mod_all, g1,
                          [(oa_c, oa_d), (ob_c, ob_d), oc_pairs[0], oc_pairs[1], (od_c, od_d)],
                          gz, gng, ones_bd, wm, bm, wb, wo, g2, w1, w2, l=l, tm=tm)

        lru_states.append(st_c)
        gdn_states.append(jnp.concatenate(sts, axis=1))

    y_prompt = x[0].reshape(batch, seq, d)
    y_sample = x[1].reshape(dec_batch, dec_seq, d)
    new_kv = [t.reshape(batch, depth, seq, 2, HEAD_DIM) for t in kv_ctx]
    new_lru = jnp.stack(lru_states, axis=1)
    new_gdn = (jnp.stack(gdn_states, axis=1).reshape(batch, depth, 2, GDN_DK, GDN_H, GDN_DK)
               .transpose(0, 1, 2, 4, 3, 5))
    return (y_prompt, y_sample, new_kv[0], new_kv[1], new_kv[2], new_kv[3], new_lru, new_gdn)
```
